```python
import math
import jax, jax.numpy as jnp
from jax import lax
import numpy as np

D_MODEL = 1024
BATCH = 8
SEQ = 2048
DEPTH = 2
DEC_BATCH = 128
DEC_SEQ = 1
PAST_LEN = 16384
PAGE_SIZE = 128

PLE_DIM = 256
D_FF = 4 * D_MODEL
CONV_W = 4
EPS = 1e-6
RG_WIDTH = D_MODEL // 2
RG_HEADS = 8
RG_HEAD_DIM = RG_WIDTH // RG_HEADS
RG_C = 8.0
SSD_WIDTH = D_MODEL // 2
SSD_HEAD_DIM = 64
SSD_HEADS = SSD_WIDTH // SSD_HEAD_DIM
SSD_STATE = 128
SSD_CHUNK = 128
SSD_CONV_DIM = SSD_WIDTH + 2 * SSD_STATE
HG_WIDTH = D_MODEL // 2
HG_HEADS = 8
HG_HEAD_DIM = HG_WIDTH // HG_HEADS
HG_CHUNK = 64
D_MIX = RG_WIDTH + SSD_WIDTH + HG_WIDTH
IN_SPLITS = (RG_WIDTH, RG_WIDTH, SSD_WIDTH, SSD_CONV_DIM, SSD_HEADS, HG_WIDTH, HG_WIDTH, HG_WIDTH, HG_WIDTH)
D_IN_PROJ = RG_WIDTH * 2 + SSD_WIDTH + SSD_CONV_DIM + SSD_HEADS + HG_WIDTH * 4

kernel_name = 'hymba_style_rglru_ssd_hgrn2_decoder'


def rmsnorm(x, g):
    xf = x.astype(jnp.float32)
    y = xf * lax.rsqrt(jnp.mean(xf * xf, axis=-1, keepdims=True) + EPS)
    return (y * g.astype(jnp.float32)).astype(x.dtype)


def causal_conv(x, buf, w, b):
    xe = jnp.concatenate([buf.astype(x.dtype), x], axis=1)
    L = x.shape[1]
    y = sum(xe[:, k:k + L] * w[k] for k in range(CONV_W)) + b
    return y, xe[:, -(CONV_W - 1):]


def to_chunks(t, q):
    bsz, L = t.shape[:2]
    lp = -(-L // q) * q
    t = jnp.pad(t, [(0, 0), (0, lp - L)] + [(0, 0)] * (t.ndim - 2))
    return jnp.moveaxis(t.reshape(bsz, lp // q, q, *t.shape[2:]), 1, 0)


def from_chunks(t, L):
    nc, bsz, q = t.shape[:3]
    return jnp.moveaxis(t, 0, 1).reshape(bsz, nc * q, *t.shape[3:])[:, :L]


def rg_lru(x, wa, ba, wx, bx, lam, h0, pos0):
    bsz, L, _ = x.shape
    xh = x.reshape(bsz, L, RG_HEADS, RG_HEAD_DIM)
    r = jax.nn.sigmoid(jnp.einsum('blhi,hij->blhj', xh, wa).reshape(bsz, L, RG_WIDTH) + ba)
    i = jax.nn.sigmoid(jnp.einsum('blhi,hij->blhj', xh, wx).reshape(bsz, L, RG_WIDTH) + bx)
    log_a = (-RG_C * r * jax.nn.softplus(-lam)).astype(jnp.float32)
    a = jnp.exp(log_a)
    mult = jnp.sqrt(-jnp.expm1(2.0 * log_a))
    pos = pos0 + jnp.arange(L)
    mult = jnp.where((pos == 0)[None, :, None], 1.0, mult)
    b = mult * (i * x).astype(jnp.float32)
    b = b.at[:, 0].add(a[:, 0] * h0.astype(jnp.float32))

    def combine(c1, c2):
        a1, b1 = c1
        a2, b2 = c2
        return a1 * a2, a2 * b1 + b2

    _, h = lax.associative_scan(combine, (a, b), axis=1)
    return h.astype(x.dtype), h[:, -1]


def ssd_chunked(x, dt, a_neg, bm, cm, s0):
    L = x.shape[1]
    q = min(SSD_CHUNK, L)
    causal = jnp.tril(jnp.ones((q, q), bool))
    xs = (to_chunks(x, q), to_chunks(dt, q), to_chunks(bm, q), to_chunks(cm, q))

    def step(s, inp):
        xc, dtc, bc, cc = inp
        cum = jnp.cumsum(dtc * a_neg, axis=1)
        seg = cum[:, :, None, :] - cum[:, None, :, :]
        lmat = jnp.exp(jnp.where(causal[None, :, :, None], seg, -jnp.inf))
        cb = jnp.einsum('btn,bsn->bts', cc, bc)
        y = jnp.einsum('bts,btsh,bsh,bshp->bthp', cb, lmat, dtc, xc)
        y = y + jnp.einsum('btn,bhpn,bth->bthp', cc, s, jnp.exp(cum))
        w_in = jnp.exp(cum[:, -1:, :] - cum) * dtc
        s = jnp.exp(cum[:, -1])[:, :, None, None] * s + jnp.einsum('bsn,bsh,bshp->bhpn', bc, w_in, xc)
        return s, y

    s, ys = lax.scan(step, s0.astype(jnp.float32), xs)
    return from_chunks(ys, L), s


def hgrn_chunked(qv, kv, logf, v, s0):
    L = qv.shape[1]
    q = min(HG_CHUNK, L)
    causal = jnp.tril(jnp.ones((q, q), bool))
    xs = (to_chunks(qv, q), to_chunks(kv, q), to_chunks(logf, q), to_chunks(v, q))

    def step(s, inp):
        qc, kc, gc, vc = inp
        b = jnp.cumsum(gc, axis=1)
        seg = b[:, :, None] - b[:, None]
        dec = jnp.exp(jnp.where(causal[None, :, :, None, None], seg, -jnp.inf))
        att = jnp.einsum('bthd,bshd,btshd->bhts', qc, kc, dec)
        o = jnp.einsum('bhts,bshe->bthe', att, vc)
        o = o + jnp.einsum('bthd,bhde->bthe', qc * jnp.exp(b), s)
        s = jnp.exp(b[:, -1])[..., None] * s + jnp.einsum('bshd,bshe->bhde', kc * jnp.exp(b[:, -1:] - b), vc)
        return s, o

    s, os_ = lax.scan(step, s0.astype(jnp.float32), xs)
    return from_chunks(os_, L), s


def layer_forward(h, p_i, rg_conv, rg_h, ssd_conv, ssd_s, hg_s, lw, lb, pos0):
    f32 = jnp.float32
    bsz, L, _ = h.shape
    u = rmsnorm(h, lw['norm_mix'])
    proj = u @ lw['w_in']
    cuts = [int(c) for c in np.cumsum(IN_SPLITS)[:-1]]
    a_x, a_g, b_z, b_xbc, b_dt, c_q, c_f, c_i, c_g = jnp.split(proj, cuts, axis=-1)

    xa, new_rg_conv = causal_conv(a_x, rg_conv, lw['conv_a_w'], lw['conv_a_b'])
    ha, new_rg_h = rg_lru(xa, lw['rg_wa'], lw['rg_ba'], lw['rg_wx'], lw['rg_bx'], lw['rg_lambda'], rg_h, pos0)
    y_a = ha * jax.nn.gelu(a_g)

    xbc, new_ssd_conv = causal_conv(b_xbc, ssd_conv, lw['conv_b_w'], lw['conv_b_b'])
    xbc = jax.nn.silu(xbc).astype(f32)
    xs, bm, cm = jnp.split(xbc, [SSD_WIDTH, SSD_WIDTH + SSD_STATE], axis=-1)
    xs = xs.reshape(bsz, L, SSD_HEADS, SSD_HEAD_DIM)
    dt = jax.nn.softplus(b_dt.astype(f32) + lw['ssd_dt_bias'].astype(f32))
    a_neg = -jnp.exp(lw['ssd_a_log'].astype(f32))
    yb, new_ssd_s = ssd_chunked(xs, dt, a_neg, bm, cm, ssd_s)
    yb = (yb + lw['ssd_d'].astype(f32)[:, None] * xs).reshape(bsz, L, SSD_WIDTH).astype(h.dtype)
    y_b = rmsnorm(yb * jax.nn.silu(b_z), lw['ssd_norm'])

    qv = jax.nn.silu(c_q.astype(f32)).reshape(bsz, L, HG_HEADS, HG_HEAD_DIM)
    lbh = lb.reshape(HG_HEADS, HG_HEAD_DIM)
    g = lbh + (1.0 - lbh) * jax.nn.sigmoid(c_f.astype(f32).reshape(bsz, L, HG_HEADS, HG_HEAD_DIM))
    kv = 1.0 - g
    logf = jnp.log(g)
    v = c_i.astype(f32).reshape(bsz, L, HG_HEADS, HG_HEAD_DIM)
    o, new_hg_s = hgrn_chunked(qv, kv, logf, v, hg_s)
    o = rmsnorm(o.astype(h.dtype), lw['hg_norm']) * jax.nn.silu(c_g.reshape(bsz, L, HG_HEADS, HG_HEAD_DIM))
    y_c = o.reshape(bsz, L, HG_WIDTH)

    h = h + jnp.concatenate([y_a, y_b, y_c], axis=-1) @ lw['w_out']
    z = jnp.square(jax.nn.relu(rmsnorm(h, lw['norm_ffn']) @ lw['w_up']))
    h = h + z @ lw['w_down']
    gate = jax.nn.sigmoid(rmsnorm(h, lw['norm_ple']) @ lw['w_ple_gate'])
    h = h + gate * (p_i @ lw['w_ple_proj'])
    return h, (new_rg_conv, new_rg_h, new_ssd_conv, new_ssd_s, new_hg_s)


def run_trunk(x, p, rg_conv, rg_h, ssd_conv, ssd_s, hg_s, W, norm_final, lbs, pos0):
    h = x
    outs = ([], [], [], [], [])
    for i in range(DEPTH):
        lw = {name: arr[i] for name, arr in W.items()}
        h, st = layer_forward(h, p[i], rg_conv[i], rg_h[i], ssd_conv[i], ssd_s[i], hg_s[i], lw, lbs[i], pos0)
        for lst, s in zip(outs, st):
            lst.append(s)
    y = rmsnorm(h, norm_final)
    return y, [jnp.stack(lst) for lst in outs]


def setup_inputs(seed: int = 0) -> dict:
    key = jax.random.key(seed)
    ks = iter(jax.random.split(key, 48))
    f32 = jnp.float32

    def nrm(shape, s):
        return jax.random.normal(next(ks), shape, f32) * s

    x_prompt = nrm((BATCH, SEQ, D_MODEL), 1.0)
    x_sample = nrm((DEC_BATCH, DEC_SEQ, D_MODEL), 1.0)
    state_rg_conv = nrm((DEPTH, DEC_BATCH, CONV_W - 1, RG_WIDTH), 1.0)
    state_rg_h = nrm((DEPTH, DEC_BATCH, RG_WIDTH), 0.5)
    state_ssd_conv = nrm((DEPTH, DEC_BATCH, CONV_W - 1, SSD_CONV_DIM), 1.0)
    state_ssd = nrm((DEPTH, DEC_BATCH, SSD_HEADS, SSD_HEAD_DIM, SSD_STATE), 0.1)
    state_hgrn = nrm((DEPTH, DEC_BATCH, HG_HEADS, HG_HEAD_DIM, HG_HEAD_DIM), 0.3)
    p_prompt = nrm((DEPTH, BATCH, SEQ, PLE_DIM), 1.0)
    p_sample = nrm((DEPTH, DEC_BATCH, DEC_SEQ, PLE_DIM), 1.0)

    norm_mix = 1.0 + nrm((DEPTH, D_MODEL), 0.05)
    w_in = nrm((DEPTH, D_MODEL, D_IN_PROJ), D_MODEL ** -0.5)
    conv_a_w = nrm((DEPTH, CONV_W, RG_WIDTH), CONV_W ** -0.5)
    conv_a_b = nrm((DEPTH, RG_WIDTH), 0.01)
    rg_wa = nrm((DEPTH, RG_HEADS, RG_HEAD_DIM, RG_HEAD_DIM), RG_HEAD_DIM ** -0.5)
    rg_ba = nrm((DEPTH, RG_WIDTH), 0.01)
    rg_wx = nrm((DEPTH, RG_HEADS, RG_HEAD_DIM, RG_HEAD_DIM), RG_HEAD_DIM ** -0.5)
    rg_bx = nrm((DEPTH, RG_WIDTH), 0.01)
    u = jax.random.uniform(next(ks), (DEPTH, RG_WIDTH), f32, 0.9, 0.999)
    a0 = u ** (1.0 / RG_C)
    rg_lambda = jnp.log(a0) - jnp.log1p(-a0)
    conv_b_w = nrm((DEPTH, CONV_W, SSD_CONV_DIM), CONV_W ** -0.5)
    conv_b_b = nrm((DEPTH, SSD_CONV_DIM), 0.01)
    dt0 = jnp.exp(jax.random.uniform(next(ks), (DEPTH, SSD_HEADS), f32, math.log(1e-3), math.log(1e-1)))
    ssd_dt_bias = dt0 + jnp.log(-jnp.expm1(-dt0))
    ssd_a_log = jnp.log(jax.random.uniform(next(ks), (DEPTH, SSD_HEADS), f32, 1.0, 16.0))
    ssd_d = 1.0 + nrm((DEPTH, SSD_HEADS), 0.1)
    ssd_norm = 1.0 + nrm((DEPTH, SSD_WIDTH), 0.05)
    hg_lower_bounds = nrm((DEPTH, HG_WIDTH), 0.5)
    hg_norm = 1.0 + nrm((DEPTH, HG_HEAD_DIM), 0.05)
    w_out = nrm((DEPTH, D_MIX, D_MODEL), D_MIX ** -0.5)
    norm_ffn = 1.0 + nrm((DEPTH, D_MODEL), 0.05)
    w_up = nrm((DEPTH, D_MODEL, D_FF), D_MODEL ** -0.5)
    w_down = nrm((DEPTH, D_FF, D_MODEL), D_FF ** -0.5)
    norm_ple = 1.0 + nrm((DEPTH, D_MODEL), 0.05)
    w_ple_gate = nrm((DEPTH, D_MODEL, D_MODEL), D_MODEL ** -0.5)
    w_ple_proj = nrm((DEPTH, PLE_DIM, D_MODEL), PLE_DIM ** -0.5)
    norm_final = 1.0 + nrm((D_MODEL,), 0.05)
    return {
        'x_prompt': x_prompt, 'x_sample': x_sample,
        'state_rg_conv': state_rg_conv, 'state_rg_h': state_rg_h,
        'state_ssd_conv': state_ssd_conv, 'state_ssd': state_ssd, 'state_hgrn': state_hgrn,
        'p_prompt': p_prompt, 'p_sample': p_sample,
        'norm_mix': norm_mix, 'w_in': w_in,
        'conv_a_w': conv_a_w, 'conv_a_b': conv_a_b,
        'rg_wa': rg_wa, 'rg_ba': rg_ba, 'rg_wx': rg_wx, 'rg_bx': rg_bx, 'rg_lambda': rg_lambda,
        'conv_b_w': conv_b_w, 'conv_b_b': conv_b_b,
        'ssd_dt_bias': ssd_dt_bias, 'ssd_a_log': ssd_a_log, 'ssd_d': ssd_d, 'ssd_norm': ssd_norm,
        'hg_lower_bounds': hg_lower_bounds, 'hg_norm': hg_norm,
        'w_out': w_out, 'norm_ffn': norm_ffn, 'w_up': w_up, 'w_down': w_down,
        'norm_ple': norm_ple, 'w_ple_gate': w_ple_gate, 'w_ple_proj': w_ple_proj,
        'norm_final': norm_final,
    }


def reference(x_prompt, x_sample, state_rg_conv, state_rg_h, state_ssd_conv, state_ssd, state_hgrn,
              p_prompt, p_sample, norm_mix, w_in, conv_a_w, conv_a_b, rg_wa, rg_ba, rg_wx, rg_bx,
              rg_lambda, conv_b_w, conv_b_b, ssd_dt_bias, ssd_a_log, ssd_d, ssd_norm,
              hg_lower_bounds, hg_norm, w_out, norm_ffn, w_up, w_down, norm_ple, w_ple_gate,
              w_ple_proj, norm_final):
    W = dict(norm_mix=norm_mix, w_in=w_in, conv_a_w=conv_a_w, conv_a_b=conv_a_b,
             rg_wa=rg_wa, rg_ba=rg_ba, rg_wx=rg_wx, rg_bx=rg_bx, rg_lambda=rg_lambda,
             conv_b_w=conv_b_w, conv_b_b=conv_b_b, ssd_dt_bias=ssd_dt_bias, ssd_a_log=ssd_a_log,
             ssd_d=ssd_d, ssd_norm=ssd_norm, hg_norm=hg_norm, w_out=w_out, norm_ffn=norm_ffn,
             w_up=w_up, w_down=w_down, norm_ple=norm_ple, w_ple_gate=w_ple_gate, w_ple_proj=w_ple_proj)
    sm = jax.nn.softmax(hg_lower_bounds.astype(jnp.float32), axis=0)
    cs = jnp.cumsum(sm, axis=0)
    lbs = cs - cs[0:1]

    f32 = jnp.float32
    z_rc = jnp.zeros((DEPTH, BATCH, CONV_W - 1, RG_WIDTH), x_prompt.dtype)
    z_rh = jnp.zeros((DEPTH, BATCH, RG_WIDTH), f32)
    z_sc = jnp.zeros((DEPTH, BATCH, CONV_W - 1, SSD_CONV_DIM), x_prompt.dtype)
    z_ss = jnp.zeros((DEPTH, BATCH, SSD_HEADS, SSD_HEAD_DIM, SSD_STATE), f32)
    z_hs = jnp.zeros((DEPTH, BATCH, HG_HEADS, HG_HEAD_DIM, HG_HEAD_DIM), f32)

    y_prompt, st_p = run_trunk(x_prompt, p_prompt, z_rc, z_rh, z_sc, z_ss, z_hs, W, norm_final, lbs, 0)
    y_sample, st_s = run_trunk(x_sample, p_sample, state_rg_conv, state_rg_h, state_ssd_conv,
                               state_ssd, state_hgrn, W, norm_final, lbs, PAST_LEN)
    rc_p, rh_p, sc_p, ss_p, hs_p = st_p
    rc_s, rh_s, sc_s, ss_s, hs_s = st_s
    return (y_prompt, y_sample, rc_p, rh_p, sc_p, ss_p, hs_p, rc_s, rh_s, sc_s, ss_s, hs_s)
```

```python
import functools

import numpy as np
import jax
import jax.numpy as jnp
from jax import lax
from jax.experimental import pallas as pl
from jax.experimental.pallas import tpu as pltpu

F32 = jnp.float32
BF16 = jnp.bfloat16

D_MODEL = 1024
DEPTH = 2
PAST_LEN = 16384
PLE_DIM = 256
D_FF = 4 * D_MODEL
CONV_W = 4
EPS = 1e-6
RG_WIDTH = 512
RG_HEADS = 8
RG_HEAD_DIM = 64
RG_C = 8.0
SSD_WIDTH = 512
SSD_HEAD_DIM = 64
SSD_HEADS = 8
SSD_STATE = 128
SSD_CONV_DIM = SSD_WIDTH + 2 * SSD_STATE
HG_WIDTH = 512
HG_HEADS = 8
HG_HEAD_DIM = 64
D_MIX = RG_WIDTH + SSD_WIDTH + HG_WIDTH

LANES = 128
SUBLANES = 8

OFF_AX = 0
OFF_AG = 512
OFF_BZ = 1024
OFF_XBC = 1536
OFF_CQ = 2304
OFF_CF = 2816
OFF_CI = 3328
OFF_CG = 3840
OFF_DT = 4352
NPROJ = OFF_DT + LANES

TC = 128
N_LEVELS = 7
DEC_BB = 8
VMEM_LIMIT = 52 * 1024 * 1024


def _bf(x):
    return x.astype(BF16)


def _dot(a, b):
    return jnp.dot(a, b, preferred_element_type=F32)


def _dot_nt(a, b):
    return lax.dot_general(a, b, (((1,), (1,)), ((), ())), preferred_element_type=F32)


def _split3(x):
    hi = _bf(x)
    r1 = x - hi.astype(F32)
    mid = _bf(r1)
    lo = _bf(r1 - mid.astype(F32))
    return hi, mid, lo


def _dot_c_x(c, x):
    hi, mid, lo = _split3(x)
    return _dot(c, hi) + _dot(c, mid) + _dot(c, lo)


def _dot_x_c(x, c):
    hi, mid, lo = _split3(x)
    return _dot(hi, c) + _dot(mid, c) + _dot(lo, c)


def _sigmoid(x):
    return 1.0 / (1.0 + jnp.exp(-x))


def _silu(x):
    return x * _sigmoid(x)


def _softplus(x):
    return jnp.maximum(x, 0.0) + jnp.log1p(jnp.exp(-jnp.abs(x)))


def _gelu_tanh(x):
    c = np.float32(np.sqrt(2.0 / np.pi))
    return 0.5 * x * (1.0 + jnp.tanh(c * (x + 0.044715 * (x * x * x))))


def _rms(x, gain):
    return x * lax.rsqrt(jnp.mean(x * x, axis=-1, keepdims=True) + EPS) * gain


def _lower_bound_row(hlb, layer):
    m = jnp.max(hlb, axis=0, keepdims=True)
    e = jnp.exp(hlb - m)
    sm = e / jnp.sum(e, axis=0, keepdims=True)
    lb = jnp.zeros((1, HG_WIDTH), F32)
    for j in range(1, layer + 1):
        lb = lb + sm[j:j + 1, :]
    return lb


def _rg_gates(xa, wg_ref, bg_ref, lam_ref):
    gates = _dot(_bf(xa), wg_ref[...]) + bg_ref[...]
    r = _sigmoid(gates[:, :RG_WIDTH])
    i = _sigmoid(gates[:, RG_WIDTH:])
    log_a = -RG_C * r * _softplus(-lam_ref[...])
    a = jnp.exp(log_a)
    mult = jnp.sqrt(-jnp.tanh(log_a) * (a * a + 1.0))
    return a, mult, i


def _head_rms_gate(o, cg, headmean_ref, hnorm_ref):
    ms = _dot(_bf(o * o), headmean_ref[...])
    return o * lax.rsqrt(ms + EPS) * hnorm_ref[...] * _silu(cg)


def _lane_halves(x):
    lane = lax.broadcasted_iota(jnp.int32, x.shape, 1)
    lo = jnp.where(lane < HG_HEAD_DIM, x, 0.0)
    hi = jnp.where(lane >= HG_HEAD_DIM, x, 0.0)
    return jnp.concatenate([lo, hi], axis=0)


def _proj_kernel(x_ref, g_ref, w_ref, o_ref):
    u = _rms(x_ref[...], g_ref[...])
    o_ref[...] = _dot(_bf(u), w_ref[...])


def _const_spec(arr):
    nd = arr.ndim
    return pl.BlockSpec(arr.shape, lambda *_: (0,) * nd, pipeline_mode=pl.Buffered(1))


def _in_proj(x2d, gain, w_bf, tm):
    m = x2d.shape[0]
    return pl.pallas_call(
        _proj_kernel,
        grid=(m // tm,),
        in_specs=[pl.BlockSpec((tm, D_MODEL), lambda i: (i, 0)),
                  _const_spec(gain), _const_spec(w_bf)],
        out_specs=pl.BlockSpec((tm, NPROJ), lambda i: (i, 0)),
        out_shape=jax.ShapeDtypeStruct((m, NPROJ), F32),
        compiler_params=pltpu.CompilerParams(
            dimension_semantics=("parallel",), vmem_limit_bytes=VMEM_LIMIT),
        name="in_proj",
    )(x2d, gain, w_bf)


def _prompt_mixer_kernel(layer, pos0,
                         proj_ref, caw, cab, wg, bg, lam, cbw, cbb, dtb, alog, dexp, snorm,
                         hlb, hnorm, ltri, utri, expand, emat, lmask, bdmask, headmean,
                         ymix_ref, rgc_o, rgh_o, sc_o, ssd_o, hg_o,
                         xea, xeb, hcar, s_t, st_hg):
    c = pl.program_id(1)
    last = pl.num_programs(1) - 1

    @pl.when(c == 0)
    def _init():
        xea[0:SUBLANES, :] = jnp.zeros((SUBLANES, RG_WIDTH), F32)
        xeb[0:SUBLANES, :] = jnp.zeros((SUBLANES, SSD_CONV_DIM), F32)
        hcar[...] = jnp.zeros_like(hcar)
        s_t[...] = jnp.zeros_like(s_t)
        st_hg[...] = jnp.zeros_like(st_hg)

    row = lax.broadcasted_iota(jnp.int32, (TC, 1), 0)

    xea[SUBLANES:SUBLANES + TC, :] = proj_ref[0, :, OFF_AX:OFF_AX + RG_WIDTH]
    xa = cab[...]
    for k in range(CONV_W):
        xa = xa + caw[k:k + 1, :] * xea[SUBLANES - 3 + k:SUBLANES - 3 + k + TC, :]
    a, mult, gi = _rg_gates(xa, wg, bg, lam)
    mult = jnp.where(row + (c * TC + pos0) == 0, 1.0, mult)
    b = mult * (gi * xa)
    k = 1
    while k < TC:
        keep = row >= k
        a_sh = jnp.where(keep, pltpu.roll(a, k, 0), 1.0)
        b_sh = jnp.where(keep, pltpu.roll(b, k, 0), 0.0)
        b = b + a * b_sh
        a = a * a_sh
        k *= 2
    h = a * hcar[0:1, :] + b
    hcar[0:1, :] = h[TC - 1:TC, :]
    ymix_ref[0, :, 0:RG_WIDTH] = h * _gelu_tanh(proj_ref[0, :, OFF_AG:OFF_AG + RG_WIDTH])

    xeb[SUBLANES:SUBLANES + TC, :] = proj_ref[0, :, OFF_XBC:OFF_XBC + SSD_CONV_DIM]
    xbc = cbb[...]
    for k in range(CONV_W):
        xbc = xbc + cbw[k:k + 1, :] * xeb[SUBLANES - 3 + k:SUBLANES - 3 + k + TC, :]
    xbc = _silu(xbc)
    xs = xbc[:, :SSD_WIDTH]
    bm = xbc[:, SSD_WIDTH:SSD_WIDTH + SSD_STATE]
    cm = xbc[:, SSD_WIDTH + SSD_STATE:]
    dtp = _softplus(proj_ref[0, :, OFF_DT:OFF_DT + LANES] + dtb[...])
    v_da = dtp * (-jnp.exp(alog[...]))
    cum = _dot_c_x(ltri[...], v_da)
    cum_t = _dot_x_c(v_da.T, utri[...])
    dt_e = _dot_x_c(dtp, expand[...])
    cum_e = _dot_x_c(cum, expand[...])
    cum_last = cum_e[TC - 1:TC, :]
    cb = _dot_nt(_bf(cm), _bf(bm))
    col = lax.broadcasted_iota(jnp.int32, (TC, TC), 1)
    causal = col <= lax.broadcasted_iota(jnp.int32, (TC, TC), 0)
    dtx = xs * dt_e
    y_parts = []
    for g in range(SSD_HEADS // 2):
        gs = []
        for hh in range(2):
            hd = 2 * g + hh
            seg = cum[:, hd:hd + 1] - cum_t[hd:hd + 1, :]
            gs.append(_bf(cb * jnp.exp(jnp.where(causal, seg, -jnp.inf))))
        gcat = jnp.concatenate(gs, axis=1)
        x2 = _bf(_lane_halves(dtx[:, g * LANES:(g + 1) * LANES]))
        y_parts.append(_dot(gcat, x2))
    y = jnp.concatenate(y_parts, axis=1)
    y = y + _dot(_bf(cm), _bf(s_t[...])) * jnp.exp(cum_e)
    yb = y + dexp[...] * xs
    w_e = jnp.exp(cum_last - cum_e) * dt_e
    s_t[...] = jnp.exp(cum_last) * s_t[...] + _dot(_bf(bm.T), _bf(w_e * xs))
    ymix_ref[0, :, RG_WIDTH:RG_WIDTH + SSD_WIDTH] = _rms(
        yb * _silu(proj_ref[0, :, OFF_BZ:OFF_BZ + SSD_WIDTH]), snorm[...])

    q = _silu(proj_ref[0, :, OFF_CQ:OFF_CQ + HG_WIDTH])
    lb = _lower_bound_row(hlb[...], layer)
    gt = lb + (1.0 - lb) * _sigmoid(proj_ref[0, :, OFF_CF:OFF_CF + HG_WIDTH])
    kk = 1.0 - gt
    lf_parts = _split3(jnp.log(gt))
    vv = proj_ref[0, :, OFF_CI:OFF_CI + HG_WIDTH]

    def lf_dot(cmat):
        return _dot(cmat, lf_parts[0]) + _dot(cmat, lf_parts[1]) + _dot(cmat, lf_parts[2])

    att = [None] * HG_HEADS
    for lev in range(N_LEVELS + 1):
        if lev == 0:
            qt, kt = q, kk
        else:
            r0 = 2 * (lev - 1) * TC
            qt = q * jnp.exp(lf_dot(emat[r0:r0 + TC, :]))
            kt = kk * jnp.exp(lf_dot(emat[r0 + TC:r0 + 2 * TC, :]))
        msk = lmask[lev]
        for g in range(HG_HEADS // 2):
            lhs = _bf(_lane_halves(qt[:, g * LANES:(g + 1) * LANES]))
            res = _dot_nt(lhs, _bf(kt[:, g * LANES:(g + 1) * LANES]))
            for hh in range(2):
                term = res[hh * TC:(hh + 1) * TC, :] * msk
                hd = 2 * g + hh
                att[hd] = term if att[hd] is None else att[hd] + term
    o_parts = []
    for g in range(HG_HEADS // 2):
        acat = jnp.concatenate([_bf(att[2 * g]), _bf(att[2 * g + 1])], axis=1)
        v2 = _bf(_lane_halves(vv[:, g * LANES:(g + 1) * LANES]))
        o_parts.append(_dot(acat, v2))
    bcum = lf_dot(ltri[...])
    blast = bcum[TC - 1:TC, :]
    o = jnp.concatenate(o_parts, axis=1) + _dot_nt(_bf(q * jnp.exp(bcum)), _bf(st_hg[...]))
    khat = kk * jnp.exp(blast - bcum)
    st_hg[...] = jnp.exp(blast) * st_hg[...] + bdmask[...] * _dot(_bf(vv.T), _bf(khat))
    ymix_ref[0, :, RG_WIDTH + SSD_WIDTH:] = _head_rms_gate(
        o, proj_ref[0, :, OFF_CG:OFF_CG + HG_WIDTH], headmean, hnorm)

    xea[0:SUBLANES, :] = xea[TC:TC + SUBLANES, :]
    xeb[0:SUBLANES, :] = xeb[TC:TC + SUBLANES, :]

    @pl.when(c == last)
    def _emit():
        rgc_o[0] = xea[SUBLANES - 3:SUBLANES, :]
        sc_o[0] = xeb[SUBLANES - 3:SUBLANES, :]
        rgh_o[0] = hcar[0:1, :]
        ssd_o[0] = s_t[...].T
        s_full = st_hg[...].T
        for hd in range(HG_HEADS):
            lo = hd * HG_HEAD_DIM
            hg_o[0, hd] = s_full[lo:lo + HG_HEAD_DIM, lo:lo + HG_HEAD_DIM]


def _prompt_mixer(layer, proj3, mp, consts):
    bsz, seq, _ = proj3.shape
    nc = seq // TC
    params = [mp["caw"], mp["cab"], mp["wg"], mp["bg"], mp["lam"], mp["cbw"], mp["cbb"],
              mp["dtb"], mp["alog"], mp["dexp"], mp["snorm"], mp["hlb"], mp["hnorm"],
              consts["ltri"], consts["utri"], consts["expand"], consts["emat"],
              consts["lmask"], consts["bdmask"], consts["headmean"]]
    out_shape = (
        jax.ShapeDtypeStruct((bsz, seq, D_MIX), F32),
        jax.ShapeDtypeStruct((bsz, CONV_W - 1, RG_WIDTH), F32),
        jax.ShapeDtypeStruct((bsz, 1, RG_WIDTH), F32),
        jax.ShapeDtypeStruct((bsz, CONV_W - 1, SSD_CONV_DIM), F32),
        jax.ShapeDtypeStruct((bsz, SSD_WIDTH, SSD_STATE), F32),
        jax.ShapeDtypeStruct((bsz, HG_HEADS, HG_HEAD_DIM, HG_HEAD_DIM), F32),
    )
    out_specs = (
        pl.BlockSpec((1, TC, D_MIX), lambda b, c: (b, c, 0)),
        pl.BlockSpec((1, CONV_W - 1, RG_WIDTH), lambda b, c: (b, 0, 0)),
        pl.BlockSpec((1, 1, RG_WIDTH), lambda b, c: (b, 0, 0)),
        pl.BlockSpec((1, CONV_W - 1, SSD_CONV_DIM), lambda b, c: (b, 0, 0)),
        pl.BlockSpec((1, SSD_WIDTH, SSD_STATE), lambda b, c: (b, 0, 0)),
        pl.BlockSpec((1, HG_HEADS, HG_HEAD_DIM, HG_HEAD_DIM), lambda b, c: (b, 0, 0, 0)),
    )
    scratch = [
        pltpu.VMEM((TC + SUBLANES, RG_WIDTH), F32),
        pltpu.VMEM((TC + SUBLANES, SSD_CONV_DIM), F32),
        pltpu.VMEM((SUBLANES, RG_WIDTH), F32),
        pltpu.VMEM((SSD_STATE, SSD_WIDTH), F32),
        pltpu.VMEM((HG_WIDTH, HG_WIDTH), F32),
    ]
    return pl.pallas_call(
        functools.partial(_prompt_mixer_kernel, layer, 0),
        grid=(bsz, nc),
        in_specs=[pl.BlockSpec((1, TC, NPROJ), lambda b, c: (b, c, 0))]
        + [_const_spec(p) for p in params],
        out_specs=out_specs,
        out_shape=out_shape,
        scratch_shapes=scratch,
        compiler_params=pltpu.CompilerParams(
            dimension_semantics=("parallel", "arbitrary"), vmem_limit_bytes=VMEM_LIMIT),
        name="prompt_mixer",
    )(proj3, *params)


def _pad_rows_t(x):
    pad = jnp.zeros((LANES - x.shape[0], x.shape[1]), F32)
    return jnp.concatenate([x, pad], axis=0).T


def _sample_mixer_kernel(layer, pos0,
                         proj_ref, rgc_ref, rgh_ref, sc_ref, ssd_ref, hg_ref,
                         caw, cab, wg, bg, lam, cbw, cbb, dtb, alog, dexp, snorm,
                         hlb, hnorm, expand, headsum, headmean,
                         ymix_ref, rgc_o, rgh_o, sc_o, ssd_o, hg_o):
    bb = DEC_BB
    ax = proj_ref[:, OFF_AX:OFF_AX + RG_WIDTH]
    xa = cab[...] + caw[CONV_W - 1:CONV_W, :] * ax
    for k in range(CONV_W - 1):
        xa = xa + caw[k:k + 1, :] * rgc_ref[:, k * RG_WIDTH:(k + 1) * RG_WIDTH]
    rgc_o[:, 0:2 * RG_WIDTH] = rgc_ref[:, RG_WIDTH:3 * RG_WIDTH]
    rgc_o[:, 2 * RG_WIDTH:] = ax
    a, mult, gi = _rg_gates(xa, wg, bg, lam)
    if pos0 == 0:
        mult = jnp.ones_like(mult)
    h = a * rgh_ref[...] + mult * (gi * xa)
    rgh_o[...] = h
    ymix_ref[:, 0:RG_WIDTH] = h * _gelu_tanh(proj_ref[:, OFF_AG:OFF_AG + RG_WIDTH])

    bx = proj_ref[:, OFF_XBC:OFF_XBC + SSD_CONV_DIM]
    xbc = cbb[...] + cbw[CONV_W - 1:CONV_W, :] * bx
    for k in range(CONV_W - 1):
        xbc = xbc + cbw[k:k + 1, :] * sc_ref[:, k * SSD_CONV_DIM:(k + 1) * SSD_CONV_DIM]
    sc_o[:, 0:2 * SSD_CONV_DIM] = sc_ref[:, SSD_CONV_DIM:3 * SSD_CONV_DIM]
    sc_o[:, 2 * SSD_CONV_DIM:] = bx
    xbc = _silu(xbc)
    xs = xbc[:, :SSD_WIDTH]
    bm = xbc[:, SSD_WIDTH:SSD_WIDTH + SSD_STATE]
    cm = xbc[:, SSD_WIDTH + SSD_STATE:]
    dtp = _softplus(proj_ref[:, OFF_DT:OFF_DT + LANES] + dtb[...])
    v_da = dtp * (-jnp.exp(alog[...]))
    dt_e = _dot_x_c(dtp, expand[...])
    e_e = jnp.exp(_dot_x_c(v_da, expand[...]))
    dtx = dt_e * xs
    cbs = _dot(_bf(cm * bm), jnp.ones((SSD_STATE, LANES), BF16))[:, 0:1]
    dtx_t = _split3(_pad_rows_t(dtx))
    e_t = _split3(_pad_rows_t(e_e))
    c_pad = _bf(jnp.concatenate([cm, jnp.zeros((LANES - bb, SSD_STATE), F32)], axis=0))
    rowi = lax.broadcasted_iota(jnp.int32, (LANES, LANES), 0)
    lane_w = lax.broadcasted_iota(jnp.int32, (SSD_WIDTH, LANES), 1)
    y_t = jnp.zeros((SSD_WIDTH, LANES), F32)
    for j in range(bb):
        pick = jnp.where(rowi == j, 1.0, 0.0).astype(BF16)
        d_col = _dot(dtx_t[0], pick) + _dot(dtx_t[1], pick) + _dot(dtx_t[2], pick)
        e_col = _dot(e_t[0], pick) + _dot(e_t[1], pick) + _dot(e_t[2], pick)
        s_old = ssd_ref[j]
        ssd_o[j] = e_col * s_old + d_col * bm[j:j + 1, :]
        y_t = y_t + jnp.where(lane_w == j, _dot_nt(_bf(s_old), c_pad), 0.0)
    y = cbs * dtx + y_t.T[0:bb, :] * e_e
    yb = y + dexp[...] * xs
    ymix_ref[:, RG_WIDTH:RG_WIDTH + SSD_WIDTH] = _rms(
        yb * _silu(proj_ref[:, OFF_BZ:OFF_BZ + SSD_WIDTH]), snorm[...])

    q = _silu(proj_ref[:, OFF_CQ:OFF_CQ + HG_WIDTH])
    lb = _lower_bound_row(hlb[...], layer)
    gt = lb + (1.0 - lb) * _sigmoid(proj_ref[:, OFF_CF:OFF_CF + HG_WIDTH])
    kk = 1.0 - gt
    fdec = jnp.exp(jnp.log(gt))
    vv = proj_ref[:, OFF_CI:OFF_CI + HG_WIDTH]
    att = _dot(_bf(q * kk), headsum[...])
    qhat = _bf(q * fdec)
    f_t = _split3(_pad_rows_t(fdec))
    k_t = _split3(_pad_rows_t(kk))
    rowb = lax.broadcasted_iota(jnp.int32, (bb, HG_WIDTH), 0)
    o_acc = jnp.zeros((bb, HG_WIDTH), F32)
    for j in range(bb):
        pick = jnp.where(rowi == j, 1.0, 0.0).astype(BF16)
        f_col = _dot(f_t[0], pick) + _dot(f_t[1], pick) + _dot(f_t[2], pick)
        k_col = _dot(k_t[0], pick) + _dot(k_t[1], pick) + _dot(k_t[2], pick)
        r_parts = []
        for hd in range(HG_HEADS):
            lo = hd * HG_HEAD_DIM
            s_old = hg_ref[j, hd]
            v_row = vv[j:j + 1, lo:lo + HG_HEAD_DIM]
            hg_o[j, hd] = (f_col[lo:lo + HG_HEAD_DIM, 0:HG_HEAD_DIM] * s_old
                           + k_col[lo:lo + HG_HEAD_DIM, 0:HG_HEAD_DIM] * v_row)
            r_parts.append(_dot(qhat[:, lo:lo + HG_HEAD_DIM], _bf(s_old)))
        o_acc = o_acc + jnp.where(rowb == j, jnp.concatenate(r_parts, axis=1), 0.0)
    o = att * vv + o_acc
    ymix_ref[:, RG_WIDTH + SSD_WIDTH:] = _head_rms_gate(
        o, proj_ref[:, OFF_CG:OFF_CG + HG_WIDTH], headmean, hnorm)


def _sample_mixer(layer, proj, rgc, rgh, sc, ssd, hg, mp, consts):
    nb = proj.shape[0]
    bb = DEC_BB
    params = [mp["caw"], mp["cab"], mp["wg"], mp["bg"], mp["lam"], mp["cbw"], mp["cbb"],
              mp["dtb"], mp["alog"], mp["dexp"], mp["snorm"], mp["hlb"], mp["hnorm"],
              consts["expand"], consts["headsum"], consts["headmean"]]
    row2 = lambda w: pl.BlockSpec((bb, w), lambda i: (i, 0))
    state_specs = [
        row2(3 * RG_WIDTH), row2(RG_WIDTH), row2(3 * SSD_CONV_DIM),
        pl.BlockSpec((bb, SSD_WIDTH, SSD_STATE), lambda i: (i, 0, 0)),
        pl.BlockSpec((bb, HG_HEADS, HG_HEAD_DIM, HG_HEAD_DIM), lambda i: (i, 0, 0, 0)),
    ]
    out_shape = (
        jax.ShapeDtypeStruct((nb, D_MIX), F32),
        jax.ShapeDtypeStruct((nb, 3 * RG_WIDTH), F32),
        jax.ShapeDtypeStruct((nb, RG_WIDTH), F32),
        jax.ShapeDtypeStruct((nb, 3 * SSD_CONV_DIM), F32),
        jax.ShapeDtypeStruct((nb, SSD_WIDTH, SSD_STATE), F32),
        jax.ShapeDtypeStruct((nb, HG_HEADS, HG_HEAD_DIM, HG_HEAD_DIM), F32),
    )
    return pl.pallas_call(
        functools.partial(_sample_mixer_kernel, layer, PAST_LEN),
        grid=(nb // bb,),
        in_specs=[row2(NPROJ)] + state_specs + [_const_spec(p) for p in params],
        out_specs=tuple([row2(D_MIX)] + state_specs),
        out_shape=out_shape,
        compiler_params=pltpu.CompilerParams(
            dimension_semantics=("parallel",), vmem_limit_bytes=VMEM_LIMIT),
        name="sample_mixer",
    )(proj, rgc, rgh, sc, ssd, hg, *params)


def _post_kernel(final, h_ref, y_ref, p_ref, wo, nffn, wup, wdown, nple, wgate, wproj, nfin, o_ref):
    h = h_ref[...] + _dot(_bf(y_ref[...]), wo[...])
    z = jnp.square(jnp.maximum(_dot(_bf(_rms(h, nffn[...])), wup[...]), 0.0))
    h = h + _dot(_bf(z), wdown[...])
    gate = _sigmoid(_dot(_bf(_rms(h, nple[...])), wgate[...]))
    h = h + gate * _dot(_bf(p_ref[...]), wproj[...])
    if final:
        h = _rms(h, nfin[...])
    o_ref[...] = h


def _post(final, h2d, ymix2d, p2d, lp, nfin, tm):
    m = h2d.shape[0]
    params = [lp["wo"], lp["nffn"], lp["wup"], lp["wdown"], lp["nple"], lp["wgate"], lp["wproj"], nfin]
    return pl.pallas_call(
        functools.partial(_post_kernel, final),
        grid=(m // tm,),
        in_specs=[pl.BlockSpec((tm, D_MODEL), lambda i: (i, 0)),
                  pl.BlockSpec((tm, D_MIX), lambda i: (i, 0)),
                  pl.BlockSpec((tm, PLE_DIM), lambda i: (i, 0))]
        + [_const_spec(p) for p in params],
        out_specs=pl.BlockSpec((tm, D_MODEL), lambda i: (i, 0)),
        out_shape=jax.ShapeDtypeStruct((m, D_MODEL), F32),
        compiler_params=pltpu.CompilerParams(
            dimension_semantics=("parallel",), vmem_limit_bytes=VMEM_LIMIT),
        name="post_mixer",
    )(h2d, ymix2d, p2d, *params)


def _constants():
    t = np.arange(TC)
    ltri = (t[None, :] <= t[:, None]).astype(np.float32)
    utri = ltri.T.copy()
    expand = np.zeros((LANES, SSD_WIDTH), np.float32)
    for hd in range(SSD_HEADS):
        expand[hd, hd * SSD_HEAD_DIM:(hd + 1) * SSD_HEAD_DIM] = 1.0
    emat = np.zeros((2 * N_LEVELS * TC, TC), np.float32)
    lmask = np.zeros((N_LEVELS + 1, TC, TC), np.float32)
    lmask[0] = np.eye(TC, dtype=np.float32)
    for lev in range(1, N_LEVELS + 1):
        blk = 1 << lev
        mid = (t // blk) * blk + blk // 2
        upper = t >= mid
        r0 = 2 * (lev - 1) * TC
        emat[r0:r0 + TC] = (upper[:, None] & (t[None, :] >= mid[:, None]) & (t[None, :] <= t[:, None]))
        emat[r0 + TC:r0 + 2 * TC] = ((~upper)[:, None] & (t[None, :] > t[:, None]) & (t[None, :] < mid[:, None]))
        lmask[lev] = (upper[:, None] & (~upper)[None, :] & ((t // blk)[:, None] == (t // blk)[None, :]))
    hidx = np.arange(HG_WIDTH) // HG_HEAD_DIM
    bd = (hidx[:, None] == hidx[None, :]).astype(np.float32)
    return {
        "ltri": jnp.asarray(ltri, BF16), "utri": jnp.asarray(utri, BF16),
        "expand": jnp.asarray(expand, BF16), "emat": jnp.asarray(emat, BF16),
        "lmask": jnp.asarray(lmask, F32), "bdmask": jnp.asarray(bd, F32),
        "headsum": jnp.asarray(bd, BF16), "headmean": jnp.asarray(bd / HG_HEAD_DIM, BF16),
    }


def _block_diag(w):
    hh, d, _ = w.shape
    eye = jnp.eye(hh, dtype=w.dtype)
    return (eye[:, None, :, None] * w[:, :, None, :]).reshape(hh * d, hh * d)


def _pad_lanes(v):
    return jnp.pad(v, (0, LANES - v.shape[0]))[None, :]


def kernel(x_prompt, x_sample, state_rg_conv, state_rg_h, state_ssd_conv, state_ssd, state_hgrn,
           p_prompt, p_sample, norm_mix, w_in, conv_a_w, conv_a_b, rg_wa, rg_ba, rg_wx, rg_bx,
           rg_lambda, conv_b_w, conv_b_b, ssd_dt_bias, ssd_a_log, ssd_d, ssd_norm,
           hg_lower_bounds, hg_norm, w_out, norm_ffn, w_up, w_down, norm_ple, w_ple_gate,
           w_ple_proj, norm_final):
    bsz, seq, _ = x_prompt.shape
    nb = x_sample.shape[0]
    consts = _constants()
    nfin = norm_final[None, :]
    dt0 = 2 * RG_WIDTH + SSD_WIDTH + SSD_CONV_DIM

    hp = x_prompt.reshape(bsz * seq, D_MODEL)
    hs = x_sample.reshape(nb, D_MODEL)
    st_p = [[] for _ in range(5)]
    st_s = [[] for _ in range(5)]
    for i in range(DEPTH):
        w_i = w_in[i]
        w_re = jnp.concatenate(
            [w_i[:, :dt0], w_i[:, dt0 + SSD_HEADS:], w_i[:, dt0:dt0 + SSD_HEADS],
             jnp.zeros((D_MODEL, LANES - SSD_HEADS), F32)], axis=1).astype(BF16)
        gmix = norm_mix[i][None, :]
        mp = {
            "caw": conv_a_w[i], "cab": conv_a_b[i][None, :],
            "wg": jnp.concatenate([_block_diag(rg_wa[i]), _block_diag(rg_wx[i])], axis=1).astype(BF16),
            "bg": jnp.concatenate([rg_ba[i], rg_bx[i]])[None, :],
            "lam": rg_lambda[i][None, :],
            "cbw": conv_b_w[i], "cbb": conv_b_b[i][None, :],
            "dtb": _pad_lanes(ssd_dt_bias[i]), "alog": _pad_lanes(ssd_a_log[i]),
            "dexp": jnp.repeat(ssd_d[i], SSD_HEAD_DIM)[None, :],
            "snorm": ssd_norm[i][None, :],
            "hlb": hg_lower_bounds, "hnorm": jnp.tile(hg_norm[i], HG_HEADS)[None, :],
        }
        lp = {
            "wo": w_out[i].astype(BF16), "nffn": norm_ffn[i][None, :],
            "wup": w_up[i].astype(BF16), "wdown": w_down[i].astype(BF16),
            "nple": norm_ple[i][None, :], "wgate": w_ple_gate[i].astype(BF16),
            "wproj": w_ple_proj[i].astype(BF16),
        }
        final = i == DEPTH - 1

        proj_p = _in_proj(hp, gmix, w_re, 256).reshape(bsz, seq, NPROJ)
        ymix_p, rgc, rgh, sc, ssd, hg = _prompt_mixer(i, proj_p, mp, consts)
        hp = _post(final, hp, ymix_p.reshape(bsz * seq, D_MIX),
                   p_prompt[i].reshape(bsz * seq, PLE_DIM), lp, nfin, 256)
        for lst, s in zip(st_p, (rgc, rgh.reshape(bsz, RG_WIDTH), sc,
                                 ssd.reshape(bsz, SSD_HEADS, SSD_HEAD_DIM, SSD_STATE), hg)):
            lst.append(s)

        proj_s = _in_proj(hs, gmix, w_re, nb)
        ymix_s, rgc, rgh, sc, ssd, hg = _sample_mixer(
            i, proj_s,
            state_rg_conv[i].reshape(nb, 3 * RG_WIDTH), state_rg_h[i],
            state_ssd_conv[i].reshape(nb, 3 * SSD_CONV_DIM),
            state_ssd[i].reshape(nb, SSD_WIDTH, SSD_STATE), state_hgrn[i], mp, consts)
        hs = _post(final, hs, ymix_s, p_sample[i].reshape(nb, PLE_DIM), lp, nfin, nb)
        for lst, s in zip(st_s, (rgc.reshape(nb, CONV_W - 1, RG_WIDTH), rgh,
                                 sc.reshape(nb, CONV_W - 1, SSD_CONV_DIM),
                                 ssd.reshape(nb, SSD_HEADS, SSD_HEAD_DIM, SSD_STATE), hg)):
            lst.append(s)

    y_prompt = hp.reshape(bsz, seq, D_MODEL)
    y_sample = hs.reshape(nb, 1, D_MODEL)
    outs_p = [jnp.stack(lst) for lst in st_p]
    outs_s = [jnp.stack(lst) for lst in st_s]
    return (y_prompt, y_sample, *outs_p, *outs_s)
```

```python
import functools

import numpy as np
import jax
import jax.numpy as jnp
from jax import lax
from jax.experimental import pallas as pl
from jax.experimental.pallas import tpu as pltpu

F32 = jnp.float32
BF16 = jnp.bfloat16

D_MODEL = 1024
DEPTH = 2
PAST_LEN = 16384
PLE_DIM = 256
D_FF = 4 * D_MODEL
CONV_W = 4
EPS = 1e-6
RG_WIDTH = 512
RG_HEADS = 8
RG_HEAD_DIM = 64
RG_C = 8.0
SSD_WIDTH = 512
SSD_HEAD_DIM = 64
SSD_HEADS = 8
SSD_STATE = 128
SSD_CONV_DIM = SSD_WIDTH + 2 * SSD_STATE
HG_WIDTH = 512
HG_HEADS = 8
HG_HEAD_DIM = 64
D_MIX = RG_WIDTH + SSD_WIDTH + HG_WIDTH

LANES = 128
SUBLANES = 8

OFF_AX = 0
OFF_AG = 512
OFF_BZ = 1024
OFF_XBC = 1536
OFF_CQ = 2304
OFF_CF = 2816
OFF_CI = 3328
OFF_CG = 3840
OFF_DT = 4352
NPROJ = OFF_DT + LANES

TC = 128
N_LEVELS = 7
DEC_BB = 8
VMEM_LIMIT = 52 * 1024 * 1024


def _bf(x):
    return x.astype(BF16)


def _dot(a, b):
    return jnp.dot(a, b, preferred_element_type=F32)


def _dot_nt(a, b):
    return lax.dot_general(a, b, (((1,), (1,)), ((), ())), preferred_element_type=F32)


def _split3(x):
    hi = _bf(x)
    r1 = x - hi.astype(F32)
    mid = _bf(r1)
    lo = _bf(r1 - mid.astype(F32))
    return hi, mid, lo


def _dot_c_x(c, x):
    hi, mid, lo = _split3(x)
    return _dot(c, hi) + _dot(c, mid) + _dot(c, lo)


def _dot_x_c(x, c):
    hi, mid, lo = _split3(x)
    return _dot(hi, c) + _dot(mid, c) + _dot(lo, c)


def _sigmoid(x):
    return 1.0 / (1.0 + jnp.exp(-x))


def _silu(x):
    return x * _sigmoid(x)


def _softplus(x):
    return jnp.maximum(x, 0.0) + jnp.log1p(jnp.exp(-jnp.abs(x)))


def _gelu_tanh(x):
    c = np.float32(np.sqrt(2.0 / np.pi))
    return 0.5 * x * (1.0 + jnp.tanh(c * (x + 0.044715 * (x * x * x))))


def _rms(x, gain):
    return x * lax.rsqrt(jnp.mean(x * x, axis=-1, keepdims=True) + EPS) * gain


def _lower_bound_row(hlb, layer):
    m = jnp.max(hlb, axis=0, keepdims=True)
    e = jnp.exp(hlb - m)
    sm = e / jnp.sum(e, axis=0, keepdims=True)
    lb = jnp.zeros((1, HG_WIDTH), F32)
    for j in range(1, layer + 1):
        lb = lb + sm[j:j + 1, :]
    return lb


def _rg_gates(xa, wg_ref, bg_ref, lam_ref):
    gates = _dot(_bf(xa), wg_ref[...]) + bg_ref[...]
    r = _sigmoid(gates[:, :RG_WIDTH])
    i = _sigmoid(gates[:, RG_WIDTH:])
    log_a = -RG_C * r * _softplus(-lam_ref[...])
    a = jnp.exp(log_a)
    mult = jnp.sqrt(-jnp.tanh(log_a) * (a * a + 1.0))
    return a, mult, i


def _head_rms_gate(o, cg, headmean_ref, hnorm_ref):
    ms = _dot(_bf(o * o), headmean_ref[...])
    return o * lax.rsqrt(ms + EPS) * hnorm_ref[...] * _silu(cg)


def _lane_halves(x):
    lane = lax.broadcasted_iota(jnp.int32, x.shape, 1)
    lo = jnp.where(lane < HG_HEAD_DIM, x, 0.0)
    hi = jnp.where(lane >= HG_HEAD_DIM, x, 0.0)
    return jnp.concatenate([lo, hi], axis=0)


def _proj_kernel(x_ref, g_ref, w_ref, o_ref):
    u = _rms(x_ref[...], g_ref[...])
    o_ref[...] = _dot(_bf(u), w_ref[...])


def _const_spec(arr):
    nd = arr.ndim
    return pl.BlockSpec(arr.shape, lambda *_: (0,) * nd, pipeline_mode=pl.Buffered(1))


def _in_proj(x2d, gain, w_bf, tm):
    m = x2d.shape[0]
    return pl.pallas_call(
        _proj_kernel,
        grid=(m // tm,),
        in_specs=[pl.BlockSpec((tm, D_MODEL), lambda i: (i, 0)),
                  _const_spec(gain), _const_spec(w_bf)],
        out_specs=pl.BlockSpec((tm, NPROJ), lambda i: (i, 0)),
        out_shape=jax.ShapeDtypeStruct((m, NPROJ), F32),
        compiler_params=pltpu.CompilerParams(
            dimension_semantics=("parallel",), vmem_limit_bytes=VMEM_LIMIT),
        name="in_proj",
    )(x2d, gain, w_bf)


def _cast_kernel(w_ref, o_ref):
    o_ref[...] = _bf(w_ref[...])


def _cast_bf16(w, rb):
    rows, cols = w.shape
    return pl.pallas_call(
        _cast_kernel,
        grid=(rows // rb,),
        in_specs=[pl.BlockSpec((rb, cols), lambda i: (i, 0))],
        out_specs=pl.BlockSpec((rb, cols), lambda i: (i, 0)),
        out_shape=jax.ShapeDtypeStruct((rows, cols), BF16),
        compiler_params=pltpu.CompilerParams(dimension_semantics=("parallel",)),
        name="cast_bf16",
    )(w)


DT_COL = 2 * RG_WIDTH + SSD_WIDTH + SSD_CONV_DIM
D_IN_PROJ = DT_COL + SSD_HEADS + 4 * HG_WIDTH


def _win_kernel(w_ref, o_ref):
    rows = w_ref.shape[0]
    o_ref[:, 0:OFF_CQ] = _bf(w_ref[:, 0:DT_COL])
    o_ref[:, OFF_CQ:OFF_DT] = _bf(w_ref[:, DT_COL + SSD_HEADS:D_IN_PROJ])
    dt = jnp.concatenate([w_ref[:, DT_COL:DT_COL + SSD_HEADS],
                          jnp.zeros((rows, LANES - SSD_HEADS), F32)], axis=1)
    o_ref[:, OFF_DT:NPROJ] = _bf(dt)


def _prep_w_in(w, rb):
    return pl.pallas_call(
        _win_kernel,
        grid=(D_MODEL // rb,),
        in_specs=[pl.BlockSpec((rb, D_IN_PROJ), lambda i: (i, 0))],
        out_specs=pl.BlockSpec((rb, NPROJ), lambda i: (i, 0)),
        out_shape=jax.ShapeDtypeStruct((D_MODEL, NPROJ), BF16),
        compiler_params=pltpu.CompilerParams(dimension_semantics=("parallel",)),
        name="prep_w_in",
    )(w)


def _prompt_mixer_kernel(layer, pos0,
                         proj_ref, caw, cab, wg, bg, lam, cbw, cbb, dtb, alog, dexp, snorm,
                         hlb, hnorm, ltri, utri, expand, emat, lmask, bdmask, headmean,
                         ymix_ref, rgc_o, rgh_o, sc_o, ssd_o, hg_o,
                         xea, xeb, hcar, s_t, st_hg):
    c = pl.program_id(1)
    last = pl.num_programs(1) - 1

    @pl.when(c == 0)
    def _init():
        xea[0:SUBLANES, :] = jnp.zeros((SUBLANES, RG_WIDTH), F32)
        xeb[0:SUBLANES, :] = jnp.zeros((SUBLANES, SSD_CONV_DIM), F32)
        hcar[...] = jnp.zeros_like(hcar)
        s_t[...] = jnp.zeros_like(s_t)
        st_hg[...] = jnp.zeros_like(st_hg)

    row = lax.broadcasted_iota(jnp.int32, (TC, 1), 0)

    xea[SUBLANES:SUBLANES + TC, :] = proj_ref[0, :, OFF_AX:OFF_AX + RG_WIDTH]
    xa = cab[...]
    for k in range(CONV_W):
        xa = xa + caw[k:k + 1, :] * xea[SUBLANES - 3 + k:SUBLANES - 3 + k + TC, :]
    a, mult, gi = _rg_gates(xa, wg, bg, lam)
    mult = jnp.where(row + (c * TC + pos0) == 0, 1.0, mult)
    b = mult * (gi * xa)
    k = 1
    while k < TC:
        keep = row >= k
        a_sh = jnp.where(keep, pltpu.roll(a, k, 0), 1.0)
        b_sh = jnp.where(keep, pltpu.roll(b, k, 0), 0.0)
        b = b + a * b_sh
        a = a * a_sh
        k *= 2
    h = a * hcar[0:1, :] + b
    hcar[0:1, :] = h[TC - 1:TC, :]
    ymix_ref[0, :, 0:RG_WIDTH] = h * _gelu_tanh(proj_ref[0, :, OFF_AG:OFF_AG + RG_WIDTH])

    xeb[SUBLANES:SUBLANES + TC, :] = proj_ref[0, :, OFF_XBC:OFF_XBC + SSD_CONV_DIM]
    xbc = cbb[...]
    for k in range(CONV_W):
        xbc = xbc + cbw[k:k + 1, :] * xeb[SUBLANES - 3 + k:SUBLANES - 3 + k + TC, :]
    xbc = _silu(xbc)
    xs = xbc[:, :SSD_WIDTH]
    bm = xbc[:, SSD_WIDTH:SSD_WIDTH + SSD_STATE]
    cm = xbc[:, SSD_WIDTH + SSD_STATE:]
    dtp = _softplus(proj_ref[0, :, OFF_DT:OFF_DT + LANES] + dtb[...])
    v_da = dtp * (-jnp.exp(alog[...]))
    cum = _dot_c_x(ltri[...], v_da)
    cum_t = _dot_x_c(v_da.T, utri[...])
    dt_e = _dot_x_c(dtp, expand[...])
    cum_e = _dot_x_c(cum, expand[...])
    cum_last = cum_e[TC - 1:TC, :]
    cb = _dot_nt(_bf(cm), _bf(bm))
    col = lax.broadcasted_iota(jnp.int32, (TC, TC), 1)
    causal = col <= lax.broadcasted_iota(jnp.int32, (TC, TC), 0)
    dtx = xs * dt_e
    y_parts = []
    for g in range(SSD_HEADS // 2):
        gs = []
        for hh in range(2):
            hd = 2 * g + hh
            seg = cum[:, hd:hd + 1] - cum_t[hd:hd + 1, :]
            gs.append(_bf(cb * jnp.exp(jnp.where(causal, seg, -jnp.inf))))
        gcat = jnp.concatenate(gs, axis=1)
        x2 = _bf(_lane_halves(dtx[:, g * LANES:(g + 1) * LANES]))
        y_parts.append(_dot(gcat, x2))
    y = jnp.concatenate(y_parts, axis=1)
    y = y + _dot(_bf(cm), _bf(s_t[...])) * jnp.exp(cum_e)
    yb = y + dexp[...] * xs
    w_e = jnp.exp(cum_last - cum_e) * dt_e
    s_t[...] = jnp.exp(cum_last) * s_t[...] + _dot(_bf(bm.T), _bf(w_e * xs))
    ymix_ref[0, :, RG_WIDTH:RG_WIDTH + SSD_WIDTH] = _rms(
        yb * _silu(proj_ref[0, :, OFF_BZ:OFF_BZ + SSD_WIDTH]), snorm[...])

    q = _silu(proj_ref[0, :, OFF_CQ:OFF_CQ + HG_WIDTH])
    lb = _lower_bound_row(hlb[...], layer)
    gt = lb + (1.0 - lb) * _sigmoid(proj_ref[0, :, OFF_CF:OFF_CF + HG_WIDTH])
    kk = 1.0 - gt
    lf_parts = _split3(jnp.log(gt))
    vv = proj_ref[0, :, OFF_CI:OFF_CI + HG_WIDTH]

    def lf_dot(cmat):
        return _dot(cmat, lf_parts[0]) + _dot(cmat, lf_parts[1]) + _dot(cmat, lf_parts[2])

    att = [None] * HG_HEADS
    for lev in range(N_LEVELS + 1):
        if lev == 0:
            qt, kt = q, kk
        else:
            r0 = 2 * (lev - 1) * TC
            qt = q * jnp.exp(lf_dot(emat[r0:r0 + TC, :]))
            kt = kk * jnp.exp(lf_dot(emat[r0 + TC:r0 + 2 * TC, :]))
        msk = lmask[lev]
        for g in range(HG_HEADS // 2):
            lhs = _bf(_lane_halves(qt[:, g * LANES:(g + 1) * LANES]))
            res = _dot_nt(lhs, _bf(kt[:, g * LANES:(g + 1) * LANES]))
            for hh in range(2):
                term = res[hh * TC:(hh + 1) * TC, :] * msk
                hd = 2 * g + hh
                att[hd] = term if att[hd] is None else att[hd] + term
    o_parts = []
    for g in range(HG_HEADS // 2):
        acat = jnp.concatenate([_bf(att[2 * g]), _bf(att[2 * g + 1])], axis=1)
        v2 = _bf(_lane_halves(vv[:, g * LANES:(g + 1) * LANES]))
        o_parts.append(_dot(acat, v2))
    bcum = lf_dot(ltri[...])
    blast = bcum[TC - 1:TC, :]
    o = jnp.concatenate(o_parts, axis=1) + _dot_nt(_bf(q * jnp.exp(bcum)), _bf(st_hg[...]))
    khat = kk * jnp.exp(blast - bcum)
    st_hg[...] = jnp.exp(blast) * st_hg[...] + bdmask[...] * _dot(_bf(vv.T), _bf(khat))
    ymix_ref[0, :, RG_WIDTH + SSD_WIDTH:] = _head_rms_gate(
        o, proj_ref[0, :, OFF_CG:OFF_CG + HG_WIDTH], headmean, hnorm)

    xea[0:SUBLANES, :] = xea[TC:TC + SUBLANES, :]
    xeb[0:SUBLANES, :] = xeb[TC:TC + SUBLANES, :]

    @pl.when(c == last)
    def _emit():
        rgc_o[0] = xea[SUBLANES - 3:SUBLANES, :]
        sc_o[0] = xeb[SUBLANES - 3:SUBLANES, :]
        rgh_o[0] = hcar[0:1, :]
        ssd_o[0] = s_t[...].T
        s_full = st_hg[...].T
        for hd in range(HG_HEADS):
            lo = hd * HG_HEAD_DIM
            hg_o[0, hd] = s_full[lo:lo + HG_HEAD_DIM, lo:lo + HG_HEAD_DIM]


def _prompt_mixer(layer, proj3, mp, consts):
    bsz, seq, _ = proj3.shape
    nc = seq // TC
    params = [mp["caw"], mp["cab"], mp["wg"], mp["bg"], mp["lam"], mp["cbw"], mp["cbb"],
              mp["dtb"], mp["alog"], mp["dexp"], mp["snorm"], mp["hlb"], mp["hnorm"],
              consts["ltri"], consts["utri"], consts["expand"], consts["emat"],
              consts["lmask"], consts["bdmask"], consts["headmean"]]
    out_shape = (
        jax.ShapeDtypeStruct((bsz, seq, D_MIX), F32),
        jax.ShapeDtypeStruct((bsz, CONV_W - 1, RG_WIDTH), F32),
        jax.ShapeDtypeStruct((bsz, 1, RG_WIDTH), F32),
        jax.ShapeDtypeStruct((bsz, CONV_W - 1, SSD_CONV_DIM), F32),
        jax.ShapeDtypeStruct((bsz, SSD_WIDTH, SSD_STATE), F32),
        jax.ShapeDtypeStruct((bsz, HG_HEADS, HG_HEAD_DIM, HG_HEAD_DIM), F32),
    )
    out_specs = (
        pl.BlockSpec((1, TC, D_MIX), lambda b, c: (b, c, 0)),
        pl.BlockSpec((1, CONV_W - 1, RG_WIDTH), lambda b, c: (b, 0, 0)),
        pl.BlockSpec((1, 1, RG_WIDTH), lambda b, c: (b, 0, 0)),
        pl.BlockSpec((1, CONV_W - 1, SSD_CONV_DIM), lambda b, c: (b, 0, 0)),
        pl.BlockSpec((1, SSD_WIDTH, SSD_STATE), lambda b, c: (b, 0, 0)),
        pl.BlockSpec((1, HG_HEADS, HG_HEAD_DIM, HG_HEAD_DIM), lambda b, c: (b, 0, 0, 0)),
    )
    scratch = [
        pltpu.VMEM((TC + SUBLANES, RG_WIDTH), F32),
        pltpu.VMEM((TC + SUBLANES, SSD_CONV_DIM), F32),
        pltpu.VMEM((SUBLANES, RG_WIDTH), F32),
        pltpu.VMEM((SSD_STATE, SSD_WIDTH), F32),
        pltpu.VMEM((HG_WIDTH, HG_WIDTH), F32),
    ]
    return pl.pallas_call(
        functools.partial(_prompt_mixer_kernel, layer, 0),
        grid=(bsz, nc),
        in_specs=[pl.BlockSpec((1, TC, NPROJ), lambda b, c: (b, c, 0))]
        + [_const_spec(p) for p in params],
        out_specs=out_specs,
        out_shape=out_shape,
        scratch_shapes=scratch,
        compiler_params=pltpu.CompilerParams(
            dimension_semantics=("parallel", "arbitrary"), vmem_limit_bytes=VMEM_LIMIT),
        name="prompt_mixer",
    )(proj3, *params)


def _pad_rows_t(x):
    pad = jnp.zeros((LANES - x.shape[0], x.shape[1]), F32)
    return jnp.concatenate([x, pad], axis=0).T


def _sample_mixer_kernel(layer, pos0, n_acc, proj_ref, rgc_ref, rgh_ref, sc_ref, ssd_ref, hg_ref, *rest):
    (caw, cab, wg, bg, lam, cbw, cbb, dtb, alog, dexp, snorm,
     hlb, hnorm, expand, headsum, headmean,
     ymix_ref, rgc_o, rgh_o, sc_o, ssd_o, hg_o) = rest[n_acc:]
    bb = DEC_BB
    ax = proj_ref[:, OFF_AX:OFF_AX + RG_WIDTH]
    xa = cab[...] + caw[CONV_W - 1:CONV_W, :] * ax
    for k in range(CONV_W - 1):
        xa = xa + caw[k:k + 1, :] * rgc_ref[:, k * RG_WIDTH:(k + 1) * RG_WIDTH]
    rgc_o[:, 0:2 * RG_WIDTH] = rgc_ref[:, RG_WIDTH:3 * RG_WIDTH]
    rgc_o[:, 2 * RG_WIDTH:] = ax
    a, mult, gi = _rg_gates(xa, wg, bg, lam)
    if pos0 == 0:
        mult = jnp.ones_like(mult)
    h = a * rgh_ref[...] + mult * (gi * xa)
    rgh_o[...] = h
    ymix_ref[:, 0:RG_WIDTH] = h * _gelu_tanh(proj_ref[:, OFF_AG:OFF_AG + RG_WIDTH])

    bx = proj_ref[:, OFF_XBC:OFF_XBC + SSD_CONV_DIM]
    xbc = cbb[...] + cbw[CONV_W - 1:CONV_W, :] * bx
    for k in range(CONV_W - 1):
        xbc = xbc + cbw[k:k + 1, :] * sc_ref[:, k * SSD_CONV_DIM:(k + 1) * SSD_CONV_DIM]
    sc_o[:, 0:2 * SSD_CONV_DIM] = sc_ref[:, SSD_CONV_DIM:3 * SSD_CONV_DIM]
    sc_o[:, 2 * SSD_CONV_DIM:] = bx
    xbc = _silu(xbc)
    xs = xbc[:, :SSD_WIDTH]
    bm = xbc[:, SSD_WIDTH:SSD_WIDTH + SSD_STATE]
    cm = xbc[:, SSD_WIDTH + SSD_STATE:]
    dtp = _softplus(proj_ref[:, OFF_DT:OFF_DT + LANES] + dtb[...])
    v_da = dtp * (-jnp.exp(alog[...]))
    dt_e = _dot_x_c(dtp, expand[...])
    e_e = jnp.exp(_dot_x_c(v_da, expand[...]))
    dtx = dt_e * xs
    cbs = _dot(_bf(cm * bm), jnp.ones((SSD_STATE, LANES), BF16))[:, 0:1]
    dtx_t = _split3(_pad_rows_t(dtx))
    e_t = _split3(_pad_rows_t(e_e))
    c_pad = _bf(jnp.concatenate([cm, jnp.zeros((LANES - bb, SSD_STATE), F32)], axis=0))
    rowi = lax.broadcasted_iota(jnp.int32, (LANES, LANES), 0)
    lane_w = lax.broadcasted_iota(jnp.int32, (SSD_WIDTH, LANES), 1)
    y_t = jnp.zeros((SSD_WIDTH, LANES), F32)
    for j in range(bb):
        pick = jnp.where(rowi == j, 1.0, 0.0).astype(BF16)
        d_col = _dot(dtx_t[0], pick) + _dot(dtx_t[1], pick) + _dot(dtx_t[2], pick)
        e_col = _dot(e_t[0], pick) + _dot(e_t[1], pick) + _dot(e_t[2], pick)
        s_old = ssd_ref[j].reshape(SSD_WIDTH, SSD_STATE)
        ssd_o[j] = (e_col * s_old + d_col * bm[j:j + 1, :]).reshape(
            SSD_HEADS, SSD_HEAD_DIM, SSD_STATE)
        y_t = y_t + jnp.where(lane_w == j, _dot_nt(_bf(s_old), c_pad), 0.0)
    y = cbs * dtx + y_t.T[0:bb, :] * e_e
    yb = y + dexp[...] * xs
    ymix_ref[:, RG_WIDTH:RG_WIDTH + SSD_WIDTH] = _rms(
        yb * _silu(proj_ref[:, OFF_BZ:OFF_BZ + SSD_WIDTH]), snorm[...])

    q = _silu(proj_ref[:, OFF_CQ:OFF_CQ + HG_WIDTH])
    lb = _lower_bound_row(hlb[...], layer)
    gt = lb + (1.0 - lb) * _sigmoid(proj_ref[:, OFF_CF:OFF_CF + HG_WIDTH])
    kk = 1.0 - gt
    fdec = jnp.exp(jnp.log(gt))
    vv = proj_ref[:, OFF_CI:OFF_CI + HG_WIDTH]
    att = _dot(_bf(q * kk), headsum[...])
    qhat = _bf(q * fdec)
    f_t = _split3(_pad_rows_t(fdec))
    k_t = _split3(_pad_rows_t(kk))
    rowb = lax.broadcasted_iota(jnp.int32, (bb, HG_WIDTH), 0)
    o_acc = jnp.zeros((bb, HG_WIDTH), F32)
    for j in range(bb):
        pick = jnp.where(rowi == j, 1.0, 0.0).astype(BF16)
        f_col = _dot(f_t[0], pick) + _dot(f_t[1], pick) + _dot(f_t[2], pick)
        k_col = _dot(k_t[0], pick) + _dot(k_t[1], pick) + _dot(k_t[2], pick)
        r_parts = []
        for hd in range(HG_HEADS):
            lo = hd * HG_HEAD_DIM
            s_old = hg_ref[j, hd]
            v_row = vv[j:j + 1, lo:lo + HG_HEAD_DIM]
            hg_o[j, hd] = (f_col[lo:lo + HG_HEAD_DIM, 0:HG_HEAD_DIM] * s_old
                           + k_col[lo:lo + HG_HEAD_DIM, 0:HG_HEAD_DIM] * v_row)
            r_parts.append(_dot(qhat[:, lo:lo + HG_HEAD_DIM], _bf(s_old)))
        o_acc = o_acc + jnp.where(rowb == j, jnp.concatenate(r_parts, axis=1), 0.0)
    o = att * vv + o_acc
    ymix_ref[:, RG_WIDTH + SSD_WIDTH:] = _head_rms_gate(
        o, proj_ref[:, OFF_CG:OFF_CG + HG_WIDTH], headmean, hnorm)


def _sample_mixer(layer, proj, rgc, rgh, sc, ssd_all, hg_all, acc, mp, consts):
    nb = proj.shape[0]
    bb = DEC_BB
    params = [mp["caw"], mp["cab"], mp["wg"], mp["bg"], mp["lam"], mp["cbw"], mp["cbb"],
              mp["dtb"], mp["alog"], mp["dexp"], mp["snorm"], mp["hlb"], mp["hnorm"],
              consts["expand"], consts["headsum"], consts["headmean"]]
    row2 = lambda w: pl.BlockSpec((bb, w), lambda i: (i, 0))
    small_specs = [row2(3 * RG_WIDTH), row2(RG_WIDTH), row2(3 * SSD_CONV_DIM)]
    big_specs = [
        pl.BlockSpec((None, bb, SSD_HEADS, SSD_HEAD_DIM, SSD_STATE), lambda i: (layer, i, 0, 0, 0)),
        pl.BlockSpec((None, bb, HG_HEADS, HG_HEAD_DIM, HG_HEAD_DIM), lambda i: (layer, i, 0, 0, 0)),
    ]
    out_shape = (
        jax.ShapeDtypeStruct((nb, D_MIX), F32),
        jax.ShapeDtypeStruct((nb, 3 * RG_WIDTH), F32),
        jax.ShapeDtypeStruct((nb, RG_WIDTH), F32),
        jax.ShapeDtypeStruct((nb, 3 * SSD_CONV_DIM), F32),
        jax.ShapeDtypeStruct(ssd_all.shape, F32),
        jax.ShapeDtypeStruct(hg_all.shape, F32),
    )
    n_acc = len(acc)
    n_lead = 6
    return pl.pallas_call(
        functools.partial(_sample_mixer_kernel, layer, PAST_LEN, n_acc),
        grid=(nb // bb,),
        in_specs=[row2(NPROJ)] + small_specs + big_specs
        + [pl.BlockSpec(memory_space=pl.ANY)] * n_acc + [_const_spec(p) for p in params],
        out_specs=tuple([row2(D_MIX)] + small_specs + big_specs),
        out_shape=out_shape,
        input_output_aliases={n_lead + k: 4 + k for k in range(n_acc)},
        compiler_params=pltpu.CompilerParams(
            dimension_semantics=("parallel",), vmem_limit_bytes=VMEM_LIMIT),
        name="sample_mixer",
    )(proj, rgc, rgh, sc, ssd_all, hg_all, *acc, *params)


def _post_kernel(final, h_ref, y_ref, p_ref, wo, nffn, wup, wdown, nple, wgate, wproj, nfin, o_ref):
    h = h_ref[...] + _dot(_bf(y_ref[...]), wo[...])
    z = jnp.square(jnp.maximum(_dot(_bf(_rms(h, nffn[...])), wup[...]), 0.0))
    h = h + _dot(_bf(z), wdown[...])
    gate = _sigmoid(_dot(_bf(_rms(h, nple[...])), wgate[...]))
    h = h + gate * _dot(_bf(p_ref[...]), wproj[...])
    if final:
        h = _rms(h, nfin[...])
    o_ref[...] = h


def _post(final, h2d, ymix2d, p2d, lp, nfin, tm):
    m = h2d.shape[0]
    params = [lp["wo"], lp["nffn"], lp["wup"], lp["wdown"], lp["nple"], lp["wgate"], lp["wproj"], nfin]
    return pl.pallas_call(
        functools.partial(_post_kernel, final),
        grid=(m // tm,),
        in_specs=[pl.BlockSpec((tm, D_MODEL), lambda i: (i, 0)),
                  pl.BlockSpec((tm, D_MIX), lambda i: (i, 0)),
                  pl.BlockSpec((tm, PLE_DIM), lambda i: (i, 0))]
        + [_const_spec(p) for p in params],
        out_specs=pl.BlockSpec((tm, D_MODEL), lambda i: (i, 0)),
        out_shape=jax.ShapeDtypeStruct((m, D_MODEL), F32),
        compiler_params=pltpu.CompilerParams(
            dimension_semantics=("parallel",), vmem_limit_bytes=VMEM_LIMIT),
        name="post_mixer",
    )(h2d, ymix2d, p2d, *params)


def _constants():
    t = np.arange(TC)
    ltri = (t[None, :] <= t[:, None]).astype(np.float32)
    utri = ltri.T.copy()
    expand = np.zeros((LANES, SSD_WIDTH), np.float32)
    for hd in range(SSD_HEADS):
        expand[hd, hd * SSD_HEAD_DIM:(hd + 1) * SSD_HEAD_DIM] = 1.0
    emat = np.zeros((2 * N_LEVELS * TC, TC), np.float32)
    lmask = np.zeros((N_LEVELS + 1, TC, TC), np.float32)
    lmask[0] = np.eye(TC, dtype=np.float32)
    for lev in range(1, N_LEVELS + 1):
        blk = 1 << lev
        mid = (t // blk) * blk + blk // 2
        upper = t >= mid
        r0 = 2 * (lev - 1) * TC
        emat[r0:r0 + TC] = (upper[:, None] & (t[None, :] >= mid[:, None]) & (t[None, :] <= t[:, None]))
        emat[r0 + TC:r0 + 2 * TC] = ((~upper)[:, None] & (t[None, :] > t[:, None]) & (t[None, :] < mid[:, None]))
        lmask[lev] = (upper[:, None] & (~upper)[None, :] & ((t // blk)[:, None] == (t // blk)[None, :]))
    hidx = np.arange(HG_WIDTH) // HG_HEAD_DIM
    bd = (hidx[:, None] == hidx[None, :]).astype(np.float32)
    return {
        "ltri": jnp.asarray(ltri, BF16), "utri": jnp.asarray(utri, BF16),
        "expand": jnp.asarray(expand, BF16), "emat": jnp.asarray(emat, BF16),
        "lmask": jnp.asarray(lmask, F32), "bdmask": jnp.asarray(bd, F32),
        "headsum": jnp.asarray(bd, BF16), "headmean": jnp.asarray(bd / HG_HEAD_DIM, BF16),
    }


def _block_diag(w):
    hh, d, _ = w.shape
    eye = jnp.eye(hh, dtype=w.dtype)
    return (eye[:, None, :, None] * w[:, :, None, :]).reshape(hh * d, hh * d)


def _pad_lanes(v):
    return jnp.pad(v, (0, LANES - v.shape[0]))[None, :]


def kernel(x_prompt, x_sample, state_rg_conv, state_rg_h, state_ssd_conv, state_ssd, state_hgrn,
           p_prompt, p_sample, norm_mix, w_in, conv_a_w, conv_a_b, rg_wa, rg_ba, rg_wx, rg_bx,
           rg_lambda, conv_b_w, conv_b_b, ssd_dt_bias, ssd_a_log, ssd_d, ssd_norm,
           hg_lower_bounds, hg_norm, w_out, norm_ffn, w_up, w_down, norm_ple, w_ple_gate,
           w_ple_proj, norm_final):
    bsz, seq, _ = x_prompt.shape
    nb = x_sample.shape[0]
    consts = _constants()
    nfin = norm_final[None, :]

    hp = x_prompt.reshape(bsz * seq, D_MODEL)
    hs = x_sample.reshape(nb, D_MODEL)
    st_p = [[] for _ in range(5)]
    st_s = [[] for _ in range(3)]
    acc = ()
    for i in range(DEPTH):
        w_re = _prep_w_in(w_in[i], 128)
        gmix = norm_mix[i][None, :]
        mp = {
            "caw": conv_a_w[i], "cab": conv_a_b[i][None, :],
            "wg": jnp.concatenate([_block_diag(rg_wa[i]), _block_diag(rg_wx[i])], axis=1).astype(BF16),
            "bg": jnp.concatenate([rg_ba[i], rg_bx[i]])[None, :],
            "lam": rg_lambda[i][None, :],
            "cbw": conv_b_w[i], "cbb": conv_b_b[i][None, :],
            "dtb": _pad_lanes(ssd_dt_bias[i]), "alog": _pad_lanes(ssd_a_log[i]),
            "dexp": jnp.repeat(ssd_d[i], SSD_HEAD_DIM)[None, :],
            "snorm": ssd_norm[i][None, :],
            "hlb": hg_lower_bounds, "hnorm": jnp.tile(hg_norm[i], HG_HEADS)[None, :],
        }
        lp = {
            "wo": _cast_bf16(w_out[i], 512), "nffn": norm_ffn[i][None, :],
            "wup": _cast_bf16(w_up[i], 256), "wdown": _cast_bf16(w_down[i], 1024),
            "nple": norm_ple[i][None, :], "wgate": _cast_bf16(w_ple_gate[i], 512),
            "wproj": _cast_bf16(w_ple_proj[i], 256),
        }
        final = i == DEPTH - 1

        proj_p = _in_proj(hp, gmix, w_re, 256).reshape(bsz, seq, NPROJ)
        ymix_p, rgc, rgh, sc, ssd, hg = _prompt_mixer(i, proj_p, mp, consts)
        hp = _post(final, hp, ymix_p.reshape(bsz * seq, D_MIX),
                   p_prompt[i].reshape(bsz * seq, PLE_DIM), lp, nfin, 256)
        for lst, s in zip(st_p, (rgc, rgh.reshape(bsz, RG_WIDTH), sc,
                                 ssd.reshape(bsz, SSD_HEADS, SSD_HEAD_DIM, SSD_STATE), hg)):
            lst.append(s)

        proj_s = _in_proj(hs, gmix, w_re, nb)
        ymix_s, rgc, rgh, sc, ssd_acc, hg_acc = _sample_mixer(
            i, proj_s,
            state_rg_conv[i].reshape(nb, 3 * RG_WIDTH), state_rg_h[i],
            state_ssd_conv[i].reshape(nb, 3 * SSD_CONV_DIM),
            state_ssd, state_hgrn, acc, mp, consts)
        acc = (ssd_acc, hg_acc)
        hs = _post(final, hs, ymix_s, p_sample[i].reshape(nb, PLE_DIM), lp, nfin, nb)
        for lst, s in zip(st_s, (rgc.reshape(nb, CONV_W - 1, RG_WIDTH), rgh,
                                 sc.reshape(nb, CONV_W - 1, SSD_CONV_DIM))):
            lst.append(s)

    y_prompt = hp.reshape(bsz, seq, D_MODEL)
    y_sample = hs.reshape(nb, 1, D_MODEL)
    outs_p = [jnp.stack(lst) for lst in st_p]
    outs_s = [jnp.stack(lst) for lst in st_s] + list(acc)
    return (y_prompt, y_sample, *outs_p, *outs_s)
```

```python
import functools

import numpy as np
import jax
import jax.numpy as jnp
from jax import lax
from jax.experimental import pallas as pl
from jax.experimental.pallas import tpu as pltpu

F32 = jnp.float32
BF16 = jnp.bfloat16

D_MODEL = 1024
DEPTH = 2
PAST_LEN = 16384
PLE_DIM = 256
D_FF = 4 * D_MODEL
CONV_W = 4
EPS = 1e-6
RG_WIDTH = 512
RG_HEADS = 8
RG_HEAD_DIM = 64
RG_C = 8.0
SSD_WIDTH = 512
SSD_HEAD_DIM = 64
SSD_HEADS = 8
SSD_STATE = 128
SSD_CONV_DIM = SSD_WIDTH + 2 * SSD_STATE
HG_WIDTH = 512
HG_HEADS = 8
HG_HEAD_DIM = 64
D_MIX = RG_WIDTH + SSD_WIDTH + HG_WIDTH

LANES = 128
SUBLANES = 8
NGRP = RG_WIDTH // LANES

OFF_AX = 0
OFF_AG = 512
OFF_BZ = 1024
OFF_XBC = 1536
OFF_CQ = 2304
OFF_CF = 2816
OFF_CI = 3328
OFF_CG = 3840
OFF_DT = 4352
NPROJ = OFF_DT + LANES
DT_COL = 2 * RG_WIDTH + SSD_WIDTH + SSD_CONV_DIM
D_IN_PROJ = DT_COL + SSD_HEADS + 4 * HG_WIDTH

TC = 128
HALF = TC // 2
N_LEVELS = 7
DEC_BB = 8
VMEM_LIMIT = 52 * 1024 * 1024


def _bf(x):
    return x.astype(BF16)


def _dot(a, b):
    return jnp.dot(a, b, preferred_element_type=F32)


def _dot_nt(a, b):
    return lax.dot_general(a, b, (((1,), (1,)), ((), ())), preferred_element_type=F32)


def _split3(x):
    hi = _bf(x)
    r1 = x - hi.astype(F32)
    mid = _bf(r1)
    lo = _bf(r1 - mid.astype(F32))
    return hi, mid, lo


def _dot_x_c(x, c):
    hi, mid, lo = _split3(x)
    return _dot(hi, c) + _dot(mid, c) + _dot(lo, c)


def _dot_x_c3(x, c3):
    return _dot(jnp.concatenate(_split3(x), axis=1), c3)


def _dot_c3_x(c3, x):
    return _dot(c3, jnp.concatenate(_split3(x), axis=0))


def _sigmoid(x):
    return 0.5 * jnp.tanh(0.5 * x) + 0.5


def _silu(x):
    return x * _sigmoid(x)


def _softplus(x):
    return jnp.maximum(x, 0.0) + jnp.log1p(jnp.exp(-jnp.abs(x)))


def _gelu_tanh(x):
    c = np.float32(np.sqrt(2.0 / np.pi))
    return 0.5 * x * (1.0 + jnp.tanh(c * (x + 0.044715 * (x * x * x))))


def _rms(x, gain):
    return x * lax.rsqrt(jnp.mean(x * x, axis=-1, keepdims=True) + EPS) * gain


def _lower_bound_row(hlb, layer):
    m = jnp.max(hlb, axis=0, keepdims=True)
    e = jnp.exp(hlb - m)
    sm = e / jnp.sum(e, axis=0, keepdims=True)
    lb = jnp.zeros((1, HG_WIDTH), F32)
    for j in range(1, layer + 1):
        lb = lb + sm[j:j + 1, :]
    return lb


def _rg_coeffs(gates, lam_row):
    w = gates.shape[1] // 2
    r = _sigmoid(gates[:, :w])
    i = _sigmoid(gates[:, w:])
    log_a = -RG_C * r * _softplus(-lam_row)
    a = jnp.exp(log_a)
    mult = jnp.sqrt(-jnp.tanh(log_a) * (a * a + 1.0))
    return a, mult, i


def _lane_halves(x):
    lane = lax.broadcasted_iota(jnp.int32, x.shape, 1)
    lo = jnp.where(lane < HG_HEAD_DIM, x, 0.0)
    hi = jnp.where(lane >= HG_HEAD_DIM, x, 0.0)
    return jnp.concatenate([lo, hi], axis=0)


def _const_spec(arr):
    nd = arr.ndim
    return pl.BlockSpec(arr.shape, lambda *_: (0,) * nd, pipeline_mode=pl.Buffered(1))


def _cast_kernel(w_ref, o_ref):
    o_ref[...] = _bf(w_ref[...])


def _cast_bf16(w_all, layer, rb):
    _, rows, cols = w_all.shape
    return pl.pallas_call(
        _cast_kernel,
        grid=(rows // rb,),
        in_specs=[pl.BlockSpec((None, rb, cols), lambda i: (layer, i, 0))],
        out_specs=pl.BlockSpec((rb, cols), lambda i: (i, 0)),
        out_shape=jax.ShapeDtypeStruct((rows, cols), BF16),
        compiler_params=pltpu.CompilerParams(dimension_semantics=("parallel",)),
        name="cast_bf16",
    )(w_all)


def _win_kernel(w_ref, o_ref):
    rows = w_ref.shape[0]
    o_ref[:, 0:OFF_CQ] = _bf(w_ref[:, 0:DT_COL])
    o_ref[:, OFF_CQ:OFF_DT] = _bf(w_ref[:, DT_COL + SSD_HEADS:D_IN_PROJ])
    dt = jnp.concatenate([w_ref[:, DT_COL:DT_COL + SSD_HEADS],
                          jnp.zeros((rows, LANES - SSD_HEADS), F32)], axis=1)
    o_ref[:, OFF_DT:NPROJ] = _bf(dt)


def _prep_w_in(w_all, layer, rb):
    return pl.pallas_call(
        _win_kernel,
        grid=(D_MODEL // rb,),
        in_specs=[pl.BlockSpec((None, rb, D_IN_PROJ), lambda i: (layer, i, 0))],
        out_specs=pl.BlockSpec((rb, NPROJ), lambda i: (i, 0)),
        out_shape=jax.ShapeDtypeStruct((D_MODEL, NPROJ), BF16),
        compiler_params=pltpu.CompilerParams(dimension_semantics=("parallel",)),
        name="prep_w_in",
    )(w_all)


def _proj_kernel(x_ref, g_ref, w_ref, o_ref):
    u = _rms(x_ref[...], g_ref[...])
    o_ref[...] = _dot(_bf(u), w_ref[...])


def _in_proj(x2d, gain, w_bf, tm):
    m = x2d.shape[0]
    return pl.pallas_call(
        _proj_kernel,
        grid=(m // tm,),
        in_specs=[pl.BlockSpec((tm, D_MODEL), lambda i: (i, 0)),
                  _const_spec(gain), _const_spec(w_bf)],
        out_specs=pl.BlockSpec((tm, NPROJ), lambda i: (i, 0)),
        out_shape=jax.ShapeDtypeStruct((m, NPROJ), F32),
        compiler_params=pltpu.CompilerParams(
            dimension_semantics=("parallel",), vmem_limit_bytes=VMEM_LIMIT),
        name="in_proj",
    )(x2d, gain, w_bf)


def _prompt_mixer_kernel(layer, pos0,
                         proj_ref, caw, cab, wgp, bgp, lam, cbw, cbb, dtb, alog, dexp, snorm,
                         hlb, hnorm, ltri3, utri3, expand3, emat2, bmask, pairmask, pairmean,
                         ymix_ref, rgc_o, rgh_o, sc_o, ssd_o, hg_o,
                         xea, xeb, hcar, s_t, st_hg):
    c = pl.program_id(1)
    last = pl.num_programs(1) - 1

    @pl.when(c == 0)
    def _init():
        xea[...] = jnp.zeros_like(xea)
        xeb[...] = jnp.zeros_like(xeb)
        hcar[...] = jnp.zeros_like(hcar)
        s_t[...] = jnp.zeros_like(s_t)
        st_hg[...] = jnp.zeros_like(st_hg)

    row = lax.broadcasted_iota(jnp.int32, (TC, 1), 0)
    sub = row & (SUBLANES - 1)
    lane = lax.broadcasted_iota(jnp.int32, (TC, LANES), 1)
    first_head = lane < HG_HEAD_DIM

    def conv(tail_ref, w_ref, b_ref, src_off, g):
        gs = slice(g * LANES, (g + 1) * LANES)
        x = proj_ref[0, :, src_off + g * LANES:src_off + (g + 1) * LANES]
        xcat = jnp.concatenate([tail_ref[:, gs], x], axis=0)
        y = b_ref[:, gs] + w_ref[CONV_W - 1:CONV_W, gs] * x
        for j in range(1, CONV_W):
            y = y + w_ref[CONV_W - 1 - j:CONV_W - j, gs] * pltpu.roll(xcat, j, 0)[SUBLANES:, :]
        tail_ref[:, gs] = x[TC - SUBLANES:, :]
        return y

    def mixer_a(g):
        gs = slice(g * LANES, (g + 1) * LANES)
        xa = conv(xea, caw, cab, OFF_AX, g)
        a, mult, gi = _rg_coeffs(_dot(_bf(xa), wgp[g]) + bgp[g], lam[:, gs])
        mult = jnp.where(row + (c * TC + pos0) == 0, 1.0, mult)
        b = mult * (gi * xa)
        k = 1
        while k < SUBLANES:
            keep = sub >= k
            a_sh = jnp.where(keep, pltpu.roll(a, k, 0), 1.0)
            b_sh = jnp.where(keep, pltpu.roll(b, k, 0), 0.0)
            b = b + a * b_sh
            a = a * a_sh
            k *= 2
        carry = hcar[0:1, gs]
        slabs = []
        for r in range(TC // SUBLANES):
            rs = slice(r * SUBLANES, (r + 1) * SUBLANES)
            h_r = a[rs, :] * carry + b[rs, :]
            slabs.append(h_r)
            carry = h_r[SUBLANES - 1:SUBLANES, :]
        hcar[0:1, gs] = carry
        ymix_ref[0, :, gs] = jnp.concatenate(slabs, axis=0) * _gelu_tanh(
            proj_ref[0, :, OFF_AG + g * LANES:OFF_AG + (g + 1) * LANES])

    bm = _silu(conv(xeb, cbw, cbb, OFF_XBC, NGRP))
    cm = _silu(conv(xeb, cbw, cbb, OFF_XBC, NGRP + 1))
    dtp = _softplus(proj_ref[0, :, OFF_DT:OFF_DT + LANES] + dtb[...])
    v_da = dtp * (-jnp.exp(alog[...]))
    cum = _dot_c3_x(ltri3[...], v_da)
    cum_t = _dot_x_c3(v_da.T, utri3[...])
    dt_e = _dot_x_c3(dtp, expand3[...])
    cum_e = _dot_x_c3(cum, expand3[...])
    cb = _dot_nt(_bf(cm), _bf(bm))
    causal = (lax.broadcasted_iota(jnp.int32, (TC, TC), 1)
              <= lax.broadcasted_iota(jnp.int32, (TC, TC), 0))
    cmb = _bf(cm)
    bmt = _bf(bm.T)
    ssq_parts = []

    def mixer_b(g):
        gs = slice(g * LANES, (g + 1) * LANES)
        xs = _silu(conv(xeb, cbw, cbb, OFF_XBC, g))
        dt_g = dt_e[:, gs]
        cum_g = cum_e[:, gs]
        cum_last = cum_g[TC - 1:TC, :]
        gmats = []
        for hh in range(2):
            hd = 2 * g + hh
            seg = cum[:, hd:hd + 1] - cum_t[hd:hd + 1, :]
            gmats.append(_bf(cb * jnp.exp(jnp.where(causal, seg, -jnp.inf))))
        y = _dot(jnp.concatenate(gmats, axis=1), _bf(_lane_halves(xs * dt_g)))
        y = y + _dot(cmb, _bf(s_t[g])) * jnp.exp(cum_g)
        yb = y + dexp[:, gs] * xs
        w_e = jnp.exp(cum_last - cum_g) * dt_g
        s_t[g] = jnp.exp(cum_last) * s_t[g] + _dot(bmt, _bf(w_e * xs))
        yz = yb * _silu(proj_ref[0, :, OFF_BZ + g * LANES:OFF_BZ + (g + 1) * LANES])
        ssq_parts.append(yz * yz)
        ymix_ref[0, :, RG_WIDTH + g * LANES:RG_WIDTH + (g + 1) * LANES] = yz

    lb_all = _lower_bound_row(hlb[...], layer)

    def forget_gate(g):
        lb = lb_all[:, g * LANES:(g + 1) * LANES]
        return lb + (1.0 - lb) * _sigmoid(proj_ref[0, :, OFF_CF + g * LANES:OFF_CF + (g + 1) * LANES])

    def mixer_c(g, gt, xall):
        gs = slice(g * LANES, (g + 1) * LANES)
        q = _silu(proj_ref[0, :, OFF_CQ + g * LANES:OFF_CQ + (g + 1) * LANES])
        kk = 1.0 - gt
        vv = proj_ref[0, :, OFF_CI + g * LANES:OFF_CI + (g + 1) * LANES]
        k_a = jnp.where(first_head, kk, 0.0)
        k_b = jnp.where(first_head, 0.0, kk)
        v_a = _bf(jnp.where(first_head, vv, 0.0))
        v_b = _bf(jnp.where(first_head, 0.0, vv))
        bcum = xall[N_LEVELS * TC:, :]
        att = [None, None]
        for lev in range(N_LEVELS):
            if lev == 0:
                qt, kta, ktb = _bf(q), _bf(k_a), _bf(k_b)
            else:
                e = jnp.exp(xall[(lev - 1) * TC:lev * TC, :])
                qt, kta, ktb = _bf(q * e), _bf(k_a * e), _bf(k_b * e)
            for t in range(2):
                rs = slice(t * HALF, (t + 1) * HALF)
                res = _dot_nt(qt[rs, :], jnp.concatenate([kta[rs, :], ktb[rs, :]], axis=0))
                term = res * bmask[lev]
                att[t] = term if att[t] is None else att[t] + term
        lo_r = slice(0, HALF)
        hi_r = slice(HALF, TC)
        e = jnp.exp(xall[(N_LEVELS - 1) * TC:N_LEVELS * TC, :])
        top = _dot_nt(_bf(q[hi_r, :] * e[hi_r, :]),
                      jnp.concatenate([_bf(k_a[lo_r, :] * e[lo_r, :]), _bf(k_b[lo_r, :] * e[lo_r, :])], axis=0))
        v_lo = jnp.concatenate([v_a[lo_r, :], v_b[lo_r, :]], axis=0)
        v_hi = jnp.concatenate([v_a[hi_r, :], v_b[hi_r, :]], axis=0)
        o_lo = _dot(_bf(att[0]), v_lo)
        o_hi = _dot(jnp.concatenate([_bf(att[1]), _bf(top)], axis=1),
                    jnp.concatenate([v_hi, v_lo], axis=0))
        blast = bcum[TC - 1:TC, :]
        o = jnp.concatenate([o_lo, o_hi], axis=0) + _dot_nt(_bf(q * jnp.exp(bcum)), _bf(st_hg[g]))
        khat = kk * jnp.exp(blast - bcum)
        st_hg[g] = jnp.exp(blast) * st_hg[g] + pairmask[...] * _dot(_bf(vv.T), _bf(khat))
        ms = _dot(_bf(o * o), pairmean[...])
        ymix_ref[0, :, RG_WIDTH + SSD_WIDTH + g * LANES:RG_WIDTH + SSD_WIDTH + (g + 1) * LANES] = (
            o * lax.rsqrt(ms + EPS) * hnorm[:, gs]
            * _silu(proj_ref[0, :, OFF_CG + g * LANES:OFF_CG + (g + 1) * LANES]))

    for pair in range(NGRP // 2):
        g0, g1 = 2 * pair, 2 * pair + 1
        gts = [forget_gate(g0), forget_gate(g1)]
        lf = jnp.log(jnp.concatenate(gts, axis=1))
        hi = _bf(lf)
        lo = _bf(lf - hi.astype(F32))
        xall2 = _dot(emat2[...], jnp.concatenate([hi, lo], axis=0))
        for g, gt in zip((g0, g1), gts):
            mixer_a(g)
            mixer_b(g)
            mixer_c(g, gt, xall2[:, (g - g0) * LANES:(g - g0 + 1) * LANES])

    ssq = ssq_parts[0] + ssq_parts[1] + ssq_parts[2] + ssq_parts[3]
    rinv = lax.rsqrt(jnp.sum(ssq, axis=-1, keepdims=True) * (1.0 / SSD_WIDTH) + EPS)
    for g in range(NGRP):
        gs = slice(g * LANES, (g + 1) * LANES)
        ys = slice(RG_WIDTH + g * LANES, RG_WIDTH + (g + 1) * LANES)
        ymix_ref[0, :, ys] = ymix_ref[0, :, ys] * rinv * snorm[:, gs]

    @pl.when(c == last)
    def _emit():
        rgc_o[0] = xea[SUBLANES - 3:SUBLANES, :]
        sc_o[0] = xeb[SUBLANES - 3:SUBLANES, :]
        rgh_o[0] = hcar[0:1, :]
        for g in range(NGRP):
            ssd_o[0, g * LANES:(g + 1) * LANES, :] = s_t[g].T
            s_pair = st_hg[g].T
            for hh in range(2):
                lo = hh * HG_HEAD_DIM
                hg_o[0, 2 * g + hh] = s_pair[lo:lo + HG_HEAD_DIM, lo:lo + HG_HEAD_DIM]


def _prompt_mixer(layer, proj3, mp, consts):
    bsz, seq, _ = proj3.shape
    nc = seq // TC
    params = [mp["caw"], mp["cab"], mp["wgp"], mp["bgp"], mp["lam"], mp["cbw"], mp["cbb"],
              mp["dtb"], mp["alog"], mp["dexp"], mp["snorm"], mp["hlb"], mp["hnorm"],
              consts["ltri3"], consts["utri3"], consts["expand3"], consts["emat2"],
              consts["bmask"], consts["pairmask"], consts["pairmean"]]
    out_shape = (
        jax.ShapeDtypeStruct((bsz, seq, D_MIX), F32),
        jax.ShapeDtypeStruct((bsz, CONV_W - 1, RG_WIDTH), F32),
        jax.ShapeDtypeStruct((bsz, 1, RG_WIDTH), F32),
        jax.ShapeDtypeStruct((bsz, CONV_W - 1, SSD_CONV_DIM), F32),
        jax.ShapeDtypeStruct((bsz, SSD_WIDTH, SSD_STATE), F32),
        jax.ShapeDtypeStruct((bsz, HG_HEADS, HG_HEAD_DIM, HG_HEAD_DIM), F32),
    )
    out_specs = (
        pl.BlockSpec((1, TC, D_MIX), lambda b, c: (b, c, 0)),
        pl.BlockSpec((1, CONV_W - 1, RG_WIDTH), lambda b, c: (b, 0, 0)),
        pl.BlockSpec((1, 1, RG_WIDTH), lambda b, c: (b, 0, 0)),
        pl.BlockSpec((1, CONV_W - 1, SSD_CONV_DIM), lambda b, c: (b, 0, 0)),
        pl.BlockSpec((1, SSD_WIDTH, SSD_STATE), lambda b, c: (b, 0, 0)),
        pl.BlockSpec((1, HG_HEADS, HG_HEAD_DIM, HG_HEAD_DIM), lambda b, c: (b, 0, 0, 0)),
    )
    scratch = [
        pltpu.VMEM((SUBLANES, RG_WIDTH), F32),
        pltpu.VMEM((SUBLANES, SSD_CONV_DIM), F32),
        pltpu.VMEM((SUBLANES, RG_WIDTH), F32),
        pltpu.VMEM((NGRP, SSD_STATE, LANES), F32),
        pltpu.VMEM((NGRP, LANES, LANES), F32),
    ]
    return pl.pallas_call(
        functools.partial(_prompt_mixer_kernel, layer, 0),
        grid=(bsz, nc),
        in_specs=[pl.BlockSpec((1, TC, NPROJ), lambda b, c: (b, c, 0))]
        + [_const_spec(p) for p in params],
        out_specs=out_specs,
        out_shape=out_shape,
        scratch_shapes=scratch,
        compiler_params=pltpu.CompilerParams(
            dimension_semantics=("parallel", "arbitrary"), vmem_limit_bytes=VMEM_LIMIT),
        name="prompt_mixer",
    )(proj3, *params)


def _pad_rows_t(x):
    pad = jnp.zeros((LANES - x.shape[0], x.shape[1]), F32)
    return jnp.concatenate([x, pad], axis=0).T


def _sample_mixer_kernel(layer, pos0, n_acc, proj_ref, rgc_ref, rgh_ref, sc_ref, ssd_ref, hg_ref, *rest):
    (caw, cab, wg, bg, lam, cbw, cbb, dtb, alog, dexp, snorm,
     hlb, hnorm, expand, headsum, headmean,
     ymix_ref, rgc_o, rgh_o, sc_o, ssd_o, hg_o) = rest[n_acc:]
    bb = DEC_BB
    if n_acc == 0:
        for other in range(DEPTH):
            if other != layer:
                ssd_o[other] = jnp.zeros(ssd_o.shape[1:], F32)
                hg_o[other] = jnp.zeros(hg_o.shape[1:], F32)
        ssd_o = ssd_o.at[layer]
        hg_o = hg_o.at[layer]
    ax = proj_ref[:, OFF_AX:OFF_AX + RG_WIDTH]
    xa = cab[...] + caw[CONV_W - 1:CONV_W, :] * ax
    for k in range(CONV_W - 1):
        xa = xa + caw[k:k + 1, :] * rgc_ref[:, k * RG_WIDTH:(k + 1) * RG_WIDTH]
    rgc_o[:, 0:2 * RG_WIDTH] = rgc_ref[:, RG_WIDTH:3 * RG_WIDTH]
    rgc_o[:, 2 * RG_WIDTH:] = ax
    a, mult, gi = _rg_coeffs(_dot(_bf(xa), wg[...]) + bg[...], lam[...])
    if pos0 == 0:
        mult = jnp.ones_like(mult)
    h = a * rgh_ref[...] + mult * (gi * xa)
    rgh_o[...] = h
    ymix_ref[:, 0:RG_WIDTH] = h * _gelu_tanh(proj_ref[:, OFF_AG:OFF_AG + RG_WIDTH])

    bx = proj_ref[:, OFF_XBC:OFF_XBC + SSD_CONV_DIM]
    xbc = cbb[...] + cbw[CONV_W - 1:CONV_W, :] * bx
    for k in range(CONV_W - 1):
        xbc = xbc + cbw[k:k + 1, :] * sc_ref[:, k * SSD_CONV_DIM:(k + 1) * SSD_CONV_DIM]
    sc_o[:, 0:2 * SSD_CONV_DIM] = sc_ref[:, SSD_CONV_DIM:3 * SSD_CONV_DIM]
    sc_o[:, 2 * SSD_CONV_DIM:] = bx
    xbc = _silu(xbc)
    xs = xbc[:, :SSD_WIDTH]
    bm = xbc[:, SSD_WIDTH:SSD_WIDTH + SSD_STATE]
    cm = xbc[:, SSD_WIDTH + SSD_STATE:]
    dtp = _softplus(proj_ref[:, OFF_DT:OFF_DT + LANES] + dtb[...])
    v_da = dtp * (-jnp.exp(alog[...]))
    dt_e = _dot_x_c(dtp, expand[...])
    e_e = jnp.exp(_dot_x_c(v_da, expand[...]))
    dtx = dt_e * xs
    cbs = _dot(_bf(cm * bm), jnp.ones((SSD_STATE, LANES), BF16))[:, 0:1]
    dtx_t = _split3(_pad_rows_t(dtx))
    e_t = _split3(_pad_rows_t(e_e))
    c_pad = _bf(jnp.concatenate([cm, jnp.zeros((LANES - bb, SSD_STATE), F32)], axis=0))
    rowi = lax.broadcasted_iota(jnp.int32, (LANES, LANES), 0)
    lane_w = lax.broadcasted_iota(jnp.int32, (SSD_WIDTH, LANES), 1)
    y_t = jnp.zeros((SSD_WIDTH, LANES), F32)
    for j in range(bb):
        pick = jnp.where(rowi == j, 1.0, 0.0).astype(BF16)
        d_col = _dot(dtx_t[0], pick) + _dot(dtx_t[1], pick) + _dot(dtx_t[2], pick)
        e_col = _dot(e_t[0], pick) + _dot(e_t[1], pick) + _dot(e_t[2], pick)
        s_old = ssd_ref[j].reshape(SSD_WIDTH, SSD_STATE)
        ssd_o[j] = (e_col * s_old + d_col * bm[j:j + 1, :]).reshape(
            SSD_HEADS, SSD_HEAD_DIM, SSD_STATE)
        y_t = y_t + jnp.where(lane_w == j, _dot_nt(_bf(s_old), c_pad), 0.0)
    y = cbs * dtx + y_t.T[0:bb, :] * e_e
    yb = y + dexp[...] * xs
    ymix_ref[:, RG_WIDTH:RG_WIDTH + SSD_WIDTH] = _rms(
        yb * _silu(proj_ref[:, OFF_BZ:OFF_BZ + SSD_WIDTH]), snorm[...])

    q = _silu(proj_ref[:, OFF_CQ:OFF_CQ + HG_WIDTH])
    lb = _lower_bound_row(hlb[...], layer)
    gt = lb + (1.0 - lb) * _sigmoid(proj_ref[:, OFF_CF:OFF_CF + HG_WIDTH])
    kk = 1.0 - gt
    fdec = jnp.exp(jnp.log(gt))
    vv = proj_ref[:, OFF_CI:OFF_CI + HG_WIDTH]
    att = _dot(_bf(q * kk), headsum[...])
    qhat = _bf(q * fdec)
    f_t = _split3(_pad_rows_t(fdec))
    k_t = _split3(_pad_rows_t(kk))
    rowb = lax.broadcasted_iota(jnp.int32, (bb, HG_WIDTH), 0)
    o_acc = jnp.zeros((bb, HG_WIDTH), F32)
    for j in range(bb):
        pick = jnp.where(rowi == j, 1.0, 0.0).astype(BF16)
        f_col = _dot(f_t[0], pick) + _dot(f_t[1], pick) + _dot(f_t[2], pick)
        k_col = _dot(k_t[0], pick) + _dot(k_t[1], pick) + _dot(k_t[2], pick)
        r_parts = []
        for hd in range(HG_HEADS):
            lo = hd * HG_HEAD_DIM
            s_old = hg_ref[j, hd]
            v_row = vv[j:j + 1, lo:lo + HG_HEAD_DIM]
            hg_o[j, hd] = (f_col[lo:lo + HG_HEAD_DIM, 0:HG_HEAD_DIM] * s_old
                           + k_col[lo:lo + HG_HEAD_DIM, 0:HG_HEAD_DIM] * v_row)
            r_parts.append(_dot(qhat[:, lo:lo + HG_HEAD_DIM], _bf(s_old)))
        o_acc = o_acc + jnp.where(rowb == j, jnp.concatenate(r_parts, axis=1), 0.0)
    o = att * vv + o_acc
    ms = _dot(_bf(o * o), headmean[...])
    ymix_ref[:, RG_WIDTH + SSD_WIDTH:] = (
        o * lax.rsqrt(ms + EPS) * hnorm[...] * _silu(proj_ref[:, OFF_CG:OFF_CG + HG_WIDTH]))


def _sample_mixer(layer, proj, rgc, rgh, sc, ssd_all, hg_all, acc, mp, consts):
    nb = proj.shape[0]
    bb = DEC_BB
    params = [mp["caw"], mp["cab"], mp["wg"], mp["bg"], mp["lam"], mp["cbw"], mp["cbb"],
              mp["dtb"], mp["alog"], mp["dexp"], mp["snorm"], mp["hlb"], mp["hnorm"],
              consts["expand"], consts["headsum"], consts["headmean"]]
    row2 = lambda w: pl.BlockSpec((bb, w), lambda i: (i, 0))
    small_specs = [row2(3 * RG_WIDTH), row2(RG_WIDTH), row2(3 * SSD_CONV_DIM)]
    big_specs = [
        pl.BlockSpec((None, bb, SSD_HEADS, SSD_HEAD_DIM, SSD_STATE), lambda i: (layer, i, 0, 0, 0)),
        pl.BlockSpec((None, bb, HG_HEADS, HG_HEAD_DIM, HG_HEAD_DIM), lambda i: (layer, i, 0, 0, 0)),
    ]
    out_shape = (
        jax.ShapeDtypeStruct((nb, D_MIX), F32),
        jax.ShapeDtypeStruct((nb, 3 * RG_WIDTH), F32),
        jax.ShapeDtypeStruct((nb, RG_WIDTH), F32),
        jax.ShapeDtypeStruct((nb, 3 * SSD_CONV_DIM), F32),
        jax.ShapeDtypeStruct(ssd_all.shape, F32),
        jax.ShapeDtypeStruct(hg_all.shape, F32),
    )
    n_acc = len(acc)
    n_lead = 6
    big_out_specs = big_specs if n_acc else [
        pl.BlockSpec((DEPTH, bb, SSD_HEADS, SSD_HEAD_DIM, SSD_STATE), lambda i: (0, i, 0, 0, 0)),
        pl.BlockSpec((DEPTH, bb, HG_HEADS, HG_HEAD_DIM, HG_HEAD_DIM), lambda i: (0, i, 0, 0, 0)),
    ]
    return pl.pallas_call(
        functools.partial(_sample_mixer_kernel, layer, PAST_LEN, n_acc),
        grid=(nb // bb,),
        in_specs=[row2(NPROJ)] + small_specs + big_specs
        + [pl.BlockSpec(memory_space=pl.ANY)] * n_acc + [_const_spec(p) for p in params],
        out_specs=tuple([row2(D_MIX)] + small_specs + big_out_specs),
        out_shape=out_shape,
        input_output_aliases={n_lead + k: 4 + k for k in range(n_acc)},
        compiler_params=pltpu.CompilerParams(
            dimension_semantics=("parallel",), vmem_limit_bytes=VMEM_LIMIT),
        name="sample_mixer",
    )(proj, rgc, rgh, sc, ssd_all, hg_all, *acc, *params)


def _post_kernel(final, h_ref, y_ref, p_ref, wo, nffn, wup, wdown, nple, wgate, wproj, nfin, o_ref):
    h = h_ref[...] + _dot(_bf(y_ref[...]), wo[...])
    z = jnp.square(jnp.maximum(_dot(_bf(_rms(h, nffn[...])), wup[...]), 0.0))
    h = h + _dot(_bf(z), wdown[...])
    gate = _sigmoid(_dot(_bf(_rms(h, nple[...])), wgate[...]))
    h = h + gate * _dot(_bf(p_ref[...]), wproj[...])
    if final:
        h = _rms(h, nfin[...])
    o_ref[...] = h


def _post(final, layer, h2d, ymix2d, p_all, lp, nfin, tm):
    m = h2d.shape[0]
    params = [lp["wo"], lp["nffn"], lp["wup"], lp["wdown"], lp["nple"], lp["wgate"], lp["wproj"], nfin]
    return pl.pallas_call(
        functools.partial(_post_kernel, final),
        grid=(m // tm,),
        in_specs=[pl.BlockSpec((tm, D_MODEL), lambda i: (i, 0)),
                  pl.BlockSpec((tm, D_MIX), lambda i: (i, 0)),
                  pl.BlockSpec((None, tm, PLE_DIM), lambda i: (layer, i, 0))]
        + [_const_spec(p) for p in params],
        out_specs=pl.BlockSpec((tm, D_MODEL), lambda i: (i, 0)),
        out_shape=jax.ShapeDtypeStruct((m, D_MODEL), F32),
        compiler_params=pltpu.CompilerParams(
            dimension_semantics=("parallel",), vmem_limit_bytes=VMEM_LIMIT),
        name="post_mixer",
    )(h2d, ymix2d, p_all, *params)


def _constants():
    t = np.arange(TC)
    ltri = (t[None, :] <= t[:, None]).astype(np.float32)
    utri = ltri.T.copy()
    expand = np.zeros((LANES, SSD_WIDTH), np.float32)
    for hd in range(SSD_HEADS):
        expand[hd, hd * SSD_HEAD_DIM:(hd + 1) * SSD_HEAD_DIM] = 1.0
    emat = np.zeros(((N_LEVELS + 1) * TC, TC), np.float32)
    for lev in range(1, N_LEVELS + 1):
        blk = 1 << lev
        mid = (t // blk) * blk + blk // 2
        upper = t >= mid
        r0 = (lev - 1) * TC
        emat[r0:r0 + TC] = (
            (upper[:, None] & (t[None, :] >= mid[:, None]) & (t[None, :] <= t[:, None]))
            | ((~upper)[:, None] & (t[None, :] > t[:, None]) & (t[None, :] < mid[:, None])))
    emat[N_LEVELS * TC:] = ltri
    th = np.arange(HALF)
    bmask = np.zeros((N_LEVELS, HALF, 2 * HALF), np.float32)
    bmask[0] = np.tile(np.eye(HALF, dtype=np.float32), (1, 2))
    for lev in range(1, N_LEVELS):
        blk = 1 << lev
        up = (th % blk) >= blk // 2
        m = up[:, None] & (~up)[None, :] & ((th // blk)[:, None] == (th // blk)[None, :])
        bmask[lev] = np.tile(m.astype(np.float32), (1, 2))
    hidx = np.arange(HG_WIDTH) // HG_HEAD_DIM
    bd = (hidx[:, None] == hidx[None, :]).astype(np.float32)
    pair = bd[:LANES, :LANES]
    return {
        "ltri3": jnp.asarray(np.tile(ltri, (1, 3)), BF16),
        "utri3": jnp.asarray(np.tile(utri, (3, 1)), BF16),
        "expand": jnp.asarray(expand, BF16),
        "expand3": jnp.asarray(np.tile(expand, (3, 1)), BF16),
        "emat2": jnp.asarray(np.tile(emat, (1, 2)), BF16),
        "bmask": jnp.asarray(bmask, F32),
        "pairmask": jnp.asarray(pair, F32),
        "pairmean": jnp.asarray(pair / HG_HEAD_DIM, BF16),
        "headsum": jnp.asarray(bd, BF16), "headmean": jnp.asarray(bd / HG_HEAD_DIM, BF16),
    }


def _block_diag(w):
    hh, d, _ = w.shape
    eye = jnp.eye(hh, dtype=w.dtype)
    return (eye[:, None, :, None] * w[:, :, None, :]).reshape(hh * d, hh * d)


def _pair_gate_weights(wa, wx):
    hh, d, _ = wa.shape
    pa = jnp.stack([_block_diag(wa[2 * g:2 * g + 2]) for g in range(hh // 2)])
    px = jnp.stack([_block_diag(wx[2 * g:2 * g + 2]) for g in range(hh // 2)])
    return jnp.concatenate([pa, px], axis=2)


def _pad_lanes(v):
    return jnp.pad(v, (0, LANES - v.shape[0]))[None, :]


def kernel(x_prompt, x_sample, state_rg_conv, state_rg_h, state_ssd_conv, state_ssd, state_hgrn,
           p_prompt, p_sample, norm_mix, w_in, conv_a_w, conv_a_b, rg_wa, rg_ba, rg_wx, rg_bx,
           rg_lambda, conv_b_w, conv_b_b, ssd_dt_bias, ssd_a_log, ssd_d, ssd_norm,
           hg_lower_bounds, hg_norm, w_out, norm_ffn, w_up, w_down, norm_ple, w_ple_gate,
           w_ple_proj, norm_final):
    bsz, seq, _ = x_prompt.shape
    nb = x_sample.shape[0]
    consts = _constants()
    nfin = norm_final[None, :]
    pp_all = p_prompt.reshape(DEPTH, bsz * seq, PLE_DIM)
    ps_all = p_sample.reshape(DEPTH, nb, PLE_DIM)

    hp = x_prompt.reshape(bsz * seq, D_MODEL)
    hs = x_sample.reshape(nb, D_MODEL)
    st_p = [[] for _ in range(5)]
    st_s = [[] for _ in range(3)]
    acc = ()
    for i in range(DEPTH):
        w_re = _prep_w_in(w_in, i, 128)
        gmix = norm_mix[i][None, :]
        mp = {
            "caw": conv_a_w[i], "cab": conv_a_b[i][None, :],
            "wg": jnp.concatenate([_block_diag(rg_wa[i]), _block_diag(rg_wx[i])], axis=1).astype(BF16),
            "bg": jnp.concatenate([rg_ba[i], rg_bx[i]])[None, :],
            "wgp": _pair_gate_weights(rg_wa[i], rg_wx[i]).astype(BF16),
            "bgp": jnp.concatenate([rg_ba[i].reshape(NGRP, 1, LANES),
                                    rg_bx[i].reshape(NGRP, 1, LANES)], axis=2),
            "lam": rg_lambda[i][None, :],
            "cbw": conv_b_w[i], "cbb": conv_b_b[i][None, :],
            "dtb": _pad_lanes(ssd_dt_bias[i]), "alog": _pad_lanes(ssd_a_log[i]),
            "dexp": jnp.repeat(ssd_d[i], SSD_HEAD_DIM)[None, :],
            "snorm": ssd_norm[i][None, :],
            "hlb": hg_lower_bounds, "hnorm": jnp.tile(hg_norm[i], HG_HEADS)[None, :],
        }
        lp = {
            "wo": _cast_bf16(w_out, i, 512), "nffn": norm_ffn[i][None, :],
            "wup": _cast_bf16(w_up, i, 256), "wdown": _cast_bf16(w_down, i, 1024),
            "nple": norm_ple[i][None, :], "wgate": _cast_bf16(w_ple_gate, i, 512),
            "wproj": _cast_bf16(w_ple_proj, i, 256),
        }
        final = i == DEPTH - 1

        proj_p = _in_proj(hp, gmix, w_re, 256).reshape(bsz, seq, NPROJ)
        ymix_p, rgc, rgh, sc, ssd, hg = _prompt_mixer(i, proj_p, mp, consts)
        hp = _post(final, i, hp, ymix_p.reshape(bsz * seq, D_MIX), pp_all, lp, nfin, 256)
        for lst, s in zip(st_p, (rgc, rgh.reshape(bsz, RG_WIDTH), sc,
                                 ssd.reshape(bsz, SSD_HEADS, SSD_HEAD_DIM, SSD_STATE), hg)):
            lst.append(s)

        proj_s = _in_proj(hs, gmix, w_re, nb)
        ymix_s, rgc, rgh, sc, ssd_acc, hg_acc = _sample_mixer(
            i, proj_s,
            state_rg_conv[i].reshape(nb, 3 * RG_WIDTH), state_rg_h[i],
            state_ssd_conv[i].reshape(nb, 3 * SSD_CONV_DIM),
            state_ssd, state_hgrn, acc, mp, consts)
        acc = (ssd_acc, hg_acc)
        hs = _post(final, i, hs, ymix_s, ps_all, lp, nfin, nb)
        for lst, s in zip(st_s, (rgc.reshape(nb, CONV_W - 1, RG_WIDTH), rgh,
                                 sc.reshape(nb, CONV_W - 1, SSD_CONV_DIM))):
            lst.append(s)

    y_prompt = hp.reshape(bsz, seq, D_MODEL)
    y_sample = hs.reshape(nb, 1, D_MODEL)
    outs_p = [jnp.stack(lst) for lst in st_p]
    outs_s = [jnp.stack(lst) for lst in st_s] + list(acc)
    return (y_prompt, y_sample, *outs_p, *outs_s)
```

```python
import functools

import numpy as np
import jax
import jax.numpy as jnp
from jax import lax
from jax.experimental import pallas as pl
from jax.experimental.pallas import tpu as pltpu

F32 = jnp.float32
BF16 = jnp.bfloat16

D_MODEL = 1024
DEPTH = 2
PAST_LEN = 16384
PLE_DIM = 256
D_FF = 4 * D_MODEL
CONV_W = 4
EPS = 1e-6
RG_WIDTH = 512
RG_HEADS = 8
RG_HEAD_DIM = 64
RG_C = 8.0
SSD_WIDTH = 512
SSD_HEAD_DIM = 64
SSD_HEADS = 8
SSD_STATE = 128
SSD_CONV_DIM = SSD_WIDTH + 2 * SSD_STATE
HG_WIDTH = 512
HG_HEADS = 8
HG_HEAD_DIM = 64
D_MIX = RG_WIDTH + SSD_WIDTH + HG_WIDTH

LOG2E = 1.4426950408889634
LANES = 128
SUBLANES = 8
NGRP = RG_WIDTH // LANES

OFF_AX = 0
OFF_AG = 512
OFF_BZ = 1024
OFF_XBC = 1536
OFF_CQ = 2304
OFF_CF = 2816
OFF_CI = 3328
OFF_CG = 3840
OFF_DT = 4352
NPROJ = OFF_DT + LANES
DT_COL = 2 * RG_WIDTH + SSD_WIDTH + SSD_CONV_DIM
D_IN_PROJ = DT_COL + SSD_HEADS + 4 * HG_WIDTH

TC = 128
HALF = TC // 2
N_LEVELS = 7
DEC_BB = 8
TM_PROMPT = 512
FF_CHUNK = 1024
VMEM_LIMIT = 52 * 1024 * 1024


def _bf(x):
    return x.astype(BF16)


def _dot(a, b):
    return jnp.dot(a, b, preferred_element_type=F32)


def _dot_nt(a, b):
    return lax.dot_general(a, b, (((1,), (1,)), ((), ())), preferred_element_type=F32)


def _split3(x):
    hi = _bf(x)
    r1 = x - hi.astype(F32)
    mid = _bf(r1)
    lo = _bf(r1 - mid.astype(F32))
    return hi, mid, lo


def _dot_x_c(x, c):
    hi, mid, lo = _split3(x)
    return _dot(hi, c) + _dot(mid, c) + _dot(lo, c)


def _dot_x_c3(x, c3):
    return _dot(jnp.concatenate(_split3(x), axis=1), c3)


def _dot_c3_x(c3, x):
    return _dot(c3, jnp.concatenate(_split3(x), axis=0))


def _sigmoid(x):
    return 0.5 * jnp.tanh(0.5 * x) + 0.5


def _silu(x):
    h = 0.5 * x
    return h * jnp.tanh(h) + h


def _softplus(x):
    return jnp.maximum(x, 0.0) + jnp.log1p(jnp.exp(-jnp.abs(x)))


def _gelu_tanh(x):
    c = np.float32(np.sqrt(2.0 / np.pi))
    return 0.5 * x * (1.0 + jnp.tanh(c * (x + 0.044715 * (x * x * x))))


def _rms(x, gain):
    return x * lax.rsqrt(jnp.mean(x * x, axis=-1, keepdims=True) + EPS) * gain


def _lower_bound_row(hlb, layer):
    m = jnp.max(hlb, axis=0, keepdims=True)
    e = jnp.exp(hlb - m)
    sm = e / jnp.sum(e, axis=0, keepdims=True)
    lb = jnp.zeros((1, HG_WIDTH), F32)
    for j in range(1, layer + 1):
        lb = lb + sm[j:j + 1, :]
    return lb


def _rg_coeffs(gates, lam_row):
    w = gates.shape[1] // 2
    r = _sigmoid(gates[:, :w])
    i = _sigmoid(gates[:, w:])
    log_a = -RG_C * r * _softplus(-lam_row)
    a = jnp.exp(log_a)
    mult = jnp.sqrt(-jnp.tanh(log_a) * (a * a + 1.0))
    return a, mult, i


def _lane_halves(x):
    lane = lax.broadcasted_iota(jnp.int32, x.shape, 1)
    lo = jnp.where(lane < HG_HEAD_DIM, x, 0.0)
    hi = jnp.where(lane >= HG_HEAD_DIM, x, 0.0)
    return jnp.concatenate([lo, hi], axis=0)


def _const_spec(arr):
    nd = arr.ndim
    return pl.BlockSpec(arr.shape, lambda *_: (0,) * nd, pipeline_mode=pl.Buffered(1))


def _cast_kernel(w_ref, o_ref):
    o_ref[...] = _bf(w_ref[...])


def _cast_bf16(w_all, layer, rb):
    _, rows, cols = w_all.shape
    return pl.pallas_call(
        _cast_kernel,
        grid=(rows // rb,),
        in_specs=[pl.BlockSpec((None, rb, cols), lambda i: (layer, i, 0))],
        out_specs=pl.BlockSpec((rb, cols), lambda i: (i, 0)),
        out_shape=jax.ShapeDtypeStruct((rows, cols), BF16),
        compiler_params=pltpu.CompilerParams(dimension_semantics=("parallel",)),
        name="cast_bf16",
    )(w_all)


def _win_kernel(w_ref, o_ref):
    rows = w_ref.shape[0]
    o_ref[:, 0:OFF_CQ] = _bf(w_ref[:, 0:DT_COL])
    o_ref[:, OFF_CQ:OFF_DT] = _bf(w_ref[:, DT_COL + SSD_HEADS:D_IN_PROJ])
    dt = jnp.concatenate([w_ref[:, DT_COL:DT_COL + SSD_HEADS],
                          jnp.zeros((rows, LANES - SSD_HEADS), F32)], axis=1)
    o_ref[:, OFF_DT:NPROJ] = _bf(dt)


def _prep_w_in(w_all, layer, rb):
    return pl.pallas_call(
        _win_kernel,
        grid=(D_MODEL // rb,),
        in_specs=[pl.BlockSpec((None, rb, D_IN_PROJ), lambda i: (layer, i, 0))],
        out_specs=pl.BlockSpec((rb, NPROJ), lambda i: (i, 0)),
        out_shape=jax.ShapeDtypeStruct((D_MODEL, NPROJ), BF16),
        compiler_params=pltpu.CompilerParams(dimension_semantics=("parallel",)),
        name="prep_w_in",
    )(w_all)


def _proj_kernel(x_ref, g_ref, w_ref, o_ref):
    u = _rms(x_ref[...], g_ref[...])
    o_ref[...] = _dot(_bf(u), w_ref[...])


def _in_proj(x2d, gain, w_bf, tm):
    m = x2d.shape[0]
    return pl.pallas_call(
        _proj_kernel,
        grid=(m // tm,),
        in_specs=[pl.BlockSpec((tm, D_MODEL), lambda i: (i, 0)),
                  _const_spec(gain), _const_spec(w_bf)],
        out_specs=pl.BlockSpec((tm, NPROJ), lambda i: (i, 0)),
        out_shape=jax.ShapeDtypeStruct((m, NPROJ), F32),
        compiler_params=pltpu.CompilerParams(
            dimension_semantics=("parallel",), vmem_limit_bytes=VMEM_LIMIT),
        name="in_proj",
    )(x2d, gain, w_bf)


def _prompt_mixer_kernel(layer, pos0,
                         proj_ref, caw, cab, wgp, bgp, lam, cbw, cbb, dtb, alog, dexp, snorm,
                         hlb, hnorm, ltri3, utri3, expand3, emat2, cmask, pairmask, pairmean,
                         ymix_ref, rgc_o, rgh_o, sc_o, ssd_o, hg_o,
                         xea, xeb, hcar, s_t, st_hg):
    c = pl.program_id(1)
    last = pl.num_programs(1) - 1

    @pl.when(c == 0)
    def _init():
        xea[...] = jnp.zeros_like(xea)
        xeb[...] = jnp.zeros_like(xeb)
        hcar[...] = jnp.zeros_like(hcar)
        s_t[...] = jnp.zeros_like(s_t)
        st_hg[...] = jnp.zeros_like(st_hg)

    row_h = lax.broadcasted_iota(jnp.int32, (HALF, 1), 0)
    sub_h = row_h & (SUBLANES - 1)
    first_head = lax.broadcasted_iota(jnp.int32, (HALF, LANES), 1) < HG_HEAD_DIM
    first_head_tc = lax.broadcasted_iota(jnp.int32, (TC, LANES), 1) < HG_HEAD_DIM
    col_tc = lax.broadcasted_iota(jnp.int32, (HALF, TC), 1)
    row_tc = lax.broadcasted_iota(jnp.int32, (HALF, TC), 0)

    tiles = [slice(t * HALF, (t + 1) * HALF) for t in range(TC // HALF)]

    def lanes(off, g):
        return slice(off + g * LANES, off + (g + 1) * LANES)

    def conv_rows(prev8, x, w_ref, b_ref, gs):
        xcat = jnp.concatenate([prev8, x], axis=0)
        y = b_ref[:, gs] + w_ref[CONV_W - 1:CONV_W, gs] * x
        for j in range(1, CONV_W):
            y = y + w_ref[CONV_W - 1 - j:CONV_W - j, gs] * pltpu.roll(xcat, j, 0)[SUBLANES:, :]
        return y, x[x.shape[0] - SUBLANES:, :]

    def conv(tail_ref, w_ref, b_ref, src_off, g):
        gs = lanes(0, g)
        y, tail = conv_rows(tail_ref[:, gs], proj_ref[0, :, lanes(src_off, g)], w_ref, b_ref, gs)
        tail_ref[:, gs] = tail
        return y

    def mixer_a(g):
        gs = lanes(0, g)
        prev8 = xea[:, gs]
        carry = hcar[0:1, gs]
        for t, rs in enumerate(tiles):
            xa, prev8 = conv_rows(prev8, proj_ref[0, rs, lanes(OFF_AX, g)], caw, cab, gs)
            a, mult, gi = _rg_coeffs(_dot(_bf(xa), wgp[g]) + bgp[g], lam[:, gs])
            mult = jnp.where(row_h + (c * TC + t * HALF + pos0) == 0, 1.0, mult)
            b = mult * (gi * xa)
            k = 1
            while k < SUBLANES:
                keep = sub_h >= k
                a_sh = jnp.where(keep, pltpu.roll(a, k, 0), 1.0)
                b_sh = jnp.where(keep, pltpu.roll(b, k, 0), 0.0)
                b = b + a * b_sh
                a = a * a_sh
                k *= 2
            slabs = []
            for r in range(HALF // SUBLANES):
                sl = slice(r * SUBLANES, (r + 1) * SUBLANES)
                h_r = a[sl, :] * carry + b[sl, :]
                slabs.append(h_r)
                carry = h_r[SUBLANES - 1:SUBLANES, :]
            ymix_ref[0, rs, gs] = jnp.concatenate(slabs, axis=0) * _gelu_tanh(
                proj_ref[0, rs, lanes(OFF_AG, g)])
        xea[:, gs] = prev8
        hcar[0:1, gs] = carry

    bm = _silu(conv(xeb, cbw, cbb, OFF_XBC, NGRP))
    cm = _silu(conv(xeb, cbw, cbb, OFF_XBC, NGRP + 1))
    dtp = _softplus(proj_ref[0, :, OFF_DT:OFF_DT + LANES] + dtb[...])
    v_da = dtp * (-LOG2E * jnp.exp(alog[...]))
    cum = _dot_c3_x(ltri3[...], v_da)
    cum_t = _dot_x_c3(v_da.T, utri3[...])
    dt_e = _dot_x_c3(dtp, expand3[...])
    cum_e = _dot_x_c3(cum, expand3[...])
    cb = _dot_nt(_bf(cm), _bf(bm))
    cmb = _bf(cm)
    bmt = _bf(bm.T)
    ssq_parts = []

    def mixer_b(g):
        gs = lanes(0, g)
        cum_last = cum_e[TC - 1:TC, gs]
        s_old = _bf(s_t[g])
        prev8 = xeb[:, gs]
        xs_t, x2a, x2b, wx = [], [], [], []
        for rs in tiles:
            xs, prev8 = conv_rows(prev8, proj_ref[0, rs, lanes(OFF_XBC, g)], cbw, cbb, gs)
            xs = _silu(xs)
            dt_g = dt_e[rs, gs]
            dtx = xs * dt_g
            xs_t.append(xs)
            x2a.append(_bf(jnp.where(first_head, dtx, 0.0)))
            x2b.append(_bf(jnp.where(first_head, 0.0, dtx)))
            wx.append(_bf(jnp.exp2(cum_last - cum_e[rs, gs]) * dt_g * xs))
        xeb[:, gs] = prev8
        x2 = jnp.concatenate(x2a + x2b, axis=0)
        for t, rs in enumerate(tiles):
            gmats = []
            for hh in range(2):
                hd = 2 * g + hh
                seg = cum[rs, hd:hd + 1] - cum_t[hd:hd + 1, :]
                gmats.append(_bf(cb[rs, :] * jnp.exp2(jnp.where(col_tc <= row_tc + t * HALF, seg, -jnp.inf))))
            y = _dot(jnp.concatenate(gmats, axis=1), x2)
            y = y + _dot(cmb[rs, :], s_old) * jnp.exp2(cum_e[rs, gs])
            yz = (y + dexp[:, gs] * xs_t[t]) * _silu(proj_ref[0, rs, lanes(OFF_BZ, g)])
            ssq_parts.append(yz * yz)
            ymix_ref[0, rs, lanes(RG_WIDTH, g)] = yz
        s_t[g] = jnp.exp2(cum_last) * s_t[g] + _dot(bmt, jnp.concatenate(wx, axis=0))

    lb_all = _lower_bound_row(hlb[...], layer)

    def forget_gate(g):
        lb = lb_all[:, g * LANES:(g + 1) * LANES]
        return lb + (1.0 - lb) * _sigmoid(proj_ref[0, :, OFF_CF + g * LANES:OFF_CF + (g + 1) * LANES])

    def mixer_c(g, gt, xall):
        gs = lanes(0, g)
        b_off = N_LEVELS * TC
        bcum = xall[b_off:b_off + TC, :]
        blast = bcum[TC - 1:TC, :]
        q = _silu(proj_ref[0, :, lanes(OFF_CQ, g)])
        kk = 1.0 - gt
        vv = proj_ref[0, :, lanes(OFF_CI, g)]
        zeros = jnp.zeros((HG_HEAD_DIM, TC), BF16)

        def keys_by_head(kt):
            ktt = _bf(kt.T)
            return jnp.concatenate(
                [jnp.concatenate([ktt[:HG_HEAD_DIM, :], zeros], axis=1),
                 jnp.concatenate([zeros, ktt[HG_HEAD_DIM:, :]], axis=1)], axis=0)

        att = _dot(_bf(q), keys_by_head(kk)) * cmask[0]
        for lev in range(1, N_LEVELS + 1):
            e = jnp.exp2(xall[(lev - 1) * TC:lev * TC, :])
            att = att + _dot(_bf(q * e), keys_by_head(kk * e)) * cmask[lev]
        v2 = jnp.concatenate([_bf(jnp.where(first_head_tc, vv, 0.0)),
                              _bf(jnp.where(first_head_tc, 0.0, vv))], axis=0)
        o = _dot(_bf(att), v2) + _dot_nt(_bf(q * jnp.exp2(bcum)), _bf(st_hg[g]))
        khat = kk * jnp.exp2(blast - bcum)
        st_hg[g] = jnp.exp2(blast) * st_hg[g] + pairmask[...] * _dot(_bf(vv.T), _bf(khat))
        ms = _dot(_bf(o * o), pairmean[...])
        ymix_ref[0, :, lanes(RG_WIDTH + SSD_WIDTH, g)] = (
            o * lax.rsqrt(ms + EPS) * hnorm[:, gs] * _silu(proj_ref[0, :, lanes(OFF_CG, g)]))

    for pair in range(NGRP // 2):
        g0, g1 = 2 * pair, 2 * pair + 1
        gts = [forget_gate(g0), forget_gate(g1)]
        lf = jnp.log(jnp.concatenate(gts, axis=1)) * LOG2E
        hi = _bf(lf)
        lo = _bf(lf - hi.astype(F32))
        xall2 = _dot(emat2[...], jnp.concatenate([hi, lo], axis=0))
        for g, gt in zip((g0, g1), gts):
            mixer_a(g)
            mixer_b(g)
            mixer_c(g, gt, xall2[:, (g - g0) * LANES:(g - g0 + 1) * LANES])

    for t, rs in enumerate(tiles):
        ssq = ssq_parts[t]
        for g in range(1, NGRP):
            ssq = ssq + ssq_parts[g * len(tiles) + t]
        rinv = lax.rsqrt(jnp.sum(ssq, axis=-1, keepdims=True) * (1.0 / SSD_WIDTH) + EPS)
        for g in range(NGRP):
            ys = lanes(RG_WIDTH, g)
            ymix_ref[0, rs, ys] = ymix_ref[0, rs, ys] * rinv * snorm[:, lanes(0, g)]

    @pl.when(c == last)
    def _emit():
        rgc_o[0] = xea[SUBLANES - 3:SUBLANES, :]
        sc_o[0] = xeb[SUBLANES - 3:SUBLANES, :]
        rgh_o[0] = hcar[0:1, :]
        for g in range(NGRP):
            ssd_o[0, g * LANES:(g + 1) * LANES, :] = s_t[g].T
            s_pair = st_hg[g].T
            for hh in range(2):
                lo = hh * HG_HEAD_DIM
                hg_o[0, 2 * g + hh] = s_pair[lo:lo + HG_HEAD_DIM, lo:lo + HG_HEAD_DIM]


def _prompt_mixer(layer, proj3, mp, consts):
    bsz, seq, _ = proj3.shape
    nc = seq // TC
    params = [mp["caw"], mp["cab"], mp["wgp"], mp["bgp"], mp["lam"], mp["cbw"], mp["cbb"],
              mp["dtb"], mp["alog"], mp["dexp"], mp["snorm"], mp["hlb"], mp["hnorm"],
              consts["ltri3"], consts["utri3"], consts["expand3"], consts["emat2"],
              consts["cmask"], consts["pairmask"], consts["pairmean"]]
    out_shape = (
        jax.ShapeDtypeStruct((bsz, seq, D_MIX), F32),
        jax.ShapeDtypeStruct((bsz, CONV_W - 1, RG_WIDTH), F32),
        jax.ShapeDtypeStruct((bsz, 1, RG_WIDTH), F32),
        jax.ShapeDtypeStruct((bsz, CONV_W - 1, SSD_CONV_DIM), F32),
        jax.ShapeDtypeStruct((bsz, SSD_WIDTH, SSD_STATE), F32),
        jax.ShapeDtypeStruct((bsz, HG_HEADS, HG_HEAD_DIM, HG_HEAD_DIM), F32),
    )
    out_specs = (
        pl.BlockSpec((1, TC, D_MIX), lambda b, c: (b, c, 0)),
        pl.BlockSpec((1, CONV_W - 1, RG_WIDTH), lambda b, c: (b, 0, 0)),
        pl.BlockSpec((1, 1, RG_WIDTH), lambda b, c: (b, 0, 0)),
        pl.BlockSpec((1, CONV_W - 1, SSD_CONV_DIM), lambda b, c: (b, 0, 0)),
        pl.BlockSpec((1, SSD_WIDTH, SSD_STATE), lambda b, c: (b, 0, 0)),
        pl.BlockSpec((1, HG_HEADS, HG_HEAD_DIM, HG_HEAD_DIM), lambda b, c: (b, 0, 0, 0)),
    )
    scratch = [
        pltpu.VMEM((SUBLANES, RG_WIDTH), F32),
        pltpu.VMEM((SUBLANES, SSD_CONV_DIM), F32),
        pltpu.VMEM((SUBLANES, RG_WIDTH), F32),
        pltpu.VMEM((NGRP, SSD_STATE, LANES), F32),
        pltpu.VMEM((NGRP, LANES, LANES), F32),
    ]
    return pl.pallas_call(
        functools.partial(_prompt_mixer_kernel, layer, 0),
        grid=(bsz, nc),
        in_specs=[pl.BlockSpec((1, TC, NPROJ), lambda b, c: (b, c, 0))]
        + [_const_spec(p) for p in params],
        out_specs=out_specs,
        out_shape=out_shape,
        scratch_shapes=scratch,
        compiler_params=pltpu.CompilerParams(
            dimension_semantics=("parallel", "arbitrary"), vmem_limit_bytes=VMEM_LIMIT),
        name="prompt_mixer",
    )(proj3, *params)


def _pad_rows_t(x):
    pad = jnp.zeros((LANES - x.shape[0], x.shape[1]), F32)
    return jnp.concatenate([x, pad], axis=0).T


def _sample_mixer_kernel(layer, pos0, n_acc, proj_ref, rgc_ref, rgh_ref, sc_ref, ssd_ref, hg_ref, *rest):
    (caw, cab, wg, bg, lam, cbw, cbb, dtb, alog, dexp, snorm,
     hlb, hnorm, expand, headsum, headmean,
     ymix_ref, rgc_o, rgh_o, sc_o, ssd_o, hg_o) = rest[n_acc:]
    bb = DEC_BB
    if n_acc == 0:
        for other in range(DEPTH):
            if other != layer:
                ssd_o[other] = jnp.zeros(ssd_o.shape[1:], F32)
                hg_o[other] = jnp.zeros(hg_o.shape[1:], F32)
        ssd_o = ssd_o.at[layer]
        hg_o = hg_o.at[layer]
    ax = proj_ref[:, OFF_AX:OFF_AX + RG_WIDTH]
    xa = cab[...] + caw[CONV_W - 1:CONV_W, :] * ax
    for k in range(CONV_W - 1):
        xa = xa + caw[k:k + 1, :] * rgc_ref[:, k * RG_WIDTH:(k + 1) * RG_WIDTH]
    rgc_o[:, 0:2 * RG_WIDTH] = rgc_ref[:, RG_WIDTH:3 * RG_WIDTH]
    rgc_o[:, 2 * RG_WIDTH:] = ax
    a, mult, gi = _rg_coeffs(_dot(_bf(xa), wg[...]) + bg[...], lam[...])
    if pos0 == 0:
        mult = jnp.ones_like(mult)
    h = a * rgh_ref[...] + mult * (gi * xa)
    rgh_o[...] = h
    ymix_ref[:, 0:RG_WIDTH] = h * _gelu_tanh(proj_ref[:, OFF_AG:OFF_AG + RG_WIDTH])

    bx = proj_ref[:, OFF_XBC:OFF_XBC + SSD_CONV_DIM]
    xbc = cbb[...] + cbw[CONV_W - 1:CONV_W, :] * bx
    for k in range(CONV_W - 1):
        xbc = xbc + cbw[k:k + 1, :] * sc_ref[:, k * SSD_CONV_DIM:(k + 1) * SSD_CONV_DIM]
    sc_o[:, 0:2 * SSD_CONV_DIM] = sc_ref[:, SSD_CONV_DIM:3 * SSD_CONV_DIM]
    sc_o[:, 2 * SSD_CONV_DIM:] = bx
    xbc = _silu(xbc)
    xs = xbc[:, :SSD_WIDTH]
    bm = xbc[:, SSD_WIDTH:SSD_WIDTH + SSD_STATE]
    cm = xbc[:, SSD_WIDTH + SSD_STATE:]
    dtp = _softplus(proj_ref[:, OFF_DT:OFF_DT + LANES] + dtb[...])
    v_da = dtp * (-jnp.exp(alog[...]))
    dt_e = _dot_x_c(dtp, expand[...])
    e_e = jnp.exp(_dot_x_c(v_da, expand[...]))
    dtx = dt_e * xs
    cbs = _dot(_bf(cm * bm), jnp.ones((SSD_STATE, LANES), BF16))[:, 0:1]
    dtx_t = _split3(_pad_rows_t(dtx))
    e_t = _split3(_pad_rows_t(e_e))
    c_pad = _bf(jnp.concatenate([cm, jnp.zeros((LANES - bb, SSD_STATE), F32)], axis=0))
    rowi = lax.broadcasted_iota(jnp.int32, (LANES, LANES), 0)
    lane_w = lax.broadcasted_iota(jnp.int32, (SSD_WIDTH, LANES), 1)
    y_t = jnp.zeros((SSD_WIDTH, LANES), F32)
    for j in range(bb):
        pick = jnp.where(rowi == j, 1.0, 0.0).astype(BF16)
        d_col = _dot(dtx_t[0], pick) + _dot(dtx_t[1], pick) + _dot(dtx_t[2], pick)
        e_col = _dot(e_t[0], pick) + _dot(e_t[1], pick) + _dot(e_t[2], pick)
        s_old = ssd_ref[j].reshape(SSD_WIDTH, SSD_STATE)
        ssd_o[j] = (e_col * s_old + d_col * bm[j:j + 1, :]).reshape(
            SSD_HEADS, SSD_HEAD_DIM, SSD_STATE)
        y_t = y_t + jnp.where(lane_w == j, _dot_nt(_bf(s_old), c_pad), 0.0)
    y = cbs * dtx + y_t.T[0:bb, :] * e_e
    yb = y + dexp[...] * xs
    ymix_ref[:, RG_WIDTH:RG_WIDTH + SSD_WIDTH] = _rms(
        yb * _silu(proj_ref[:, OFF_BZ:OFF_BZ + SSD_WIDTH]), snorm[...])

    q = _silu(proj_ref[:, OFF_CQ:OFF_CQ + HG_WIDTH])
    lb = _lower_bound_row(hlb[...], layer)
    gt = lb + (1.0 - lb) * _sigmoid(proj_ref[:, OFF_CF:OFF_CF + HG_WIDTH])
    kk = 1.0 - gt
    fdec = jnp.exp(jnp.log(gt))
    vv = proj_ref[:, OFF_CI:OFF_CI + HG_WIDTH]
    att = _dot(_bf(q * kk), headsum[...])
    qhat = _bf(q * fdec)
    f_t = _split3(_pad_rows_t(fdec))
    k_t = _split3(_pad_rows_t(kk))
    rowb = lax.broadcasted_iota(jnp.int32, (bb, HG_WIDTH), 0)
    o_acc = jnp.zeros((bb, HG_WIDTH), F32)
    for j in range(bb):
        pick = jnp.where(rowi == j, 1.0, 0.0).astype(BF16)
        f_col = _dot(f_t[0], pick) + _dot(f_t[1], pick) + _dot(f_t[2], pick)
        k_col = _dot(k_t[0], pick) + _dot(k_t[1], pick) + _dot(k_t[2], pick)
        r_parts = []
        for hd in range(HG_HEADS):
            lo = hd * HG_HEAD_DIM
            s_old = hg_ref[j, hd]
            v_row = vv[j:j + 1, lo:lo + HG_HEAD_DIM]
            hg_o[j, hd] = (f_col[lo:lo + HG_HEAD_DIM, 0:HG_HEAD_DIM] * s_old
                           + k_col[lo:lo + HG_HEAD_DIM, 0:HG_HEAD_DIM] * v_row)
            r_parts.append(_dot(qhat[:, lo:lo + HG_HEAD_DIM], _bf(s_old)))
        o_acc = o_acc + jnp.where(rowb == j, jnp.concatenate(r_parts, axis=1), 0.0)
    o = att * vv + o_acc
    ms = _dot(_bf(o * o), headmean[...])
    ymix_ref[:, RG_WIDTH + SSD_WIDTH:] = (
        o * lax.rsqrt(ms + EPS) * hnorm[...] * _silu(proj_ref[:, OFF_CG:OFF_CG + HG_WIDTH]))


def _sample_mixer(layer, proj, rgc, rgh, sc, ssd_all, hg_all, acc, mp, consts):
    nb = proj.shape[0]
    bb = DEC_BB
    params = [mp["caw"], mp["cab"], mp["wg"], mp["bg"], mp["lam"], mp["cbw"], mp["cbb"],
              mp["dtb"], mp["alog"], mp["dexp"], mp["snorm"], mp["hlb"], mp["hnorm"],
              consts["expand"], consts["headsum"], consts["headmean"]]
    row2 = lambda w: pl.BlockSpec((bb, w), lambda i: (i, 0))
    small_specs = [row2(3 * RG_WIDTH), row2(RG_WIDTH), row2(3 * SSD_CONV_DIM)]
    big_specs = [
        pl.BlockSpec((None, bb, SSD_HEADS, SSD_HEAD_DIM, SSD_STATE), lambda i: (layer, i, 0, 0, 0)),
        pl.BlockSpec((None, bb, HG_HEADS, HG_HEAD_DIM, HG_HEAD_DIM), lambda i: (layer, i, 0, 0, 0)),
    ]
    out_shape = (
        jax.ShapeDtypeStruct((nb, D_MIX), F32),
        jax.ShapeDtypeStruct((nb, 3 * RG_WIDTH), F32),
        jax.ShapeDtypeStruct((nb, RG_WIDTH), F32),
        jax.ShapeDtypeStruct((nb, 3 * SSD_CONV_DIM), F32),
        jax.ShapeDtypeStruct(ssd_all.shape, F32),
        jax.ShapeDtypeStruct(hg_all.shape, F32),
    )
    n_acc = len(acc)
    n_lead = 6
    big_out_specs = big_specs if n_acc else [
        pl.BlockSpec((DEPTH, bb, SSD_HEADS, SSD_HEAD_DIM, SSD_STATE), lambda i: (0, i, 0, 0, 0)),
        pl.BlockSpec((DEPTH, bb, HG_HEADS, HG_HEAD_DIM, HG_HEAD_DIM), lambda i: (0, i, 0, 0, 0)),
    ]
    return pl.pallas_call(
        functools.partial(_sample_mixer_kernel, layer, PAST_LEN, n_acc),
        grid=(nb // bb,),
        in_specs=[row2(NPROJ)] + small_specs + big_specs
        + [pl.BlockSpec(memory_space=pl.ANY)] * n_acc + [_const_spec(p) for p in params],
        out_specs=tuple([row2(D_MIX)] + small_specs + big_out_specs),
        out_shape=out_shape,
        input_output_aliases={n_lead + k: 4 + k for k in range(n_acc)},
        compiler_params=pltpu.CompilerParams(
            dimension_semantics=("parallel",), vmem_limit_bytes=VMEM_LIMIT),
        name="sample_mixer",
    )(proj, rgc, rgh, sc, ssd_all, hg_all, *acc, *params)


def _post_kernel(final, h_ref, y_ref, p_ref, wo, nffn, wup, wdown, nple, wgate, wproj, nfin, o_ref):
    h = h_ref[...] + _dot(_bf(y_ref[...]), wo[...])
    hn = _bf(_rms(h, nffn[...]))
    for j in range(D_FF // FF_CHUNK):
        cs = slice(j * FF_CHUNK, (j + 1) * FF_CHUNK)
        z = jnp.square(jnp.maximum(_dot(hn, wup[:, cs]), 0.0))
        h = h + _dot(_bf(z), wdown[cs, :])
    gate = _sigmoid(_dot(_bf(_rms(h, nple[...])), wgate[...]))
    h = h + gate * _dot(_bf(p_ref[...]), wproj[...])
    if final:
        h = _rms(h, nfin[...])
    o_ref[...] = h


def _post(final, layer, h2d, ymix2d, p_all, lp, nfin, tm):
    m = h2d.shape[0]
    params = [lp["wo"], lp["nffn"], lp["wup"], lp["wdown"], lp["nple"], lp["wgate"], lp["wproj"], nfin]
    return pl.pallas_call(
        functools.partial(_post_kernel, final),
        grid=(m // tm,),
        in_specs=[pl.BlockSpec((tm, D_MODEL), lambda i: (i, 0)),
                  pl.BlockSpec((tm, D_MIX), lambda i: (i, 0)),
                  pl.BlockSpec((None, tm, PLE_DIM), lambda i: (layer, i, 0))]
        + [_const_spec(p) for p in params],
        out_specs=pl.BlockSpec((tm, D_MODEL), lambda i: (i, 0)),
        out_shape=jax.ShapeDtypeStruct((m, D_MODEL), F32),
        compiler_params=pltpu.CompilerParams(
            dimension_semantics=("parallel",), vmem_limit_bytes=VMEM_LIMIT),
        name="post_mixer",
    )(h2d, ymix2d, p_all, *params)


def _constants():
    t = np.arange(TC)
    ltri = (t[None, :] <= t[:, None]).astype(np.float32)
    utri = ltri.T.copy()
    expand = np.zeros((LANES, SSD_WIDTH), np.float32)
    for hd in range(SSD_HEADS):
        expand[hd, hd * SSD_HEAD_DIM:(hd + 1) * SSD_HEAD_DIM] = 1.0
    emat = np.zeros(((N_LEVELS + 1) * TC, TC), np.float32)
    for lev in range(1, N_LEVELS + 1):
        blk = 1 << lev
        mid = (t // blk) * blk + blk // 2
        upper = t >= mid
        r0 = (lev - 1) * TC
        emat[r0:r0 + TC] = (
            (upper[:, None] & (t[None, :] >= mid[:, None]) & (t[None, :] <= t[:, None]))
            | ((~upper)[:, None] & (t[None, :] > t[:, None]) & (t[None, :] < mid[:, None])))
    emat[N_LEVELS * TC:] = ltri
    cmask = np.zeros((N_LEVELS + 1, TC, 2 * TC), np.float32)
    cmask[0] = np.tile(np.eye(TC, dtype=np.float32), (1, 2))
    for lev in range(1, N_LEVELS + 1):
        blk = 1 << lev
        up = (t % blk) >= blk // 2
        m = up[:, None] & (~up)[None, :] & ((t // blk)[:, None] == (t // blk)[None, :])
        cmask[lev] = np.tile(m.astype(np.float32), (1, 2))
    hidx = np.arange(HG_WIDTH) // HG_HEAD_DIM
    bd = (hidx[:, None] == hidx[None, :]).astype(np.float32)
    pair = bd[:LANES, :LANES]
    return {
        "ltri3": jnp.asarray(np.tile(ltri, (1, 3)), BF16),
        "utri3": jnp.asarray(np.tile(utri, (3, 1)), BF16),
        "expand": jnp.asarray(expand, BF16),
        "expand3": jnp.asarray(np.tile(expand, (3, 1)), BF16),
        "emat2": jnp.asarray(np.tile(emat, (1, 2)), BF16),
        "cmask": jnp.asarray(cmask, F32),
        "pairmask": jnp.asarray(pair, F32),
        "pairmean": jnp.asarray(pair / HG_HEAD_DIM, BF16),
        "headsum": jnp.asarray(bd, BF16), "headmean": jnp.asarray(bd / HG_HEAD_DIM, BF16),
    }


def _block_diag(w):
    hh, d, _ = w.shape
    eye = jnp.eye(hh, dtype=w.dtype)
    return (eye[:, None, :, None] * w[:, :, None, :]).reshape(hh * d, hh * d)


def _pair_gate_weights(wa, wx):
    hh, d, _ = wa.shape
    pa = jnp.stack([_block_diag(wa[2 * g:2 * g + 2]) for g in range(hh // 2)])
    px = jnp.stack([_block_diag(wx[2 * g:2 * g + 2]) for g in range(hh // 2)])
    return jnp.concatenate([pa, px], axis=2)


def _pad_lanes(v):
    return jnp.pad(v, (0, LANES - v.shape[0]))[None, :]


def kernel(x_prompt, x_sample, state_rg_conv, state_rg_h, state_ssd_conv, state_ssd, state_hgrn,
           p_prompt, p_sample, norm_mix, w_in, conv_a_w, conv_a_b, rg_wa, rg_ba, rg_wx, rg_bx,
           rg_lambda, conv_b_w, conv_b_b, ssd_dt_bias, ssd_a_log, ssd_d, ssd_norm,
           hg_lower_bounds, hg_norm, w_out, norm_ffn, w_up, w_down, norm_ple, w_ple_gate,
           w_ple_proj, norm_final):
    bsz, seq, _ = x_prompt.shape
    nb = x_sample.shape[0]
    consts = _constants()
    nfin = norm_final[None, :]
    pp_all = p_prompt.reshape(DEPTH, bsz * seq, PLE_DIM)
    ps_all = p_sample.reshape(DEPTH, nb, PLE_DIM)

    hp = x_prompt.reshape(bsz * seq, D_MODEL)
    hs = x_sample.reshape(nb, D_MODEL)
    st_p = [[] for _ in range(5)]
    st_s = [[] for _ in range(3)]
    acc = ()
    for i in range(DEPTH):
        w_re = _prep_w_in(w_in, i, 128)
        gmix = norm_mix[i][None, :]
        mp = {
            "caw": conv_a_w[i], "cab": conv_a_b[i][None, :],
            "wg": jnp.concatenate([_block_diag(rg_wa[i]), _block_diag(rg_wx[i])], axis=1).astype(BF16),
            "bg": jnp.concatenate([rg_ba[i], rg_bx[i]])[None, :],
            "wgp": _pair_gate_weights(rg_wa[i], rg_wx[i]).astype(BF16),
            "bgp": jnp.concatenate([rg_ba[i].reshape(NGRP, 1, LANES),
                                    rg_bx[i].reshape(NGRP, 1, LANES)], axis=2),
            "lam": rg_lambda[i][None, :],
            "cbw": conv_b_w[i], "cbb": conv_b_b[i][None, :],
            "dtb": _pad_lanes(ssd_dt_bias[i]), "alog": _pad_lanes(ssd_a_log[i]),
            "dexp": jnp.repeat(ssd_d[i], SSD_HEAD_DIM)[None, :],
            "snorm": ssd_norm[i][None, :],
            "hlb": hg_lower_bounds, "hnorm": jnp.tile(hg_norm[i], HG_HEADS)[None, :],
        }
        lp = {
            "wo": _cast_bf16(w_out, i, 512), "nffn": norm_ffn[i][None, :],
            "wup": _cast_bf16(w_up, i, 256), "wdown": _cast_bf16(w_down, i, 1024),
            "nple": norm_ple[i][None, :], "wgate": _cast_bf16(w_ple_gate, i, 512),
            "wproj": _cast_bf16(w_ple_proj, i, 256),
        }
        final = i == DEPTH - 1

        proj_p = _in_proj(hp, gmix, w_re, TM_PROMPT).reshape(bsz, seq, NPROJ)
        ymix_p, rgc, rgh, sc, ssd, hg = _prompt_mixer(i, proj_p, mp, consts)
        hp = _post(final, i, hp, ymix_p.reshape(bsz * seq, D_MIX), pp_all, lp, nfin, TM_PROMPT)
        for lst, s in zip(st_p, (rgc, rgh.reshape(bsz, RG_WIDTH), sc,
                                 ssd.reshape(bsz, SSD_HEADS, SSD_HEAD_DIM, SSD_STATE), hg)):
            lst.append(s)

        proj_s = _in_proj(hs, gmix, w_re, nb)
        ymix_s, rgc, rgh, sc, ssd_acc, hg_acc = _sample_mixer(
            i, proj_s,
            state_rg_conv[i].reshape(nb, 3 * RG_WIDTH), state_rg_h[i],
            state_ssd_conv[i].reshape(nb, 3 * SSD_CONV_DIM),
            state_ssd, state_hgrn, acc, mp, consts)
        acc = (ssd_acc, hg_acc)
        hs = _post(final, i, hs, ymix_s, ps_all, lp, nfin, nb)
        for lst, s in zip(st_s, (rgc.reshape(nb, CONV_W - 1, RG_WIDTH), rgh,
                                 sc.reshape(nb, CONV_W - 1, SSD_CONV_DIM))):
            lst.append(s)

    y_prompt = hp.reshape(bsz, seq, D_MODEL)
    y_sample = hs.reshape(nb, 1, D_MODEL)
    outs_p = [jnp.stack(lst) for lst in st_p]
    outs_s = [jnp.stack(lst) for lst in st_s] + list(acc)
    return (y_prompt, y_sample, *outs_p, *outs_s)
```

```python
import functools

import numpy as np
import jax
import jax.numpy as jnp
from jax import lax
from jax.experimental import pallas as pl
from jax.experimental.pallas import tpu as pltpu

F32 = jnp.float32
BF16 = jnp.bfloat16

D_MODEL = 1024
DEPTH = 2
PAST_LEN = 16384
PLE_DIM = 256
D_FF = 4 * D_MODEL
CONV_W = 4
EPS = 1e-6
RG_WIDTH = 512
RG_HEADS = 8
RG_HEAD_DIM = 64
RG_C = 8.0
SSD_WIDTH = 512
SSD_HEAD_DIM = 64
SSD_HEADS = 8
SSD_STATE = 128
SSD_CONV_DIM = SSD_WIDTH + 2 * SSD_STATE
HG_WIDTH = 512
HG_HEADS = 8
HG_HEAD_DIM = 64
D_MIX = RG_WIDTH + SSD_WIDTH + HG_WIDTH

LOG2E = 1.4426950408889634
LANES = 128
SUBLANES = 8
NGRP = RG_WIDTH // LANES

OFF_AX = 0
OFF_AG = 512
OFF_BZ = 1024
OFF_XBC = 1536
OFF_CQ = 2304
OFF_CF = 2816
OFF_CI = 3328
OFF_CG = 3840
OFF_DT = 4352
NPROJ = OFF_DT + LANES
DT_COL = 2 * RG_WIDTH + SSD_WIDTH + SSD_CONV_DIM
D_IN_PROJ = DT_COL + SSD_HEADS + 4 * HG_WIDTH

TC = 128
HALF = TC // 2
N_LEVELS = 7
DEC_BB = 8
TM_PROMPT = 512
FF_CHUNK = 1024
VMEM_LIMIT = 52 * 1024 * 1024


def _bf(x):
    return x.astype(BF16)


def _dot(a, b):
    return jnp.dot(a, b, preferred_element_type=F32)


def _dot_nt(a, b):
    return lax.dot_general(a, b, (((1,), (1,)), ((), ())), preferred_element_type=F32)


def _split3(x):
    hi = _bf(x)
    r1 = x - hi.astype(F32)
    mid = _bf(r1)
    lo = _bf(r1 - mid.astype(F32))
    return hi, mid, lo


def _dot_x_c(x, c):
    hi, mid, lo = _split3(x)
    return _dot(hi, c) + _dot(mid, c) + _dot(lo, c)


def _dot_x_c3(x, c3):
    return _dot(jnp.concatenate(_split3(x), axis=1), c3)


def _dot_c3_x(c3, x):
    return _dot(c3, jnp.concatenate(_split3(x), axis=0))


def _sigmoid(x):
    return 0.5 * jnp.tanh(0.5 * x) + 0.5


def _silu(x):
    h = 0.5 * x
    return h * jnp.tanh(h) + h


def _softplus(x):
    return jnp.maximum(x, 0.0) + jnp.log1p(jnp.exp(-jnp.abs(x)))


def _gelu_tanh(x):
    c = np.float32(np.sqrt(2.0 / np.pi))
    return 0.5 * x * (1.0 + jnp.tanh(c * (x + 0.044715 * (x * x * x))))


def _rms(x, gain):
    return x * lax.rsqrt(jnp.mean(x * x, axis=-1, keepdims=True) + EPS) * gain


def _lower_bound_row(hlb, layer):
    m = jnp.max(hlb, axis=0, keepdims=True)
    e = jnp.exp(hlb - m)
    sm = e / jnp.sum(e, axis=0, keepdims=True)
    lb = jnp.zeros((1, HG_WIDTH), F32)
    for j in range(1, layer + 1):
        lb = lb + sm[j:j + 1, :]
    return lb


def _rg_coeffs(gates, lam_row):
    w = gates.shape[1] // 2
    r = _sigmoid(gates[:, :w])
    i = _sigmoid(gates[:, w:])
    log_a = -RG_C * r * _softplus(-lam_row)
    a = jnp.exp(log_a)
    mult = jnp.sqrt(-jnp.tanh(log_a) * (a * a + 1.0))
    return a, mult, i


def _lane_halves(x):
    lane = lax.broadcasted_iota(jnp.int32, x.shape, 1)
    lo = jnp.where(lane < HG_HEAD_DIM, x, 0.0)
    hi = jnp.where(lane >= HG_HEAD_DIM, x, 0.0)
    return jnp.concatenate([lo, hi], axis=0)


def _const_spec(arr):
    nd = arr.ndim
    return pl.BlockSpec(arr.shape, lambda *_: (0,) * nd, pipeline_mode=pl.Buffered(1))


def _cast_kernel(w_ref, o_ref):
    o_ref[...] = _bf(w_ref[...])


def _cast_bf16(w_all, layer, rb):
    _, rows, cols = w_all.shape
    return pl.pallas_call(
        _cast_kernel,
        grid=(rows // rb,),
        in_specs=[pl.BlockSpec((None, rb, cols), lambda i: (layer, i, 0))],
        out_specs=pl.BlockSpec((rb, cols), lambda i: (i, 0)),
        out_shape=jax.ShapeDtypeStruct((rows, cols), BF16),
        compiler_params=pltpu.CompilerParams(dimension_semantics=("parallel",)),
        name="cast_bf16",
    )(w_all)


def _win_kernel(w_ref, o_ref):
    rows = w_ref.shape[0]
    o_ref[:, 0:OFF_CQ] = _bf(w_ref[:, 0:DT_COL])
    o_ref[:, OFF_CQ:OFF_DT] = _bf(w_ref[:, DT_COL + SSD_HEADS:D_IN_PROJ])
    dt = jnp.concatenate([w_ref[:, DT_COL:DT_COL + SSD_HEADS],
                          jnp.zeros((rows, LANES - SSD_HEADS), F32)], axis=1)
    o_ref[:, OFF_DT:NPROJ] = _bf(dt)


def _prep_w_in(w_all, layer, rb):
    return pl.pallas_call(
        _win_kernel,
        grid=(D_MODEL // rb,),
        in_specs=[pl.BlockSpec((None, rb, D_IN_PROJ), lambda i: (layer, i, 0))],
        out_specs=pl.BlockSpec((rb, NPROJ), lambda i: (i, 0)),
        out_shape=jax.ShapeDtypeStruct((D_MODEL, NPROJ), BF16),
        compiler_params=pltpu.CompilerParams(dimension_semantics=("parallel",)),
        name="prep_w_in",
    )(w_all)


def _proj_kernel(x_ref, g_ref, w_ref, o_ref):
    u = _rms(x_ref[...], g_ref[...])
    o_ref[...] = _dot(_bf(u), w_ref[...])


def _in_proj(x2d, gain, w_bf, tm):
    m = x2d.shape[0]
    return pl.pallas_call(
        _proj_kernel,
        grid=(m // tm,),
        in_specs=[pl.BlockSpec((tm, D_MODEL), lambda i: (i, 0)),
                  _const_spec(gain), _const_spec(w_bf)],
        out_specs=pl.BlockSpec((tm, NPROJ), lambda i: (i, 0)),
        out_shape=jax.ShapeDtypeStruct((m, NPROJ), F32),
        compiler_params=pltpu.CompilerParams(
            dimension_semantics=("parallel",), vmem_limit_bytes=VMEM_LIMIT),
        name="in_proj",
    )(x2d, gain, w_bf)


def _prompt_mixer_kernel(layer, pos0,
                         proj_ref, caw, cab, wgp, bgp, lam, cbw, cbb, dtb, alog, dexp, snorm,
                         hlb, hnorm, ltri3, utri3, expand3, emat2, cmask, pairmask, pairmean,
                         ymix_ref, rgc_o, rgh_o, sc_o, ssd_o, hg_o,
                         xea, xeb, hcar, s_t, st_hg):
    c = pl.program_id(1)
    last = pl.num_programs(1) - 1

    @pl.when(c == 0)
    def _init():
        xea[...] = jnp.zeros_like(xea)
        xeb[...] = jnp.zeros_like(xeb)
        hcar[...] = jnp.zeros_like(hcar)
        s_t[...] = jnp.zeros_like(s_t)
        st_hg[...] = jnp.zeros_like(st_hg)

    row_h = lax.broadcasted_iota(jnp.int32, (HALF, 1), 0)
    sub_h = row_h & (SUBLANES - 1)
    first_head = lax.broadcasted_iota(jnp.int32, (HALF, LANES), 1) < HG_HEAD_DIM
    first_head_tc = lax.broadcasted_iota(jnp.int32, (TC, LANES), 1) < HG_HEAD_DIM
    col_tc = lax.broadcasted_iota(jnp.int32, (HALF, TC), 1)
    row_tc = lax.broadcasted_iota(jnp.int32, (HALF, TC), 0)

    tiles = [slice(t * HALF, (t + 1) * HALF) for t in range(TC // HALF)]

    def lanes(off, g):
        return slice(off + g * LANES, off + (g + 1) * LANES)

    def conv_rows(prev8, x, w_ref, b_ref, gs):
        xcat = jnp.concatenate([prev8, x], axis=0)
        y = b_ref[:, gs] + w_ref[CONV_W - 1:CONV_W, gs] * x
        for j in range(1, CONV_W):
            y = y + w_ref[CONV_W - 1 - j:CONV_W - j, gs] * pltpu.roll(xcat, j, 0)[SUBLANES:, :]
        return y, x[x.shape[0] - SUBLANES:, :]

    def conv(tail_ref, w_ref, b_ref, src_off, g):
        gs = lanes(0, g)
        y, tail = conv_rows(tail_ref[:, gs], proj_ref[0, :, lanes(src_off, g)], w_ref, b_ref, gs)
        tail_ref[:, gs] = tail
        return y

    def mixer_a(g):
        gs = lanes(0, g)
        prev8 = xea[:, gs]
        carry = hcar[0:1, gs]
        for t, rs in enumerate(tiles):
            xa, prev8 = conv_rows(prev8, proj_ref[0, rs, lanes(OFF_AX, g)], caw, cab, gs)
            a, mult, gi = _rg_coeffs(_dot(_bf(xa), wgp[g]) + bgp[g], lam[:, gs])
            mult = jnp.where(row_h + (c * TC + t * HALF + pos0) == 0, 1.0, mult)
            b = mult * (gi * xa)
            k = 1
            while k < SUBLANES:
                keep = sub_h >= k
                a_sh = jnp.where(keep, pltpu.roll(a, k, 0), 1.0)
                b_sh = jnp.where(keep, pltpu.roll(b, k, 0), 0.0)
                b = b + a * b_sh
                a = a * a_sh
                k *= 2
            slabs = []
            for r in range(HALF // SUBLANES):
                sl = slice(r * SUBLANES, (r + 1) * SUBLANES)
                h_r = a[sl, :] * carry + b[sl, :]
                slabs.append(h_r)
                carry = h_r[SUBLANES - 1:SUBLANES, :]
            ymix_ref[0, rs, gs] = jnp.concatenate(slabs, axis=0) * _gelu_tanh(
                proj_ref[0, rs, lanes(OFF_AG, g)])
        xea[:, gs] = prev8
        hcar[0:1, gs] = carry

    bm = _silu(conv(xeb, cbw, cbb, OFF_XBC, NGRP))
    cm = _silu(conv(xeb, cbw, cbb, OFF_XBC, NGRP + 1))
    dtp = _softplus(proj_ref[0, :, OFF_DT:OFF_DT + LANES] + dtb[...])
    v_da = dtp * (-LOG2E * jnp.exp(alog[...]))
    cum = _dot_c3_x(ltri3[...], v_da)
    cum_t = _dot_x_c3(v_da.T, utri3[...])
    dt_e = _dot_x_c3(dtp, expand3[...])
    cum_e = _dot_x_c3(cum, expand3[...])
    cb = _dot_nt(_bf(cm), _bf(bm))
    cmb = _bf(cm)
    bmt = _bf(bm.T)
    ssq_parts = []

    def mixer_b(g):
        gs = lanes(0, g)
        cum_last = cum_e[TC - 1:TC, gs]
        s_old = _bf(s_t[g])
        prev8 = xeb[:, gs]
        xs_t, x2a, x2b, wx = [], [], [], []
        for rs in tiles:
            xs, prev8 = conv_rows(prev8, proj_ref[0, rs, lanes(OFF_XBC, g)], cbw, cbb, gs)
            xs = _silu(xs)
            dt_g = dt_e[rs, gs]
            dtx = xs * dt_g
            xs_t.append(xs)
            x2a.append(_bf(jnp.where(first_head, dtx, 0.0)))
            x2b.append(_bf(jnp.where(first_head, 0.0, dtx)))
            wx.append(_bf(jnp.exp2(cum_last - cum_e[rs, gs]) * dt_g * xs))
        xeb[:, gs] = prev8
        x2 = jnp.concatenate(x2a + x2b, axis=0)
        for t, rs in enumerate(tiles):
            gmats = []
            for hh in range(2):
                hd = 2 * g + hh
                seg = cum[rs, hd:hd + 1] - cum_t[hd:hd + 1, :]
                gmats.append(_bf(cb[rs, :] * jnp.exp2(jnp.where(col_tc <= row_tc + t * HALF, seg, -jnp.inf))))
            y = _dot(jnp.concatenate(gmats, axis=1), x2)
            y = y + _dot(cmb[rs, :], s_old) * jnp.exp2(cum_e[rs, gs])
            yz = (y + dexp[:, gs] * xs_t[t]) * _silu(proj_ref[0, rs, lanes(OFF_BZ, g)])
            ssq_parts.append(yz * yz)
            ymix_ref[0, rs, lanes(RG_WIDTH, g)] = yz
        s_t[g] = jnp.exp2(cum_last) * s_t[g] + _dot(bmt, jnp.concatenate(wx, axis=0))

    lb_all = _lower_bound_row(hlb[...], layer)

    def forget_gate(g):
        lb = lb_all[:, g * LANES:(g + 1) * LANES]
        return lb + (1.0 - lb) * _sigmoid(proj_ref[0, :, OFF_CF + g * LANES:OFF_CF + (g + 1) * LANES])

    def mixer_c(g, gt, xall):
        gs = lanes(0, g)
        b_off = N_LEVELS * TC
        bcum = xall[b_off:b_off + TC, :]
        blast = bcum[TC - 1:TC, :]
        q = _silu(proj_ref[0, :, lanes(OFF_CQ, g)])
        kk = 1.0 - gt
        vv = proj_ref[0, :, lanes(OFF_CI, g)]
        zeros = jnp.zeros((HG_HEAD_DIM, TC), BF16)

        def keys_by_head(kt):
            ktt = _bf(kt.T)
            return jnp.concatenate(
                [jnp.concatenate([ktt[:HG_HEAD_DIM, :], zeros], axis=1),
                 jnp.concatenate([zeros, ktt[HG_HEAD_DIM:, :]], axis=1)], axis=0)

        att = _dot(_bf(q), keys_by_head(kk)) * cmask[0]
        for lev in range(1, N_LEVELS + 1):
            e = jnp.exp2(xall[(lev - 1) * TC:lev * TC, :])
            att = att + _dot(_bf(q * e), keys_by_head(kk * e)) * cmask[lev]
        v2 = jnp.concatenate([_bf(jnp.where(first_head_tc, vv, 0.0)),
                              _bf(jnp.where(first_head_tc, 0.0, vv))], axis=0)
        o = _dot(_bf(att), v2) + _dot_nt(_bf(q * jnp.exp2(bcum)), _bf(st_hg[g]))
        khat = kk * jnp.exp2(blast - bcum)
        st_hg[g] = jnp.exp2(blast) * st_hg[g] + pairmask[...] * _dot(_bf(vv.T), _bf(khat))
        ms = _dot(_bf(o * o), pairmean[...])
        ymix_ref[0, :, lanes(RG_WIDTH + SSD_WIDTH, g)] = (
            o * lax.rsqrt(ms + EPS) * hnorm[:, gs] * _silu(proj_ref[0, :, lanes(OFF_CG, g)]))

    for pair in range(NGRP // 2):
        g0, g1 = 2 * pair, 2 * pair + 1
        gts = [forget_gate(g0), forget_gate(g1)]
        lf = jnp.log(jnp.concatenate(gts, axis=1)) * LOG2E
        hi = _bf(lf)
        lo = _bf(lf - hi.astype(F32))
        xall2 = _dot(emat2[...], jnp.concatenate([hi, lo], axis=0))
        for g, gt in zip((g0, g1), gts):
            mixer_a(g)
            mixer_b(g)
            mixer_c(g, gt, xall2[:, (g - g0) * LANES:(g - g0 + 1) * LANES])

    for t, rs in enumerate(tiles):
        ssq = ssq_parts[t]
        for g in range(1, NGRP):
            ssq = ssq + ssq_parts[g * len(tiles) + t]
        rinv = lax.rsqrt(jnp.sum(ssq, axis=-1, keepdims=True) * (1.0 / SSD_WIDTH) + EPS)
        for g in range(NGRP):
            ys = lanes(RG_WIDTH, g)
            ymix_ref[0, rs, ys] = ymix_ref[0, rs, ys] * rinv * snorm[:, lanes(0, g)]

    @pl.when(c == last)
    def _emit():
        rgc_o[0] = xea[SUBLANES - 3:SUBLANES, :]
        sc_o[0] = xeb[SUBLANES - 3:SUBLANES, :]
        rgh_o[0] = hcar[0:1, :]
        for g in range(NGRP):
            ssd_o[0, g * LANES:(g + 1) * LANES, :] = s_t[g].T
            s_pair = st_hg[g].T
            for hh in range(2):
                lo = hh * HG_HEAD_DIM
                hg_o[0, 2 * g + hh] = s_pair[lo:lo + HG_HEAD_DIM, lo:lo + HG_HEAD_DIM]


def _prompt_mixer(layer, proj3, mp, consts):
    bsz, seq, _ = proj3.shape
    nc = seq // TC
    params = [mp["caw"], mp["cab"], mp["wgp"], mp["bgp"], mp["lam"], mp["cbw"], mp["cbb"],
              mp["dtb"], mp["alog"], mp["dexp"], mp["snorm"], mp["hlb"], mp["hnorm"],
              consts["ltri3"], consts["utri3"], consts["expand3"], consts["emat2"],
              consts["cmask"], consts["pairmask"], consts["pairmean"]]
    out_shape = (
        jax.ShapeDtypeStruct((bsz, seq, D_MIX), F32),
        jax.ShapeDtypeStruct((bsz, CONV_W - 1, RG_WIDTH), F32),
        jax.ShapeDtypeStruct((bsz, 1, RG_WIDTH), F32),
        jax.ShapeDtypeStruct((bsz, CONV_W - 1, SSD_CONV_DIM), F32),
        jax.ShapeDtypeStruct((bsz, SSD_WIDTH, SSD_STATE), F32),
        jax.ShapeDtypeStruct((bsz, HG_HEADS, HG_HEAD_DIM, HG_HEAD_DIM), F32),
    )
    out_specs = (
        pl.BlockSpec((1, TC, D_MIX), lambda b, c: (b, c, 0)),
        pl.BlockSpec((1, CONV_W - 1, RG_WIDTH), lambda b, c: (b, 0, 0)),
        pl.BlockSpec((1, 1, RG_WIDTH), lambda b, c: (b, 0, 0)),
        pl.BlockSpec((1, CONV_W - 1, SSD_CONV_DIM), lambda b, c: (b, 0, 0)),
        pl.BlockSpec((1, SSD_WIDTH, SSD_STATE), lambda b, c: (b, 0, 0)),
        pl.BlockSpec((1, HG_HEADS, HG_HEAD_DIM, HG_HEAD_DIM), lambda b, c: (b, 0, 0, 0)),
    )
    scratch = [
        pltpu.VMEM((SUBLANES, RG_WIDTH), F32),
        pltpu.VMEM((SUBLANES, SSD_CONV_DIM), F32),
        pltpu.VMEM((SUBLANES, RG_WIDTH), F32),
        pltpu.VMEM((NGRP, SSD_STATE, LANES), F32),
        pltpu.VMEM((NGRP, LANES, LANES), F32),
    ]
    return pl.pallas_call(
        functools.partial(_prompt_mixer_kernel, layer, 0),
        grid=(bsz, nc),
        in_specs=[pl.BlockSpec((1, TC, NPROJ), lambda b, c: (b, c, 0))]
        + [_const_spec(p) for p in params],
        out_specs=out_specs,
        out_shape=out_shape,
        scratch_shapes=scratch,
        compiler_params=pltpu.CompilerParams(
            dimension_semantics=("parallel", "arbitrary"), vmem_limit_bytes=VMEM_LIMIT),
        name="prompt_mixer",
    )(proj3, *params)


def _hg_step_kernel(layer, n_acc, proj_ref, hlb, st_ref, *rest):
    ot_ref, st_o, qh_s, f_s, k_s, v_s = rest[n_acc:]
    h = pl.program_id(0)
    if n_acc == 0:
        for other in range(DEPTH):
            if other != layer:
                st_o[other] = jnp.zeros(st_o.shape[1:], F32)
        st_o = st_o.at[layer]

    @pl.when(h == 0)
    def _prep():
        q = _silu(proj_ref[:, OFF_CQ:OFF_CQ + HG_WIDTH])
        lb = _lower_bound_row(hlb[...], layer)
        gt = lb + (1.0 - lb) * _sigmoid(proj_ref[:, OFF_CF:OFF_CF + HG_WIDTH])
        fdec = jnp.exp(jnp.log(gt))
        qh_s[...] = (q * fdec).T
        f_s[...] = fdec.T
        k_s[...] = (1.0 - gt).T
        v_s[...] = proj_ref[:, OFF_CI:OFF_CI + HG_WIDTH].T

    base = pl.multiple_of(h * HG_HEAD_DIM, HG_HEAD_DIM)
    v_h = v_s[pl.ds(base, HG_HEAD_DIM), :]

    def body(dk, acc):
        s = st_ref[dk]
        st_o[dk] = f_s[pl.ds(base + dk, 1), :] * s + k_s[pl.ds(base + dk, 1), :] * v_h
        return acc + qh_s[pl.ds(base + dk, 1), :] * s

    ot_ref[...] = lax.fori_loop(0, HG_HEAD_DIM, body,
                                jnp.zeros((HG_HEAD_DIM, LANES), F32), unroll=8)


def _hg_step(layer, proj, hlb, st_all, acc):
    nb = proj.shape[0]
    blk = (HG_HEAD_DIM, HG_HEAD_DIM, nb)
    st_spec = pl.BlockSpec((None, None) + blk, lambda h: (layer, h, 0, 0, 0))
    n_acc = len(acc)
    st_out_spec = st_spec if n_acc else pl.BlockSpec(
        (DEPTH, None) + blk, lambda h: (0, h, 0, 0, 0))
    return pl.pallas_call(
        functools.partial(_hg_step_kernel, layer, n_acc),
        grid=(HG_HEADS,),
        in_specs=[_const_spec(proj), _const_spec(hlb), st_spec]
        + [pl.BlockSpec(memory_space=pl.ANY)] * n_acc,
        out_specs=(pl.BlockSpec((HG_HEAD_DIM, nb), lambda h: (h, 0)), st_out_spec),
        out_shape=(jax.ShapeDtypeStruct((HG_WIDTH, nb), F32),
                   jax.ShapeDtypeStruct(st_all.shape, F32)),
        scratch_shapes=[pltpu.VMEM((HG_WIDTH, nb), F32)] * 4,
        input_output_aliases={3 + k: 1 + k for k in range(n_acc)},
        compiler_params=pltpu.CompilerParams(
            dimension_semantics=("arbitrary",), vmem_limit_bytes=VMEM_LIMIT),
        name="hg_step",
    )(proj, hlb, st_all, *acc)


def _pad_rows_t(x):
    pad = jnp.zeros((LANES - x.shape[0], x.shape[1]), F32)
    return jnp.concatenate([x, pad], axis=0).T


def _sample_mixer_kernel(layer, pos0, n_acc, proj_ref, rgc_ref, rgh_ref, sc_ref, ssd_ref, ohg_ref, *rest):
    (caw, cab, wg, bg, lam, cbw, cbb, dtb, alog, dexp, snorm,
     hlb, hnorm, expand, headsum, headmean,
     ymix_ref, rgc_o, rgh_o, sc_o, ssd_o) = rest[n_acc:]
    bb = DEC_BB
    if n_acc == 0:
        for other in range(DEPTH):
            if other != layer:
                ssd_o[other] = jnp.zeros(ssd_o.shape[1:], F32)
        ssd_o = ssd_o.at[layer]
    ax = proj_ref[:, OFF_AX:OFF_AX + RG_WIDTH]
    xa = cab[...] + caw[CONV_W - 1:CONV_W, :] * ax
    for k in range(CONV_W - 1):
        xa = xa + caw[k:k + 1, :] * rgc_ref[:, k * RG_WIDTH:(k + 1) * RG_WIDTH]
    rgc_o[:, 0:2 * RG_WIDTH] = rgc_ref[:, RG_WIDTH:3 * RG_WIDTH]
    rgc_o[:, 2 * RG_WIDTH:] = ax
    a, mult, gi = _rg_coeffs(_dot(_bf(xa), wg[...]) + bg[...], lam[...])
    if pos0 == 0:
        mult = jnp.ones_like(mult)
    h = a * rgh_ref[...] + mult * (gi * xa)
    rgh_o[...] = h
    ymix_ref[:, 0:RG_WIDTH] = h * _gelu_tanh(proj_ref[:, OFF_AG:OFF_AG + RG_WIDTH])

    bx = proj_ref[:, OFF_XBC:OFF_XBC + SSD_CONV_DIM]
    xbc = cbb[...] + cbw[CONV_W - 1:CONV_W, :] * bx
    for k in range(CONV_W - 1):
        xbc = xbc + cbw[k:k + 1, :] * sc_ref[:, k * SSD_CONV_DIM:(k + 1) * SSD_CONV_DIM]
    sc_o[:, 0:2 * SSD_CONV_DIM] = sc_ref[:, SSD_CONV_DIM:3 * SSD_CONV_DIM]
    sc_o[:, 2 * SSD_CONV_DIM:] = bx
    xbc = _silu(xbc)
    xs = xbc[:, :SSD_WIDTH]
    bm = xbc[:, SSD_WIDTH:SSD_WIDTH + SSD_STATE]
    cm = xbc[:, SSD_WIDTH + SSD_STATE:]
    dtp = _softplus(proj_ref[:, OFF_DT:OFF_DT + LANES] + dtb[...])
    v_da = dtp * (-jnp.exp(alog[...]))
    dt_e = _dot_x_c(dtp, expand[...])
    e_e = jnp.exp(_dot_x_c(v_da, expand[...]))
    dtx = dt_e * xs
    cbs = _dot(_bf(cm * bm), jnp.ones((SSD_STATE, LANES), BF16))[:, 0:1]
    dtx_t = _pad_rows_t(dtx)
    e_t = _pad_rows_t(e_e)
    c_pad = _bf(jnp.concatenate([cm, jnp.zeros((LANES - bb, SSD_STATE), F32)], axis=0))
    lane_w = lax.broadcasted_iota(jnp.int32, (SSD_WIDTH, LANES), 1)
    y_t = jnp.zeros((SSD_WIDTH, LANES), F32)
    for j in range(bb):
        s_old = ssd_ref[j].reshape(SSD_WIDTH, SSD_STATE)
        ssd_o[j] = (e_t[:, j:j + 1] * s_old + dtx_t[:, j:j + 1] * bm[j:j + 1, :]).reshape(
            SSD_HEADS, SSD_HEAD_DIM, SSD_STATE)
        y_t = y_t + jnp.where(lane_w == j, _dot_nt(_bf(s_old), c_pad), 0.0)
    y = cbs * dtx + y_t.T[0:bb, :] * e_e
    yb = y + dexp[...] * xs
    ymix_ref[:, RG_WIDTH:RG_WIDTH + SSD_WIDTH] = _rms(
        yb * _silu(proj_ref[:, OFF_BZ:OFF_BZ + SSD_WIDTH]), snorm[...])

    q = _silu(proj_ref[:, OFF_CQ:OFF_CQ + HG_WIDTH])
    lb = _lower_bound_row(hlb[...], layer)
    gt = lb + (1.0 - lb) * _sigmoid(proj_ref[:, OFF_CF:OFF_CF + HG_WIDTH])
    kk = 1.0 - gt
    vv = proj_ref[:, OFF_CI:OFF_CI + HG_WIDTH]
    att = _dot(_bf(q * kk), headsum[...])
    o = att * vv + ohg_ref[...]
    ms = _dot(_bf(o * o), headmean[...])
    ymix_ref[:, RG_WIDTH + SSD_WIDTH:] = (
        o * lax.rsqrt(ms + EPS) * hnorm[...] * _silu(proj_ref[:, OFF_CG:OFF_CG + HG_WIDTH]))


def _sample_mixer(layer, proj, rgc, rgh, sc, ssd_all, o_hg, acc, mp, consts):
    nb = proj.shape[0]
    bb = DEC_BB
    params = [mp["caw"], mp["cab"], mp["wg"], mp["bg"], mp["lam"], mp["cbw"], mp["cbb"],
              mp["dtb"], mp["alog"], mp["dexp"], mp["snorm"], mp["hlb"], mp["hnorm"],
              consts["expand"], consts["headsum"], consts["headmean"]]
    row2 = lambda w: pl.BlockSpec((bb, w), lambda i: (i, 0))
    small_specs = [row2(3 * RG_WIDTH), row2(RG_WIDTH), row2(3 * SSD_CONV_DIM)]
    ssd_spec = pl.BlockSpec((None, bb, SSD_HEADS, SSD_HEAD_DIM, SSD_STATE),
                            lambda i: (layer, i, 0, 0, 0))
    out_shape = (
        jax.ShapeDtypeStruct((nb, D_MIX), F32),
        jax.ShapeDtypeStruct((nb, 3 * RG_WIDTH), F32),
        jax.ShapeDtypeStruct((nb, RG_WIDTH), F32),
        jax.ShapeDtypeStruct((nb, 3 * SSD_CONV_DIM), F32),
        jax.ShapeDtypeStruct(ssd_all.shape, F32),
    )
    n_acc = len(acc)
    n_lead = 6
    ssd_out_spec = ssd_spec if n_acc else pl.BlockSpec(
        (DEPTH, bb, SSD_HEADS, SSD_HEAD_DIM, SSD_STATE), lambda i: (0, i, 0, 0, 0))
    return pl.pallas_call(
        functools.partial(_sample_mixer_kernel, layer, PAST_LEN, n_acc),
        grid=(nb // bb,),
        in_specs=[row2(NPROJ)] + small_specs + [ssd_spec, row2(HG_WIDTH)]
        + [pl.BlockSpec(memory_space=pl.ANY)] * n_acc + [_const_spec(p) for p in params],
        out_specs=tuple([row2(D_MIX)] + small_specs + [ssd_out_spec]),
        out_shape=out_shape,
        input_output_aliases={n_lead + k: 4 + k for k in range(n_acc)},
        compiler_params=pltpu.CompilerParams(
            dimension_semantics=("parallel",), vmem_limit_bytes=VMEM_LIMIT),
        name="sample_mixer",
    )(proj, rgc, rgh, sc, ssd_all, o_hg, *acc, *params)


def _post_kernel(final, h_ref, y_ref, p_ref, wo, nffn, wup, wdown, nple, wgate, wproj, nfin, o_ref):
    h = h_ref[...] + _dot(_bf(y_ref[...]), wo[...])
    hn = _bf(_rms(h, nffn[...]))
    for j in range(D_FF // FF_CHUNK):
        cs = slice(j * FF_CHUNK, (j + 1) * FF_CHUNK)
        z = jnp.square(jnp.maximum(_dot(hn, wup[:, cs]), 0.0))
        h = h + _dot(_bf(z), wdown[cs, :])
    gate = _sigmoid(_dot(_bf(_rms(h, nple[...])), wgate[...]))
    h = h + gate * _dot(_bf(p_ref[...]), wproj[...])
    if final:
        h = _rms(h, nfin[...])
    o_ref[...] = h


def _post(final, layer, h2d, ymix2d, p_all, lp, nfin, tm):
    m = h2d.shape[0]
    params = [lp["wo"], lp["nffn"], lp["wup"], lp["wdown"], lp["nple"], lp["wgate"], lp["wproj"], nfin]
    return pl.pallas_call(
        functools.partial(_post_kernel, final),
        grid=(m // tm,),
        in_specs=[pl.BlockSpec((tm, D_MODEL), lambda i: (i, 0)),
                  pl.BlockSpec((tm, D_MIX), lambda i: (i, 0)),
                  pl.BlockSpec((None, tm, PLE_DIM), lambda i: (layer, i, 0))]
        + [_const_spec(p) for p in params],
        out_specs=pl.BlockSpec((tm, D_MODEL), lambda i: (i, 0)),
        out_shape=jax.ShapeDtypeStruct((m, D_MODEL), F32),
        compiler_params=pltpu.CompilerParams(
            dimension_semantics=("parallel",), vmem_limit_bytes=VMEM_LIMIT),
        name="post_mixer",
    )(h2d, ymix2d, p_all, *params)


def _constants():
    t = np.arange(TC)
    ltri = (t[None, :] <= t[:, None]).astype(np.float32)
    utri = ltri.T.copy()
    expand = np.zeros((LANES, SSD_WIDTH), np.float32)
    for hd in range(SSD_HEADS):
        expand[hd, hd * SSD_HEAD_DIM:(hd + 1) * SSD_HEAD_DIM] = 1.0
    emat = np.zeros(((N_LEVELS + 1) * TC, TC), np.float32)
    for lev in range(1, N_LEVELS + 1):
        blk = 1 << lev
        mid = (t // blk) * blk + blk // 2
        upper = t >= mid
        r0 = (lev - 1) * TC
        emat[r0:r0 + TC] = (
            (upper[:, None] & (t[None, :] >= mid[:, None]) & (t[None, :] <= t[:, None]))
            | ((~upper)[:, None] & (t[None, :] > t[:, None]) & (t[None, :] < mid[:, None])))
    emat[N_LEVELS * TC:] = ltri
    cmask = np.zeros((N_LEVELS + 1, TC, 2 * TC), np.float32)
    cmask[0] = np.tile(np.eye(TC, dtype=np.float32), (1, 2))
    for lev in range(1, N_LEVELS + 1):
        blk = 1 << lev
        up = (t % blk) >= blk // 2
        m = up[:, None] & (~up)[None, :] & ((t // blk)[:, None] == (t // blk)[None, :])
        cmask[lev] = np.tile(m.astype(np.float32), (1, 2))
    hidx = np.arange(HG_WIDTH) // HG_HEAD_DIM
    bd = (hidx[:, None] == hidx[None, :]).astype(np.float32)
    pair = bd[:LANES, :LANES]
    return {
        "ltri3": jnp.asarray(np.tile(ltri, (1, 3)), BF16),
        "utri3": jnp.asarray(np.tile(utri, (3, 1)), BF16),
        "expand": jnp.asarray(expand, BF16),
        "expand3": jnp.asarray(np.tile(expand, (3, 1)), BF16),
        "emat2": jnp.asarray(np.tile(emat, (1, 2)), BF16),
        "cmask": jnp.asarray(cmask, F32),
        "pairmask": jnp.asarray(pair, F32),
        "pairmean": jnp.asarray(pair / HG_HEAD_DIM, BF16),
        "headsum": jnp.asarray(bd, BF16), "headmean": jnp.asarray(bd / HG_HEAD_DIM, BF16),
    }


def _block_diag(w):
    hh, d, _ = w.shape
    eye = jnp.eye(hh, dtype=w.dtype)
    return (eye[:, None, :, None] * w[:, :, None, :]).reshape(hh * d, hh * d)


def _pair_gate_weights(wa, wx):
    hh, d, _ = wa.shape
    pa = jnp.stack([_block_diag(wa[2 * g:2 * g + 2]) for g in range(hh // 2)])
    px = jnp.stack([_block_diag(wx[2 * g:2 * g + 2]) for g in range(hh // 2)])
    return jnp.concatenate([pa, px], axis=2)


def _pad_lanes(v):
    return jnp.pad(v, (0, LANES - v.shape[0]))[None, :]


def kernel(x_prompt, x_sample, state_rg_conv, state_rg_h, state_ssd_conv, state_ssd, state_hgrn,
           p_prompt, p_sample, norm_mix, w_in, conv_a_w, conv_a_b, rg_wa, rg_ba, rg_wx, rg_bx,
           rg_lambda, conv_b_w, conv_b_b, ssd_dt_bias, ssd_a_log, ssd_d, ssd_norm,
           hg_lower_bounds, hg_norm, w_out, norm_ffn, w_up, w_down, norm_ple, w_ple_gate,
           w_ple_proj, norm_final):
    bsz, seq, _ = x_prompt.shape
    nb = x_sample.shape[0]
    consts = _constants()
    nfin = norm_final[None, :]
    pp_all = p_prompt.reshape(DEPTH, bsz * seq, PLE_DIM)
    ps_all = p_sample.reshape(DEPTH, nb, PLE_DIM)

    hp = x_prompt.reshape(bsz * seq, D_MODEL)
    hs = x_sample.reshape(nb, D_MODEL)
    st_p = [[] for _ in range(5)]
    st_s = [[] for _ in range(3)]
    hg_t_all = jnp.transpose(state_hgrn, (0, 2, 3, 4, 1))
    acc_ssd = ()
    acc_hg = ()
    for i in range(DEPTH):
        w_re = _prep_w_in(w_in, i, 128)
        gmix = norm_mix[i][None, :]
        mp = {
            "caw": conv_a_w[i], "cab": conv_a_b[i][None, :],
            "wg": jnp.concatenate([_block_diag(rg_wa[i]), _block_diag(rg_wx[i])], axis=1).astype(BF16),
            "bg": jnp.concatenate([rg_ba[i], rg_bx[i]])[None, :],
            "wgp": _pair_gate_weights(rg_wa[i], rg_wx[i]).astype(BF16),
            "bgp": jnp.concatenate([rg_ba[i].reshape(NGRP, 1, LANES),
                                    rg_bx[i].reshape(NGRP, 1, LANES)], axis=2),
            "lam": rg_lambda[i][None, :],
            "cbw": conv_b_w[i], "cbb": conv_b_b[i][None, :],
            "dtb": _pad_lanes(ssd_dt_bias[i]), "alog": _pad_lanes(ssd_a_log[i]),
            "dexp": jnp.repeat(ssd_d[i], SSD_HEAD_DIM)[None, :],
            "snorm": ssd_norm[i][None, :],
            "hlb": hg_lower_bounds, "hnorm": jnp.tile(hg_norm[i], HG_HEADS)[None, :],
        }
        lp = {
            "wo": _cast_bf16(w_out, i, 512), "nffn": norm_ffn[i][None, :],
            "wup": _cast_bf16(w_up, i, 256), "wdown": _cast_bf16(w_down, i, 1024),
            "nple": norm_ple[i][None, :], "wgate": _cast_bf16(w_ple_gate, i, 512),
            "wproj": _cast_bf16(w_ple_proj, i, 256),
        }
        final = i == DEPTH - 1

        proj_p = _in_proj(hp, gmix, w_re, TM_PROMPT).reshape(bsz, seq, NPROJ)
        ymix_p, rgc, rgh, sc, ssd, hg = _prompt_mixer(i, proj_p, mp, consts)
        hp = _post(final, i, hp, ymix_p.reshape(bsz * seq, D_MIX), pp_all, lp, nfin, TM_PROMPT)
        for lst, s in zip(st_p, (rgc, rgh.reshape(bsz, RG_WIDTH), sc,
                                 ssd.reshape(bsz, SSD_HEADS, SSD_HEAD_DIM, SSD_STATE), hg)):
            lst.append(s)

        proj_s = _in_proj(hs, gmix, w_re, nb)
        o_hg_t, hg_acc = _hg_step(i, proj_s, hg_lower_bounds, hg_t_all, acc_hg)
        acc_hg = (hg_acc,)
        ymix_s, rgc, rgh, sc, ssd_acc = _sample_mixer(
            i, proj_s,
            state_rg_conv[i].reshape(nb, 3 * RG_WIDTH), state_rg_h[i],
            state_ssd_conv[i].reshape(nb, 3 * SSD_CONV_DIM),
            state_ssd, o_hg_t.T, acc_ssd, mp, consts)
        acc_ssd = (ssd_acc,)
        hs = _post(final, i, hs, ymix_s, ps_all, lp, nfin, nb)
        for lst, s in zip(st_s, (rgc.reshape(nb, CONV_W - 1, RG_WIDTH), rgh,
                                 sc.reshape(nb, CONV_W - 1, SSD_CONV_DIM))):
            lst.append(s)

    y_prompt = hp.reshape(bsz, seq, D_MODEL)
    y_sample = hs.reshape(nb, 1, D_MODEL)
    outs_p = [jnp.stack(lst) for lst in st_p]
    outs_s = [jnp.stack(lst) for lst in st_s] + [
        acc_ssd[0], jnp.transpose(acc_hg[0], (0, 4, 1, 2, 3))]
    return (y_prompt, y_sample, *outs_p, *outs_s)
```

```python
import functools

import numpy as np
import jax
import jax.numpy as jnp
from jax import lax
from jax.experimental import pallas as pl
from jax.experimental.pallas import tpu as pltpu

F32 = jnp.float32
BF16 = jnp.bfloat16

D_MODEL = 1024
DEPTH = 2
PAST_LEN = 16384
PLE_DIM = 256
D_FF = 4 * D_MODEL
CONV_W = 4
EPS = 1e-6
RG_WIDTH = 512
RG_HEADS = 8
RG_HEAD_DIM = 64
RG_C = 8.0
SSD_WIDTH = 512
SSD_HEAD_DIM = 64
SSD_HEADS = 8
SSD_STATE = 128
SSD_CONV_DIM = SSD_WIDTH + 2 * SSD_STATE
HG_WIDTH = 512
HG_HEADS = 8
HG_HEAD_DIM = 64
D_MIX = RG_WIDTH + SSD_WIDTH + HG_WIDTH

LOG2E = 1.4426950408889634
LANES = 128
SUBLANES = 8
NGRP = RG_WIDTH // LANES

OFF_AX = 0
OFF_AG = 512
OFF_BZ = 1024
OFF_XBC = 1536
OFF_CQ = 2304
OFF_CF = 2816
OFF_CI = 3328
OFF_CG = 3840
OFF_DT = 4352
NPROJ = OFF_DT + LANES
DT_COL = 2 * RG_WIDTH + SSD_WIDTH + SSD_CONV_DIM
D_IN_PROJ = DT_COL + SSD_HEADS + 4 * HG_WIDTH

TC = 128
HALF = TC // 2
N_LEVELS = 7
DEC_BB = 8
TM_PROMPT = 512
FF_CHUNK = 1024
VMEM_LIMIT = 52 * 1024 * 1024


def _bf(x):
    return x.astype(BF16)


def _dot(a, b):
    return jnp.dot(a, b, preferred_element_type=F32)


def _dot_nt(a, b):
    return lax.dot_general(a, b, (((1,), (1,)), ((), ())), preferred_element_type=F32)


def _split3(x):
    hi = _bf(x)
    r1 = x - hi.astype(F32)
    mid = _bf(r1)
    lo = _bf(r1 - mid.astype(F32))
    return hi, mid, lo


def _dot_x_c(x, c):
    hi, mid, lo = _split3(x)
    return _dot(hi, c) + _dot(mid, c) + _dot(lo, c)


def _dot_x_c3(x, c3):
    return _dot(jnp.concatenate(_split3(x), axis=1), c3)


def _dot_c3_x(c3, x):
    return _dot(c3, jnp.concatenate(_split3(x), axis=0))


def _sigmoid(x):
    return 0.5 * jnp.tanh(0.5 * x) + 0.5


def _silu(x):
    h = 0.5 * x
    return h * jnp.tanh(h) + h


def _softplus(x):
    return jnp.maximum(x, 0.0) + jnp.log1p(jnp.exp(-jnp.abs(x)))


def _gelu_tanh(x):
    c = np.float32(np.sqrt(2.0 / np.pi))
    return 0.5 * x * (1.0 + jnp.tanh(c * (x + 0.044715 * (x * x * x))))


def _rms(x, gain):
    return x * lax.rsqrt(jnp.mean(x * x, axis=-1, keepdims=True) + EPS) * gain


def _lower_bound_row(hlb, layer):
    m = jnp.max(hlb, axis=0, keepdims=True)
    e = jnp.exp(hlb - m)
    sm = e / jnp.sum(e, axis=0, keepdims=True)
    lb = jnp.zeros((1, HG_WIDTH), F32)
    for j in range(1, layer + 1):
        lb = lb + sm[j:j + 1, :]
    return lb


def _rg_coeffs(gates, lam_row):
    w = gates.shape[1] // 2
    r = _sigmoid(gates[:, :w])
    i = _sigmoid(gates[:, w:])
    log_a = -RG_C * r * _softplus(-lam_row)
    a = jnp.exp(log_a)
    mult = jnp.sqrt(-jnp.tanh(log_a) * (a * a + 1.0))
    return a, mult, i


def _lane_halves(x):
    lane = lax.broadcasted_iota(jnp.int32, x.shape, 1)
    lo = jnp.where(lane < HG_HEAD_DIM, x, 0.0)
    hi = jnp.where(lane >= HG_HEAD_DIM, x, 0.0)
    return jnp.concatenate([lo, hi], axis=0)


def _const_spec(arr):
    nd = arr.ndim
    return pl.BlockSpec(arr.shape, lambda *_: (0,) * nd, pipeline_mode=pl.Buffered(1))


def _cast_kernel(w_ref, o_ref):
    o_ref[...] = _bf(w_ref[...])


def _cast_bf16(w_all, layer, rb):
    _, rows, cols = w_all.shape
    return pl.pallas_call(
        _cast_kernel,
        grid=(rows // rb,),
        in_specs=[pl.BlockSpec((None, rb, cols), lambda i: (layer, i, 0))],
        out_specs=pl.BlockSpec((rb, cols), lambda i: (i, 0)),
        out_shape=jax.ShapeDtypeStruct((rows, cols), BF16),
        compiler_params=pltpu.CompilerParams(dimension_semantics=("parallel",)),
        name="cast_bf16",
    )(w_all)


def _win_kernel(wt_ref, o_ref):
    o_ref[:, 0:OFF_CQ] = _bf(wt_ref[0:DT_COL, :].T)
    o_ref[:, OFF_CQ:OFF_DT] = _bf(wt_ref[DT_COL + SSD_HEADS:D_IN_PROJ, :].T)
    dt = jnp.concatenate([wt_ref[DT_COL:DT_COL + SSD_HEADS, :],
                          jnp.zeros((LANES - SSD_HEADS, LANES), F32)], axis=0)
    o_ref[:, OFF_DT:NPROJ] = _bf(dt.T)


def _prep_w_in(wt_all, layer):
    return pl.pallas_call(
        _win_kernel,
        grid=(D_MODEL // LANES,),
        in_specs=[pl.BlockSpec((None, D_IN_PROJ, LANES), lambda i: (layer, 0, i))],
        out_specs=pl.BlockSpec((LANES, NPROJ), lambda i: (i, 0)),
        out_shape=jax.ShapeDtypeStruct((D_MODEL, NPROJ), BF16),
        compiler_params=pltpu.CompilerParams(dimension_semantics=("parallel",)),
        name="prep_w_in",
    )(wt_all)


def _proj_kernel(x_ref, g_ref, w_ref, o_ref):
    u = _rms(x_ref[...], g_ref[...])
    o_ref[...] = _dot(_bf(u), w_ref[...])


def _in_proj(x2d, gain, w_bf, tm):
    m = x2d.shape[0]
    return pl.pallas_call(
        _proj_kernel,
        grid=(m // tm,),
        in_specs=[pl.BlockSpec((tm, D_MODEL), lambda i: (i, 0)),
                  _const_spec(gain), _const_spec(w_bf)],
        out_specs=pl.BlockSpec((tm, NPROJ), lambda i: (i, 0)),
        out_shape=jax.ShapeDtypeStruct((m, NPROJ), F32),
        compiler_params=pltpu.CompilerParams(
            dimension_semantics=("parallel",), vmem_limit_bytes=VMEM_LIMIT),
        name="in_proj",
    )(x2d, gain, w_bf)


def _mixer_chunk(layer, row0, pv, ymix, side_work,
                 caw, cab, wgp, bgp, lam, cbw, cbb, dtb, alog, dexp, snorm,
                 hlb, hnorm, ltri3, utri3, expand3, emat2, cmask, pairmask, pairmean,
                 xea, xeb, hcar, s_t, st_hg):
    row_h = lax.broadcasted_iota(jnp.int32, (HALF, 1), 0)
    sub_h = row_h & (SUBLANES - 1)
    first_head = lax.broadcasted_iota(jnp.int32, (HALF, LANES), 1) < HG_HEAD_DIM
    first_head_tc = lax.broadcasted_iota(jnp.int32, (TC, LANES), 1) < HG_HEAD_DIM
    col_tc = lax.broadcasted_iota(jnp.int32, (HALF, TC), 1)
    row_tc = lax.broadcasted_iota(jnp.int32, (HALF, TC), 0)

    tiles = [slice(t * HALF, (t + 1) * HALF) for t in range(TC // HALF)]

    def lanes(off, g):
        return slice(off + g * LANES, off + (g + 1) * LANES)

    def conv_rows(prev8, x, w_ref, b_ref, gs):
        xcat = jnp.concatenate([prev8, x], axis=0)
        y = b_ref[:, gs] + w_ref[CONV_W - 1:CONV_W, gs] * x
        for j in range(1, CONV_W):
            y = y + w_ref[CONV_W - 1 - j:CONV_W - j, gs] * pltpu.roll(xcat, j, 0)[SUBLANES:, :]
        return y, x[x.shape[0] - SUBLANES:, :]

    def conv(tail_ref, w_ref, b_ref, src_off, g):
        gs = lanes(0, g)
        y, tail = conv_rows(tail_ref[:, gs], pv[:, lanes(src_off, g)], w_ref, b_ref, gs)
        tail_ref[:, gs] = tail
        return y

    def mixer_a(g):
        gs = lanes(0, g)
        prev8 = xea[:, gs]
        carry = hcar[0:1, gs]
        for t, rs in enumerate(tiles):
            xa, prev8 = conv_rows(prev8, pv[rs, lanes(OFF_AX, g)], caw, cab, gs)
            a, mult, gi = _rg_coeffs(_dot(_bf(xa), wgp[g]) + bgp[g], lam[:, gs])
            mult = jnp.where(row_h + (row0 + t * HALF) == 0, 1.0, mult)
            b = mult * (gi * xa)
            k = 1
            while k < SUBLANES:
                keep = sub_h >= k
                a_sh = jnp.where(keep, pltpu.roll(a, k, 0), 1.0)
                b_sh = jnp.where(keep, pltpu.roll(b, k, 0), 0.0)
                b = b + a * b_sh
                a = a * a_sh
                k *= 2
            slabs = []
            for r in range(HALF // SUBLANES):
                sl = slice(r * SUBLANES, (r + 1) * SUBLANES)
                h_r = a[sl, :] * carry + b[sl, :]
                slabs.append(h_r)
                carry = h_r[SUBLANES - 1:SUBLANES, :]
            ymix[rs, gs] = jnp.concatenate(slabs, axis=0) * _gelu_tanh(
                pv[rs, lanes(OFF_AG, g)])
        xea[:, gs] = prev8
        hcar[0:1, gs] = carry

    bm = _silu(conv(xeb, cbw, cbb, OFF_XBC, NGRP))
    cm = _silu(conv(xeb, cbw, cbb, OFF_XBC, NGRP + 1))
    dtp = _softplus(pv[:, OFF_DT:OFF_DT + LANES] + dtb[...])
    v_da = dtp * (-LOG2E * jnp.exp(alog[...]))
    cum = _dot_c3_x(ltri3[...], v_da)
    cum_t = _dot_x_c3(v_da.T, utri3[...])
    dt_e = _dot_x_c3(dtp, expand3[...])
    cum_e = _dot_x_c3(cum, expand3[...])
    cb = _dot_nt(_bf(cm), _bf(bm))
    cmb = _bf(cm)
    bmt = _bf(bm.T)
    ssq_parts = []

    def mixer_b(g):
        gs = lanes(0, g)
        cum_last = cum_e[TC - 1:TC, gs]
        s_old = _bf(s_t[g])
        prev8 = xeb[:, gs]
        xs_t, x2a, x2b, wx = [], [], [], []
        for rs in tiles:
            xs, prev8 = conv_rows(prev8, pv[rs, lanes(OFF_XBC, g)], cbw, cbb, gs)
            xs = _silu(xs)
            dt_g = dt_e[rs, gs]
            dtx = xs * dt_g
            xs_t.append(xs)
            x2a.append(_bf(jnp.where(first_head, dtx, 0.0)))
            x2b.append(_bf(jnp.where(first_head, 0.0, dtx)))
            wx.append(_bf(jnp.exp2(cum_last - cum_e[rs, gs]) * dt_g * xs))
        xeb[:, gs] = prev8
        x2 = jnp.concatenate(x2a + x2b, axis=0)
        for t, rs in enumerate(tiles):
            gmats = []
            for hh in range(2):
                hd = 2 * g + hh
                seg = cum[rs, hd:hd + 1] - cum_t[hd:hd + 1, :]
                gmats.append(_bf(cb[rs, :] * jnp.exp2(jnp.where(col_tc <= row_tc + t * HALF, seg, -jnp.inf))))
            y = _dot(jnp.concatenate(gmats, axis=1), x2)
            y = y + _dot(cmb[rs, :], s_old) * jnp.exp2(cum_e[rs, gs])
            yz = (y + dexp[:, gs] * xs_t[t]) * _silu(pv[rs, lanes(OFF_BZ, g)])
            ssq_parts.append(yz * yz)
            ymix[rs, lanes(RG_WIDTH, g)] = yz
        s_t[g] = jnp.exp2(cum_last) * s_t[g] + _dot(bmt, jnp.concatenate(wx, axis=0))

    lb_all = _lower_bound_row(hlb[...], layer)

    def forget_gate(g):
        lb = lb_all[:, g * LANES:(g + 1) * LANES]
        return lb + (1.0 - lb) * _sigmoid(pv[:, OFF_CF + g * LANES:OFF_CF + (g + 1) * LANES])

    def mixer_c(g, gt, xall):
        gs = lanes(0, g)
        b_off = N_LEVELS * TC
        bcum = xall[b_off:b_off + TC, :]
        blast = bcum[TC - 1:TC, :]
        q = _silu(pv[:, lanes(OFF_CQ, g)])
        kk = 1.0 - gt
        vv = pv[:, lanes(OFF_CI, g)]
        zeros = jnp.zeros((HG_HEAD_DIM, TC), BF16)

        def keys_by_head(kt):
            ktt = _bf(kt.T)
            return jnp.concatenate(
                [jnp.concatenate([ktt[:HG_HEAD_DIM, :], zeros], axis=1),
                 jnp.concatenate([zeros, ktt[HG_HEAD_DIM:, :]], axis=1)], axis=0)

        att = _dot(_bf(q), keys_by_head(kk)) * cmask[0]
        for lev in range(1, N_LEVELS + 1):
            e = jnp.exp2(xall[(lev - 1) * TC:lev * TC, :])
            att = att + _dot(_bf(q * e), keys_by_head(kk * e)) * cmask[lev]
        v2 = jnp.concatenate([_bf(jnp.where(first_head_tc, vv, 0.0)),
                              _bf(jnp.where(first_head_tc, 0.0, vv))], axis=0)
        o = _dot(_bf(att), v2) + _dot_nt(_bf(q * jnp.exp2(bcum)), _bf(st_hg[g]))
        khat = kk * jnp.exp2(blast - bcum)
        st_hg[g] = jnp.exp2(blast) * st_hg[g] + pairmask[...] * _dot(_bf(vv.T), _bf(khat))
        ms = _dot(_bf(o * o), pairmean[...])
        ymix[:, lanes(RG_WIDTH + SSD_WIDTH, g)] = (
            o * lax.rsqrt(ms + EPS) * hnorm[:, gs] * _silu(pv[:, lanes(OFF_CG, g)]))

    for pair in range(NGRP // 2):
        g0, g1 = 2 * pair, 2 * pair + 1
        gts = [forget_gate(g0), forget_gate(g1)]
        lf = jnp.log(jnp.concatenate(gts, axis=1)) * LOG2E
        hi = _bf(lf)
        lo = _bf(lf - hi.astype(F32))
        xall2 = _dot(emat2[...], jnp.concatenate([hi, lo], axis=0))
        for g, gt in zip((g0, g1), gts):
            side_work[g]()
            mixer_a(g)
            mixer_b(g)
            mixer_c(g, gt, xall2[:, (g - g0) * LANES:(g - g0 + 1) * LANES])

    for t, rs in enumerate(tiles):
        ssq = ssq_parts[t]
        for g in range(1, NGRP):
            ssq = ssq + ssq_parts[g * len(tiles) + t]
        rinv = lax.rsqrt(jnp.sum(ssq, axis=-1, keepdims=True) * (1.0 / SSD_WIDTH) + EPS)
        for g in range(NGRP):
            ys = lanes(RG_WIDTH, g)
            ymix[rs, ys] = ymix[rs, ys] * rinv * snorm[:, lanes(0, g)]


N_MIX_PARAMS = 20
MXU_N = 256
PROJ_BOUNDS = (0, 5 * MXU_N, 9 * MXU_N, 13 * MXU_N, NPROJ)
assert len(PROJ_BOUNDS) == NGRP + 1
CHUNKS_PER_STEP = 2


def _prompt_mixer_kernel(layer, pos0, x_ref, xn_ref, gmix, w_in_ref, *rest):
    params = rest[:N_MIX_PARAMS]
    ymix_ref, rgc_o, rgh_o, sc_o, ssd_o, hg_o = rest[N_MIX_PARAMS:N_MIX_PARAMS + 6]
    proj_a, proj_b, xea, xeb, hcar, s_t, st_hg = rest[N_MIX_PARAMS + 6:]
    states = (xea, xeb, hcar, s_t, st_hg)
    b = pl.program_id(0)
    c = pl.program_id(1)
    last = pl.num_programs(1) - 1

    def project(x):
        return _dot(_bf(_rms(x, gmix[...])), w_in_ref[...])

    @pl.when(jnp.logical_and(b == 0, c == 0))
    def _first_projection():
        proj_a[...] = project(x_ref[0, 0:TC, :])

    @pl.when(c == 0)
    def _init():
        for ref in states:
            ref[...] = jnp.zeros_like(ref)

    def projection_parts(x, dst):
        u = _bf(_rms(x, gmix[...]))
        def part(j):
            cs = slice(PROJ_BOUNDS[j], PROJ_BOUNDS[j + 1])
            dst[:, cs] = _dot(u, w_in_ref[:, cs])

        return [functools.partial(part, j) for j in range(NGRP)]

    row0 = c * (CHUNKS_PER_STEP * TC) + pos0
    _mixer_chunk(layer, row0, proj_a, ymix_ref.at[0, 0:TC],
                 projection_parts(x_ref[0, TC:2 * TC, :], proj_b), *params, *states)
    _mixer_chunk(layer, row0 + TC, proj_b, ymix_ref.at[0, TC:2 * TC],
                 projection_parts(xn_ref[0], proj_a), *params, *states)

    @pl.when(c == last)
    def _emit():
        rgc_o[0] = xea[SUBLANES - 3:SUBLANES, :]
        sc_o[0] = xeb[SUBLANES - 3:SUBLANES, :]
        rgh_o[0] = hcar[0:1, :]
        for g in range(NGRP):
            ssd_o[0, g * LANES:(g + 1) * LANES, :] = s_t[g].T
            s_pair = st_hg[g].T
            for hh in range(2):
                lo = hh * HG_HEAD_DIM
                hg_o[0, 2 * g + hh] = s_pair[lo:lo + HG_HEAD_DIM, lo:lo + HG_HEAD_DIM]


def _prompt_mixer(layer, x3, gmix, w_in_bf, mp, consts):
    bsz, seq, _ = x3.shape
    step_rows = CHUNKS_PER_STEP * TC
    nsteps = seq // step_rows
    n_chunks = bsz * (seq // TC)
    params = [mp["caw"], mp["cab"], mp["wgp"], mp["bgp"], mp["lam"], mp["cbw"], mp["cbb"],
              mp["dtb"], mp["alog"], mp["dexp"], mp["snorm"], mp["hlb"], mp["hnorm"],
              consts["ltri3"], consts["utri3"], consts["expand3"], consts["emat2"],
              consts["cmask"], consts["pairmask"], consts["pairmean"]]
    assert len(params) == N_MIX_PARAMS

    def next_chunk(b, c):
        flat = jnp.minimum((b * nsteps + c + 1) * CHUNKS_PER_STEP, n_chunks - 1)
        return flat // (seq // TC), flat % (seq // TC), 0

    out_shape = (
        jax.ShapeDtypeStruct((bsz, seq, D_MIX), F32),
        jax.ShapeDtypeStruct((bsz, CONV_W - 1, RG_WIDTH), F32),
        jax.ShapeDtypeStruct((bsz, 1, RG_WIDTH), F32),
        jax.ShapeDtypeStruct((bsz, CONV_W - 1, SSD_CONV_DIM), F32),
        jax.ShapeDtypeStruct((bsz, SSD_WIDTH, SSD_STATE), F32),
        jax.ShapeDtypeStruct((bsz, HG_HEADS, HG_HEAD_DIM, HG_HEAD_DIM), F32),
    )
    out_specs = (
        pl.BlockSpec((1, step_rows, D_MIX), lambda b, c: (b, c, 0)),
        pl.BlockSpec((1, CONV_W - 1, RG_WIDTH), lambda b, c: (b, 0, 0)),
        pl.BlockSpec((1, 1, RG_WIDTH), lambda b, c: (b, 0, 0)),
        pl.BlockSpec((1, CONV_W - 1, SSD_CONV_DIM), lambda b, c: (b, 0, 0)),
        pl.BlockSpec((1, SSD_WIDTH, SSD_STATE), lambda b, c: (b, 0, 0)),
        pl.BlockSpec((1, HG_HEADS, HG_HEAD_DIM, HG_HEAD_DIM), lambda b, c: (b, 0, 0, 0)),
    )
    scratch = [
        pltpu.VMEM((TC, NPROJ), F32),
        pltpu.VMEM((TC, NPROJ), F32),
        pltpu.VMEM((SUBLANES, RG_WIDTH), F32),
        pltpu.VMEM((SUBLANES, SSD_CONV_DIM), F32),
        pltpu.VMEM((SUBLANES, RG_WIDTH), F32),
        pltpu.VMEM((NGRP, SSD_STATE, LANES), F32),
        pltpu.VMEM((NGRP, LANES, LANES), F32),
    ]
    return pl.pallas_call(
        functools.partial(_prompt_mixer_kernel, layer, 0),
        grid=(bsz, nsteps),
        in_specs=[pl.BlockSpec((1, step_rows, D_MODEL), lambda b, c: (b, c, 0)),
                  pl.BlockSpec((1, TC, D_MODEL), next_chunk),
                  _const_spec(gmix), _const_spec(w_in_bf)]
        + [_const_spec(p) for p in params],
        out_specs=out_specs,
        out_shape=out_shape,
        scratch_shapes=scratch,
        compiler_params=pltpu.CompilerParams(
            dimension_semantics=("arbitrary", "arbitrary"), vmem_limit_bytes=VMEM_LIMIT),
        name="prompt_mixer",
    )(x3, x3, gmix, w_in_bf, *params)


def _hg_step_kernel(layer, n_acc, proj_ref, hlb, st_ref, *rest):
    ot_ref, st_o, qh_s, f_s, k_s, v_s = rest[n_acc:]
    h = pl.program_id(0)
    if n_acc == 0:
        for other in range(DEPTH):
            if other != layer:
                st_o[other] = jnp.zeros(st_o.shape[1:], F32)
        st_o = st_o.at[layer]

    @pl.when(h == 0)
    def _prep():
        q = _silu(proj_ref[:, OFF_CQ:OFF_CQ + HG_WIDTH])
        lb = _lower_bound_row(hlb[...], layer)
        gt = lb + (1.0 - lb) * _sigmoid(proj_ref[:, OFF_CF:OFF_CF + HG_WIDTH])
        fdec = jnp.exp(jnp.log(gt))
        qh_s[...] = (q * fdec).T
        f_s[...] = fdec.T
        k_s[...] = (1.0 - gt).T
        v_s[...] = proj_ref[:, OFF_CI:OFF_CI + HG_WIDTH].T

    base = pl.multiple_of(h * HG_HEAD_DIM, HG_HEAD_DIM)
    v_h = v_s[pl.ds(base, HG_HEAD_DIM), :]

    def body(dk, acc):
        s = st_ref[dk]
        st_o[dk] = f_s[pl.ds(base + dk, 1), :] * s + k_s[pl.ds(base + dk, 1), :] * v_h
        return acc + qh_s[pl.ds(base + dk, 1), :] * s

    ot_ref[...] = lax.fori_loop(0, HG_HEAD_DIM, body,
                                jnp.zeros((HG_HEAD_DIM, LANES), F32), unroll=8)


def _hg_step(layer, proj, hlb, st_all, acc):
    nb = proj.shape[0]
    blk = (HG_HEAD_DIM, HG_HEAD_DIM, nb)
    st_spec = pl.BlockSpec((None, None) + blk, lambda h: (layer, h, 0, 0, 0))
    n_acc = len(acc)
    st_out_spec = st_spec if n_acc else pl.BlockSpec(
        (DEPTH, None) + blk, lambda h: (0, h, 0, 0, 0))
    return pl.pallas_call(
        functools.partial(_hg_step_kernel, layer, n_acc),
        grid=(HG_HEADS,),
        in_specs=[_const_spec(proj), _const_spec(hlb), st_spec]
        + [pl.BlockSpec(memory_space=pl.ANY)] * n_acc,
        out_specs=(pl.BlockSpec((HG_HEAD_DIM, nb), lambda h: (h, 0)), st_out_spec),
        out_shape=(jax.ShapeDtypeStruct((HG_WIDTH, nb), F32),
                   jax.ShapeDtypeStruct(st_all.shape, F32)),
        scratch_shapes=[pltpu.VMEM((HG_WIDTH, nb), F32)] * 4,
        input_output_aliases={3 + k: 1 + k for k in range(n_acc)},
        compiler_params=pltpu.CompilerParams(
            dimension_semantics=("arbitrary",), vmem_limit_bytes=VMEM_LIMIT),
        name="hg_step",
    )(proj, hlb, st_all, *acc)


def _pad_rows_t(x):
    pad = jnp.zeros((LANES - x.shape[0], x.shape[1]), F32)
    return jnp.concatenate([x, pad], axis=0).T


def _sample_mixer_kernel(layer, pos0, n_acc, proj_ref, rgc_ref, rgh_ref, sc_ref, ssd_ref, ohg_ref, *rest):
    (caw, cab, wg, bg, lam, cbw, cbb, dtb, alog, dexp, snorm,
     hlb, hnorm, expand, headsum, headmean,
     ymix_ref, rgc_o, rgh_o, sc_o, ssd_o) = rest[n_acc:]
    bb = DEC_BB
    if n_acc == 0:
        for other in range(DEPTH):
            if other != layer:
                ssd_o[other] = jnp.zeros(ssd_o.shape[1:], F32)
        ssd_o = ssd_o.at[layer]
    ax = proj_ref[:, OFF_AX:OFF_AX + RG_WIDTH]
    xa = cab[...] + caw[CONV_W - 1:CONV_W, :] * ax
    for k in range(CONV_W - 1):
        xa = xa + caw[k:k + 1, :] * rgc_ref[:, k * RG_WIDTH:(k + 1) * RG_WIDTH]
    rgc_o[:, 0:2 * RG_WIDTH] = rgc_ref[:, RG_WIDTH:3 * RG_WIDTH]
    rgc_o[:, 2 * RG_WIDTH:] = ax
    a, mult, gi = _rg_coeffs(_dot(_bf(xa), wg[...]) + bg[...], lam[...])
    if pos0 == 0:
        mult = jnp.ones_like(mult)
    h = a * rgh_ref[...] + mult * (gi * xa)
    rgh_o[...] = h
    ymix_ref[:, 0:RG_WIDTH] = h * _gelu_tanh(proj_ref[:, OFF_AG:OFF_AG + RG_WIDTH])

    bx = proj_ref[:, OFF_XBC:OFF_XBC + SSD_CONV_DIM]
    xbc = cbb[...] + cbw[CONV_W - 1:CONV_W, :] * bx
    for k in range(CONV_W - 1):
        xbc = xbc + cbw[k:k + 1, :] * sc_ref[:, k * SSD_CONV_DIM:(k + 1) * SSD_CONV_DIM]
    sc_o[:, 0:2 * SSD_CONV_DIM] = sc_ref[:, SSD_CONV_DIM:3 * SSD_CONV_DIM]
    sc_o[:, 2 * SSD_CONV_DIM:] = bx
    xbc = _silu(xbc)
    xs = xbc[:, :SSD_WIDTH]
    bm = xbc[:, SSD_WIDTH:SSD_WIDTH + SSD_STATE]
    cm = xbc[:, SSD_WIDTH + SSD_STATE:]
    dtp = _softplus(proj_ref[:, OFF_DT:OFF_DT + LANES] + dtb[...])
    v_da = dtp * (-jnp.exp(alog[...]))
    dt_e = _dot_x_c(dtp, expand[...])
    e_e = jnp.exp(_dot_x_c(v_da, expand[...]))
    dtx = dt_e * xs
    cbs = _dot(_bf(cm * bm), jnp.ones((SSD_STATE, LANES), BF16))[:, 0:1]
    dtx_t = _pad_rows_t(dtx)
    e_t = _pad_rows_t(e_e)
    c_pad = _bf(jnp.concatenate([cm, jnp.zeros((LANES - bb, SSD_STATE), F32)], axis=0))
    lane_w = lax.broadcasted_iota(jnp.int32, (SSD_WIDTH, LANES), 1)
    y_t = jnp.zeros((SSD_WIDTH, LANES), F32)
    for j in range(bb):
        s_old = ssd_ref[j].reshape(SSD_WIDTH, SSD_STATE)
        ssd_o[j] = (e_t[:, j:j + 1] * s_old + dtx_t[:, j:j + 1] * bm[j:j + 1, :]).reshape(
            SSD_HEADS, SSD_HEAD_DIM, SSD_STATE)
        y_t = y_t + jnp.where(lane_w == j, _dot_nt(_bf(s_old), c_pad), 0.0)
    y = cbs * dtx + y_t.T[0:bb, :] * e_e
    yb = y + dexp[...] * xs
    ymix_ref[:, RG_WIDTH:RG_WIDTH + SSD_WIDTH] = _rms(
        yb * _silu(proj_ref[:, OFF_BZ:OFF_BZ + SSD_WIDTH]), snorm[...])

    q = _silu(proj_ref[:, OFF_CQ:OFF_CQ + HG_WIDTH])
    lb = _lower_bound_row(hlb[...], layer)
    gt = lb + (1.0 - lb) * _sigmoid(proj_ref[:, OFF_CF:OFF_CF + HG_WIDTH])
    kk = 1.0 - gt
    vv = proj_ref[:, OFF_CI:OFF_CI + HG_WIDTH]
    att = _dot(_bf(q * kk), headsum[...])
    o = att * vv + ohg_ref[...]
    ms = _dot(_bf(o * o), headmean[...])
    ymix_ref[:, RG_WIDTH + SSD_WIDTH:] = (
        o * lax.rsqrt(ms + EPS) * hnorm[...] * _silu(proj_ref[:, OFF_CG:OFF_CG + HG_WIDTH]))


def _sample_mixer(layer, proj, rgc, rgh, sc, ssd_all, o_hg, acc, mp, consts):
    nb = proj.shape[0]
    bb = DEC_BB
    params = [mp["caw"], mp["cab"], mp["wg"], mp["bg"], mp["lam"], mp["cbw"], mp["cbb"],
              mp["dtb"], mp["alog"], mp["dexp"], mp["snorm"], mp["hlb"], mp["hnorm"],
              consts["expand"], consts["headsum"], consts["headmean"]]
    row2 = lambda w: pl.BlockSpec((bb, w), lambda i: (i, 0))
    small_specs = [row2(3 * RG_WIDTH), row2(RG_WIDTH), row2(3 * SSD_CONV_DIM)]
    ssd_spec = pl.BlockSpec((None, bb, SSD_HEADS, SSD_HEAD_DIM, SSD_STATE),
                            lambda i: (layer, i, 0, 0, 0))
    out_shape = (
        jax.ShapeDtypeStruct((nb, D_MIX), F32),
        jax.ShapeDtypeStruct((nb, 3 * RG_WIDTH), F32),
        jax.ShapeDtypeStruct((nb, RG_WIDTH), F32),
        jax.ShapeDtypeStruct((nb, 3 * SSD_CONV_DIM), F32),
        jax.ShapeDtypeStruct(ssd_all.shape, F32),
    )
    n_acc = len(acc)
    n_lead = 6
    ssd_out_spec = ssd_spec if n_acc else pl.BlockSpec(
        (DEPTH, bb, SSD_HEADS, SSD_HEAD_DIM, SSD_STATE), lambda i: (0, i, 0, 0, 0))
    return pl.pallas_call(
        functools.partial(_sample_mixer_kernel, layer, PAST_LEN, n_acc),
        grid=(nb // bb,),
        in_specs=[row2(NPROJ)] + small_specs + [ssd_spec, row2(HG_WIDTH)]
        + [pl.BlockSpec(memory_space=pl.ANY)] * n_acc + [_const_spec(p) for p in params],
        out_specs=tuple([row2(D_MIX)] + small_specs + [ssd_out_spec]),
        out_shape=out_shape,
        input_output_aliases={n_lead + k: 4 + k for k in range(n_acc)},
        compiler_params=pltpu.CompilerParams(
            dimension_semantics=("parallel",), vmem_limit_bytes=VMEM_LIMIT),
        name="sample_mixer",
    )(proj, rgc, rgh, sc, ssd_all, o_hg, *acc, *params)


def _post_kernel(final, h_ref, y_ref, p_ref, wo, nffn, wup, wdown, nple, wgate, wproj, nfin, o_ref):
    h = h_ref[...] + _dot(_bf(y_ref[...]), wo[...])
    hn = _bf(_rms(h, nffn[...]))
    for j in range(D_FF // FF_CHUNK):
        cs = slice(j * FF_CHUNK, (j + 1) * FF_CHUNK)
        z = jnp.square(jnp.maximum(_dot(hn, wup[:, cs]), 0.0))
        h = h + _dot(_bf(z), wdown[cs, :])
    gate = _sigmoid(_dot(_bf(_rms(h, nple[...])), wgate[...]))
    h = h + gate * _dot(_bf(p_ref[...]), wproj[...])
    if final:
        h = _rms(h, nfin[...])
    o_ref[...] = h


def _post(final, layer, h2d, ymix2d, p_all, lp, nfin, tm):
    m = h2d.shape[0]
    params = [lp["wo"], lp["nffn"], lp["wup"], lp["wdown"], lp["nple"], lp["wgate"], lp["wproj"], nfin]
    return pl.pallas_call(
        functools.partial(_post_kernel, final),
        grid=(m // tm,),
        in_specs=[pl.BlockSpec((tm, D_MODEL), lambda i: (i, 0)),
                  pl.BlockSpec((tm, D_MIX), lambda i: (i, 0)),
                  pl.BlockSpec((None, tm, PLE_DIM), lambda i: (layer, i, 0))]
        + [_const_spec(p) for p in params],
        out_specs=pl.BlockSpec((tm, D_MODEL), lambda i: (i, 0)),
        out_shape=jax.ShapeDtypeStruct((m, D_MODEL), F32),
        compiler_params=pltpu.CompilerParams(
            dimension_semantics=("parallel",), vmem_limit_bytes=VMEM_LIMIT),
        name="post_mixer",
    )(h2d, ymix2d, p_all, *params)


def _constants():
    t = np.arange(TC)
    ltri = (t[None, :] <= t[:, None]).astype(np.float32)
    utri = ltri.T.copy()
    expand = np.zeros((LANES, SSD_WIDTH), np.float32)
    for hd in range(SSD_HEADS):
        expand[hd, hd * SSD_HEAD_DIM:(hd + 1) * SSD_HEAD_DIM] = 1.0
    emat = np.zeros(((N_LEVELS + 1) * TC, TC), np.float32)
    for lev in range(1, N_LEVELS + 1):
        blk = 1 << lev
        mid = (t // blk) * blk + blk // 2
        upper = t >= mid
        r0 = (lev - 1) * TC
        emat[r0:r0 + TC] = (
            (upper[:, None] & (t[None, :] >= mid[:, None]) & (t[None, :] <= t[:, None]))
            | ((~upper)[:, None] & (t[None, :] > t[:, None]) & (t[None, :] < mid[:, None])))
    emat[N_LEVELS * TC:] = ltri
    cmask = np.zeros((N_LEVELS + 1, TC, 2 * TC), np.float32)
    cmask[0] = np.tile(np.eye(TC, dtype=np.float32), (1, 2))
    for lev in range(1, N_LEVELS + 1):
        blk = 1 << lev
        up = (t % blk) >= blk // 2
        m = up[:, None] & (~up)[None, :] & ((t // blk)[:, None] == (t // blk)[None, :])
        cmask[lev] = np.tile(m.astype(np.float32), (1, 2))
    hidx = np.arange(HG_WIDTH) // HG_HEAD_DIM
    bd = (hidx[:, None] == hidx[None, :]).astype(np.float32)
    pair = bd[:LANES, :LANES]
    return {
        "ltri3": jnp.asarray(np.tile(ltri, (1, 3)), BF16),
        "utri3": jnp.asarray(np.tile(utri, (3, 1)), BF16),
        "expand": jnp.asarray(expand, BF16),
        "expand3": jnp.asarray(np.tile(expand, (3, 1)), BF16),
        "emat2": jnp.asarray(np.tile(emat, (1, 2)), BF16),
        "cmask": jnp.asarray(cmask, F32),
        "pairmask": jnp.asarray(pair, F32),
        "pairmean": jnp.asarray(pair / HG_HEAD_DIM, BF16),
        "headsum": jnp.asarray(bd, BF16), "headmean": jnp.asarray(bd / HG_HEAD_DIM, BF16),
    }


def _block_diag(w):
    hh, d, _ = w.shape
    eye = jnp.eye(hh, dtype=w.dtype)
    return (eye[:, None, :, None] * w[:, :, None, :]).reshape(hh * d, hh * d)


def _pair_gate_weights(wa, wx):
    hh, d, _ = wa.shape
    pa = jnp.stack([_block_diag(wa[2 * g:2 * g + 2]) for g in range(hh // 2)])
    px = jnp.stack([_block_diag(wx[2 * g:2 * g + 2]) for g in range(hh // 2)])
    return jnp.concatenate([pa, px], axis=2)


def _pad_lanes(v):
    return jnp.pad(v, (0, LANES - v.shape[0]))[None, :]


def kernel(x_prompt, x_sample, state_rg_conv, state_rg_h, state_ssd_conv, state_ssd, state_hgrn,
           p_prompt, p_sample, norm_mix, w_in, conv_a_w, conv_a_b, rg_wa, rg_ba, rg_wx, rg_bx,
           rg_lambda, conv_b_w, conv_b_b, ssd_dt_bias, ssd_a_log, ssd_d, ssd_norm,
           hg_lower_bounds, hg_norm, w_out, norm_ffn, w_up, w_down, norm_ple, w_ple_gate,
           w_ple_proj, norm_final):
    bsz, seq, _ = x_prompt.shape
    nb = x_sample.shape[0]
    consts = _constants()
    w_in_t = jnp.swapaxes(w_in, 1, 2)
    nfin = norm_final[None, :]
    pp_all = p_prompt.reshape(DEPTH, bsz * seq, PLE_DIM)
    ps_all = p_sample.reshape(DEPTH, nb, PLE_DIM)

    hp = x_prompt.reshape(bsz * seq, D_MODEL)
    hs = x_sample.reshape(nb, D_MODEL)
    st_p = [[] for _ in range(5)]
    st_s = [[] for _ in range(3)]
    hg_t_all = jnp.transpose(state_hgrn, (0, 2, 3, 4, 1))
    acc_ssd = ()
    acc_hg = ()
    for i in range(DEPTH):
        w_re = _prep_w_in(w_in_t, i)
        gmix = norm_mix[i][None, :]
        mp = {
            "caw": conv_a_w[i], "cab": conv_a_b[i][None, :],
            "wg": jnp.concatenate([_block_diag(rg_wa[i]), _block_diag(rg_wx[i])], axis=1).astype(BF16),
            "bg": jnp.concatenate([rg_ba[i], rg_bx[i]])[None, :],
            "wgp": _pair_gate_weights(rg_wa[i], rg_wx[i]).astype(BF16),
            "bgp": jnp.concatenate([rg_ba[i].reshape(NGRP, 1, LANES),
                                    rg_bx[i].reshape(NGRP, 1, LANES)], axis=2),
            "lam": rg_lambda[i][None, :],
            "cbw": conv_b_w[i], "cbb": conv_b_b[i][None, :],
            "dtb": _pad_lanes(ssd_dt_bias[i]), "alog": _pad_lanes(ssd_a_log[i]),
            "dexp": jnp.repeat(ssd_d[i], SSD_HEAD_DIM)[None, :],
            "snorm": ssd_norm[i][None, :],
            "hlb": hg_lower_bounds, "hnorm": jnp.tile(hg_norm[i], HG_HEADS)[None, :],
        }
        lp = {
            "wo": _cast_bf16(w_out, i, 512), "nffn": norm_ffn[i][None, :],
            "wup": _cast_bf16(w_up, i, 256), "wdown": _cast_bf16(w_down, i, 1024),
            "nple": norm_ple[i][None, :], "wgate": _cast_bf16(w_ple_gate, i, 512),
            "wproj": _cast_bf16(w_ple_proj, i, 256),
        }
        final = i == DEPTH - 1

        ymix_p, rgc, rgh, sc, ssd, hg = _prompt_mixer(
            i, hp.reshape(bsz, seq, D_MODEL), gmix, w_re, mp, consts)
        hp = _post(final, i, hp, ymix_p.reshape(bsz * seq, D_MIX), pp_all, lp, nfin, TM_PROMPT)
        for lst, s in zip(st_p, (rgc, rgh.reshape(bsz, RG_WIDTH), sc,
                                 ssd.reshape(bsz, SSD_HEADS, SSD_HEAD_DIM, SSD_STATE), hg)):
            lst.append(s)

        proj_s = _in_proj(hs, gmix, w_re, nb)
        o_hg_t, hg_acc = _hg_step(i, proj_s, hg_lower_bounds, hg_t_all, acc_hg)
        acc_hg = (hg_acc,)
        ymix_s, rgc, rgh, sc, ssd_acc = _sample_mixer(
            i, proj_s,
            state_rg_conv[i].reshape(nb, 3 * RG_WIDTH), state_rg_h[i],
            state_ssd_conv[i].reshape(nb, 3 * SSD_CONV_DIM),
            state_ssd, o_hg_t.T, acc_ssd, mp, consts)
        acc_ssd = (ssd_acc,)
        hs = _post(final, i, hs, ymix_s, ps_all, lp, nfin, nb)
        for lst, s in zip(st_s, (rgc.reshape(nb, CONV_W - 1, RG_WIDTH), rgh,
                                 sc.reshape(nb, CONV_W - 1, SSD_CONV_DIM))):
            lst.append(s)

    y_prompt = hp.reshape(bsz, seq, D_MODEL)
    y_sample = hs.reshape(nb, 1, D_MODEL)
    outs_p = [jnp.stack(lst) for lst in st_p]
    outs_s = [jnp.stack(lst) for lst in st_s] + [
        acc_ssd[0], jnp.transpose(acc_hg[0], (0, 4, 1, 2, 3))]
    return (y_prompt, y_sample, *outs_p, *outs_s)
```

```python
import functools

import numpy as np
import jax
import jax.numpy as jnp
from jax import lax
from jax.experimental import pallas as pl
from jax.experimental.pallas import tpu as pltpu

F32 = jnp.float32
BF16 = jnp.bfloat16

D_MODEL = 1024
DEPTH = 2
PAST_LEN = 16384
PLE_DIM = 256
D_FF = 4 * D_MODEL
CONV_W = 4
EPS = 1e-6
RG_WIDTH = 512
RG_HEADS = 8
RG_HEAD_DIM = 64
RG_C = 8.0
SSD_WIDTH = 512
SSD_HEAD_DIM = 64
SSD_HEADS = 8
SSD_STATE = 128
SSD_CONV_DIM = SSD_WIDTH + 2 * SSD_STATE
HG_WIDTH = 512
HG_HEADS = 8
HG_HEAD_DIM = 64
D_MIX = RG_WIDTH + SSD_WIDTH + HG_WIDTH

LOG2E = 1.4426950408889634
LANES = 128
SUBLANES = 8
NGRP = RG_WIDTH // LANES

OFF_AX = 0
OFF_AG = 512
OFF_BZ = 1024
OFF_XBC = 1536
OFF_CQ = 2304
OFF_CF = 2816
OFF_CI = 3328
OFF_CG = 3840
OFF_DT = 4352
NPROJ = OFF_DT + LANES
DT_COL = 2 * RG_WIDTH + SSD_WIDTH + SSD_CONV_DIM
D_IN_PROJ = DT_COL + SSD_HEADS + 4 * HG_WIDTH

TC = 128
HALF = TC // 2
N_LEVELS = 7
DEC_BB = 8
TM_PROMPT = 512
FF_CHUNK = 1024
VMEM_LIMIT = 58 * 1024 * 1024


def _bf(x):
    return x.astype(BF16)


def _dot(a, b):
    return jnp.dot(a, b, preferred_element_type=F32)


def _dot_nt(a, b):
    return lax.dot_general(a, b, (((1,), (1,)), ((), ())), preferred_element_type=F32)


def _split3(x):
    hi = _bf(x)
    r1 = x - hi.astype(F32)
    mid = _bf(r1)
    lo = _bf(r1 - mid.astype(F32))
    return hi, mid, lo


def _dot_x_c(x, c):
    hi, mid, lo = _split3(x)
    return _dot(hi, c) + _dot(mid, c) + _dot(lo, c)


def _dot_x_c3(x, c3):
    return _dot(jnp.concatenate(_split3(x), axis=1), c3)


def _dot_c3_x(c3, x):
    return _dot(c3, jnp.concatenate(_split3(x), axis=0))


def _sigmoid(x):
    return 0.5 * jnp.tanh(0.5 * x) + 0.5


def _silu(x):
    h = 0.5 * x
    return h * jnp.tanh(h) + h


def _softplus(x):
    return jnp.maximum(x, 0.0) + jnp.log1p(jnp.exp(-jnp.abs(x)))


def _gelu_tanh(x):
    c = np.float32(np.sqrt(2.0 / np.pi))
    return 0.5 * x * (1.0 + jnp.tanh(c * (x + 0.044715 * (x * x * x))))


def _rms(x, gain):
    return x * lax.rsqrt(jnp.mean(x * x, axis=-1, keepdims=True) + EPS) * gain


def _lower_bound_row(hlb, layer):
    m = jnp.max(hlb, axis=0, keepdims=True)
    e = jnp.exp(hlb - m)
    sm = e / jnp.sum(e, axis=0, keepdims=True)
    lb = jnp.zeros((1, HG_WIDTH), F32)
    for j in range(1, layer + 1):
        lb = lb + sm[j:j + 1, :]
    return lb


def _rg_coeffs(gates, lam_row):
    w = gates.shape[1] // 2
    r = _sigmoid(gates[:, :w])
    i = _sigmoid(gates[:, w:])
    log_a = -RG_C * r * _softplus(-lam_row)
    a = jnp.exp(log_a)
    mult = jnp.sqrt(-jnp.tanh(log_a) * (a * a + 1.0))
    return a, mult, i


def _lane_halves(x):
    lane = lax.broadcasted_iota(jnp.int32, x.shape, 1)
    lo = jnp.where(lane < HG_HEAD_DIM, x, 0.0)
    hi = jnp.where(lane >= HG_HEAD_DIM, x, 0.0)
    return jnp.concatenate([lo, hi], axis=0)


def _const_spec(arr):
    nd = arr.ndim
    return pl.BlockSpec(arr.shape, lambda *_: (0,) * nd, pipeline_mode=pl.Buffered(1))


def _cast_kernel(w_ref, o_ref):
    o_ref[...] = _bf(w_ref[...])


def _cast_bf16(w_all, layer, rb):
    _, rows, cols = w_all.shape
    return pl.pallas_call(
        _cast_kernel,
        grid=(rows // rb,),
        in_specs=[pl.BlockSpec((None, rb, cols), lambda i: (layer, i, 0))],
        out_specs=pl.BlockSpec((rb, cols), lambda i: (i, 0)),
        out_shape=jax.ShapeDtypeStruct((rows, cols), BF16),
        compiler_params=pltpu.CompilerParams(dimension_semantics=("parallel",)),
        name="cast_bf16",
    )(w_all)


def _win_kernel(wt_ref, o_ref):
    o_ref[:, 0:OFF_CQ] = _bf(wt_ref[0:DT_COL, :].T)
    o_ref[:, OFF_CQ:OFF_DT] = _bf(wt_ref[DT_COL + SSD_HEADS:D_IN_PROJ, :].T)
    dt = jnp.concatenate([wt_ref[DT_COL:DT_COL + SSD_HEADS, :],
                          jnp.zeros((LANES - SSD_HEADS, LANES), F32)], axis=0)
    o_ref[:, OFF_DT:NPROJ] = _bf(dt.T)


def _prep_w_in(wt_all, layer):
    return pl.pallas_call(
        _win_kernel,
        grid=(D_MODEL // LANES,),
        in_specs=[pl.BlockSpec((None, D_IN_PROJ, LANES), lambda i: (layer, 0, i))],
        out_specs=pl.BlockSpec((LANES, NPROJ), lambda i: (i, 0)),
        out_shape=jax.ShapeDtypeStruct((D_MODEL, NPROJ), BF16),
        compiler_params=pltpu.CompilerParams(dimension_semantics=("parallel",)),
        name="prep_w_in",
    )(wt_all)


def _proj_kernel(x_ref, g_ref, w_ref, o_ref):
    u = _rms(x_ref[...], g_ref[...])
    o_ref[...] = _dot(_bf(u), w_ref[...])


def _in_proj(x2d, gain, w_bf, tm):
    m = x2d.shape[0]
    return pl.pallas_call(
        _proj_kernel,
        grid=(m // tm,),
        in_specs=[pl.BlockSpec((tm, D_MODEL), lambda i: (i, 0)),
                  _const_spec(gain), _const_spec(w_bf)],
        out_specs=pl.BlockSpec((tm, NPROJ), lambda i: (i, 0)),
        out_shape=jax.ShapeDtypeStruct((m, NPROJ), F32),
        compiler_params=pltpu.CompilerParams(
            dimension_semantics=("parallel",), vmem_limit_bytes=VMEM_LIMIT),
        name="in_proj",
    )(x2d, gain, w_bf)


def _mixer_chunk(layer, row0, pv, ymix, side_work,
                 caw, cab, wgp, bgp, lam, cbw, cbb, dtb, alog, dexp, snorm,
                 hlb, hnorm, ltri3, utri3, expand3, emat2, cmask, pairmask, pairmean,
                 xea, xeb, hcar, s_t, st_hg):
    row_h = lax.broadcasted_iota(jnp.int32, (HALF, 1), 0)
    sub_h = row_h & (SUBLANES - 1)
    first_head = lax.broadcasted_iota(jnp.int32, (HALF, LANES), 1) < HG_HEAD_DIM
    first_head_tc = lax.broadcasted_iota(jnp.int32, (TC, LANES), 1) < HG_HEAD_DIM
    col_tc = lax.broadcasted_iota(jnp.int32, (HALF, TC), 1)
    row_tc = lax.broadcasted_iota(jnp.int32, (HALF, TC), 0)

    tiles = [slice(t * HALF, (t + 1) * HALF) for t in range(TC // HALF)]

    def lanes(off, g):
        return slice(off + g * LANES, off + (g + 1) * LANES)

    def conv_rows(prev8, x, w_ref, b_ref, gs):
        xcat = jnp.concatenate([prev8, x], axis=0)
        y = b_ref[:, gs] + w_ref[CONV_W - 1:CONV_W, gs] * x
        for j in range(1, CONV_W):
            y = y + w_ref[CONV_W - 1 - j:CONV_W - j, gs] * pltpu.roll(xcat, j, 0)[SUBLANES:, :]
        return y, x[x.shape[0] - SUBLANES:, :]

    def conv(tail_ref, w_ref, b_ref, src_off, g):
        gs = lanes(0, g)
        y, tail = conv_rows(tail_ref[:, gs], pv[:, lanes(src_off, g)], w_ref, b_ref, gs)
        tail_ref[:, gs] = tail
        return y

    def mixer_a(g):
        gs = lanes(0, g)
        prev8 = xea[:, gs]
        carry = hcar[0:1, gs]
        for t, rs in enumerate(tiles):
            xa, prev8 = conv_rows(prev8, pv[rs, lanes(OFF_AX, g)], caw, cab, gs)
            a, mult, gi = _rg_coeffs(_dot(_bf(xa), wgp[g]) + bgp[g], lam[:, gs])
            mult = jnp.where(row_h + (row0 + t * HALF) == 0, 1.0, mult)
            b = mult * (gi * xa)
            k = 1
            while k < SUBLANES:
                keep = sub_h >= k
                a_sh = jnp.where(keep, pltpu.roll(a, k, 0), 1.0)
                b_sh = jnp.where(keep, pltpu.roll(b, k, 0), 0.0)
                b = b + a * b_sh
                a = a * a_sh
                k *= 2
            slabs = []
            for r in range(HALF // SUBLANES):
                sl = slice(r * SUBLANES, (r + 1) * SUBLANES)
                h_r = a[sl, :] * carry + b[sl, :]
                slabs.append(h_r)
                carry = h_r[SUBLANES - 1:SUBLANES, :]
            ymix[rs, gs] = jnp.concatenate(slabs, axis=0) * _gelu_tanh(
                pv[rs, lanes(OFF_AG, g)])
        xea[:, gs] = prev8
        hcar[0:1, gs] = carry

    bm = _silu(conv(xeb, cbw, cbb, OFF_XBC, NGRP))
    cm = _silu(conv(xeb, cbw, cbb, OFF_XBC, NGRP + 1))
    dtp = _softplus(pv[:, OFF_DT:OFF_DT + LANES] + dtb[...])
    v_da = dtp * (-LOG2E * jnp.exp(alog[...]))
    cum = _dot_c3_x(ltri3[...], v_da)
    cum_t = _dot_x_c3(v_da.T, utri3[...])
    dt_e = _dot_x_c3(dtp, expand3[...])
    cum_e = _dot_x_c3(cum, expand3[...])
    cb = _dot_nt(_bf(cm), _bf(bm))
    cmb = _bf(cm)
    bmt = _bf(bm.T)
    ssq_parts = []

    def mixer_b(g):
        gs = lanes(0, g)
        cum_last = cum_e[TC - 1:TC, gs]
        s_old = _bf(s_t[g])
        prev8 = xeb[:, gs]
        xs_t, x2a, x2b, wx = [], [], [], []
        for rs in tiles:
            xs, prev8 = conv_rows(prev8, pv[rs, lanes(OFF_XBC, g)], cbw, cbb, gs)
            xs = _silu(xs)
            dt_g = dt_e[rs, gs]
            dtx = xs * dt_g
            xs_t.append(xs)
            x2a.append(_bf(jnp.where(first_head, dtx, 0.0)))
            x2b.append(_bf(jnp.where(first_head, 0.0, dtx)))
            wx.append(_bf(jnp.exp2(cum_last - cum_e[rs, gs]) * dt_g * xs))
        xeb[:, gs] = prev8
        x2 = jnp.concatenate(x2a + x2b, axis=0)
        for t, rs in enumerate(tiles):
            gmats = []
            for hh in range(2):
                hd = 2 * g + hh
                seg = cum[rs, hd:hd + 1] - cum_t[hd:hd + 1, :]
                gmats.append(_bf(cb[rs, :] * jnp.exp2(jnp.where(col_tc <= row_tc + t * HALF, seg, -jnp.inf))))
            y = _dot(jnp.concatenate(gmats, axis=1), x2)
            y = y + _dot(cmb[rs, :], s_old) * jnp.exp2(cum_e[rs, gs])
            yz = (y + dexp[:, gs] * xs_t[t]) * _silu(pv[rs, lanes(OFF_BZ, g)])
            ssq_parts.append(yz * yz)
            ymix[rs, lanes(RG_WIDTH, g)] = yz
        s_t[g] = jnp.exp2(cum_last) * s_t[g] + _dot(bmt, jnp.concatenate(wx, axis=0))

    lb_all = _lower_bound_row(hlb[...], layer)

    def forget_gate(g):
        lb = lb_all[:, g * LANES:(g + 1) * LANES]
        return lb + (1.0 - lb) * _sigmoid(pv[:, OFF_CF + g * LANES:OFF_CF + (g + 1) * LANES])

    def mixer_c(g, gt, xall):
        gs = lanes(0, g)
        b_off = N_LEVELS * TC
        bcum = xall[b_off:b_off + TC, :]
        blast = bcum[TC - 1:TC, :]
        q = _silu(pv[:, lanes(OFF_CQ, g)])
        kk = 1.0 - gt
        vv = pv[:, lanes(OFF_CI, g)]
        zeros = jnp.zeros((HG_HEAD_DIM, TC), BF16)

        def keys_by_head(kt):
            ktt = _bf(kt.T)
            return jnp.concatenate(
                [jnp.concatenate([ktt[:HG_HEAD_DIM, :], zeros], axis=1),
                 jnp.concatenate([zeros, ktt[HG_HEAD_DIM:, :]], axis=1)], axis=0)

        att = _dot(_bf(q), keys_by_head(kk)) * cmask[0]
        for lev in range(1, N_LEVELS + 1):
            e = jnp.exp2(xall[(lev - 1) * TC:lev * TC, :])
            att = att + _dot(_bf(q * e), keys_by_head(kk * e)) * cmask[lev]
        v2 = jnp.concatenate([_bf(jnp.where(first_head_tc, vv, 0.0)),
                              _bf(jnp.where(first_head_tc, 0.0, vv))], axis=0)
        o = _dot(_bf(att), v2) + _dot_nt(_bf(q * jnp.exp2(bcum)), _bf(st_hg[g]))
        khat = kk * jnp.exp2(blast - bcum)
        st_hg[g] = jnp.exp2(blast) * st_hg[g] + pairmask[...] * _dot(_bf(vv.T), _bf(khat))
        ms = _dot(_bf(o * o), pairmean[...])
        ymix[:, lanes(RG_WIDTH + SSD_WIDTH, g)] = (
            o * lax.rsqrt(ms + EPS) * hnorm[:, gs] * _silu(pv[:, lanes(OFF_CG, g)]))

    for pair in range(NGRP // 2):
        g0, g1 = 2 * pair, 2 * pair + 1
        gts = [forget_gate(g0), forget_gate(g1)]
        lf = jnp.log(jnp.concatenate(gts, axis=1)) * LOG2E
        hi = _bf(lf)
        lo = _bf(lf - hi.astype(F32))
        xall2 = _dot(emat2[...], jnp.concatenate([hi, lo], axis=0))
        for g, gt in zip((g0, g1), gts):
            side_work[g]()
            mixer_a(g)
            mixer_b(g)
            mixer_c(g, gt, xall2[:, (g - g0) * LANES:(g - g0 + 1) * LANES])

    for t, rs in enumerate(tiles):
        ssq = ssq_parts[t]
        for g in range(1, NGRP):
            ssq = ssq + ssq_parts[g * len(tiles) + t]
        rinv = lax.rsqrt(jnp.sum(ssq, axis=-1, keepdims=True) * (1.0 / SSD_WIDTH) + EPS)
        for g in range(NGRP):
            ys = lanes(RG_WIDTH, g)
            ymix[rs, ys] = ymix[rs, ys] * rinv * snorm[:, lanes(0, g)]


N_MIX_PARAMS = 20
MXU_N = 256
PROJ_BOUNDS = (0, 5 * MXU_N, 9 * MXU_N, 13 * MXU_N, NPROJ)
assert len(PROJ_BOUNDS) == NGRP + 1
CHUNKS_PER_STEP = 2


def _prompt_mixer_kernel(layer, pos0, x_ref, xn_ref, gmix, w_in_ref, *rest):
    params = rest[:N_MIX_PARAMS]
    ymix_ref, rgc_o, rgh_o, sc_o, ssd_o, hg_o = rest[N_MIX_PARAMS:N_MIX_PARAMS + 6]
    proj_a, proj_b, xea, xeb, hcar, s_t, st_hg = rest[N_MIX_PARAMS + 6:]
    states = (xea, xeb, hcar, s_t, st_hg)
    b = pl.program_id(0)
    c = pl.program_id(1)
    last = pl.num_programs(1) - 1

    def project(x):
        return _dot(_bf(_rms(x, gmix[...])), w_in_ref[...])

    @pl.when(jnp.logical_and(b == 0, c == 0))
    def _first_projection():
        proj_a[...] = project(x_ref[0, 0:TC, :])

    @pl.when(c == 0)
    def _init():
        for ref in states:
            ref[...] = jnp.zeros_like(ref)

    def projection_parts(x, dst):
        u = _bf(_rms(x, gmix[...]))
        def part(j):
            cs = slice(PROJ_BOUNDS[j], PROJ_BOUNDS[j + 1])
            dst[:, cs] = _dot(u, w_in_ref[:, cs])

        return [functools.partial(part, j) for j in range(NGRP)]

    row0 = c * (CHUNKS_PER_STEP * TC) + pos0
    _mixer_chunk(layer, row0, proj_a, ymix_ref.at[0, 0:TC],
                 projection_parts(x_ref[0, TC:2 * TC, :], proj_b), *params, *states)
    _mixer_chunk(layer, row0 + TC, proj_b, ymix_ref.at[0, TC:2 * TC],
                 projection_parts(xn_ref[0], proj_a), *params, *states)

    @pl.when(c == last)
    def _emit():
        rgc_o[0] = xea[SUBLANES - 3:SUBLANES, :]
        sc_o[0] = xeb[SUBLANES - 3:SUBLANES, :]
        rgh_o[0] = hcar[0:1, :]
        for g in range(NGRP):
            ssd_o[0, g * LANES:(g + 1) * LANES, :] = s_t[g].T
            s_pair = st_hg[g].T
            for hh in range(2):
                lo = hh * HG_HEAD_DIM
                hg_o[0, 2 * g + hh] = s_pair[lo:lo + HG_HEAD_DIM, lo:lo + HG_HEAD_DIM]


def _prompt_mixer(layer, x3, gmix, w_in_bf, mp, consts):
    bsz, seq, _ = x3.shape
    step_rows = CHUNKS_PER_STEP * TC
    nsteps = seq // step_rows
    n_chunks = bsz * (seq // TC)
    params = [mp["caw"], mp["cab"], mp["wgp"], mp["bgp"], mp["lam"], mp["cbw"], mp["cbb"],
              mp["dtb"], mp["alog"], mp["dexp"], mp["snorm"], mp["hlb"], mp["hnorm"],
              consts["ltri3"], consts["utri3"], consts["expand3"], consts["emat2"],
              consts["cmask"], consts["pairmask"], consts["pairmean"]]
    assert len(params) == N_MIX_PARAMS

    def next_chunk(b, c):
        flat = jnp.minimum((b * nsteps + c + 1) * CHUNKS_PER_STEP, n_chunks - 1)
        return flat // (seq // TC), flat % (seq // TC), 0

    out_shape = (
        jax.ShapeDtypeStruct((bsz, seq, D_MIX), F32),
        jax.ShapeDtypeStruct((bsz, CONV_W - 1, RG_WIDTH), F32),
        jax.ShapeDtypeStruct((bsz, 1, RG_WIDTH), F32),
        jax.ShapeDtypeStruct((bsz, CONV_W - 1, SSD_CONV_DIM), F32),
        jax.ShapeDtypeStruct((bsz, SSD_WIDTH, SSD_STATE), F32),
        jax.ShapeDtypeStruct((bsz, HG_HEADS, HG_HEAD_DIM, HG_HEAD_DIM), F32),
    )
    out_specs = (
        pl.BlockSpec((1, step_rows, D_MIX), lambda b, c: (b, c, 0)),
        pl.BlockSpec((1, CONV_W - 1, RG_WIDTH), lambda b, c: (b, 0, 0)),
        pl.BlockSpec((1, 1, RG_WIDTH), lambda b, c: (b, 0, 0)),
        pl.BlockSpec((1, CONV_W - 1, SSD_CONV_DIM), lambda b, c: (b, 0, 0)),
        pl.BlockSpec((1, SSD_WIDTH, SSD_STATE), lambda b, c: (b, 0, 0)),
        pl.BlockSpec((1, HG_HEADS, HG_HEAD_DIM, HG_HEAD_DIM), lambda b, c: (b, 0, 0, 0)),
    )
    scratch = [
        pltpu.VMEM((TC, NPROJ), F32),
        pltpu.VMEM((TC, NPROJ), F32),
        pltpu.VMEM((SUBLANES, RG_WIDTH), F32),
        pltpu.VMEM((SUBLANES, SSD_CONV_DIM), F32),
        pltpu.VMEM((SUBLANES, RG_WIDTH), F32),
        pltpu.VMEM((NGRP, SSD_STATE, LANES), F32),
        pltpu.VMEM((NGRP, LANES, LANES), F32),
    ]
    return pl.pallas_call(
        functools.partial(_prompt_mixer_kernel, layer, 0),
        grid=(bsz, nsteps),
        in_specs=[pl.BlockSpec((1, step_rows, D_MODEL), lambda b, c: (b, c, 0)),
                  pl.BlockSpec((1, TC, D_MODEL), next_chunk),
                  _const_spec(gmix), _const_spec(w_in_bf)]
        + [_const_spec(p) for p in params],
        out_specs=out_specs,
        out_shape=out_shape,
        scratch_shapes=scratch,
        compiler_params=pltpu.CompilerParams(
            dimension_semantics=("arbitrary", "arbitrary"), vmem_limit_bytes=VMEM_LIMIT),
        name="prompt_mixer",
    )(x3, x3, gmix, w_in_bf, *params)


N_POST_PARAMS = 8


def _prompt_layer_kernel(layer, final, pos0, nsteps, n_valid,
                         x_ref, xn_ref, xp_ref, p_ref, gmix, w_in_ref, *rest):
    params = rest[:N_MIX_PARAMS]
    k = N_MIX_PARAMS + N_POST_PARAMS
    wo, nffn, wup, wdown, nple, wgate, wproj, nfin = rest[N_MIX_PARAMS:k]
    h_out, rgc_o, rgh_o, sc_o, ssd_o, hg_o = rest[k:k + 6]
    proj_a, proj_b, ymix_s, xea, xeb, hcar, s_t, st_hg = rest[k + 6:]
    states = (xea, xeb, hcar, s_t, st_hg)
    s = pl.program_id(0)
    live = s < n_valid
    c = jnp.minimum(s, n_valid - 1) % nsteps

    @pl.when(s == 0)
    def _prologue():
        proj_a[...] = _dot(_bf(_rms(x_ref[0, 0:TC, :], gmix[...])), w_in_ref[...])
        ymix_s[...] = jnp.zeros_like(ymix_s)

    @pl.when(jnp.logical_and(c == 0, live))
    def _init():
        for ref in states:
            ref[...] = jnp.zeros_like(ref)

    ymix_prev = _bf(ymix_s[...])
    n_ff = D_FF // FF_CHUNK
    carry = {}

    def post_piece(j):
        if j == 0:
            carry["h"] = xp_ref[0] + _dot(ymix_prev, wo[...])
            carry["hn"] = _bf(_rms(carry["h"], nffn[...]))
        elif j <= n_ff:
            cs = slice((j - 1) * FF_CHUNK, j * FF_CHUNK)
            z = jnp.square(jnp.maximum(_dot(carry["hn"], wup[:, cs]), 0.0))
            carry["h"] = carry["h"] + _dot(_bf(z), wdown[cs, :])
        elif j == n_ff + 1:
            h = carry["h"]
            gate = _sigmoid(_dot(_bf(_rms(h, nple[...])), wgate[...]))
            h = h + gate * _dot(_bf(p_ref[0]), wproj[...])
            h_out[0] = _rms(h, nfin[...]) if final else h

    def side_work(x, dst, first_piece):
        u = _bf(_rms(x, gmix[...]))

        def slot(j):
            cs = slice(PROJ_BOUNDS[j], PROJ_BOUNDS[j + 1])
            dst[:, cs] = _dot(u, w_in_ref[:, cs])
            post_piece(first_piece + j)

        return [functools.partial(slot, j) for j in range(NGRP)]

    row0 = c * (CHUNKS_PER_STEP * TC) + pos0
    _mixer_chunk(layer, row0, proj_a, ymix_s.at[0:TC],
                 side_work(x_ref[0, TC:2 * TC, :], proj_b, 0), *params, *states)
    _mixer_chunk(layer, row0 + TC, proj_b, ymix_s.at[TC:2 * TC],
                 side_work(xn_ref[0], proj_a, NGRP), *params, *states)

    @pl.when(jnp.logical_and(c == nsteps - 1, live))
    def _emit():
        rgc_o[0] = xea[SUBLANES - 3:SUBLANES, :]
        sc_o[0] = xeb[SUBLANES - 3:SUBLANES, :]
        rgh_o[0] = hcar[0:1, :]
        for g in range(NGRP):
            ssd_o[0, g * LANES:(g + 1) * LANES, :] = s_t[g].T
            s_pair = st_hg[g].T
            for hh in range(2):
                lo = hh * HG_HEAD_DIM
                hg_o[0, 2 * g + hh] = s_pair[lo:lo + HG_HEAD_DIM, lo:lo + HG_HEAD_DIM]


def _prompt_layer(layer, final, x3, p_all4, gmix, w_in_bf, mp, lp, nfin, consts):
    bsz, seq, _ = x3.shape
    step_rows = CHUNKS_PER_STEP * TC
    nsteps = seq // step_rows
    n_valid = bsz * nsteps
    chunks_per_row = seq // TC
    n_chunks = bsz * chunks_per_row
    params = [mp["caw"], mp["cab"], mp["wgp"], mp["bgp"], mp["lam"], mp["cbw"], mp["cbb"],
              mp["dtb"], mp["alog"], mp["dexp"], mp["snorm"], mp["hlb"], mp["hnorm"],
              consts["ltri3"], consts["utri3"], consts["expand3"], consts["emat2"],
              consts["cmask"], consts["pairmask"], consts["pairmean"]]
    post_params = [lp["wo"], lp["nffn"], lp["wup"], lp["wdown"], lp["nple"], lp["wgate"],
                   lp["wproj"], nfin]
    assert len(params) == N_MIX_PARAMS and len(post_params) == N_POST_PARAMS

    def mix_step(s):
        return jnp.minimum(s, n_valid - 1)

    def cur(s):
        return mix_step(s) // nsteps, mix_step(s) % nsteps, 0

    def nxt(s):
        flat = jnp.minimum((mix_step(s) + 1) * CHUNKS_PER_STEP, n_chunks - 1)
        return flat // chunks_per_row, flat % chunks_per_row, 0

    def prv(s):
        sp = jnp.maximum(s - 1, 0)
        return sp // nsteps, sp % nsteps, 0

    def state_idx(nd):
        return lambda s: (mix_step(s) // nsteps,) + (0,) * (nd - 1)

    out_shape = (
        jax.ShapeDtypeStruct((bsz, seq, D_MODEL), F32),
        jax.ShapeDtypeStruct((bsz, CONV_W - 1, RG_WIDTH), F32),
        jax.ShapeDtypeStruct((bsz, 1, RG_WIDTH), F32),
        jax.ShapeDtypeStruct((bsz, CONV_W - 1, SSD_CONV_DIM), F32),
        jax.ShapeDtypeStruct((bsz, SSD_WIDTH, SSD_STATE), F32),
        jax.ShapeDtypeStruct((bsz, HG_HEADS, HG_HEAD_DIM, HG_HEAD_DIM), F32),
    )
    out_specs = (
        pl.BlockSpec((1, step_rows, D_MODEL), prv),
        pl.BlockSpec((1, CONV_W - 1, RG_WIDTH), state_idx(3)),
        pl.BlockSpec((1, 1, RG_WIDTH), state_idx(3)),
        pl.BlockSpec((1, CONV_W - 1, SSD_CONV_DIM), state_idx(3)),
        pl.BlockSpec((1, SSD_WIDTH, SSD_STATE), state_idx(3)),
        pl.BlockSpec((1, HG_HEADS, HG_HEAD_DIM, HG_HEAD_DIM), state_idx(4)),
    )
    scratch = [
        pltpu.VMEM((TC, NPROJ), F32),
        pltpu.VMEM((TC, NPROJ), F32),
        pltpu.VMEM((step_rows, D_MIX), F32),
        pltpu.VMEM((SUBLANES, RG_WIDTH), F32),
        pltpu.VMEM((SUBLANES, SSD_CONV_DIM), F32),
        pltpu.VMEM((SUBLANES, RG_WIDTH), F32),
        pltpu.VMEM((NGRP, SSD_STATE, LANES), F32),
        pltpu.VMEM((NGRP, LANES, LANES), F32),
    ]
    return pl.pallas_call(
        functools.partial(_prompt_layer_kernel, layer, final, 0, nsteps, n_valid),
        grid=(n_valid + 1,),
        in_specs=[pl.BlockSpec((1, step_rows, D_MODEL), cur),
                  pl.BlockSpec((1, TC, D_MODEL), nxt),
                  pl.BlockSpec((1, step_rows, D_MODEL), prv),
                  pl.BlockSpec((None, 1, step_rows, PLE_DIM), lambda s: (layer,) + prv(s)),
                  _const_spec(gmix), _const_spec(w_in_bf)]
        + [_const_spec(p) for p in params + post_params],
        out_specs=out_specs,
        out_shape=out_shape,
        scratch_shapes=scratch,
        compiler_params=pltpu.CompilerParams(
            dimension_semantics=("arbitrary",), vmem_limit_bytes=VMEM_LIMIT),
        name="prompt_layer",
    )(x3, x3, x3, p_all4, gmix, w_in_bf, *params, *post_params)


def _hg_step_kernel(layer, n_acc, proj_ref, hlb, st_ref, *rest):
    ot_ref, st_o, qh_s, f_s, k_s, v_s = rest[n_acc:]
    h = pl.program_id(0)
    if n_acc == 0:
        for other in range(DEPTH):
            if other != layer:
                st_o[other] = jnp.zeros(st_o.shape[1:], F32)
        st_o = st_o.at[layer]

    @pl.when(h == 0)
    def _prep():
        q = _silu(proj_ref[:, OFF_CQ:OFF_CQ + HG_WIDTH])
        lb = _lower_bound_row(hlb[...], layer)
        gt = lb + (1.0 - lb) * _sigmoid(proj_ref[:, OFF_CF:OFF_CF + HG_WIDTH])
        fdec = jnp.exp(jnp.log(gt))
        qh_s[...] = (q * fdec).T
        f_s[...] = fdec.T
        k_s[...] = (1.0 - gt).T
        v_s[...] = proj_ref[:, OFF_CI:OFF_CI + HG_WIDTH].T

    base = pl.multiple_of(h * HG_HEAD_DIM, HG_HEAD_DIM)
    v_h = v_s[pl.ds(base, HG_HEAD_DIM), :]

    def body(dk, acc):
        s = st_ref[dk]
        st_o[dk] = f_s[pl.ds(base + dk, 1), :] * s + k_s[pl.ds(base + dk, 1), :] * v_h
        return acc + qh_s[pl.ds(base + dk, 1), :] * s

    ot_ref[...] = lax.fori_loop(0, HG_HEAD_DIM, body,
                                jnp.zeros((HG_HEAD_DIM, LANES), F32), unroll=8)


def _hg_step(layer, proj, hlb, st_all, acc):
    nb = proj.shape[0]
    blk = (HG_HEAD_DIM, HG_HEAD_DIM, nb)
    st_spec = pl.BlockSpec((None, None) + blk, lambda h: (layer, h, 0, 0, 0))
    n_acc = len(acc)
    st_out_spec = st_spec if n_acc else pl.BlockSpec(
        (DEPTH, None) + blk, lambda h: (0, h, 0, 0, 0))
    return pl.pallas_call(
        functools.partial(_hg_step_kernel, layer, n_acc),
        grid=(HG_HEADS,),
        in_specs=[_const_spec(proj), _const_spec(hlb), st_spec]
        + [pl.BlockSpec(memory_space=pl.ANY)] * n_acc,
        out_specs=(pl.BlockSpec((HG_HEAD_DIM, nb), lambda h: (h, 0)), st_out_spec),
        out_shape=(jax.ShapeDtypeStruct((HG_WIDTH, nb), F32),
                   jax.ShapeDtypeStruct(st_all.shape, F32)),
        scratch_shapes=[pltpu.VMEM((HG_WIDTH, nb), F32)] * 4,
        input_output_aliases={3 + k: 1 + k for k in range(n_acc)},
        compiler_params=pltpu.CompilerParams(
            dimension_semantics=("arbitrary",), vmem_limit_bytes=VMEM_LIMIT),
        name="hg_step",
    )(proj, hlb, st_all, *acc)


def _pad_rows_t(x):
    pad = jnp.zeros((LANES - x.shape[0], x.shape[1]), F32)
    return jnp.concatenate([x, pad], axis=0).T


def _sample_mixer_kernel(layer, pos0, n_acc, proj_ref, rgc_ref, rgh_ref, sc_ref, ssd_ref, ohg_ref, *rest):
    (caw, cab, wg, bg, lam, cbw, cbb, dtb, alog, dexp, snorm,
     hlb, hnorm, expand, headsum, headmean,
     ymix_ref, rgc_o, rgh_o, sc_o, ssd_o) = rest[n_acc:]
    bb = DEC_BB
    if n_acc == 0:
        for other in range(DEPTH):
            if other != layer:
                ssd_o[other] = jnp.zeros(ssd_o.shape[1:], F32)
        ssd_o = ssd_o.at[layer]
    ax = proj_ref[:, OFF_AX:OFF_AX + RG_WIDTH]
    xa = cab[...] + caw[CONV_W - 1:CONV_W, :] * ax
    for k in range(CONV_W - 1):
        xa = xa + caw[k:k + 1, :] * rgc_ref[:, k * RG_WIDTH:(k + 1) * RG_WIDTH]
    rgc_o[:, 0:2 * RG_WIDTH] = rgc_ref[:, RG_WIDTH:3 * RG_WIDTH]
    rgc_o[:, 2 * RG_WIDTH:] = ax
    a, mult, gi = _rg_coeffs(_dot(_bf(xa), wg[...]) + bg[...], lam[...])
    if pos0 == 0:
        mult = jnp.ones_like(mult)
    h = a * rgh_ref[...] + mult * (gi * xa)
    rgh_o[...] = h
    ymix_ref[:, 0:RG_WIDTH] = h * _gelu_tanh(proj_ref[:, OFF_AG:OFF_AG + RG_WIDTH])

    bx = proj_ref[:, OFF_XBC:OFF_XBC + SSD_CONV_DIM]
    xbc = cbb[...] + cbw[CONV_W - 1:CONV_W, :] * bx
    for k in range(CONV_W - 1):
        xbc = xbc + cbw[k:k + 1, :] * sc_ref[:, k * SSD_CONV_DIM:(k + 1) * SSD_CONV_DIM]
    sc_o[:, 0:2 * SSD_CONV_DIM] = sc_ref[:, SSD_CONV_DIM:3 * SSD_CONV_DIM]
    sc_o[:, 2 * SSD_CONV_DIM:] = bx
    xbc = _silu(xbc)
    xs = xbc[:, :SSD_WIDTH]
    bm = xbc[:, SSD_WIDTH:SSD_WIDTH + SSD_STATE]
    cm = xbc[:, SSD_WIDTH + SSD_STATE:]
    dtp = _softplus(proj_ref[:, OFF_DT:OFF_DT + LANES] + dtb[...])
    v_da = dtp * (-jnp.exp(alog[...]))
    dt_e = _dot_x_c(dtp, expand[...])
    e_e = jnp.exp(_dot_x_c(v_da, expand[...]))
    dtx = dt_e * xs
    cbs = _dot(_bf(cm * bm), jnp.ones((SSD_STATE, LANES), BF16))[:, 0:1]
    dtx_t = _pad_rows_t(dtx)
    e_t = _pad_rows_t(e_e)
    c_pad = _bf(jnp.concatenate([cm, jnp.zeros((LANES - bb, SSD_STATE), F32)], axis=0))
    lane_w = lax.broadcasted_iota(jnp.int32, (SSD_WIDTH, LANES), 1)
    y_t = jnp.zeros((SSD_WIDTH, LANES), F32)
    for j in range(bb):
        s_old = ssd_ref[j].reshape(SSD_WIDTH, SSD_STATE)
        ssd_o[j] = (e_t[:, j:j + 1] * s_old + dtx_t[:, j:j + 1] * bm[j:j + 1, :]).reshape(
            SSD_HEADS, SSD_HEAD_DIM, SSD_STATE)
        y_t = y_t + jnp.where(lane_w == j, _dot_nt(_bf(s_old), c_pad), 0.0)
    y = cbs * dtx + y_t.T[0:bb, :] * e_e
    yb = y + dexp[...] * xs
    ymix_ref[:, RG_WIDTH:RG_WIDTH + SSD_WIDTH] = _rms(
        yb * _silu(proj_ref[:, OFF_BZ:OFF_BZ + SSD_WIDTH]), snorm[...])

    q = _silu(proj_ref[:, OFF_CQ:OFF_CQ + HG_WIDTH])
    lb = _lower_bound_row(hlb[...], layer)
    gt = lb + (1.0 - lb) * _sigmoid(proj_ref[:, OFF_CF:OFF_CF + HG_WIDTH])
    kk = 1.0 - gt
    vv = proj_ref[:, OFF_CI:OFF_CI + HG_WIDTH]
    att = _dot(_bf(q * kk), headsum[...])
    o = att * vv + ohg_ref[...]
    ms = _dot(_bf(o * o), headmean[...])
    ymix_ref[:, RG_WIDTH + SSD_WIDTH:] = (
        o * lax.rsqrt(ms + EPS) * hnorm[...] * _silu(proj_ref[:, OFF_CG:OFF_CG + HG_WIDTH]))


def _sample_mixer(layer, proj, rgc, rgh, sc, ssd_all, o_hg, acc, mp, consts):
    nb = proj.shape[0]
    bb = DEC_BB
    params = [mp["caw"], mp["cab"], mp["wg"], mp["bg"], mp["lam"], mp["cbw"], mp["cbb"],
              mp["dtb"], mp["alog"], mp["dexp"], mp["snorm"], mp["hlb"], mp["hnorm"],
              consts["expand"], consts["headsum"], consts["headmean"]]
    row2 = lambda w: pl.BlockSpec((bb, w), lambda i: (i, 0))
    small_specs = [row2(3 * RG_WIDTH), row2(RG_WIDTH), row2(3 * SSD_CONV_DIM)]
    ssd_spec = pl.BlockSpec((None, bb, SSD_HEADS, SSD_HEAD_DIM, SSD_STATE),
                            lambda i: (layer, i, 0, 0, 0))
    out_shape = (
        jax.ShapeDtypeStruct((nb, D_MIX), F32),
        jax.ShapeDtypeStruct((nb, 3 * RG_WIDTH), F32),
        jax.ShapeDtypeStruct((nb, RG_WIDTH), F32),
        jax.ShapeDtypeStruct((nb, 3 * SSD_CONV_DIM), F32),
        jax.ShapeDtypeStruct(ssd_all.shape, F32),
    )
    n_acc = len(acc)
    n_lead = 6
    ssd_out_spec = ssd_spec if n_acc else pl.BlockSpec(
        (DEPTH, bb, SSD_HEADS, SSD_HEAD_DIM, SSD_STATE), lambda i: (0, i, 0, 0, 0))
    return pl.pallas_call(
        functools.partial(_sample_mixer_kernel, layer, PAST_LEN, n_acc),
        grid=(nb // bb,),
        in_specs=[row2(NPROJ)] + small_specs + [ssd_spec, row2(HG_WIDTH)]
        + [pl.BlockSpec(memory_space=pl.ANY)] * n_acc + [_const_spec(p) for p in params],
        out_specs=tuple([row2(D_MIX)] + small_specs + [ssd_out_spec]),
        out_shape=out_shape,
        input_output_aliases={n_lead + k: 4 + k for k in range(n_acc)},
        compiler_params=pltpu.CompilerParams(
            dimension_semantics=("parallel",), vmem_limit_bytes=VMEM_LIMIT),
        name="sample_mixer",
    )(proj, rgc, rgh, sc, ssd_all, o_hg, *acc, *params)


def _post_kernel(final, h_ref, y_ref, p_ref, wo, nffn, wup, wdown, nple, wgate, wproj, nfin, o_ref):
    h = h_ref[...] + _dot(_bf(y_ref[...]), wo[...])
    hn = _bf(_rms(h, nffn[...]))
    for j in range(D_FF // FF_CHUNK):
        cs = slice(j * FF_CHUNK, (j + 1) * FF_CHUNK)
        z = jnp.square(jnp.maximum(_dot(hn, wup[:, cs]), 0.0))
        h = h + _dot(_bf(z), wdown[cs, :])
    gate = _sigmoid(_dot(_bf(_rms(h, nple[...])), wgate[...]))
    h = h + gate * _dot(_bf(p_ref[...]), wproj[...])
    if final:
        h = _rms(h, nfin[...])
    o_ref[...] = h


def _post(final, layer, h2d, ymix2d, p_all, lp, nfin, tm):
    m = h2d.shape[0]
    params = [lp["wo"], lp["nffn"], lp["wup"], lp["wdown"], lp["nple"], lp["wgate"], lp["wproj"], nfin]
    return pl.pallas_call(
        functools.partial(_post_kernel, final),
        grid=(m // tm,),
        in_specs=[pl.BlockSpec((tm, D_MODEL), lambda i: (i, 0)),
                  pl.BlockSpec((tm, D_MIX), lambda i: (i, 0)),
                  pl.BlockSpec((None, tm, PLE_DIM), lambda i: (layer, i, 0))]
        + [_const_spec(p) for p in params],
        out_specs=pl.BlockSpec((tm, D_MODEL), lambda i: (i, 0)),
        out_shape=jax.ShapeDtypeStruct((m, D_MODEL), F32),
        compiler_params=pltpu.CompilerParams(
            dimension_semantics=("parallel",), vmem_limit_bytes=VMEM_LIMIT),
        name="post_mixer",
    )(h2d, ymix2d, p_all, *params)


def _constants():
    t = np.arange(TC)
    ltri = (t[None, :] <= t[:, None]).astype(np.float32)
    utri = ltri.T.copy()
    expand = np.zeros((LANES, SSD_WIDTH), np.float32)
    for hd in range(SSD_HEADS):
        expand[hd, hd * SSD_HEAD_DIM:(hd + 1) * SSD_HEAD_DIM] = 1.0
    emat = np.zeros(((N_LEVELS + 1) * TC, TC), np.float32)
    for lev in range(1, N_LEVELS + 1):
        blk = 1 << lev
        mid = (t // blk) * blk + blk // 2
        upper = t >= mid
        r0 = (lev - 1) * TC
        emat[r0:r0 + TC] = (
            (upper[:, None] & (t[None, :] >= mid[:, None]) & (t[None, :] <= t[:, None]))
            | ((~upper)[:, None] & (t[None, :] > t[:, None]) & (t[None, :] < mid[:, None])))
    emat[N_LEVELS * TC:] = ltri
    cmask = np.zeros((N_LEVELS + 1, TC, 2 * TC), np.float32)
    cmask[0] = np.tile(np.eye(TC, dtype=np.float32), (1, 2))
    for lev in range(1, N_LEVELS + 1):
        blk = 1 << lev
        up = (t % blk) >= blk // 2
        m = up[:, None] & (~up)[None, :] & ((t // blk)[:, None] == (t // blk)[None, :])
        cmask[lev] = np.tile(m.astype(np.float32), (1, 2))
    hidx = np.arange(HG_WIDTH) // HG_HEAD_DIM
    bd = (hidx[:, None] == hidx[None, :]).astype(np.float32)
    pair = bd[:LANES, :LANES]
    return {
        "ltri3": jnp.asarray(np.tile(ltri, (1, 3)), BF16),
        "utri3": jnp.asarray(np.tile(utri, (3, 1)), BF16),
        "expand": jnp.asarray(expand, BF16),
        "expand3": jnp.asarray(np.tile(expand, (3, 1)), BF16),
        "emat2": jnp.asarray(np.tile(emat, (1, 2)), BF16),
        "cmask": jnp.asarray(cmask, F32),
        "pairmask": jnp.asarray(pair, F32),
        "pairmean": jnp.asarray(pair / HG_HEAD_DIM, BF16),
        "headsum": jnp.asarray(bd, BF16), "headmean": jnp.asarray(bd / HG_HEAD_DIM, BF16),
    }


def _block_diag(w):
    hh, d, _ = w.shape
    eye = jnp.eye(hh, dtype=w.dtype)
    return (eye[:, None, :, None] * w[:, :, None, :]).reshape(hh * d, hh * d)


def _pair_gate_weights(wa, wx):
    hh, d, _ = wa.shape
    pa = jnp.stack([_block_diag(wa[2 * g:2 * g + 2]) for g in range(hh // 2)])
    px = jnp.stack([_block_diag(wx[2 * g:2 * g + 2]) for g in range(hh // 2)])
    return jnp.concatenate([pa, px], axis=2)


def _pad_lanes(v):
    return jnp.pad(v, (0, LANES - v.shape[0]))[None, :]


def kernel(x_prompt, x_sample, state_rg_conv, state_rg_h, state_ssd_conv, state_ssd, state_hgrn,
           p_prompt, p_sample, norm_mix, w_in, conv_a_w, conv_a_b, rg_wa, rg_ba, rg_wx, rg_bx,
           rg_lambda, conv_b_w, conv_b_b, ssd_dt_bias, ssd_a_log, ssd_d, ssd_norm,
           hg_lower_bounds, hg_norm, w_out, norm_ffn, w_up, w_down, norm_ple, w_ple_gate,
           w_ple_proj, norm_final):
    bsz, seq, _ = x_prompt.shape
    nb = x_sample.shape[0]
    consts = _constants()
    w_in_t = jnp.swapaxes(w_in, 1, 2)
    nfin = norm_final[None, :]
    ps_all = p_sample.reshape(DEPTH, nb, PLE_DIM)

    hp = x_prompt
    hs = x_sample.reshape(nb, D_MODEL)
    st_p = [[] for _ in range(5)]
    st_s = [[] for _ in range(3)]
    hg_t_all = jnp.transpose(state_hgrn, (0, 2, 3, 4, 1))
    acc_ssd = ()
    acc_hg = ()
    for i in range(DEPTH):
        w_re = _prep_w_in(w_in_t, i)
        gmix = norm_mix[i][None, :]
        mp = {
            "caw": conv_a_w[i], "cab": conv_a_b[i][None, :],
            "wg": jnp.concatenate([_block_diag(rg_wa[i]), _block_diag(rg_wx[i])], axis=1).astype(BF16),
            "bg": jnp.concatenate([rg_ba[i], rg_bx[i]])[None, :],
            "wgp": _pair_gate_weights(rg_wa[i], rg_wx[i]).astype(BF16),
            "bgp": jnp.concatenate([rg_ba[i].reshape(NGRP, 1, LANES),
                                    rg_bx[i].reshape(NGRP, 1, LANES)], axis=2),
            "lam": rg_lambda[i][None, :],
            "cbw": conv_b_w[i], "cbb": conv_b_b[i][None, :],
            "dtb": _pad_lanes(ssd_dt_bias[i]), "alog": _pad_lanes(ssd_a_log[i]),
            "dexp": jnp.repeat(ssd_d[i], SSD_HEAD_DIM)[None, :],
            "snorm": ssd_norm[i][None, :],
            "hlb": hg_lower_bounds, "hnorm": jnp.tile(hg_norm[i], HG_HEADS)[None, :],
        }
        lp = {
            "wo": _cast_bf16(w_out, i, 512), "nffn": norm_ffn[i][None, :],
            "wup": _cast_bf16(w_up, i, 256), "wdown": _cast_bf16(w_down, i, 1024),
            "nple": norm_ple[i][None, :], "wgate": _cast_bf16(w_ple_gate, i, 512),
            "wproj": _cast_bf16(w_ple_proj, i, 256),
        }
        final = i == DEPTH - 1

        hp, rgc, rgh, sc, ssd, hg = _prompt_layer(
            i, final, hp, p_prompt, gmix, w_re, mp, lp, nfin, consts)
        for lst, s in zip(st_p, (rgc, rgh.reshape(bsz, RG_WIDTH), sc,
                                 ssd.reshape(bsz, SSD_HEADS, SSD_HEAD_DIM, SSD_STATE), hg)):
            lst.append(s)

        proj_s = _in_proj(hs, gmix, w_re, nb)
        o_hg_t, hg_acc = _hg_step(i, proj_s, hg_lower_bounds, hg_t_all, acc_hg)
        acc_hg = (hg_acc,)
        ymix_s, rgc, rgh, sc, ssd_acc = _sample_mixer(
            i, proj_s,
            state_rg_conv[i].reshape(nb, 3 * RG_WIDTH), state_rg_h[i],
            state_ssd_conv[i].reshape(nb, 3 * SSD_CONV_DIM),
            state_ssd, o_hg_t.T, acc_ssd, mp, consts)
        acc_ssd = (ssd_acc,)
        hs = _post(final, i, hs, ymix_s, ps_all, lp, nfin, nb)
        for lst, s in zip(st_s, (rgc.reshape(nb, CONV_W - 1, RG_WIDTH), rgh,
                                 sc.reshape(nb, CONV_W - 1, SSD_CONV_DIM))):
            lst.append(s)

    y_prompt = hp
    y_sample = hs.reshape(nb, 1, D_MODEL)
    outs_p = [jnp.stack(lst) for lst in st_p]
    outs_s = [jnp.stack(lst) for lst in st_s] + [
        acc_ssd[0], jnp.transpose(acc_hg[0], (0, 4, 1, 2, 3))]
    return (y_prompt, y_sample, *outs_p, *outs_s)
```

```python
import functools

import numpy as np
import jax
import jax.numpy as jnp
from jax import lax
from jax.experimental import pallas as pl
from jax.experimental.pallas import tpu as pltpu

F32 = jnp.float32
BF16 = jnp.bfloat16

D_MODEL = 1024
DEPTH = 2
PAST_LEN = 16384
PLE_DIM = 256
D_FF = 4 * D_MODEL
CONV_W = 4
EPS = 1e-6
RG_WIDTH = 512
RG_HEADS = 8
RG_HEAD_DIM = 64
RG_C = 8.0
SSD_WIDTH = 512
SSD_HEAD_DIM = 64
SSD_HEADS = 8
SSD_STATE = 128
SSD_CONV_DIM = SSD_WIDTH + 2 * SSD_STATE
HG_WIDTH = 512
HG_HEADS = 8
HG_HEAD_DIM = 64
D_MIX = RG_WIDTH + SSD_WIDTH + HG_WIDTH

LOG2E = 1.4426950408889634
LANES = 128
SUBLANES = 8
NGRP = RG_WIDTH // LANES

OFF_AX = 0
OFF_AG = 512
OFF_BZ = 1024
OFF_XBC = 1536
OFF_CQ = 2304
OFF_CF = 2816
OFF_CI = 3328
OFF_CG = 3840
OFF_DT = 4352
NPROJ = OFF_DT + LANES
DT_COL = 2 * RG_WIDTH + SSD_WIDTH + SSD_CONV_DIM
D_IN_PROJ = DT_COL + SSD_HEADS + 4 * HG_WIDTH

TC = 128
HALF = TC // 2
N_LEVELS = 7
N_MXU_LEVELS = 3
DEC_BB = 8
TM_PROMPT = 512
FF_CHUNK = 1024
VMEM_LIMIT = 58 * 1024 * 1024


def _bf(x):
    return x.astype(BF16)


def _dot(a, b):
    return jnp.dot(a, b, preferred_element_type=F32)


def _dot_nt(a, b):
    return lax.dot_general(a, b, (((1,), (1,)), ((), ())), preferred_element_type=F32)


def _split3(x):
    hi = _bf(x)
    r1 = x - hi.astype(F32)
    mid = _bf(r1)
    lo = _bf(r1 - mid.astype(F32))
    return hi, mid, lo


def _dot_x_c(x, c):
    hi, mid, lo = _split3(x)
    return _dot(hi, c) + _dot(mid, c) + _dot(lo, c)


def _split2(x):
    hi = _bf(x)
    return hi, _bf(x - hi.astype(F32))


def _sigmoid(x):
    return 0.5 * jnp.tanh(0.5 * x) + 0.5


def _silu(x):
    h = 0.5 * x
    return h * jnp.tanh(h) + h


def _softplus(x):
    return jnp.maximum(x, 0.0) + jnp.log1p(jnp.exp(-jnp.abs(x)))


def _gelu_tanh(x):
    c = np.float32(np.sqrt(2.0 / np.pi))
    return 0.5 * x * (1.0 + jnp.tanh(c * (x + 0.044715 * (x * x * x))))


def _rms(x, gain):
    return x * lax.rsqrt(jnp.mean(x * x, axis=-1, keepdims=True) + EPS) * gain


def _lower_bound_row(hlb, layer):
    m = jnp.max(hlb, axis=0, keepdims=True)
    e = jnp.exp(hlb - m)
    sm = e / jnp.sum(e, axis=0, keepdims=True)
    lb = jnp.zeros((1, HG_WIDTH), F32)
    for j in range(1, layer + 1):
        lb = lb + sm[j:j + 1, :]
    return lb


def _rg_coeffs(gates, lam_row):
    w = gates.shape[1] // 2
    r = _sigmoid(gates[:, :w])
    i = _sigmoid(gates[:, w:])
    log_a = -RG_C * r * _softplus(-lam_row)
    a = jnp.exp(log_a)
    mult = jnp.sqrt(-jnp.tanh(log_a) * (a * a + 1.0))
    return a, mult, i


def _lane_halves(x):
    lane = lax.broadcasted_iota(jnp.int32, x.shape, 1)
    lo = jnp.where(lane < HG_HEAD_DIM, x, 0.0)
    hi = jnp.where(lane >= HG_HEAD_DIM, x, 0.0)
    return jnp.concatenate([lo, hi], axis=0)


def _const_spec(arr):
    nd = arr.ndim
    return pl.BlockSpec(arr.shape, lambda *_: (0,) * nd, pipeline_mode=pl.Buffered(1))


def _cast_kernel(w_ref, o_ref):
    o_ref[...] = _bf(w_ref[...])


def _cast_bf16(w_all, layer, rb):
    _, rows, cols = w_all.shape
    return pl.pallas_call(
        _cast_kernel,
        grid=(rows // rb,),
        in_specs=[pl.BlockSpec((None, rb, cols), lambda i: (layer, i, 0))],
        out_specs=pl.BlockSpec((rb, cols), lambda i: (i, 0)),
        out_shape=jax.ShapeDtypeStruct((rows, cols), BF16),
        compiler_params=pltpu.CompilerParams(dimension_semantics=("parallel",)),
        name="cast_bf16",
    )(w_all)


def _win_kernel(wt_ref, o_ref):
    o_ref[:, 0:OFF_CQ] = _bf(wt_ref[0:DT_COL, :].T)
    o_ref[:, OFF_CQ:OFF_DT] = _bf(wt_ref[DT_COL + SSD_HEADS:D_IN_PROJ, :].T)
    dt = jnp.concatenate([wt_ref[DT_COL:DT_COL + SSD_HEADS, :],
                          jnp.zeros((LANES - SSD_HEADS, LANES), F32)], axis=0)
    o_ref[:, OFF_DT:NPROJ] = _bf(dt.T)


def _prep_w_in(wt_all, layer):
    return pl.pallas_call(
        _win_kernel,
        grid=(D_MODEL // LANES,),
        in_specs=[pl.BlockSpec((None, D_IN_PROJ, LANES), lambda i: (layer, 0, i))],
        out_specs=pl.BlockSpec((LANES, NPROJ), lambda i: (i, 0)),
        out_shape=jax.ShapeDtypeStruct((D_MODEL, NPROJ), BF16),
        compiler_params=pltpu.CompilerParams(dimension_semantics=("parallel",)),
        name="prep_w_in",
    )(wt_all)


def _proj_kernel(x_ref, g_ref, w_ref, o_ref):
    u = _rms(x_ref[...], g_ref[...])
    o_ref[...] = _dot(_bf(u), w_ref[...])


def _in_proj(x2d, gain, w_bf, tm):
    m = x2d.shape[0]
    return pl.pallas_call(
        _proj_kernel,
        grid=(m // tm,),
        in_specs=[pl.BlockSpec((tm, D_MODEL), lambda i: (i, 0)),
                  _const_spec(gain), _const_spec(w_bf)],
        out_specs=pl.BlockSpec((tm, NPROJ), lambda i: (i, 0)),
        out_shape=jax.ShapeDtypeStruct((m, NPROJ), F32),
        compiler_params=pltpu.CompilerParams(
            dimension_semantics=("parallel",), vmem_limit_bytes=VMEM_LIMIT),
        name="in_proj",
    )(x2d, gain, w_bf)


def _mixer_chunk(layer, row0, pv, ymix, side_work,
                 caw, cab, wgp, bgp, lam, cbw, cbb, dtb, alog, dexp, snorm,
                 hlb, hnorm, ltri2, emat2, cmask, pairmask, pairmean,
                 xea, xeb, hcar, s_t, st_hg):
    row_h = lax.broadcasted_iota(jnp.int32, (HALF, 1), 0)
    sub_h = row_h & (SUBLANES - 1)
    first_head = lax.broadcasted_iota(jnp.int32, (HALF, LANES), 1) < HG_HEAD_DIM
    first_head_tc = lax.broadcasted_iota(jnp.int32, (TC, LANES), 1) < HG_HEAD_DIM
    col_tc = lax.broadcasted_iota(jnp.int32, (HALF, TC), 1)
    row_tc = lax.broadcasted_iota(jnp.int32, (HALF, TC), 0)

    tiles = [slice(t * HALF, (t + 1) * HALF) for t in range(TC // HALF)]

    def lanes(off, g):
        return slice(off + g * LANES, off + (g + 1) * LANES)

    def conv_rows(prev8, x, w_ref, b_ref, gs):
        xcat = jnp.concatenate([prev8, x], axis=0)
        y = b_ref[:, gs] + w_ref[CONV_W - 1:CONV_W, gs] * x
        for j in range(1, CONV_W):
            y = y + w_ref[CONV_W - 1 - j:CONV_W - j, gs] * pltpu.roll(xcat, j, 0)[SUBLANES:, :]
        return y, x[x.shape[0] - SUBLANES:, :]

    def conv(tail_ref, w_ref, b_ref, src_off, g):
        gs = lanes(0, g)
        y, tail = conv_rows(tail_ref[:, gs], pv[:, lanes(src_off, g)], w_ref, b_ref, gs)
        tail_ref[:, gs] = tail
        return y

    def mixer_a(g):
        gs = lanes(0, g)
        prev8 = xea[:, gs]
        carry = hcar[0:1, gs]
        for t, rs in enumerate(tiles):
            xa, prev8 = conv_rows(prev8, pv[rs, lanes(OFF_AX, g)], caw, cab, gs)
            a, mult, gi = _rg_coeffs(_dot(_bf(xa), wgp[g]) + bgp[g], lam[:, gs])
            mult = jnp.where(row_h + (row0 + t * HALF) == 0, 1.0, mult)
            b = mult * (gi * xa)
            k = 1
            while k < SUBLANES:
                keep = sub_h >= k
                a_sh = jnp.where(keep, pltpu.roll(a, k, 0), 1.0)
                b_sh = jnp.where(keep, pltpu.roll(b, k, 0), 0.0)
                b = b + a * b_sh
                a = a * a_sh
                k *= 2
            slabs = []
            for r in range(HALF // SUBLANES):
                sl = slice(r * SUBLANES, (r + 1) * SUBLANES)
                h_r = a[sl, :] * carry + b[sl, :]
                slabs.append(h_r)
                carry = h_r[SUBLANES - 1:SUBLANES, :]
            ymix[rs, gs] = jnp.concatenate(slabs, axis=0) * _gelu_tanh(
                pv[rs, lanes(OFF_AG, g)])
        xea[:, gs] = prev8
        hcar[0:1, gs] = carry

    bm = _silu(conv(xeb, cbw, cbb, OFF_XBC, NGRP))
    cm = _silu(conv(xeb, cbw, cbb, OFF_XBC, NGRP + 1))
    dtp = _softplus(pv[:, OFF_DT:OFF_DT + LANES] + dtb[...])
    v_da = dtp * (-LOG2E * jnp.exp(alog[...]))
    cum = _dot(ltri2[...], jnp.concatenate(_split2(v_da), axis=0))
    cum_t = cum.T

    def per_head(arr, g, rs):
        rows = arr[rs, :]
        return jnp.where(first_head[0:rows.shape[0], :], rows[:, 2 * g:2 * g + 1], rows[:, 2 * g + 1:2 * g + 2])
    cb = _dot_nt(_bf(cm), _bf(bm))
    cmb = _bf(cm)
    bmt = _bf(bm.T)
    ssq_parts = []

    def mixer_b(g):
        gs = lanes(0, g)
        cum_last = per_head(cum, g, slice(TC - 1, TC))
        s_old = _bf(s_t[g])
        prev8 = xeb[:, gs]
        xs_t, x2a, x2b, wx, cum_g = [], [], [], [], []
        for rs in tiles:
            xs, prev8 = conv_rows(prev8, pv[rs, lanes(OFF_XBC, g)], cbw, cbb, gs)
            xs = _silu(xs)
            dt_g = per_head(dtp, g, rs)
            cum_g.append(per_head(cum, g, rs))
            dtx = xs * dt_g
            xs_t.append(xs)
            x2a.append(_bf(jnp.where(first_head, dtx, 0.0)))
            x2b.append(_bf(jnp.where(first_head, 0.0, dtx)))
            wx.append(_bf(jnp.exp2(cum_last - cum_g[-1]) * dtx))
        xeb[:, gs] = prev8
        x2 = jnp.concatenate(x2a + x2b, axis=0)
        for t, rs in enumerate(tiles):
            gmats = []
            for hh in range(2):
                hd = 2 * g + hh
                seg = cum[rs, hd:hd + 1] - cum_t[hd:hd + 1, :]
                gmats.append(_bf(cb[rs, :] * jnp.exp2(jnp.where(col_tc <= row_tc + t * HALF, seg, -jnp.inf))))
            y = _dot(jnp.concatenate(gmats, axis=1), x2)
            y = y + _dot(cmb[rs, :], s_old) * jnp.exp2(cum_g[t])
            yz = (y + dexp[:, gs] * xs_t[t]) * _silu(pv[rs, lanes(OFF_BZ, g)])
            ssq_parts.append(yz * yz)
            ymix[rs, lanes(RG_WIDTH, g)] = yz
        s_t[g] = jnp.exp2(cum_last) * s_t[g] + _dot(bmt, jnp.concatenate(wx, axis=0))

    lb_all = _lower_bound_row(hlb[...], layer)

    def forget_gate(g):
        lb = lb_all[:, g * LANES:(g + 1) * LANES]
        return lb + (1.0 - lb) * _sigmoid(pv[:, OFF_CF + g * LANES:OFF_CF + (g + 1) * LANES])

    def mixer_c(g, gt, xall):
        gs = lanes(0, g)
        b_off = N_MXU_LEVELS * TC
        bcum = xall[b_off:b_off + TC, :]
        blast = bcum[TC - 1:TC, :]

        def level_exponent(lev):
            if lev <= N_MXU_LEVELS:
                return xall[(lev - 1) * TC:lev * TC, :]
            blk = 1 << lev
            refs = [jnp.broadcast_to(bcum[j * blk + blk // 2 - 1:j * blk + blk // 2, :], (blk, LANES))
                    for j in range(TC // blk)]
            return -jnp.abs(bcum - jnp.concatenate(refs, axis=0))
        q = _silu(pv[:, lanes(OFF_CQ, g)])
        kk = 1.0 - gt
        vv = pv[:, lanes(OFF_CI, g)]
        zeros = jnp.zeros((HG_HEAD_DIM, TC), BF16)

        def keys_by_head(kt):
            ktt = _bf(kt.T)
            return jnp.concatenate(
                [jnp.concatenate([ktt[:HG_HEAD_DIM, :], zeros], axis=1),
                 jnp.concatenate([zeros, ktt[HG_HEAD_DIM:, :]], axis=1)], axis=0)

        att = _dot(_bf(q), keys_by_head(kk)) * cmask[0]
        for lev in range(1, N_LEVELS + 1):
            e = jnp.exp2(level_exponent(lev))
            att = att + _dot(_bf(q * e), keys_by_head(kk * e)) * cmask[lev]
        v2 = jnp.concatenate([_bf(jnp.where(first_head_tc, vv, 0.0)),
                              _bf(jnp.where(first_head_tc, 0.0, vv))], axis=0)
        o = _dot(_bf(att), v2) + _dot_nt(_bf(q * jnp.exp2(bcum)), _bf(st_hg[g]))
        khat = kk * jnp.exp2(blast - bcum)
        st_hg[g] = jnp.exp2(blast) * st_hg[g] + pairmask[...] * _dot(_bf(vv.T), _bf(khat))
        ms = _dot(_bf(o * o), pairmean[...])
        ymix[:, lanes(RG_WIDTH + SSD_WIDTH, g)] = (
            o * lax.rsqrt(ms + EPS) * hnorm[:, gs] * _silu(pv[:, lanes(OFF_CG, g)]))

    for pair in range(NGRP // 2):
        g0, g1 = 2 * pair, 2 * pair + 1
        gts = [forget_gate(g0), forget_gate(g1)]
        lf = jnp.log(jnp.concatenate(gts, axis=1)) * LOG2E
        hi = _bf(lf)
        lo = _bf(lf - hi.astype(F32))
        xall2 = _dot(emat2[...], jnp.concatenate([hi, lo], axis=0))
        for g, gt in zip((g0, g1), gts):
            side_work[g]()
            mixer_a(g)
            mixer_b(g)
            mixer_c(g, gt, xall2[:, (g - g0) * LANES:(g - g0 + 1) * LANES])

    for t, rs in enumerate(tiles):
        ssq = ssq_parts[t]
        for g in range(1, NGRP):
            ssq = ssq + ssq_parts[g * len(tiles) + t]
        rinv = lax.rsqrt(jnp.sum(ssq, axis=-1, keepdims=True) * (1.0 / SSD_WIDTH) + EPS)
        for g in range(NGRP):
            ys = lanes(RG_WIDTH, g)
            ymix[rs, ys] = ymix[rs, ys] * rinv * snorm[:, lanes(0, g)]


N_MIX_PARAMS = 18
MXU_N = 256
PROJ_BOUNDS = (0, 5 * MXU_N, 9 * MXU_N, 13 * MXU_N, NPROJ)
assert len(PROJ_BOUNDS) == NGRP + 1
CHUNKS_PER_STEP = 2


N_POST_PARAMS = 8


def _prompt_layer_kernel(layer, final, pos0, nsteps, n_valid,
                         x_ref, xn_ref, xp_ref, p_ref, gmix, w_in_ref, *rest):
    params = rest[:N_MIX_PARAMS]
    k = N_MIX_PARAMS + N_POST_PARAMS
    wo, nffn, wup, wdown, nple, wgate, wproj, nfin = rest[N_MIX_PARAMS:k]
    h_out, rgc_o, rgh_o, sc_o, ssd_o, hg_o = rest[k:k + 6]
    proj_a, proj_b, ymix_s, xea, xeb, hcar, s_t, st_hg = rest[k + 6:]
    states = (xea, xeb, hcar, s_t, st_hg)
    s = pl.program_id(0)
    live = s < n_valid
    c = jnp.minimum(s, n_valid - 1) % nsteps

    @pl.when(s == 0)
    def _prologue():
        proj_a[...] = _dot(_bf(_rms(x_ref[0, 0:TC, :], gmix[...])), w_in_ref[...])
        ymix_s[...] = jnp.zeros_like(ymix_s)

    @pl.when(jnp.logical_and(c == 0, live))
    def _init():
        for ref in states:
            ref[...] = jnp.zeros_like(ref)

    ymix_prev = _bf(ymix_s[...])
    n_ff = D_FF // FF_CHUNK
    carry = {}

    def post_piece(j):
        if j == 0:
            carry["h"] = xp_ref[0] + _dot(ymix_prev, wo[...])
            carry["hn"] = _bf(_rms(carry["h"], nffn[...]))
        elif j <= n_ff:
            cs = slice((j - 1) * FF_CHUNK, j * FF_CHUNK)
            z = jnp.square(jnp.maximum(_dot(carry["hn"], wup[:, cs]), 0.0))
            carry["h"] = carry["h"] + _dot(_bf(z), wdown[cs, :])
        elif j == n_ff + 1:
            h = carry["h"]
            gate = _sigmoid(_dot(_bf(_rms(h, nple[...])), wgate[...]))
            h = h + gate * _dot(_bf(p_ref[0]), wproj[...])
            h_out[0] = _rms(h, nfin[...]) if final else h

    def side_work(x, dst, first_piece):
        u = _bf(_rms(x, gmix[...]))

        def slot(j):
            cs = slice(PROJ_BOUNDS[j], PROJ_BOUNDS[j + 1])
            dst[:, cs] = _dot(u, w_in_ref[:, cs])
            post_piece(first_piece + j)

        return [functools.partial(slot, j) for j in range(NGRP)]

    row0 = c * (CHUNKS_PER_STEP * TC) + pos0
    _mixer_chunk(layer, row0, proj_a, ymix_s.at[0:TC],
                 side_work(x_ref[0, TC:2 * TC, :], proj_b, 0), *params, *states)
    _mixer_chunk(layer, row0 + TC, proj_b, ymix_s.at[TC:2 * TC],
                 side_work(xn_ref[0], proj_a, NGRP), *params, *states)

    @pl.when(jnp.logical_and(c == nsteps - 1, live))
    def _emit():
        rgc_o[0] = xea[SUBLANES - 3:SUBLANES, :]
        sc_o[0] = xeb[SUBLANES - 3:SUBLANES, :]
        rgh_o[0] = hcar[0:1, :]
        for g in range(NGRP):
            ssd_o[0, g * LANES:(g + 1) * LANES, :] = s_t[g].T
            s_pair = st_hg[g].T
            for hh in range(2):
                lo = hh * HG_HEAD_DIM
                hg_o[0, 2 * g + hh] = s_pair[lo:lo + HG_HEAD_DIM, lo:lo + HG_HEAD_DIM]


def _prompt_layer(layer, final, x3, p_all4, gmix, w_in_bf, mp, lp, nfin, consts):
    bsz, seq, _ = x3.shape
    step_rows = CHUNKS_PER_STEP * TC
    nsteps = seq // step_rows
    n_valid = bsz * nsteps
    chunks_per_row = seq // TC
    n_chunks = bsz * chunks_per_row
    params = [mp["caw"], mp["cab"], mp["wgp"], mp["bgp"], mp["lam"], mp["cbw"], mp["cbb"],
              mp["dtb"], mp["alog"], mp["dexp"], mp["snorm"], mp["hlb"], mp["hnorm"],
              consts["ltri2"], consts["emat2"],
              consts["cmask"], consts["pairmask"], consts["pairmean"]]
    post_params = [lp["wo"], lp["nffn"], lp["wup"], lp["wdown"], lp["nple"], lp["wgate"],
                   lp["wproj"], nfin]
    assert len(params) == N_MIX_PARAMS and len(post_params) == N_POST_PARAMS

    def mix_step(s):
        return jnp.minimum(s, n_valid - 1)

    def cur(s):
        return mix_step(s) // nsteps, mix_step(s) % nsteps, 0

    def nxt(s):
        flat = jnp.minimum((mix_step(s) + 1) * CHUNKS_PER_STEP, n_chunks - 1)
        return flat // chunks_per_row, flat % chunks_per_row, 0

    def prv(s):
        sp = jnp.maximum(s - 1, 0)
        return sp // nsteps, sp % nsteps, 0

    def state_idx(nd):
        return lambda s: (mix_step(s) // nsteps,) + (0,) * (nd - 1)

    out_shape = (
        jax.ShapeDtypeStruct((bsz, seq, D_MODEL), F32),
        jax.ShapeDtypeStruct((bsz, CONV_W - 1, RG_WIDTH), F32),
        jax.ShapeDtypeStruct((bsz, 1, RG_WIDTH), F32),
        jax.ShapeDtypeStruct((bsz, CONV_W - 1, SSD_CONV_DIM), F32),
        jax.ShapeDtypeStruct((bsz, SSD_WIDTH, SSD_STATE), F32),
        jax.ShapeDtypeStruct((bsz, HG_HEADS, HG_HEAD_DIM, HG_HEAD_DIM), F32),
    )
    out_specs = (
        pl.BlockSpec((1, step_rows, D_MODEL), prv),
        pl.BlockSpec((1, CONV_W - 1, RG_WIDTH), state_idx(3)),
        pl.BlockSpec((1, 1, RG_WIDTH), state_idx(3)),
        pl.BlockSpec((1, CONV_W - 1, SSD_CONV_DIM), state_idx(3)),
        pl.BlockSpec((1, SSD_WIDTH, SSD_STATE), state_idx(3)),
        pl.BlockSpec((1, HG_HEADS, HG_HEAD_DIM, HG_HEAD_DIM), state_idx(4)),
    )
    scratch = [
        pltpu.VMEM((TC, NPROJ), F32),
        pltpu.VMEM((TC, NPROJ), F32),
        pltpu.VMEM((step_rows, D_MIX), F32),
        pltpu.VMEM((SUBLANES, RG_WIDTH), F32),
        pltpu.VMEM((SUBLANES, SSD_CONV_DIM), F32),
        pltpu.VMEM((SUBLANES, RG_WIDTH), F32),
        pltpu.VMEM((NGRP, SSD_STATE, LANES), F32),
        pltpu.VMEM((NGRP, LANES, LANES), F32),
    ]
    return pl.pallas_call(
        functools.partial(_prompt_layer_kernel, layer, final, 0, nsteps, n_valid),
        grid=(n_valid + 1,),
        in_specs=[pl.BlockSpec((1, step_rows, D_MODEL), cur),
                  pl.BlockSpec((1, TC, D_MODEL), nxt),
                  pl.BlockSpec((1, step_rows, D_MODEL), prv),
                  pl.BlockSpec((None, 1, step_rows, PLE_DIM), lambda s: (layer,) + prv(s)),
                  _const_spec(gmix), _const_spec(w_in_bf)]
        + [_const_spec(p) for p in params + post_params],
        out_specs=out_specs,
        out_shape=out_shape,
        scratch_shapes=scratch,
        compiler_params=pltpu.CompilerParams(
            dimension_semantics=("arbitrary",), vmem_limit_bytes=VMEM_LIMIT),
        name="prompt_layer",
    )(x3, x3, x3, p_all4, gmix, w_in_bf, *params, *post_params)


def _hg_step_kernel(layer, n_acc, proj_ref, hlb, st_ref, *rest):
    ot_ref, st_o, qh_s, f_s, k_s, v_s = rest[n_acc:]
    h = pl.program_id(0)
    if n_acc == 0:
        for other in range(DEPTH):
            if other != layer:
                st_o[other] = jnp.zeros(st_o.shape[1:], F32)
        st_o = st_o.at[layer]

    @pl.when(h == 0)
    def _prep():
        q = _silu(proj_ref[:, OFF_CQ:OFF_CQ + HG_WIDTH])
        lb = _lower_bound_row(hlb[...], layer)
        gt = lb + (1.0 - lb) * _sigmoid(proj_ref[:, OFF_CF:OFF_CF + HG_WIDTH])
        fdec = jnp.exp(jnp.log(gt))
        qh_s[...] = (q * fdec).T
        f_s[...] = fdec.T
        k_s[...] = (1.0 - gt).T
        v_s[...] = proj_ref[:, OFF_CI:OFF_CI + HG_WIDTH].T

    base = pl.multiple_of(h * HG_HEAD_DIM, HG_HEAD_DIM)
    v_h = v_s[pl.ds(base, HG_HEAD_DIM), :]

    def body(dk, acc):
        s = st_ref[dk]
        st_o[dk] = f_s[pl.ds(base + dk, 1), :] * s + k_s[pl.ds(base + dk, 1), :] * v_h
        return acc + qh_s[pl.ds(base + dk, 1), :] * s

    ot_ref[...] = lax.fori_loop(0, HG_HEAD_DIM, body,
                                jnp.zeros((HG_HEAD_DIM, LANES), F32), unroll=8)


def _hg_step(layer, proj, hlb, st_all, acc):
    nb = proj.shape[0]
    blk = (HG_HEAD_DIM, HG_HEAD_DIM, nb)
    st_spec = pl.BlockSpec((None, None) + blk, lambda h: (layer, h, 0, 0, 0))
    n_acc = len(acc)
    st_out_spec = st_spec if n_acc else pl.BlockSpec(
        (DEPTH, None) + blk, lambda h: (0, h, 0, 0, 0))
    return pl.pallas_call(
        functools.partial(_hg_step_kernel, layer, n_acc),
        grid=(HG_HEADS,),
        in_specs=[_const_spec(proj), _const_spec(hlb), st_spec]
        + [pl.BlockSpec(memory_space=pl.ANY)] * n_acc,
        out_specs=(pl.BlockSpec((HG_HEAD_DIM, nb), lambda h: (h, 0)), st_out_spec),
        out_shape=(jax.ShapeDtypeStruct((HG_WIDTH, nb), F32),
                   jax.ShapeDtypeStruct(st_all.shape, F32)),
        scratch_shapes=[pltpu.VMEM((HG_WIDTH, nb), F32)] * 4,
        input_output_aliases={3 + k: 1 + k for k in range(n_acc)},
        compiler_params=pltpu.CompilerParams(
            dimension_semantics=("arbitrary",), vmem_limit_bytes=VMEM_LIMIT),
        name="hg_step",
    )(proj, hlb, st_all, *acc)


def _pad_rows_t(x):
    pad = jnp.zeros((LANES - x.shape[0], x.shape[1]), F32)
    return jnp.concatenate([x, pad], axis=0).T


def _sample_mixer_kernel(layer, pos0, n_acc, proj_ref, rgc_ref, rgh_ref, sc_ref, ssd_ref, ohg_ref, *rest):
    (caw, cab, wg, bg, lam, cbw, cbb, dtb, alog, dexp, snorm,
     hlb, hnorm, expand, headsum, headmean,
     ymix_ref, rgc_o, rgh_o, sc_o, ssd_o) = rest[n_acc:]
    bb = DEC_BB
    if n_acc == 0:
        for other in range(DEPTH):
            if other != layer:
                ssd_o[other] = jnp.zeros(ssd_o.shape[1:], F32)
        ssd_o = ssd_o.at[layer]
    ax = proj_ref[:, OFF_AX:OFF_AX + RG_WIDTH]
    xa = cab[...] + caw[CONV_W - 1:CONV_W, :] * ax
    for k in range(CONV_W - 1):
        xa = xa + caw[k:k + 1, :] * rgc_ref[:, k * RG_WIDTH:(k + 1) * RG_WIDTH]
    rgc_o[:, 0:2 * RG_WIDTH] = rgc_ref[:, RG_WIDTH:3 * RG_WIDTH]
    rgc_o[:, 2 * RG_WIDTH:] = ax
    a, mult, gi = _rg_coeffs(_dot(_bf(xa), wg[...]) + bg[...], lam[...])
    if pos0 == 0:
        mult = jnp.ones_like(mult)
    h = a * rgh_ref[...] + mult * (gi * xa)
    rgh_o[...] = h
    ymix_ref[:, 0:RG_WIDTH] = h * _gelu_tanh(proj_ref[:, OFF_AG:OFF_AG + RG_WIDTH])

    bx = proj_ref[:, OFF_XBC:OFF_XBC + SSD_CONV_DIM]
    xbc = cbb[...] + cbw[CONV_W - 1:CONV_W, :] * bx
    for k in range(CONV_W - 1):
        xbc = xbc + cbw[k:k + 1, :] * sc_ref[:, k * SSD_CONV_DIM:(k + 1) * SSD_CONV_DIM]
    sc_o[:, 0:2 * SSD_CONV_DIM] = sc_ref[:, SSD_CONV_DIM:3 * SSD_CONV_DIM]
    sc_o[:, 2 * SSD_CONV_DIM:] = bx
    xbc = _silu(xbc)
    xs = xbc[:, :SSD_WIDTH]
    bm = xbc[:, SSD_WIDTH:SSD_WIDTH + SSD_STATE]
    cm = xbc[:, SSD_WIDTH + SSD_STATE:]
    dtp = _softplus(proj_ref[:, OFF_DT:OFF_DT + LANES] + dtb[...])
    v_da = dtp * (-jnp.exp(alog[...]))
    dt_e = _dot_x_c(dtp, expand[...])
    e_e = jnp.exp(_dot_x_c(v_da, expand[...]))
    dtx = dt_e * xs
    cbs = _dot(_bf(cm * bm), jnp.ones((SSD_STATE, LANES), BF16))[:, 0:1]
    dtx_t = _pad_rows_t(dtx)
    e_t = _pad_rows_t(e_e)
    c_pad = _bf(jnp.concatenate([cm, jnp.zeros((LANES - bb, SSD_STATE), F32)], axis=0))
    lane_w = lax.broadcasted_iota(jnp.int32, (SSD_WIDTH, LANES), 1)
    y_t = jnp.zeros((SSD_WIDTH, LANES), F32)
    for j in range(bb):
        s_old = ssd_ref[j].reshape(SSD_WIDTH, SSD_STATE)
        ssd_o[j] = (e_t[:, j:j + 1] * s_old + dtx_t[:, j:j + 1] * bm[j:j + 1, :]).reshape(
            SSD_HEADS, SSD_HEAD_DIM, SSD_STATE)
        y_t = y_t + jnp.where(lane_w == j, _dot_nt(_bf(s_old), c_pad), 0.0)
    y = cbs * dtx + y_t.T[0:bb, :] * e_e
    yb = y + dexp[...] * xs
    ymix_ref[:, RG_WIDTH:RG_WIDTH + SSD_WIDTH] = _rms(
        yb * _silu(proj_ref[:, OFF_BZ:OFF_BZ + SSD_WIDTH]), snorm[...])

    q = _silu(proj_ref[:, OFF_CQ:OFF_CQ + HG_WIDTH])
    lb = _lower_bound_row(hlb[...], layer)
    gt = lb + (1.0 - lb) * _sigmoid(proj_ref[:, OFF_CF:OFF_CF + HG_WIDTH])
    kk = 1.0 - gt
    vv = proj_ref[:, OFF_CI:OFF_CI + HG_WIDTH]
    att = _dot(_bf(q * kk), headsum[...])
    o = att * vv + ohg_ref[...]
    ms = _dot(_bf(o * o), headmean[...])
    ymix_ref[:, RG_WIDTH + SSD_WIDTH:] = (
        o * lax.rsqrt(ms + EPS) * hnorm[...] * _silu(proj_ref[:, OFF_CG:OFF_CG + HG_WIDTH]))


def _sample_mixer(layer, proj, rgc, rgh, sc, ssd_all, o_hg, acc, mp, consts):
    nb = proj.shape[0]
    bb = DEC_BB
    params = [mp["caw"], mp["cab"], mp["wg"], mp["bg"], mp["lam"], mp["cbw"], mp["cbb"],
              mp["dtb"], mp["alog"], mp["dexp"], mp["snorm"], mp["hlb"], mp["hnorm"],
              consts["expand"], consts["headsum"], consts["headmean"]]
    row2 = lambda w: pl.BlockSpec((bb, w), lambda i: (i, 0))
    small_specs = [row2(3 * RG_WIDTH), row2(RG_WIDTH), row2(3 * SSD_CONV_DIM)]
    ssd_spec = pl.BlockSpec((None, bb, SSD_HEADS, SSD_HEAD_DIM, SSD_STATE),
                            lambda i: (layer, i, 0, 0, 0))
    out_shape = (
        jax.ShapeDtypeStruct((nb, D_MIX), F32),
        jax.ShapeDtypeStruct((nb, 3 * RG_WIDTH), F32),
        jax.ShapeDtypeStruct((nb, RG_WIDTH), F32),
        jax.ShapeDtypeStruct((nb, 3 * SSD_CONV_DIM), F32),
        jax.ShapeDtypeStruct(ssd_all.shape, F32),
    )
    n_acc = len(acc)
    n_lead = 6
    ssd_out_spec = ssd_spec if n_acc else pl.BlockSpec(
        (DEPTH, bb, SSD_HEADS, SSD_HEAD_DIM, SSD_STATE), lambda i: (0, i, 0, 0, 0))
    return pl.pallas_call(
        functools.partial(_sample_mixer_kernel, layer, PAST_LEN, n_acc),
        grid=(nb // bb,),
        in_specs=[row2(NPROJ)] + small_specs + [ssd_spec, row2(HG_WIDTH)]
        + [pl.BlockSpec(memory_space=pl.ANY)] * n_acc + [_const_spec(p) for p in params],
        out_specs=tuple([row2(D_MIX)] + small_specs + [ssd_out_spec]),
        out_shape=out_shape,
        input_output_aliases={n_lead + k: 4 + k for k in range(n_acc)},
        compiler_params=pltpu.CompilerParams(
            dimension_semantics=("parallel",), vmem_limit_bytes=VMEM_LIMIT),
        name="sample_mixer",
    )(proj, rgc, rgh, sc, ssd_all, o_hg, *acc, *params)


def _post_kernel(final, h_ref, y_ref, p_ref, wo, nffn, wup, wdown, nple, wgate, wproj, nfin, o_ref):
    h = h_ref[...] + _dot(_bf(y_ref[...]), wo[...])
    hn = _bf(_rms(h, nffn[...]))
    for j in range(D_FF // FF_CHUNK):
        cs = slice(j * FF_CHUNK, (j + 1) * FF_CHUNK)
        z = jnp.square(jnp.maximum(_dot(hn, wup[:, cs]), 0.0))
        h = h + _dot(_bf(z), wdown[cs, :])
    gate = _sigmoid(_dot(_bf(_rms(h, nple[...])), wgate[...]))
    h = h + gate * _dot(_bf(p_ref[...]), wproj[...])
    if final:
        h = _rms(h, nfin[...])
    o_ref[...] = h


def _post(final, layer, h2d, ymix2d, p_all, lp, nfin, tm):
    m = h2d.shape[0]
    params = [lp["wo"], lp["nffn"], lp["wup"], lp["wdown"], lp["nple"], lp["wgate"], lp["wproj"], nfin]
    return pl.pallas_call(
        functools.partial(_post_kernel, final),
        grid=(m // tm,),
        in_specs=[pl.BlockSpec((tm, D_MODEL), lambda i: (i, 0)),
                  pl.BlockSpec((tm, D_MIX), lambda i: (i, 0)),
                  pl.BlockSpec((None, tm, PLE_DIM), lambda i: (layer, i, 0))]
        + [_const_spec(p) for p in params],
        out_specs=pl.BlockSpec((tm, D_MODEL), lambda i: (i, 0)),
        out_shape=jax.ShapeDtypeStruct((m, D_MODEL), F32),
        compiler_params=pltpu.CompilerParams(
            dimension_semantics=("parallel",), vmem_limit_bytes=VMEM_LIMIT),
        name="post_mixer",
    )(h2d, ymix2d, p_all, *params)


def _constants():
    t = np.arange(TC)
    ltri = (t[None, :] <= t[:, None]).astype(np.float32)
    expand = np.zeros((LANES, SSD_WIDTH), np.float32)
    for hd in range(SSD_HEADS):
        expand[hd, hd * SSD_HEAD_DIM:(hd + 1) * SSD_HEAD_DIM] = 1.0
    emat = np.zeros(((N_MXU_LEVELS + 1) * TC, TC), np.float32)
    for lev in range(1, N_MXU_LEVELS + 1):
        blk = 1 << lev
        mid = (t // blk) * blk + blk // 2
        upper = t >= mid
        r0 = (lev - 1) * TC
        emat[r0:r0 + TC] = (
            (upper[:, None] & (t[None, :] >= mid[:, None]) & (t[None, :] <= t[:, None]))
            | ((~upper)[:, None] & (t[None, :] > t[:, None]) & (t[None, :] < mid[:, None])))
    emat[N_MXU_LEVELS * TC:] = ltri
    cmask = np.zeros((N_LEVELS + 1, TC, 2 * TC), np.float32)
    cmask[0] = np.tile(np.eye(TC, dtype=np.float32), (1, 2))
    for lev in range(1, N_LEVELS + 1):
        blk = 1 << lev
        up = (t % blk) >= blk // 2
        m = up[:, None] & (~up)[None, :] & ((t // blk)[:, None] == (t // blk)[None, :])
        cmask[lev] = np.tile(m.astype(np.float32), (1, 2))
    hidx = np.arange(HG_WIDTH) // HG_HEAD_DIM
    bd = (hidx[:, None] == hidx[None, :]).astype(np.float32)
    pair = bd[:LANES, :LANES]
    return {
        "ltri2": jnp.asarray(np.tile(ltri, (1, 2)), BF16),
        "expand": jnp.asarray(expand, BF16),
        "emat2": jnp.asarray(np.tile(emat, (1, 2)), BF16),
        "cmask": jnp.asarray(cmask, F32),
        "pairmask": jnp.asarray(pair, F32),
        "pairmean": jnp.asarray(pair / HG_HEAD_DIM, BF16),
        "headsum": jnp.asarray(bd, BF16), "headmean": jnp.asarray(bd / HG_HEAD_DIM, BF16),
    }


def _block_diag(w):
    hh, d, _ = w.shape
    eye = jnp.eye(hh, dtype=w.dtype)
    return (eye[:, None, :, None] * w[:, :, None, :]).reshape(hh * d, hh * d)


def _pair_gate_weights(wa, wx):
    hh, d, _ = wa.shape
    pa = jnp.stack([_block_diag(wa[2 * g:2 * g + 2]) for g in range(hh // 2)])
    px = jnp.stack([_block_diag(wx[2 * g:2 * g + 2]) for g in range(hh // 2)])
    return jnp.concatenate([pa, px], axis=2)


def _pad_lanes(v):
    return jnp.pad(v, (0, LANES - v.shape[0]))[None, :]


def kernel(x_prompt, x_sample, state_rg_conv, state_rg_h, state_ssd_conv, state_ssd, state_hgrn,
           p_prompt, p_sample, norm_mix, w_in, conv_a_w, conv_a_b, rg_wa, rg_ba, rg_wx, rg_bx,
           rg_lambda, conv_b_w, conv_b_b, ssd_dt_bias, ssd_a_log, ssd_d, ssd_norm,
           hg_lower_bounds, hg_norm, w_out, norm_ffn, w_up, w_down, norm_ple, w_ple_gate,
           w_ple_proj, norm_final):
    bsz, seq, _ = x_prompt.shape
    nb = x_sample.shape[0]
    consts = _constants()
    w_in_t = jnp.swapaxes(w_in, 1, 2)
    nfin = norm_final[None, :]
    ps_all = p_sample.reshape(DEPTH, nb, PLE_DIM)

    hp = x_prompt
    hs = x_sample.reshape(nb, D_MODEL)
    st_p = [[] for _ in range(5)]
    st_s = [[] for _ in range(3)]
    hg_t_all = jnp.transpose(state_hgrn, (0, 2, 3, 4, 1))
    acc_ssd = ()
    acc_hg = ()
    for i in range(DEPTH):
        w_re = _prep_w_in(w_in_t, i)
        gmix = norm_mix[i][None, :]
        mp = {
            "caw": conv_a_w[i], "cab": conv_a_b[i][None, :],
            "wg": jnp.concatenate([_block_diag(rg_wa[i]), _block_diag(rg_wx[i])], axis=1).astype(BF16),
            "bg": jnp.concatenate([rg_ba[i], rg_bx[i]])[None, :],
            "wgp": _pair_gate_weights(rg_wa[i], rg_wx[i]).astype(BF16),
            "bgp": jnp.concatenate([rg_ba[i].reshape(NGRP, 1, LANES),
                                    rg_bx[i].reshape(NGRP, 1, LANES)], axis=2),
            "lam": rg_lambda[i][None, :],
            "cbw": conv_b_w[i], "cbb": conv_b_b[i][None, :],
            "dtb": _pad_lanes(ssd_dt_bias[i]), "alog": _pad_lanes(ssd_a_log[i]),
            "dexp": jnp.repeat(ssd_d[i], SSD_HEAD_DIM)[None, :],
            "snorm": ssd_norm[i][None, :],
            "hlb": hg_lower_bounds, "hnorm": jnp.tile(hg_norm[i], HG_HEADS)[None, :],
        }
        lp = {
            "wo": _cast_bf16(w_out, i, 512), "nffn": norm_ffn[i][None, :],
            "wup": _cast_bf16(w_up, i, 256), "wdown": _cast_bf16(w_down, i, 1024),
            "nple": norm_ple[i][None, :], "wgate": _cast_bf16(w_ple_gate, i, 512),
            "wproj": _cast_bf16(w_ple_proj, i, 256),
        }
        final = i == DEPTH - 1

        hp, rgc, rgh, sc, ssd, hg = _prompt_layer(
            i, final, hp, p_prompt, gmix, w_re, mp, lp, nfin, consts)
        for lst, s in zip(st_p, (rgc, rgh.reshape(bsz, RG_WIDTH), sc,
                                 ssd.reshape(bsz, SSD_HEADS, SSD_HEAD_DIM, SSD_STATE), hg)):
            lst.append(s)

        proj_s = _in_proj(hs, gmix, w_re, nb)
        o_hg_t, hg_acc = _hg_step(i, proj_s, hg_lower_bounds, hg_t_all, acc_hg)
        acc_hg = (hg_acc,)
        ymix_s, rgc, rgh, sc, ssd_acc = _sample_mixer(
            i, proj_s,
            state_rg_conv[i].reshape(nb, 3 * RG_WIDTH), state_rg_h[i],
            state_ssd_conv[i].reshape(nb, 3 * SSD_CONV_DIM),
            state_ssd, o_hg_t.T, acc_ssd, mp, consts)
        acc_ssd = (ssd_acc,)
        hs = _post(final, i, hs, ymix_s, ps_all, lp, nfin, nb)
        for lst, s in zip(st_s, (rgc.reshape(nb, CONV_W - 1, RG_WIDTH), rgh,
                                 sc.reshape(nb, CONV_W - 1, SSD_CONV_DIM))):
            lst.append(s)

    y_prompt = hp
    y_sample = hs.reshape(nb, 1, D_MODEL)
    outs_p = [jnp.stack(lst) for lst in st_p]
    outs_s = [jnp.stack(lst) for lst in st_s] + [
        acc_ssd[0], jnp.transpose(acc_hg[0], (0, 4, 1, 2, 3))]
    return (y_prompt, y_sample, *outs_p, *outs_s)
```

```python
import functools

import numpy as np
import jax
import jax.numpy as jnp
from jax import lax
from jax.experimental import pallas as pl
from jax.experimental.pallas import tpu as pltpu

F32 = jnp.float32
BF16 = jnp.bfloat16

D_MODEL = 1024
DEPTH = 2
PAST_LEN = 16384
PLE_DIM = 256
D_FF = 4 * D_MODEL
CONV_W = 4
EPS = 1e-6
RG_WIDTH = 512
RG_HEADS = 8
RG_HEAD_DIM = 64
RG_C = 8.0
SSD_WIDTH = 512
SSD_HEAD_DIM = 64
SSD_HEADS = 8
SSD_STATE = 128
SSD_CONV_DIM = SSD_WIDTH + 2 * SSD_STATE
HG_WIDTH = 512
HG_HEADS = 8
HG_HEAD_DIM = 64
D_MIX = RG_WIDTH + SSD_WIDTH + HG_WIDTH

LOG2E = 1.4426950408889634
LANES = 128
SUBLANES = 8
NGRP = RG_WIDTH // LANES

OFF_AX = 0
OFF_AG = 512
OFF_BZ = 1024
OFF_XBC = 1536
OFF_CQ = 2304
OFF_CF = 2816
OFF_CI = 3328
OFF_CG = 3840
OFF_DT = 4352
NPROJ = OFF_DT + LANES
DT_COL = 2 * RG_WIDTH + SSD_WIDTH + SSD_CONV_DIM
D_IN_PROJ = DT_COL + SSD_HEADS + 4 * HG_WIDTH

TC = 128
HALF = TC // 2
N_LEVELS = 7
N_MXU_LEVELS = 3
DEC_BB = 16
FF_CHUNK = 1024
VMEM_LIMIT = 58 * 1024 * 1024


def _bf(x):
    return x.astype(BF16)


def _dot(a, b):
    return jnp.dot(a, b, preferred_element_type=F32)


def _dot_nt(a, b):
    return lax.dot_general(a, b, (((1,), (1,)), ((), ())), preferred_element_type=F32)


def _split3(x):
    hi = _bf(x)
    r1 = x - hi.astype(F32)
    mid = _bf(r1)
    lo = _bf(r1 - mid.astype(F32))
    return hi, mid, lo


def _dot_x_c(x, c):
    hi, mid, lo = _split3(x)
    return _dot(hi, c) + _dot(mid, c) + _dot(lo, c)


def _split2(x):
    hi = _bf(x)
    return hi, _bf(x - hi.astype(F32))


def _sigmoid(x):
    return 0.5 * jnp.tanh(0.5 * x) + 0.5


def _silu(x):
    h = 0.5 * x
    return h * jnp.tanh(h) + h


def _softplus(x):
    return jnp.maximum(x, 0.0) + jnp.log1p(jnp.exp(-jnp.abs(x)))


def _gelu_tanh(x):
    c = np.float32(np.sqrt(2.0 / np.pi))
    return 0.5 * x * (1.0 + jnp.tanh(c * (x + 0.044715 * (x * x * x))))


def _rms(x, gain):
    return x * lax.rsqrt(jnp.mean(x * x, axis=-1, keepdims=True) + EPS) * gain


def _lower_bound_row(hlb, layer):
    m = jnp.max(hlb, axis=0, keepdims=True)
    e = jnp.exp(hlb - m)
    sm = e / jnp.sum(e, axis=0, keepdims=True)
    lb = jnp.zeros((1, HG_WIDTH), F32)
    for j in range(1, layer + 1):
        lb = lb + sm[j:j + 1, :]
    return lb


def _rg_coeffs(gates, lam_row):
    w = gates.shape[1] // 2
    r = _sigmoid(gates[:, :w])
    i = _sigmoid(gates[:, w:])
    log_a = -RG_C * r * _softplus(-lam_row)
    a = jnp.exp(log_a)
    mult = jnp.sqrt(-jnp.tanh(log_a) * (a * a + 1.0))
    return a, mult, i


def _const_spec(arr):
    nd = arr.ndim
    return pl.BlockSpec(arr.shape, lambda *_: (0,) * nd, pipeline_mode=pl.Buffered(1))


def _cast_kernel(w_ref, o_ref):
    o_ref[...] = _bf(w_ref[...])


def _cast_bf16(w_all, layer, rb):
    _, rows, cols = w_all.shape
    return pl.pallas_call(
        _cast_kernel,
        grid=(rows // rb,),
        in_specs=[pl.BlockSpec((None, rb, cols), lambda i: (layer, i, 0))],
        out_specs=pl.BlockSpec((rb, cols), lambda i: (i, 0)),
        out_shape=jax.ShapeDtypeStruct((rows, cols), BF16),
        compiler_params=pltpu.CompilerParams(dimension_semantics=("parallel",)),
        name="cast_bf16",
    )(w_all)


def _win_kernel(wt_ref, o_ref):
    o_ref[:, 0:OFF_CQ] = _bf(wt_ref[0:DT_COL, :].T)
    o_ref[:, OFF_CQ:OFF_DT] = _bf(wt_ref[DT_COL + SSD_HEADS:D_IN_PROJ, :].T)
    dt = jnp.concatenate([wt_ref[DT_COL:DT_COL + SSD_HEADS, :],
                          jnp.zeros((LANES - SSD_HEADS, LANES), F32)], axis=0)
    o_ref[:, OFF_DT:NPROJ] = _bf(dt.T)


def _prep_w_in(wt_all, layer):
    return pl.pallas_call(
        _win_kernel,
        grid=(D_MODEL // LANES,),
        in_specs=[pl.BlockSpec((None, D_IN_PROJ, LANES), lambda i: (layer, 0, i))],
        out_specs=pl.BlockSpec((LANES, NPROJ), lambda i: (i, 0)),
        out_shape=jax.ShapeDtypeStruct((D_MODEL, NPROJ), BF16),
        compiler_params=pltpu.CompilerParams(dimension_semantics=("parallel",)),
        name="prep_w_in",
    )(wt_all)


def _proj_kernel(x_ref, g_ref, w_ref, o_ref):
    u = _rms(x_ref[...], g_ref[...])
    o_ref[...] = _dot(_bf(u), w_ref[...])


def _in_proj(x2d, gain, w_bf, tm):
    m = x2d.shape[0]
    return pl.pallas_call(
        _proj_kernel,
        grid=(m // tm,),
        in_specs=[pl.BlockSpec((tm, D_MODEL), lambda i: (i, 0)),
                  _const_spec(gain), _const_spec(w_bf)],
        out_specs=pl.BlockSpec((tm, NPROJ), lambda i: (i, 0)),
        out_shape=jax.ShapeDtypeStruct((m, NPROJ), F32),
        compiler_params=pltpu.CompilerParams(
            dimension_semantics=("parallel",), vmem_limit_bytes=VMEM_LIMIT),
        name="in_proj",
    )(x2d, gain, w_bf)


def _mixer_chunk(layer, row0, pv, ymix, side_work,
                 caw, cab, wgp, bgp, lam, cbw, cbb, dtb, alog, dexp, snorm,
                 hlb, hnorm, ltri2, emat2, cmask, pairmask, pairmean,
                 xea, xeb, hcar, s_t, st_hg):
    row_h = lax.broadcasted_iota(jnp.int32, (HALF, 1), 0)
    sub_h = row_h & (SUBLANES - 1)
    first_head = lax.broadcasted_iota(jnp.int32, (HALF, LANES), 1) < HG_HEAD_DIM
    first_head_tc = lax.broadcasted_iota(jnp.int32, (TC, LANES), 1) < HG_HEAD_DIM
    col_tc = lax.broadcasted_iota(jnp.int32, (HALF, TC), 1)
    row_tc = lax.broadcasted_iota(jnp.int32, (HALF, TC), 0)

    tiles = [slice(t * HALF, (t + 1) * HALF) for t in range(TC // HALF)]

    def lanes(off, g):
        return slice(off + g * LANES, off + (g + 1) * LANES)

    def conv_rows(prev8, x, w_ref, b_ref, gs):
        xcat = jnp.concatenate([prev8, x], axis=0)
        y = b_ref[:, gs] + w_ref[CONV_W - 1:CONV_W, gs] * x
        for j in range(1, CONV_W):
            y = y + w_ref[CONV_W - 1 - j:CONV_W - j, gs] * pltpu.roll(xcat, j, 0)[SUBLANES:, :]
        return y, x[x.shape[0] - SUBLANES:, :]

    def conv(tail_ref, w_ref, b_ref, src_off, g):
        gs = lanes(0, g)
        y, tail = conv_rows(tail_ref[:, gs], pv[:, lanes(src_off, g)], w_ref, b_ref, gs)
        tail_ref[:, gs] = tail
        return y

    def mixer_a(g):
        gs = lanes(0, g)
        prev8 = xea[:, gs]
        carry = hcar[0:1, gs]
        xas = []
        for rs in tiles:
            xa, prev8 = conv_rows(prev8, pv[rs, lanes(OFF_AX, g)], caw, cab, gs)
            xas.append(xa)
        gates = _dot(_bf(jnp.concatenate(xas, axis=0)), wgp[g]) + bgp[g]
        for t, rs in enumerate(tiles):
            xa = xas[t]
            a, mult, gi = _rg_coeffs(gates[rs, :], lam[:, gs])
            mult = jnp.where(row_h + (row0 + t * HALF) == 0, 1.0, mult)
            b = mult * (gi * xa)
            k = 1
            while k < SUBLANES:
                keep = sub_h >= k
                a_sh = jnp.where(keep, pltpu.roll(a, k, 0), 1.0)
                b_sh = jnp.where(keep, pltpu.roll(b, k, 0), 0.0)
                b = b + a * b_sh
                a = a * a_sh
                k *= 2
            slabs = []
            for r in range(HALF // SUBLANES):
                sl = slice(r * SUBLANES, (r + 1) * SUBLANES)
                h_r = a[sl, :] * carry + b[sl, :]
                slabs.append(h_r)
                carry = h_r[SUBLANES - 1:SUBLANES, :]
            ymix[rs, gs] = jnp.concatenate(slabs, axis=0) * _gelu_tanh(
                pv[rs, lanes(OFF_AG, g)])
        xea[:, gs] = prev8
        hcar[0:1, gs] = carry

    bm = _silu(conv(xeb, cbw, cbb, OFF_XBC, NGRP))
    cm = _silu(conv(xeb, cbw, cbb, OFF_XBC, NGRP + 1))
    dtp = _softplus(pv[:, OFF_DT:OFF_DT + LANES] + dtb[...])
    v_da = dtp * (-LOG2E * jnp.exp(alog[...]))
    cum = _dot(ltri2[...], jnp.concatenate(_split2(v_da), axis=0))
    cum_t = cum.T

    def per_head(arr, g, rs):
        rows = arr[rs, :]
        return jnp.where(first_head[0:rows.shape[0], :], rows[:, 2 * g:2 * g + 1], rows[:, 2 * g + 1:2 * g + 2])
    cb = _dot_nt(_bf(cm), _bf(bm))
    cmb = _bf(cm)
    bmt = _bf(bm.T)
    ssq_parts = []

    def mixer_b(g):
        gs = lanes(0, g)
        cum_last = per_head(cum, g, slice(TC - 1, TC))
        s_old = _bf(s_t[g])
        prev8 = xeb[:, gs]
        xs_t, x2a, x2b, wx, cum_g = [], [], [], [], []
        for rs in tiles:
            xs, prev8 = conv_rows(prev8, pv[rs, lanes(OFF_XBC, g)], cbw, cbb, gs)
            xs = _silu(xs)
            dt_g = per_head(dtp, g, rs)
            cum_g.append(per_head(cum, g, rs))
            dtx = xs * dt_g
            xs_t.append(xs)
            x2a.append(_bf(jnp.where(first_head, dtx, 0.0)))
            x2b.append(_bf(jnp.where(first_head, 0.0, dtx)))
            wx.append(_bf(jnp.exp2(cum_last - cum_g[-1]) * dtx))
        xeb[:, gs] = prev8
        x2 = jnp.concatenate(x2a + x2b, axis=0)
        g_rows = []
        for t, rs in enumerate(tiles):
            gmats = []
            for hh in range(2):
                hd = 2 * g + hh
                seg = cum[rs, hd:hd + 1] - cum_t[hd:hd + 1, :]
                gmats.append(_bf(cb[rs, :] * jnp.exp2(jnp.where(col_tc <= row_tc + t * HALF, seg, -jnp.inf))))
            g_rows.append(jnp.concatenate(gmats, axis=1))
        y_all = (_dot(jnp.concatenate(g_rows, axis=0), x2)
                 + _dot(cmb, s_old) * jnp.exp2(jnp.concatenate(cum_g, axis=0)))
        for t, rs in enumerate(tiles):
            yz = (y_all[rs, :] + dexp[:, gs] * xs_t[t]) * _silu(pv[rs, lanes(OFF_BZ, g)])
            ssq_parts.append(yz * yz)
            ymix[rs, lanes(RG_WIDTH, g)] = yz
        s_t[g] = jnp.exp2(cum_last) * s_t[g] + _dot(bmt, jnp.concatenate(wx, axis=0))

    lb_all = _lower_bound_row(hlb[...], layer)

    def forget_gate(g):
        lb = lb_all[:, g * LANES:(g + 1) * LANES]
        return lb + (1.0 - lb) * _sigmoid(pv[:, OFF_CF + g * LANES:OFF_CF + (g + 1) * LANES])

    def mixer_c(g, gt, xall):
        gs = lanes(0, g)
        b_off = N_MXU_LEVELS * TC
        bcum = xall[b_off:b_off + TC, :]
        blast = bcum[TC - 1:TC, :]

        def level_exponent(lev):
            if lev <= N_MXU_LEVELS:
                return xall[(lev - 1) * TC:lev * TC, :]
            blk = 1 << lev
            refs = [jnp.broadcast_to(bcum[j * blk + blk // 2 - 1:j * blk + blk // 2, :], (blk, LANES))
                    for j in range(TC // blk)]
            return -jnp.abs(bcum - jnp.concatenate(refs, axis=0))
        q = _silu(pv[:, lanes(OFF_CQ, g)])
        kk = 1.0 - gt
        vv = pv[:, lanes(OFF_CI, g)]
        zeros = jnp.zeros((HG_HEAD_DIM, TC), BF16)

        def keys_by_head(kt):
            ktt = _bf(kt.T)
            return jnp.concatenate(
                [jnp.concatenate([ktt[:HG_HEAD_DIM, :], zeros], axis=1),
                 jnp.concatenate([zeros, ktt[HG_HEAD_DIM:, :]], axis=1)], axis=0)

        att = _dot(_bf(q), keys_by_head(kk)) * cmask[0]
        for lev in range(1, N_LEVELS + 1):
            e = jnp.exp2(level_exponent(lev))
            att = att + _dot(_bf(q * e), keys_by_head(kk * e)) * cmask[lev]
        v2 = jnp.concatenate([_bf(jnp.where(first_head_tc, vv, 0.0)),
                              _bf(jnp.where(first_head_tc, 0.0, vv))], axis=0)
        o = _dot(_bf(att), v2) + _dot_nt(_bf(q * jnp.exp2(bcum)), _bf(st_hg[g]))
        khat = kk * jnp.exp2(blast - bcum)
        st_hg[g] = jnp.exp2(blast) * st_hg[g] + pairmask[...] * _dot(_bf(vv.T), _bf(khat))
        ms = _dot(_bf(o * o), pairmean[...])
        ymix[:, lanes(RG_WIDTH + SSD_WIDTH, g)] = (
            o * lax.rsqrt(ms + EPS) * hnorm[:, gs] * _silu(pv[:, lanes(OFF_CG, g)]))

    for pair in range(NGRP // 2):
        g0, g1 = 2 * pair, 2 * pair + 1
        gts = [forget_gate(g0), forget_gate(g1)]
        lf = jnp.log(jnp.concatenate(gts, axis=1)) * LOG2E
        hi = _bf(lf)
        lo = _bf(lf - hi.astype(F32))
        xall2 = _dot(emat2[...], jnp.concatenate([hi, lo], axis=0))
        for g, gt in zip((g0, g1), gts):
            side_work[g]()
            mixer_a(g)
            mixer_b(g)
            mixer_c(g, gt, xall2[:, (g - g0) * LANES:(g - g0 + 1) * LANES])

    for t, rs in enumerate(tiles):
        ssq = ssq_parts[t]
        for g in range(1, NGRP):
            ssq = ssq + ssq_parts[g * len(tiles) + t]
        rinv = lax.rsqrt(jnp.sum(ssq, axis=-1, keepdims=True) * (1.0 / SSD_WIDTH) + EPS)
        for g in range(NGRP):
            ys = lanes(RG_WIDTH, g)
            ymix[rs, ys] = ymix[rs, ys] * rinv * snorm[:, lanes(0, g)]


N_MIX_PARAMS = 18
MXU_N = 256
PROJ_BOUNDS = (0, 5 * MXU_N, 9 * MXU_N, 13 * MXU_N, NPROJ)
assert len(PROJ_BOUNDS) == NGRP + 1
CHUNKS_PER_STEP = 2


N_POST_PARAMS = 8


def _prompt_layer_kernel(layer, final, pos0, nsteps, n_valid,
                         x_ref, xn_ref, xp_ref, p_ref, gmix, w_in_ref, *rest):
    params = rest[:N_MIX_PARAMS]
    k = N_MIX_PARAMS + N_POST_PARAMS
    wo, nffn, wup, wdown, nple, wgate, wproj, nfin = rest[N_MIX_PARAMS:k]
    h_out, rgc_o, rgh_o, sc_o, ssd_o, hg_o = rest[k:k + 6]
    proj_a, proj_b, ymix_s, xea, xeb, hcar, s_t, st_hg = rest[k + 6:]
    states = (xea, xeb, hcar, s_t, st_hg)
    s = pl.program_id(0)
    live = s < n_valid
    c = jnp.minimum(s, n_valid - 1) % nsteps

    @pl.when(s == 0)
    def _prologue():
        proj_a[...] = _dot(_bf(_rms(x_ref[0, 0:TC, :], gmix[...])), w_in_ref[...])
        ymix_s[...] = jnp.zeros_like(ymix_s)

    @pl.when(jnp.logical_and(c == 0, live))
    def _init():
        for ref in states:
            ref[...] = jnp.zeros_like(ref)

    ymix_prev = _bf(ymix_s[...])
    n_ff = D_FF // FF_CHUNK
    carry = {}

    def post_piece(j):
        if j == 0:
            carry["h"] = xp_ref[0] + _dot(ymix_prev, wo[...])
            carry["hn"] = _bf(_rms(carry["h"], nffn[...]))
        elif j <= n_ff:
            cs = slice((j - 1) * FF_CHUNK, j * FF_CHUNK)
            z = jnp.square(jnp.maximum(_dot(carry["hn"], wup[:, cs]), 0.0))
            carry["h"] = carry["h"] + _dot(_bf(z), wdown[cs, :])
        elif j == n_ff + 1:
            h = carry["h"]
            gate = _sigmoid(_dot(_bf(_rms(h, nple[...])), wgate[...]))
            h = h + gate * _dot(_bf(p_ref[0]), wproj[...])
            h_out[0] = _rms(h, nfin[...]) if final else h

    def side_work(x, dst, first_piece):
        u = _bf(_rms(x, gmix[...]))

        def slot(j):
            cs = slice(PROJ_BOUNDS[j], PROJ_BOUNDS[j + 1])
            dst[:, cs] = _dot(u, w_in_ref[:, cs])
            post_piece(first_piece + j)

        return [functools.partial(slot, j) for j in range(NGRP)]

    @pl.when(live)
    def _mix_and_side_work():
        row0 = c * (CHUNKS_PER_STEP * TC) + pos0
        _mixer_chunk(layer, row0, proj_a, ymix_s.at[0:TC],
                     side_work(x_ref[0, TC:2 * TC, :], proj_b, 0), *params, *states)
        _mixer_chunk(layer, row0 + TC, proj_b, ymix_s.at[TC:2 * TC],
                     side_work(xn_ref[0], proj_a, NGRP), *params, *states)

    @pl.when(jnp.logical_not(live))
    def _drain():
        for j in range(n_ff + 2):
            post_piece(j)

    @pl.when(jnp.logical_and(c == nsteps - 1, live))
    def _emit():
        rgc_o[0] = xea[SUBLANES - 3:SUBLANES, :]
        sc_o[0] = xeb[SUBLANES - 3:SUBLANES, :]
        rgh_o[0] = hcar[0:1, :]
        for g in range(NGRP):
            ssd_o[0, g * LANES:(g + 1) * LANES, :] = s_t[g].T
            s_pair = st_hg[g].T
            for hh in range(2):
                lo = hh * HG_HEAD_DIM
                hg_o[0, 2 * g + hh] = s_pair[lo:lo + HG_HEAD_DIM, lo:lo + HG_HEAD_DIM]


def _prompt_layer(layer, final, x3, p_all4, gmix, w_in_bf, mp, lp, nfin, consts):
    bsz, seq, _ = x3.shape
    step_rows = CHUNKS_PER_STEP * TC
    nsteps = seq // step_rows
    n_valid = bsz * nsteps
    chunks_per_row = seq // TC
    n_chunks = bsz * chunks_per_row
    params = [mp["caw"], mp["cab"], mp["wgp"], mp["bgp"], mp["lam"], mp["cbw"], mp["cbb"],
              mp["dtb"], mp["alog"], mp["dexp"], mp["snorm"], mp["hlb"], mp["hnorm"],
              consts["ltri2"], consts["emat2"],
              consts["cmask"], consts["pairmask"], consts["pairmean"]]
    post_params = [lp["wo"], lp["nffn"], lp["wup"], lp["wdown"], lp["nple"], lp["wgate"],
                   lp["wproj"], nfin]
    assert len(params) == N_MIX_PARAMS and len(post_params) == N_POST_PARAMS

    def mix_step(s):
        return jnp.minimum(s, n_valid - 1)

    def cur(s):
        return mix_step(s) // nsteps, mix_step(s) % nsteps, 0

    def nxt(s):
        flat = jnp.minimum((mix_step(s) + 1) * CHUNKS_PER_STEP, n_chunks - 1)
        return flat // chunks_per_row, flat % chunks_per_row, 0

    def prv(s):
        sp = jnp.maximum(s - 1, 0)
        return sp // nsteps, sp % nsteps, 0

    def state_idx(nd):
        return lambda s: (mix_step(s) // nsteps,) + (0,) * (nd - 1)

    out_shape = (
        jax.ShapeDtypeStruct((bsz, seq, D_MODEL), F32),
        jax.ShapeDtypeStruct((bsz, CONV_W - 1, RG_WIDTH), F32),
        jax.ShapeDtypeStruct((bsz, 1, RG_WIDTH), F32),
        jax.ShapeDtypeStruct((bsz, CONV_W - 1, SSD_CONV_DIM), F32),
        jax.ShapeDtypeStruct((bsz, SSD_WIDTH, SSD_STATE), F32),
        jax.ShapeDtypeStruct((bsz, HG_HEADS, HG_HEAD_DIM, HG_HEAD_DIM), F32),
    )
    out_specs = (
        pl.BlockSpec((1, step_rows, D_MODEL), prv),
        pl.BlockSpec((1, CONV_W - 1, RG_WIDTH), state_idx(3)),
        pl.BlockSpec((1, 1, RG_WIDTH), state_idx(3)),
        pl.BlockSpec((1, CONV_W - 1, SSD_CONV_DIM), state_idx(3)),
        pl.BlockSpec((1, SSD_WIDTH, SSD_STATE), state_idx(3)),
        pl.BlockSpec((1, HG_HEADS, HG_HEAD_DIM, HG_HEAD_DIM), state_idx(4)),
    )
    scratch = [
        pltpu.VMEM((TC, NPROJ), F32),
        pltpu.VMEM((TC, NPROJ), F32),
        pltpu.VMEM((step_rows, D_MIX), F32),
        pltpu.VMEM((SUBLANES, RG_WIDTH), F32),
        pltpu.VMEM((SUBLANES, SSD_CONV_DIM), F32),
        pltpu.VMEM((SUBLANES, RG_WIDTH), F32),
        pltpu.VMEM((NGRP, SSD_STATE, LANES), F32),
        pltpu.VMEM((NGRP, LANES, LANES), F32),
    ]
    return pl.pallas_call(
        functools.partial(_prompt_layer_kernel, layer, final, 0, nsteps, n_valid),
        grid=(n_valid + 1,),
        in_specs=[pl.BlockSpec((1, step_rows, D_MODEL), cur),
                  pl.BlockSpec((1, TC, D_MODEL), nxt),
                  pl.BlockSpec((1, step_rows, D_MODEL), prv),
                  pl.BlockSpec((None, 1, step_rows, PLE_DIM), lambda s: (layer,) + prv(s)),
                  _const_spec(gmix), _const_spec(w_in_bf)]
        + [_const_spec(p) for p in params + post_params],
        out_specs=out_specs,
        out_shape=out_shape,
        scratch_shapes=scratch,
        compiler_params=pltpu.CompilerParams(
            dimension_semantics=("arbitrary",), vmem_limit_bytes=VMEM_LIMIT),
        name="prompt_layer",
    )(x3, x3, x3, p_all4, gmix, w_in_bf, *params, *post_params)


def _hg_step_kernel(layer, n_acc, proj_ref, hlb, st_ref, *rest):
    ot_ref, st_o, qh_s, f_s, k_s, v_s = rest[n_acc:]
    h = pl.program_id(0)
    if n_acc == 0:
        for other in range(DEPTH):
            if other != layer:
                st_o[other] = jnp.zeros(st_o.shape[1:], F32)
        st_o = st_o.at[layer]

    @pl.when(h == 0)
    def _prep():
        q = _silu(proj_ref[:, OFF_CQ:OFF_CQ + HG_WIDTH])
        lb = _lower_bound_row(hlb[...], layer)
        gt = lb + (1.0 - lb) * _sigmoid(proj_ref[:, OFF_CF:OFF_CF + HG_WIDTH])
        fdec = jnp.exp(jnp.log(gt))
        qh_s[...] = (q * fdec).T
        f_s[...] = fdec.T
        k_s[...] = (1.0 - gt).T
        v_s[...] = proj_ref[:, OFF_CI:OFF_CI + HG_WIDTH].T

    base = pl.multiple_of(h * HG_HEAD_DIM, HG_HEAD_DIM)
    v_h = v_s[pl.ds(base, HG_HEAD_DIM), :]

    def body(dk, acc):
        s = st_ref[dk]
        st_o[dk] = f_s[pl.ds(base + dk, 1), :] * s + k_s[pl.ds(base + dk, 1), :] * v_h
        return acc + qh_s[pl.ds(base + dk, 1), :] * s

    ot_ref[...] = lax.fori_loop(0, HG_HEAD_DIM, body,
                                jnp.zeros((HG_HEAD_DIM, LANES), F32), unroll=8)


def _hg_step(layer, proj, hlb, st_all, acc):
    nb = proj.shape[0]
    blk = (HG_HEAD_DIM, HG_HEAD_DIM, nb)
    st_spec = pl.BlockSpec((None, None) + blk, lambda h: (layer, h, 0, 0, 0))
    n_acc = len(acc)
    st_out_spec = st_spec if n_acc else pl.BlockSpec(
        (DEPTH, None) + blk, lambda h: (0, h, 0, 0, 0))
    return pl.pallas_call(
        functools.partial(_hg_step_kernel, layer, n_acc),
        grid=(HG_HEADS,),
        in_specs=[_const_spec(proj), _const_spec(hlb), st_spec]
        + [pl.BlockSpec(memory_space=pl.ANY)] * n_acc,
        out_specs=(pl.BlockSpec((HG_HEAD_DIM, nb), lambda h: (h, 0)), st_out_spec),
        out_shape=(jax.ShapeDtypeStruct((HG_WIDTH, nb), F32),
                   jax.ShapeDtypeStruct(st_all.shape, F32)),
        scratch_shapes=[pltpu.VMEM((HG_WIDTH, nb), F32)] * 4,
        input_output_aliases={3 + k: 1 + k for k in range(n_acc)},
        compiler_params=pltpu.CompilerParams(
            dimension_semantics=("arbitrary",), vmem_limit_bytes=VMEM_LIMIT),
        name="hg_step",
    )(proj, hlb, st_all, *acc)


def _pad_rows_t(x):
    pad = jnp.zeros((LANES - x.shape[0], x.shape[1]), F32)
    return jnp.concatenate([x, pad], axis=0).T


def _sample_mixer_kernel(layer, pos0, n_acc, proj_ref, rgc_ref, rgh_ref, sc_ref, ssd_ref, ohg_ref, *rest):
    (caw, cab, wg, bg, lam, cbw, cbb, dtb, alog, dexp, snorm,
     hlb, hnorm, expand, headsum, headmean,
     ymix_ref, rgc_o, rgh_o, sc_o, ssd_o) = rest[n_acc:]
    bb = DEC_BB
    if n_acc == 0:
        for other in range(DEPTH):
            if other != layer:
                ssd_o[other] = jnp.zeros(ssd_o.shape[1:], F32)
        ssd_o = ssd_o.at[layer]
    ax = proj_ref[:, OFF_AX:OFF_AX + RG_WIDTH]
    xa = cab[...] + caw[CONV_W - 1:CONV_W, :] * ax
    for k in range(CONV_W - 1):
        xa = xa + caw[k:k + 1, :] * rgc_ref[:, k * RG_WIDTH:(k + 1) * RG_WIDTH]
    rgc_o[:, 0:2 * RG_WIDTH] = rgc_ref[:, RG_WIDTH:3 * RG_WIDTH]
    rgc_o[:, 2 * RG_WIDTH:] = ax
    a, mult, gi = _rg_coeffs(_dot(_bf(xa), wg[...]) + bg[...], lam[...])
    if pos0 == 0:
        mult = jnp.ones_like(mult)
    h = a * rgh_ref[...] + mult * (gi * xa)
    rgh_o[...] = h
    ymix_ref[:, 0:RG_WIDTH] = h * _gelu_tanh(proj_ref[:, OFF_AG:OFF_AG + RG_WIDTH])

    bx = proj_ref[:, OFF_XBC:OFF_XBC + SSD_CONV_DIM]
    xbc = cbb[...] + cbw[CONV_W - 1:CONV_W, :] * bx
    for k in range(CONV_W - 1):
        xbc = xbc + cbw[k:k + 1, :] * sc_ref[:, k * SSD_CONV_DIM:(k + 1) * SSD_CONV_DIM]
    sc_o[:, 0:2 * SSD_CONV_DIM] = sc_ref[:, SSD_CONV_DIM:3 * SSD_CONV_DIM]
    sc_o[:, 2 * SSD_CONV_DIM:] = bx
    xbc = _silu(xbc)
    xs = xbc[:, :SSD_WIDTH]
    bm = xbc[:, SSD_WIDTH:SSD_WIDTH + SSD_STATE]
    cm = xbc[:, SSD_WIDTH + SSD_STATE:]
    dtp = _softplus(proj_ref[:, OFF_DT:OFF_DT + LANES] + dtb[...])
    v_da = dtp * (-jnp.exp(alog[...]))
    dt_e = _dot_x_c(dtp, expand[...])
    e_e = jnp.exp(_dot_x_c(v_da, expand[...]))
    dtx = dt_e * xs
    cbs = _dot(_bf(cm * bm), jnp.ones((SSD_STATE, LANES), BF16))[:, 0:1]
    dtx_t = _pad_rows_t(dtx)
    e_t = _pad_rows_t(e_e)
    c_pad = _bf(jnp.concatenate([cm, jnp.zeros((LANES - bb, SSD_STATE), F32)], axis=0))
    lane_w = lax.broadcasted_iota(jnp.int32, (SSD_WIDTH, LANES), 1)
    y_t = jnp.zeros((SSD_WIDTH, LANES), F32)
    for j in range(bb):
        s_old = ssd_ref[j].reshape(SSD_WIDTH, SSD_STATE)
        ssd_o[j] = (e_t[:, j:j + 1] * s_old + dtx_t[:, j:j + 1] * bm[j:j + 1, :]).reshape(
            SSD_HEADS, SSD_HEAD_DIM, SSD_STATE)
        y_t = y_t + jnp.where(lane_w == j, _dot_nt(_bf(s_old), c_pad), 0.0)
    y = cbs * dtx + y_t.T[0:bb, :] * e_e
    yb = y + dexp[...] * xs
    ymix_ref[:, RG_WIDTH:RG_WIDTH + SSD_WIDTH] = _rms(
        yb * _silu(proj_ref[:, OFF_BZ:OFF_BZ + SSD_WIDTH]), snorm[...])

    q = _silu(proj_ref[:, OFF_CQ:OFF_CQ + HG_WIDTH])
    lb = _lower_bound_row(hlb[...], layer)
    gt = lb + (1.0 - lb) * _sigmoid(proj_ref[:, OFF_CF:OFF_CF + HG_WIDTH])
    kk = 1.0 - gt
    vv = proj_ref[:, OFF_CI:OFF_CI + HG_WIDTH]
    att = _dot(_bf(q * kk), headsum[...])
    o = att * vv + ohg_ref[...]
    ms = _dot(_bf(o * o), headmean[...])
    ymix_ref[:, RG_WIDTH + SSD_WIDTH:] = (
        o * lax.rsqrt(ms + EPS) * hnorm[...] * _silu(proj_ref[:, OFF_CG:OFF_CG + HG_WIDTH]))


def _sample_mixer(layer, proj, rgc, rgh, sc, ssd_all, o_hg, acc, mp, consts):
    nb = proj.shape[0]
    bb = DEC_BB
    params = [mp["caw"], mp["cab"], mp["wg"], mp["bg"], mp["lam"], mp["cbw"], mp["cbb"],
              mp["dtb"], mp["alog"], mp["dexp"], mp["snorm"], mp["hlb"], mp["hnorm"],
              consts["expand"], consts["headsum"], consts["headmean"]]
    row2 = lambda w: pl.BlockSpec((bb, w), lambda i: (i, 0))
    small_specs = [row2(3 * RG_WIDTH), row2(RG_WIDTH), row2(3 * SSD_CONV_DIM)]
    ssd_spec = pl.BlockSpec((None, bb, SSD_HEADS, SSD_HEAD_DIM, SSD_STATE),
                            lambda i: (layer, i, 0, 0, 0))
    out_shape = (
        jax.ShapeDtypeStruct((nb, D_MIX), F32),
        jax.ShapeDtypeStruct((nb, 3 * RG_WIDTH), F32),
        jax.ShapeDtypeStruct((nb, RG_WIDTH), F32),
        jax.ShapeDtypeStruct((nb, 3 * SSD_CONV_DIM), F32),
        jax.ShapeDtypeStruct(ssd_all.shape, F32),
    )
    n_acc = len(acc)
    n_lead = 6
    ssd_out_spec = ssd_spec if n_acc else pl.BlockSpec(
        (DEPTH, bb, SSD_HEADS, SSD_HEAD_DIM, SSD_STATE), lambda i: (0, i, 0, 0, 0))
    return pl.pallas_call(
        functools.partial(_sample_mixer_kernel, layer, PAST_LEN, n_acc),
        grid=(nb // bb,),
        in_specs=[row2(NPROJ)] + small_specs + [ssd_spec, row2(HG_WIDTH)]
        + [pl.BlockSpec(memory_space=pl.ANY)] * n_acc + [_const_spec(p) for p in params],
        out_specs=tuple([row2(D_MIX)] + small_specs + [ssd_out_spec]),
        out_shape=out_shape,
        input_output_aliases={n_lead + k: 4 + k for k in range(n_acc)},
        compiler_params=pltpu.CompilerParams(
            dimension_semantics=("parallel",), vmem_limit_bytes=VMEM_LIMIT),
        name="sample_mixer",
    )(proj, rgc, rgh, sc, ssd_all, o_hg, *acc, *params)


def _post_kernel(final, h_ref, y_ref, p_ref, wo, nffn, wup, wdown, nple, wgate, wproj, nfin, o_ref):
    h = h_ref[...] + _dot(_bf(y_ref[...]), wo[...])
    hn = _bf(_rms(h, nffn[...]))
    for j in range(D_FF // FF_CHUNK):
        cs = slice(j * FF_CHUNK, (j + 1) * FF_CHUNK)
        z = jnp.square(jnp.maximum(_dot(hn, wup[:, cs]), 0.0))
        h = h + _dot(_bf(z), wdown[cs, :])
    gate = _sigmoid(_dot(_bf(_rms(h, nple[...])), wgate[...]))
    h = h + gate * _dot(_bf(p_ref[...]), wproj[...])
    if final:
        h = _rms(h, nfin[...])
    o_ref[...] = h


def _post(final, layer, h2d, ymix2d, p_all, lp, nfin, tm):
    m = h2d.shape[0]
    params = [lp["wo"], lp["nffn"], lp["wup"], lp["wdown"], lp["nple"], lp["wgate"], lp["wproj"], nfin]
    return pl.pallas_call(
        functools.partial(_post_kernel, final),
        grid=(m // tm,),
        in_specs=[pl.BlockSpec((tm, D_MODEL), lambda i: (i, 0)),
                  pl.BlockSpec((tm, D_MIX), lambda i: (i, 0)),
                  pl.BlockSpec((None, tm, PLE_DIM), lambda i: (layer, i, 0))]
        + [_const_spec(p) for p in params],
        out_specs=pl.BlockSpec((tm, D_MODEL), lambda i: (i, 0)),
        out_shape=jax.ShapeDtypeStruct((m, D_MODEL), F32),
        compiler_params=pltpu.CompilerParams(
            dimension_semantics=("parallel",), vmem_limit_bytes=VMEM_LIMIT),
        name="post_mixer",
    )(h2d, ymix2d, p_all, *params)


def _constants():
    t = np.arange(TC)
    ltri = (t[None, :] <= t[:, None]).astype(np.float32)
    expand = np.zeros((LANES, SSD_WIDTH), np.float32)
    for hd in range(SSD_HEADS):
        expand[hd, hd * SSD_HEAD_DIM:(hd + 1) * SSD_HEAD_DIM] = 1.0
    emat = np.zeros(((N_MXU_LEVELS + 1) * TC, TC), np.float32)
    for lev in range(1, N_MXU_LEVELS + 1):
        blk = 1 << lev
        mid = (t // blk) * blk + blk // 2
        upper = t >= mid
        r0 = (lev - 1) * TC
        emat[r0:r0 + TC] = (
            (upper[:, None] & (t[None, :] >= mid[:, None]) & (t[None, :] <= t[:, None]))
            | ((~upper)[:, None] & (t[None, :] > t[:, None]) & (t[None, :] < mid[:, None])))
    emat[N_MXU_LEVELS * TC:] = ltri
    cmask = np.zeros((N_LEVELS + 1, TC, 2 * TC), np.float32)
    cmask[0] = np.tile(np.eye(TC, dtype=np.float32), (1, 2))
    for lev in range(1, N_LEVELS + 1):
        blk = 1 << lev
        up = (t % blk) >= blk // 2
        m = up[:, None] & (~up)[None, :] & ((t // blk)[:, None] == (t // blk)[None, :])
        cmask[lev] = np.tile(m.astype(np.float32), (1, 2))
    hidx = np.arange(HG_WIDTH) // HG_HEAD_DIM
    bd = (hidx[:, None] == hidx[None, :]).astype(np.float32)
    pair = bd[:LANES, :LANES]
    return {
        "ltri2": jnp.asarray(np.tile(ltri, (1, 2)), BF16),
        "expand": jnp.asarray(expand, BF16),
        "emat2": jnp.asarray(np.tile(emat, (1, 2)), BF16),
        "cmask": jnp.asarray(cmask, F32),
        "pairmask": jnp.asarray(pair, F32),
        "pairmean": jnp.asarray(pair / HG_HEAD_DIM, BF16),
        "headsum": jnp.asarray(bd, BF16), "headmean": jnp.asarray(bd / HG_HEAD_DIM, BF16),
    }


def _block_diag(w):
    hh, d, _ = w.shape
    eye = jnp.eye(hh, dtype=w.dtype)
    return (eye[:, None, :, None] * w[:, :, None, :]).reshape(hh * d, hh * d)


def _pair_gate_weights(wa, wx):
    hh, d, _ = wa.shape
    pa = jnp.stack([_block_diag(wa[2 * g:2 * g + 2]) for g in range(hh // 2)])
    px = jnp.stack([_block_diag(wx[2 * g:2 * g + 2]) for g in range(hh // 2)])
    return jnp.concatenate([pa, px], axis=2)


def _pad_lanes(v):
    return jnp.pad(v, (0, LANES - v.shape[0]))[None, :]


def kernel(x_prompt, x_sample, state_rg_conv, state_rg_h, state_ssd_conv, state_ssd, state_hgrn,
           p_prompt, p_sample, norm_mix, w_in, conv_a_w, conv_a_b, rg_wa, rg_ba, rg_wx, rg_bx,
           rg_lambda, conv_b_w, conv_b_b, ssd_dt_bias, ssd_a_log, ssd_d, ssd_norm,
           hg_lower_bounds, hg_norm, w_out, norm_ffn, w_up, w_down, norm_ple, w_ple_gate,
           w_ple_proj, norm_final):
    bsz, seq, _ = x_prompt.shape
    nb = x_sample.shape[0]
    consts = _constants()
    w_in_t = jnp.swapaxes(w_in, 1, 2)
    nfin = norm_final[None, :]
    ps_all = p_sample.reshape(DEPTH, nb, PLE_DIM)

    hp = x_prompt
    hs = x_sample.reshape(nb, D_MODEL)
    st_p = [[] for _ in range(5)]
    st_s = [[] for _ in range(3)]
    hg_t_all = jnp.transpose(state_hgrn, (0, 2, 3, 4, 1))
    acc_ssd = ()
    acc_hg = ()
    for i in range(DEPTH):
        w_re = _prep_w_in(w_in_t, i)
        gmix = norm_mix[i][None, :]
        mp = {
            "caw": conv_a_w[i], "cab": conv_a_b[i][None, :],
            "wg": jnp.concatenate([_block_diag(rg_wa[i]), _block_diag(rg_wx[i])], axis=1).astype(BF16),
            "bg": jnp.concatenate([rg_ba[i], rg_bx[i]])[None, :],
            "wgp": _pair_gate_weights(rg_wa[i], rg_wx[i]).astype(BF16),
            "bgp": jnp.concatenate([rg_ba[i].reshape(NGRP, 1, LANES),
                                    rg_bx[i].reshape(NGRP, 1, LANES)], axis=2),
            "lam": rg_lambda[i][None, :],
            "cbw": conv_b_w[i], "cbb": conv_b_b[i][None, :],
            "dtb": _pad_lanes(ssd_dt_bias[i]), "alog": _pad_lanes(ssd_a_log[i]),
            "dexp": jnp.repeat(ssd_d[i], SSD_HEAD_DIM)[None, :],
            "snorm": ssd_norm[i][None, :],
            "hlb": hg_lower_bounds, "hnorm": jnp.tile(hg_norm[i], HG_HEADS)[None, :],
        }
        lp = {
            "wo": _cast_bf16(w_out, i, 512), "nffn": norm_ffn[i][None, :],
            "wup": _cast_bf16(w_up, i, 256), "wdown": _cast_bf16(w_down, i, 1024),
            "nple": norm_ple[i][None, :], "wgate": _cast_bf16(w_ple_gate, i, 512),
            "wproj": _cast_bf16(w_ple_proj, i, 256),
        }
        final = i == DEPTH - 1

        hp, rgc, rgh, sc, ssd, hg = _prompt_layer(
            i, final, hp, p_prompt, gmix, w_re, mp, lp, nfin, consts)
        for lst, s in zip(st_p, (rgc, rgh.reshape(bsz, RG_WIDTH), sc,
                                 ssd.reshape(bsz, SSD_HEADS, SSD_HEAD_DIM, SSD_STATE), hg)):
            lst.append(s)

        proj_s = _in_proj(hs, gmix, w_re, nb)
        o_hg_t, hg_acc = _hg_step(i, proj_s, hg_lower_bounds, hg_t_all, acc_hg)
        acc_hg = (hg_acc,)
        ymix_s, rgc, rgh, sc, ssd_acc = _sample_mixer(
            i, proj_s,
            state_rg_conv[i].reshape(nb, 3 * RG_WIDTH), state_rg_h[i],
            state_ssd_conv[i].reshape(nb, 3 * SSD_CONV_DIM),
            state_ssd, o_hg_t.T, acc_ssd, mp, consts)
        acc_ssd = (ssd_acc,)
        hs = _post(final, i, hs, ymix_s, ps_all, lp, nfin, nb)
        for lst, s in zip(st_s, (rgc.reshape(nb, CONV_W - 1, RG_WIDTH), rgh,
                                 sc.reshape(nb, CONV_W - 1, SSD_CONV_DIM))):
            lst.append(s)

    y_prompt = hp
    y_sample = hs.reshape(nb, 1, D_MODEL)
    outs_p = [jnp.stack(lst) for lst in st_p]
    outs_s = [jnp.stack(lst) for lst in st_s] + [
        acc_ssd[0], jnp.transpose(acc_hg[0], (0, 4, 1, 2, 3))]
    return (y_prompt, y_sample, *outs_p, *outs_s)
```

```python
import functools

import numpy as np
import jax
import jax.numpy as jnp
from jax import lax
from jax.experimental import pallas as pl
from jax.experimental.pallas import tpu as pltpu

F32 = jnp.float32
BF16 = jnp.bfloat16

D_MODEL = 1024
DEPTH = 2
PAST_LEN = 16384
PLE_DIM = 256
D_FF = 4 * D_MODEL
CONV_W = 4
EPS = 1e-6
RG_WIDTH = 512
RG_HEADS = 8
RG_HEAD_DIM = 64
RG_C = 8.0
SSD_WIDTH = 512
SSD_HEAD_DIM = 64
SSD_HEADS = 8
SSD_STATE = 128
SSD_CONV_DIM = SSD_WIDTH + 2 * SSD_STATE
HG_WIDTH = 512
HG_HEADS = 8
HG_HEAD_DIM = 64
D_MIX = RG_WIDTH + SSD_WIDTH + HG_WIDTH

LOG2E = 1.4426950408889634
LANES = 128
SUBLANES = 8
NGRP = RG_WIDTH // LANES

OFF_AX = 0
OFF_AG = 512
OFF_BZ = 1024
OFF_XBC = 1536
OFF_CQ = 2304
OFF_CF = 2816
OFF_CI = 3328
OFF_CG = 3840
OFF_DT = 4352
NPROJ = OFF_DT + LANES
DT_COL = 2 * RG_WIDTH + SSD_WIDTH + SSD_CONV_DIM
D_IN_PROJ = DT_COL + SSD_HEADS + 4 * HG_WIDTH

TC = 128
HALF = TC // 2
N_LEVELS = 7
N_MXU_LEVELS = 3
DEC_BB = 16
FF_CHUNK = 1024
VMEM_LIMIT = 58 * 1024 * 1024


def _bf(x):
    return x.astype(BF16)


def _dot(a, b):
    return jnp.dot(a, b, preferred_element_type=F32)


def _dot_nt(a, b):
    return lax.dot_general(a, b, (((1,), (1,)), ((), ())), preferred_element_type=F32)


def _split3(x):
    hi = _bf(x)
    r1 = x - hi.astype(F32)
    mid = _bf(r1)
    lo = _bf(r1 - mid.astype(F32))
    return hi, mid, lo


def _dot_x_c(x, c):
    hi, mid, lo = _split3(x)
    return _dot(hi, c) + _dot(mid, c) + _dot(lo, c)


def _split2(x):
    hi = _bf(x)
    return hi, _bf(x - hi.astype(F32))


def _logistic(x):
    return 1.0 / (1.0 + jnp.exp(-x))


def _sigmoid(x):
    return 0.5 * jnp.tanh(0.5 * x) + 0.5


def _silu(x):
    h = 0.5 * x
    return h * jnp.tanh(h) + h


def _softplus(x):
    return jnp.maximum(x, 0.0) + jnp.log1p(jnp.exp(-jnp.abs(x)))


def _gelu_tanh(x):
    c = np.float32(np.sqrt(2.0 / np.pi))
    return 0.5 * x * (1.0 + jnp.tanh(c * (x + 0.044715 * (x * x * x))))


def _rms(x, gain):
    return x * lax.rsqrt(jnp.mean(x * x, axis=-1, keepdims=True) + EPS) * gain


def _lower_bound_row(hlb, layer):
    m = jnp.max(hlb, axis=0, keepdims=True)
    e = jnp.exp(hlb - m)
    sm = e / jnp.sum(e, axis=0, keepdims=True)
    lb = jnp.zeros((1, HG_WIDTH), F32)
    for j in range(1, layer + 1):
        lb = lb + sm[j:j + 1, :]
    return lb


def _rg_coeffs(gates, lam_row):
    w = gates.shape[1] // 2
    r = _sigmoid(gates[:, :w])
    i = _sigmoid(gates[:, w:])
    log_a = -RG_C * r * _softplus(-lam_row)
    a = jnp.exp(log_a)
    mult = jnp.sqrt(-jnp.tanh(log_a) * (a * a + 1.0))
    return a, mult, i


def _const_spec(arr):
    nd = arr.ndim
    return pl.BlockSpec(arr.shape, lambda *_: (0,) * nd, pipeline_mode=pl.Buffered(1))


def _cast_kernel(w_ref, o_ref):
    o_ref[...] = _bf(w_ref[...])


def _cast_bf16(w_all, layer, rb):
    _, rows, cols = w_all.shape
    return pl.pallas_call(
        _cast_kernel,
        grid=(rows // rb,),
        in_specs=[pl.BlockSpec((None, rb, cols), lambda i: (layer, i, 0))],
        out_specs=pl.BlockSpec((rb, cols), lambda i: (i, 0)),
        out_shape=jax.ShapeDtypeStruct((rows, cols), BF16),
        compiler_params=pltpu.CompilerParams(dimension_semantics=("parallel",)),
        name="cast_bf16",
    )(w_all)


def _win_kernel(wt_ref, o_ref):
    o_ref[:, 0:OFF_CQ] = _bf(wt_ref[0:DT_COL, :].T)
    o_ref[:, OFF_CQ:OFF_DT] = _bf(wt_ref[DT_COL + SSD_HEADS:D_IN_PROJ, :].T)
    dt = jnp.concatenate([wt_ref[DT_COL:DT_COL + SSD_HEADS, :],
                          jnp.zeros((LANES - SSD_HEADS, LANES), F32)], axis=0)
    o_ref[:, OFF_DT:NPROJ] = _bf(dt.T)


def _prep_w_in(wt_all, layer):
    return pl.pallas_call(
        _win_kernel,
        grid=(D_MODEL // LANES,),
        in_specs=[pl.BlockSpec((None, D_IN_PROJ, LANES), lambda i: (layer, 0, i))],
        out_specs=pl.BlockSpec((LANES, NPROJ), lambda i: (i, 0)),
        out_shape=jax.ShapeDtypeStruct((D_MODEL, NPROJ), BF16),
        compiler_params=pltpu.CompilerParams(dimension_semantics=("parallel",)),
        name="prep_w_in",
    )(wt_all)


def _proj_kernel(x_ref, g_ref, w_ref, o_ref):
    u = _rms(x_ref[...], g_ref[...])
    o_ref[...] = _dot(_bf(u), w_ref[...])


def _in_proj(x2d, gain, w_bf, tm):
    m = x2d.shape[0]
    return pl.pallas_call(
        _proj_kernel,
        grid=(m // tm,),
        in_specs=[pl.BlockSpec((tm, D_MODEL), lambda i: (i, 0)),
                  _const_spec(gain), _const_spec(w_bf)],
        out_specs=pl.BlockSpec((tm, NPROJ), lambda i: (i, 0)),
        out_shape=jax.ShapeDtypeStruct((m, NPROJ), F32),
        compiler_params=pltpu.CompilerParams(
            dimension_semantics=("parallel",), vmem_limit_bytes=VMEM_LIMIT),
        name="in_proj",
    )(x2d, gain, w_bf)


def _mixer_chunk(layer, row0, pv, ymix, side_work,
                 caw, cab, wgp, bgp, lam, cbw, cbb, dtb, alog, dexp, snorm,
                 hlb, hnorm, ltri2, emat2, cmask, pairmask, pairmean,
                 xea, xeb, hcar, s_t, st_hg):
    row_h = lax.broadcasted_iota(jnp.int32, (HALF, 1), 0)
    sub_h = row_h & (SUBLANES - 1)
    first_head = lax.broadcasted_iota(jnp.int32, (HALF, LANES), 1) < HG_HEAD_DIM
    first_head_tc = lax.broadcasted_iota(jnp.int32, (TC, LANES), 1) < HG_HEAD_DIM
    col_tc = lax.broadcasted_iota(jnp.int32, (HALF, TC), 1)
    row_tc = lax.broadcasted_iota(jnp.int32, (HALF, TC), 0)

    tiles = [slice(t * HALF, (t + 1) * HALF) for t in range(TC // HALF)]

    def lanes(off, g):
        return slice(off + g * LANES, off + (g + 1) * LANES)

    def conv_rows(prev8, x, w_ref, b_ref, gs):
        xcat = jnp.concatenate([prev8, x], axis=0)
        y = b_ref[:, gs] + w_ref[CONV_W - 1:CONV_W, gs] * x
        for j in range(1, CONV_W):
            y = y + w_ref[CONV_W - 1 - j:CONV_W - j, gs] * pltpu.roll(xcat, j, 0)[SUBLANES:, :]
        return y, x[x.shape[0] - SUBLANES:, :]

    def conv(tail_ref, w_ref, b_ref, src_off, g):
        gs = lanes(0, g)
        y, tail = conv_rows(tail_ref[:, gs], pv[:, lanes(src_off, g)], w_ref, b_ref, gs)
        tail_ref[:, gs] = tail
        return y

    def mixer_a(g):
        gs = lanes(0, g)
        prev8 = xea[:, gs]
        carry = hcar[0:1, gs]
        xas = []
        for rs in tiles:
            xa, prev8 = conv_rows(prev8, pv[rs, lanes(OFF_AX, g)], caw, cab, gs)
            xas.append(xa)
        gates = _dot(_bf(jnp.concatenate(xas, axis=0)), wgp[g]) + bgp[g]
        for t, rs in enumerate(tiles):
            xa = xas[t]
            a, mult, gi = _rg_coeffs(gates[rs, :], lam[:, gs])
            mult = jnp.where(row_h + (row0 + t * HALF) == 0, 1.0, mult)
            b = mult * (gi * xa)
            k = 1
            while k < SUBLANES:
                keep = sub_h >= k
                a_sh = jnp.where(keep, pltpu.roll(a, k, 0), 1.0)
                b_sh = jnp.where(keep, pltpu.roll(b, k, 0), 0.0)
                b = b + a * b_sh
                a = a * a_sh
                k *= 2
            slabs = []
            for r in range(HALF // SUBLANES):
                sl = slice(r * SUBLANES, (r + 1) * SUBLANES)
                h_r = a[sl, :] * carry + b[sl, :]
                slabs.append(h_r)
                carry = h_r[SUBLANES - 1:SUBLANES, :]
            ymix[rs, gs] = jnp.concatenate(slabs, axis=0) * _gelu_tanh(
                pv[rs, lanes(OFF_AG, g)])
        xea[:, gs] = prev8
        hcar[0:1, gs] = carry

    bm = _silu(conv(xeb, cbw, cbb, OFF_XBC, NGRP))
    cm = _silu(conv(xeb, cbw, cbb, OFF_XBC, NGRP + 1))
    dtp = _softplus(pv[:, OFF_DT:OFF_DT + LANES] + dtb[...])
    v_da = dtp * (-LOG2E * jnp.exp(alog[...]))
    cum = _dot(ltri2[...], jnp.concatenate(_split2(v_da), axis=0))
    cum_t = cum.T

    def per_head(arr, g, rs):
        rows = arr[rs, :]
        return jnp.where(first_head[0:rows.shape[0], :], rows[:, 2 * g:2 * g + 1], rows[:, 2 * g + 1:2 * g + 2])
    cb = _dot_nt(_bf(cm), _bf(bm))
    cmb = _bf(cm)
    bmt = _bf(bm.T)
    ssq_parts = []

    def mixer_b(g):
        gs = lanes(0, g)
        cum_last = per_head(cum, g, slice(TC - 1, TC))
        s_old = _bf(s_t[g])
        prev8 = xeb[:, gs]
        xs_t, x2a, x2b, wx, cum_g = [], [], [], [], []
        for rs in tiles:
            xs, prev8 = conv_rows(prev8, pv[rs, lanes(OFF_XBC, g)], cbw, cbb, gs)
            xs = _silu(xs)
            dt_g = per_head(dtp, g, rs)
            cum_g.append(per_head(cum, g, rs))
            dtx = xs * dt_g
            xs_t.append(xs)
            x2a.append(_bf(jnp.where(first_head, dtx, 0.0)))
            x2b.append(_bf(jnp.where(first_head, 0.0, dtx)))
            wx.append(_bf(jnp.exp2(cum_last - cum_g[-1]) * dtx))
        xeb[:, gs] = prev8
        x2 = jnp.concatenate(x2a + x2b, axis=0)
        g_rows = []
        for t, rs in enumerate(tiles):
            gmats = []
            for hh in range(2):
                hd = 2 * g + hh
                seg = cum[rs, hd:hd + 1] - cum_t[hd:hd + 1, :]
                gmats.append(_bf(cb[rs, :] * jnp.exp2(jnp.where(col_tc <= row_tc + t * HALF, seg, -jnp.inf))))
            g_rows.append(jnp.concatenate(gmats, axis=1))
        y_all = (_dot(jnp.concatenate(g_rows, axis=0), x2)
                 + _dot(cmb, s_old) * jnp.exp2(jnp.concatenate(cum_g, axis=0)))
        for t, rs in enumerate(tiles):
            yz = (y_all[rs, :] + dexp[:, gs] * xs_t[t]) * _silu(pv[rs, lanes(OFF_BZ, g)])
            ssq_parts.append(yz * yz)
            ymix[rs, lanes(RG_WIDTH, g)] = yz
        s_t[g] = jnp.exp2(cum_last) * s_t[g] + _dot(bmt, jnp.concatenate(wx, axis=0))

    lb_all = _lower_bound_row(hlb[...], layer)

    def forget_gate(g):
        lb = lb_all[:, g * LANES:(g + 1) * LANES]
        return lb + (1.0 - lb) * _logistic(pv[:, OFF_CF + g * LANES:OFF_CF + (g + 1) * LANES])

    def mixer_c(g, gt, xall):
        gs = lanes(0, g)
        b_off = N_MXU_LEVELS * TC
        bcum = xall[b_off:b_off + TC, :]
        blast = bcum[TC - 1:TC, :]

        def level_exponent(lev):
            if lev <= N_MXU_LEVELS:
                return xall[(lev - 1) * TC:lev * TC, :]
            blk = 1 << lev
            refs = [jnp.broadcast_to(bcum[j * blk + blk // 2 - 1:j * blk + blk // 2, :], (blk, LANES))
                    for j in range(TC // blk)]
            return -jnp.abs(bcum - jnp.concatenate(refs, axis=0))
        q = _silu(pv[:, lanes(OFF_CQ, g)])
        kk = 1.0 - gt
        vv = pv[:, lanes(OFF_CI, g)]
        zeros = jnp.zeros((HG_HEAD_DIM, TC), BF16)

        def keys_by_head(kt):
            ktt = _bf(kt.T)
            return jnp.concatenate(
                [jnp.concatenate([ktt[:HG_HEAD_DIM, :], zeros], axis=1),
                 jnp.concatenate([zeros, ktt[HG_HEAD_DIM:, :]], axis=1)], axis=0)

        att = _dot(_bf(q), keys_by_head(kk)) * cmask[0]
        for lev in range(1, N_LEVELS + 1):
            e = jnp.exp2(level_exponent(lev))
            att = att + _dot(_bf(q * e), keys_by_head(kk * e)) * cmask[lev]
        v2 = jnp.concatenate([_bf(jnp.where(first_head_tc, vv, 0.0)),
                              _bf(jnp.where(first_head_tc, 0.0, vv))], axis=0)
        o = _dot(_bf(att), v2) + _dot_nt(_bf(q * jnp.exp2(bcum)), _bf(st_hg[g]))
        khat = kk * jnp.exp2(blast - bcum)
        st_hg[g] = jnp.exp2(blast) * st_hg[g] + pairmask[...] * _dot(_bf(vv.T), _bf(khat))
        ms = _dot(_bf(o * o), pairmean[...])
        ymix[:, lanes(RG_WIDTH + SSD_WIDTH, g)] = (
            o * lax.rsqrt(ms + EPS) * hnorm[:, gs] * _silu(pv[:, lanes(OFF_CG, g)]))

    for pair in range(NGRP // 2):
        g0, g1 = 2 * pair, 2 * pair + 1
        gts = [forget_gate(g0), forget_gate(g1)]
        lf = jnp.log(jnp.concatenate(gts, axis=1)) * LOG2E
        hi = _bf(lf)
        lo = _bf(lf - hi.astype(F32))
        xall2 = _dot(emat2[...], jnp.concatenate([hi, lo], axis=0))
        for g, gt in zip((g0, g1), gts):
            side_work[g]()
            mixer_a(g)
            mixer_b(g)
            mixer_c(g, gt, xall2[:, (g - g0) * LANES:(g - g0 + 1) * LANES])

    for t, rs in enumerate(tiles):
        ssq = ssq_parts[t]
        for g in range(1, NGRP):
            ssq = ssq + ssq_parts[g * len(tiles) + t]
        rinv = lax.rsqrt(jnp.sum(ssq, axis=-1, keepdims=True) * (1.0 / SSD_WIDTH) + EPS)
        for g in range(NGRP):
            ys = lanes(RG_WIDTH, g)
            ymix[rs, ys] = ymix[rs, ys] * rinv * snorm[:, lanes(0, g)]


N_MIX_PARAMS = 18
MXU_N = 256
PROJ_BOUNDS = (0, 5 * MXU_N, 9 * MXU_N, 13 * MXU_N, NPROJ)
assert len(PROJ_BOUNDS) == NGRP + 1
CHUNKS_PER_STEP = 2


N_POST_PARAMS = 8


def _prompt_layer_kernel(layer, final, pos0, nsteps, n_valid,
                         x_ref, xn_ref, xp_ref, p_ref, gmix, w_in_ref, *rest):
    params = rest[:N_MIX_PARAMS]
    k = N_MIX_PARAMS + N_POST_PARAMS
    wo, nffn, wup, wdown, nple, wgate, wproj, nfin = rest[N_MIX_PARAMS:k]
    h_out, rgc_o, rgh_o, sc_o, ssd_o, hg_o = rest[k:k + 6]
    proj_a, proj_b, ymix_s, xea, xeb, hcar, s_t, st_hg = rest[k + 6:]
    states = (xea, xeb, hcar, s_t, st_hg)
    s = pl.program_id(0)
    live = s < n_valid
    c = jnp.minimum(s, n_valid - 1) % nsteps

    @pl.when(s == 0)
    def _prologue():
        proj_a[...] = _dot(_bf(_rms(x_ref[0, 0:TC, :], gmix[...])), w_in_ref[...])
        ymix_s[...] = jnp.zeros_like(ymix_s)

    @pl.when(jnp.logical_and(c == 0, live))
    def _init():
        for ref in states:
            ref[...] = jnp.zeros_like(ref)

    ymix_prev = _bf(ymix_s[...])
    n_ff = D_FF // FF_CHUNK
    carry = {}

    def post_piece(j):
        if j == 0:
            carry["h"] = xp_ref[0] + _dot(ymix_prev, wo[...])
            carry["hn"] = _bf(_rms(carry["h"], nffn[...]))
        elif j <= n_ff:
            cs = slice((j - 1) * FF_CHUNK, j * FF_CHUNK)
            z = jnp.square(jnp.maximum(_dot(carry["hn"], wup[:, cs]), 0.0))
            carry["h"] = carry["h"] + _dot(_bf(z), wdown[cs, :])
        elif j == n_ff + 1:
            h = carry["h"]
            gate = _sigmoid(_dot(_bf(_rms(h, nple[...])), wgate[...]))
            h = h + gate * _dot(_bf(p_ref[0]), wproj[...])
            h_out[0] = _rms(h, nfin[...]) if final else h

    def side_work(x, dst, first_piece):
        u = _bf(_rms(x, gmix[...]))

        def slot(j):
            cs = slice(PROJ_BOUNDS[j], PROJ_BOUNDS[j + 1])
            dst[:, cs] = _dot(u, w_in_ref[:, cs])
            post_piece(first_piece + j)

        return [functools.partial(slot, j) for j in range(NGRP)]

    @pl.when(live)
    def _mix_and_side_work():
        row0 = c * (CHUNKS_PER_STEP * TC) + pos0
        _mixer_chunk(layer, row0, proj_a, ymix_s.at[0:TC],
                     side_work(x_ref[0, TC:2 * TC, :], proj_b, 0), *params, *states)
        _mixer_chunk(layer, row0 + TC, proj_b, ymix_s.at[TC:2 * TC],
                     side_work(xn_ref[0], proj_a, NGRP), *params, *states)

    @pl.when(jnp.logical_not(live))
    def _drain():
        for j in range(n_ff + 2):
            post_piece(j)

    @pl.when(jnp.logical_and(c == nsteps - 1, live))
    def _emit():
        rgc_o[0] = xea[SUBLANES - 3:SUBLANES, :]
        sc_o[0] = xeb[SUBLANES - 3:SUBLANES, :]
        rgh_o[0] = hcar[0:1, :]
        for g in range(NGRP):
            ssd_o[0, g * LANES:(g + 1) * LANES, :] = s_t[g].T
            s_pair = st_hg[g].T
            for hh in range(2):
                lo = hh * HG_HEAD_DIM
                hg_o[0, 2 * g + hh] = s_pair[lo:lo + HG_HEAD_DIM, lo:lo + HG_HEAD_DIM]


def _prompt_layer(layer, final, x3, p_all4, gmix, w_in_bf, mp, lp, nfin, consts):
    bsz, seq, _ = x3.shape
    step_rows = CHUNKS_PER_STEP * TC
    nsteps = seq // step_rows
    n_valid = bsz * nsteps
    chunks_per_row = seq // TC
    n_chunks = bsz * chunks_per_row
    params = [mp["caw"], mp["cab"], mp["wgp"], mp["bgp"], mp["lam"], mp["cbw"], mp["cbb"],
              mp["dtb"], mp["alog"], mp["dexp"], mp["snorm"], mp["hlb"], mp["hnorm"],
              consts["ltri2"], consts["emat2"],
              consts["cmask"], consts["pairmask"], consts["pairmean"]]
    post_params = [lp["wo"], lp["nffn"], lp["wup"], lp["wdown"], lp["nple"], lp["wgate"],
                   lp["wproj"], nfin]
    assert len(params) == N_MIX_PARAMS and len(post_params) == N_POST_PARAMS

    def mix_step(s):
        return jnp.minimum(s, n_valid - 1)

    def cur(s):
        return mix_step(s) // nsteps, mix_step(s) % nsteps, 0

    def nxt(s):
        flat = jnp.minimum((mix_step(s) + 1) * CHUNKS_PER_STEP, n_chunks - 1)
        return flat // chunks_per_row, flat % chunks_per_row, 0

    def prv(s):
        sp = jnp.maximum(s - 1, 0)
        return sp // nsteps, sp % nsteps, 0

    def state_idx(nd):
        return lambda s: (mix_step(s) // nsteps,) + (0,) * (nd - 1)

    out_shape = (
        jax.ShapeDtypeStruct((bsz, seq, D_MODEL), F32),
        jax.ShapeDtypeStruct((bsz, CONV_W - 1, RG_WIDTH), F32),
        jax.ShapeDtypeStruct((bsz, 1, RG_WIDTH), F32),
        jax.ShapeDtypeStruct((bsz, CONV_W - 1, SSD_CONV_DIM), F32),
        jax.ShapeDtypeStruct((bsz, SSD_WIDTH, SSD_STATE), F32),
        jax.ShapeDtypeStruct((bsz, HG_HEADS, HG_HEAD_DIM, HG_HEAD_DIM), F32),
    )
    out_specs = (
        pl.BlockSpec((1, step_rows, D_MODEL), prv),
        pl.BlockSpec((1, CONV_W - 1, RG_WIDTH), state_idx(3)),
        pl.BlockSpec((1, 1, RG_WIDTH), state_idx(3)),
        pl.BlockSpec((1, CONV_W - 1, SSD_CONV_DIM), state_idx(3)),
        pl.BlockSpec((1, SSD_WIDTH, SSD_STATE), state_idx(3)),
        pl.BlockSpec((1, HG_HEADS, HG_HEAD_DIM, HG_HEAD_DIM), state_idx(4)),
    )
    scratch = [
        pltpu.VMEM((TC, NPROJ), F32),
        pltpu.VMEM((TC, NPROJ), F32),
        pltpu.VMEM((step_rows, D_MIX), F32),
        pltpu.VMEM((SUBLANES, RG_WIDTH), F32),
        pltpu.VMEM((SUBLANES, SSD_CONV_DIM), F32),
        pltpu.VMEM((SUBLANES, RG_WIDTH), F32),
        pltpu.VMEM((NGRP, SSD_STATE, LANES), F32),
        pltpu.VMEM((NGRP, LANES, LANES), F32),
    ]
    return pl.pallas_call(
        functools.partial(_prompt_layer_kernel, layer, final, 0, nsteps, n_valid),
        grid=(n_valid + 1,),
        in_specs=[pl.BlockSpec((1, step_rows, D_MODEL), cur),
                  pl.BlockSpec((1, TC, D_MODEL), nxt),
                  pl.BlockSpec((1, step_rows, D_MODEL), prv),
                  pl.BlockSpec((None, 1, step_rows, PLE_DIM), lambda s: (layer,) + prv(s)),
                  _const_spec(gmix), _const_spec(w_in_bf)]
        + [_const_spec(p) for p in params + post_params],
        out_specs=out_specs,
        out_shape=out_shape,
        scratch_shapes=scratch,
        compiler_params=pltpu.CompilerParams(
            dimension_semantics=("arbitrary",), vmem_limit_bytes=VMEM_LIMIT),
        name="prompt_layer",
    )(x3, x3, x3, p_all4, gmix, w_in_bf, *params, *post_params)


def _hg_step_kernel(layer, n_acc, proj_ref, hlb, st_ref, *rest):
    ot_ref, st_o, qh_s, f_s, k_s, v_s = rest[n_acc:]
    h = pl.program_id(0)
    if n_acc == 0:
        for other in range(DEPTH):
            if other != layer:
                st_o[other] = jnp.zeros(st_o.shape[1:], F32)
        st_o = st_o.at[layer]

    @pl.when(h == 0)
    def _prep():
        q = _silu(proj_ref[:, OFF_CQ:OFF_CQ + HG_WIDTH])
        lb = _lower_bound_row(hlb[...], layer)
        gt = lb + (1.0 - lb) * _logistic(proj_ref[:, OFF_CF:OFF_CF + HG_WIDTH])
        fdec = jnp.exp(jnp.log(gt))
        qh_s[...] = (q * fdec).T
        f_s[...] = fdec.T
        k_s[...] = (1.0 - gt).T
        v_s[...] = proj_ref[:, OFF_CI:OFF_CI + HG_WIDTH].T

    base = pl.multiple_of(h * HG_HEAD_DIM, HG_HEAD_DIM)
    v_h = v_s[pl.ds(base, HG_HEAD_DIM), :]

    def body(dk, acc):
        s = st_ref[dk]
        st_o[dk] = f_s[pl.ds(base + dk, 1), :] * s + k_s[pl.ds(base + dk, 1), :] * v_h
        return acc + qh_s[pl.ds(base + dk, 1), :] * s

    ot_ref[...] = lax.fori_loop(0, HG_HEAD_DIM, body,
                                jnp.zeros((HG_HEAD_DIM, LANES), F32), unroll=8)


def _hg_step(layer, proj, hlb, st_all, acc):
    nb = proj.shape[0]
    blk = (HG_HEAD_DIM, HG_HEAD_DIM, nb)
    st_spec = pl.BlockSpec((None, None) + blk, lambda h: (layer, h, 0, 0, 0))
    n_acc = len(acc)
    st_out_spec = st_spec if n_acc else pl.BlockSpec(
        (DEPTH, None) + blk, lambda h: (0, h, 0, 0, 0))
    return pl.pallas_call(
        functools.partial(_hg_step_kernel, layer, n_acc),
        grid=(HG_HEADS,),
        in_specs=[_const_spec(proj), _const_spec(hlb), st_spec]
        + [pl.BlockSpec(memory_space=pl.ANY)] * n_acc,
        out_specs=(pl.BlockSpec((HG_HEAD_DIM, nb), lambda h: (h, 0)), st_out_spec),
        out_shape=(jax.ShapeDtypeStruct((HG_WIDTH, nb), F32),
                   jax.ShapeDtypeStruct(st_all.shape, F32)),
        scratch_shapes=[pltpu.VMEM((HG_WIDTH, nb), F32)] * 4,
        input_output_aliases={3 + k: 1 + k for k in range(n_acc)},
        compiler_params=pltpu.CompilerParams(
            dimension_semantics=("arbitrary",), vmem_limit_bytes=VMEM_LIMIT),
        name="hg_step",
    )(proj, hlb, st_all, *acc)


def _pad_rows_t(x):
    pad = jnp.zeros((LANES - x.shape[0], x.shape[1]), F32)
    return jnp.concatenate([x, pad], axis=0).T


def _sample_mixer_kernel(layer, pos0, n_acc, proj_ref, rgc_ref, rgh_ref, sc_ref, ssd_ref, ohg_ref, *rest):
    (caw, cab, wg, bg, lam, cbw, cbb, dtb, alog, dexp, snorm,
     hlb, hnorm, expand, headsum, headmean,
     ymix_ref, rgc_o, rgh_o, sc_o, ssd_o) = rest[n_acc:]
    bb = DEC_BB
    if n_acc == 0:
        for other in range(DEPTH):
            if other != layer:
                ssd_o[other] = jnp.zeros(ssd_o.shape[1:], F32)
        ssd_o = ssd_o.at[layer]
    ax = proj_ref[:, OFF_AX:OFF_AX + RG_WIDTH]
    xa = cab[...] + caw[CONV_W - 1:CONV_W, :] * ax
    for k in range(CONV_W - 1):
        xa = xa + caw[k:k + 1, :] * rgc_ref[:, k * RG_WIDTH:(k + 1) * RG_WIDTH]
    rgc_o[:, 0:2 * RG_WIDTH] = rgc_ref[:, RG_WIDTH:3 * RG_WIDTH]
    rgc_o[:, 2 * RG_WIDTH:] = ax
    a, mult, gi = _rg_coeffs(_dot(_bf(xa), wg[...]) + bg[...], lam[...])
    if pos0 == 0:
        mult = jnp.ones_like(mult)
    h = a * rgh_ref[...] + mult * (gi * xa)
    rgh_o[...] = h
    ymix_ref[:, 0:RG_WIDTH] = h * _gelu_tanh(proj_ref[:, OFF_AG:OFF_AG + RG_WIDTH])

    bx = proj_ref[:, OFF_XBC:OFF_XBC + SSD_CONV_DIM]
    xbc = cbb[...] + cbw[CONV_W - 1:CONV_W, :] * bx
    for k in range(CONV_W - 1):
        xbc = xbc + cbw[k:k + 1, :] * sc_ref[:, k * SSD_CONV_DIM:(k + 1) * SSD_CONV_DIM]
    sc_o[:, 0:2 * SSD_CONV_DIM] = sc_ref[:, SSD_CONV_DIM:3 * SSD_CONV_DIM]
    sc_o[:, 2 * SSD_CONV_DIM:] = bx
    xbc = _silu(xbc)
    xs = xbc[:, :SSD_WIDTH]
    bm = xbc[:, SSD_WIDTH:SSD_WIDTH + SSD_STATE]
    cm = xbc[:, SSD_WIDTH + SSD_STATE:]
    dtp = _softplus(proj_ref[:, OFF_DT:OFF_DT + LANES] + dtb[...])
    v_da = dtp * (-jnp.exp(alog[...]))
    dt_e = _dot_x_c(dtp, expand[...])
    e_e = jnp.exp(_dot_x_c(v_da, expand[...]))
    dtx = dt_e * xs
    cbs = _dot(_bf(cm * bm), jnp.ones((SSD_STATE, LANES), BF16))[:, 0:1]
    dtx_t = _pad_rows_t(dtx)
    e_t = _pad_rows_t(e_e)
    c_pad = _bf(jnp.concatenate([cm, jnp.zeros((LANES - bb, SSD_STATE), F32)], axis=0))
    lane_w = lax.broadcasted_iota(jnp.int32, (SSD_WIDTH, LANES), 1)
    y_t = jnp.zeros((SSD_WIDTH, LANES), F32)
    for j in range(bb):
        s_old = ssd_ref[j].reshape(SSD_WIDTH, SSD_STATE)
        ssd_o[j] = (e_t[:, j:j + 1] * s_old + dtx_t[:, j:j + 1] * bm[j:j + 1, :]).reshape(
            SSD_HEADS, SSD_HEAD_DIM, SSD_STATE)
        y_t = y_t + jnp.where(lane_w == j, _dot_nt(_bf(s_old), c_pad), 0.0)
    y = cbs * dtx + y_t.T[0:bb, :] * e_e
    yb = y + dexp[...] * xs
    ymix_ref[:, RG_WIDTH:RG_WIDTH + SSD_WIDTH] = _rms(
        yb * _silu(proj_ref[:, OFF_BZ:OFF_BZ + SSD_WIDTH]), snorm[...])

    q = _silu(proj_ref[:, OFF_CQ:OFF_CQ + HG_WIDTH])
    lb = _lower_bound_row(hlb[...], layer)
    gt = lb + (1.0 - lb) * _logistic(proj_ref[:, OFF_CF:OFF_CF + HG_WIDTH])
    kk = 1.0 - gt
    vv = proj_ref[:, OFF_CI:OFF_CI + HG_WIDTH]
    att = _dot(_bf(q * kk), headsum[...])
    o = att * vv + ohg_ref[...]
    ms = _dot(_bf(o * o), headmean[...])
    ymix_ref[:, RG_WIDTH + SSD_WIDTH:] = (
        o * lax.rsqrt(ms + EPS) * hnorm[...] * _silu(proj_ref[:, OFF_CG:OFF_CG + HG_WIDTH]))


def _sample_mixer(layer, proj, rgc, rgh, sc, ssd_all, o_hg, acc, mp, consts):
    nb = proj.shape[0]
    bb = DEC_BB
    params = [mp["caw"], mp["cab"], mp["wg"], mp["bg"], mp["lam"], mp["cbw"], mp["cbb"],
              mp["dtb"], mp["alog"], mp["dexp"], mp["snorm"], mp["hlb"], mp["hnorm"],
              consts["expand"], consts["headsum"], consts["headmean"]]
    row2 = lambda w: pl.BlockSpec((bb, w), lambda i: (i, 0))
    small_specs = [row2(3 * RG_WIDTH), row2(RG_WIDTH), row2(3 * SSD_CONV_DIM)]
    ssd_spec = pl.BlockSpec((None, bb, SSD_HEADS, SSD_HEAD_DIM, SSD_STATE),
                            lambda i: (layer, i, 0, 0, 0))
    out_shape = (
        jax.ShapeDtypeStruct((nb, D_MIX), F32),
        jax.ShapeDtypeStruct((nb, 3 * RG_WIDTH), F32),
        jax.ShapeDtypeStruct((nb, RG_WIDTH), F32),
        jax.ShapeDtypeStruct((nb, 3 * SSD_CONV_DIM), F32),
        jax.ShapeDtypeStruct(ssd_all.shape, F32),
    )
    n_acc = len(acc)
    n_lead = 6
    ssd_out_spec = ssd_spec if n_acc else pl.BlockSpec(
        (DEPTH, bb, SSD_HEADS, SSD_HEAD_DIM, SSD_STATE), lambda i: (0, i, 0, 0, 0))
    return pl.pallas_call(
        functools.partial(_sample_mixer_kernel, layer, PAST_LEN, n_acc),
        grid=(nb // bb,),
        in_specs=[row2(NPROJ)] + small_specs + [ssd_spec, row2(HG_WIDTH)]
        + [pl.BlockSpec(memory_space=pl.ANY)] * n_acc + [_const_spec(p) for p in params],
        out_specs=tuple([row2(D_MIX)] + small_specs + [ssd_out_spec]),
        out_shape=out_shape,
        input_output_aliases={n_lead + k: 4 + k for k in range(n_acc)},
        compiler_params=pltpu.CompilerParams(
            dimension_semantics=("parallel",), vmem_limit_bytes=VMEM_LIMIT),
        name="sample_mixer",
    )(proj, rgc, rgh, sc, ssd_all, o_hg, *acc, *params)


def _post_kernel(final, h_ref, y_ref, p_ref, wo, nffn, wup, wdown, nple, wgate, wproj, nfin, o_ref):
    h = h_ref[...] + _dot(_bf(y_ref[...]), wo[...])
    hn = _bf(_rms(h, nffn[...]))
    for j in range(D_FF // FF_CHUNK):
        cs = slice(j * FF_CHUNK, (j + 1) * FF_CHUNK)
        z = jnp.square(jnp.maximum(_dot(hn, wup[:, cs]), 0.0))
        h = h + _dot(_bf(z), wdown[cs, :])
    gate = _sigmoid(_dot(_bf(_rms(h, nple[...])), wgate[...]))
    h = h + gate * _dot(_bf(p_ref[...]), wproj[...])
    if final:
        h = _rms(h, nfin[...])
    o_ref[...] = h


def _post(final, layer, h2d, ymix2d, p_all, lp, nfin, tm):
    m = h2d.shape[0]
    params = [lp["wo"], lp["nffn"], lp["wup"], lp["wdown"], lp["nple"], lp["wgate"], lp["wproj"], nfin]
    return pl.pallas_call(
        functools.partial(_post_kernel, final),
        grid=(m // tm,),
        in_specs=[pl.BlockSpec((tm, D_MODEL), lambda i: (i, 0)),
                  pl.BlockSpec((tm, D_MIX), lambda i: (i, 0)),
                  pl.BlockSpec((None, tm, PLE_DIM), lambda i: (layer, i, 0))]
        + [_const_spec(p) for p in params],
        out_specs=pl.BlockSpec((tm, D_MODEL), lambda i: (i, 0)),
        out_shape=jax.ShapeDtypeStruct((m, D_MODEL), F32),
        compiler_params=pltpu.CompilerParams(
            dimension_semantics=("parallel",), vmem_limit_bytes=VMEM_LIMIT),
        name="post_mixer",
    )(h2d, ymix2d, p_all, *params)


def _constants():
    t = np.arange(TC)
    ltri = (t[None, :] <= t[:, None]).astype(np.float32)
    expand = np.zeros((LANES, SSD_WIDTH), np.float32)
    for hd in range(SSD_HEADS):
        expand[hd, hd * SSD_HEAD_DIM:(hd + 1) * SSD_HEAD_DIM] = 1.0
    emat = np.zeros(((N_MXU_LEVELS + 1) * TC, TC), np.float32)
    for lev in range(1, N_MXU_LEVELS + 1):
        blk = 1 << lev
        mid = (t // blk) * blk + blk // 2
        upper = t >= mid
        r0 = (lev - 1) * TC
        emat[r0:r0 + TC] = (
            (upper[:, None] & (t[None, :] >= mid[:, None]) & (t[None, :] <= t[:, None]))
            | ((~upper)[:, None] & (t[None, :] > t[:, None]) & (t[None, :] < mid[:, None])))
    emat[N_MXU_LEVELS * TC:] = ltri
    cmask = np.zeros((N_LEVELS + 1, TC, 2 * TC), np.float32)
    cmask[0] = np.tile(np.eye(TC, dtype=np.float32), (1, 2))
    for lev in range(1, N_LEVELS + 1):
        blk = 1 << lev
        up = (t % blk) >= blk // 2
        m = up[:, None] & (~up)[None, :] & ((t // blk)[:, None] == (t // blk)[None, :])
        cmask[lev] = np.tile(m.astype(np.float32), (1, 2))
    hidx = np.arange(HG_WIDTH) // HG_HEAD_DIM
    bd = (hidx[:, None] == hidx[None, :]).astype(np.float32)
    pair = bd[:LANES, :LANES]
    return {
        "ltri2": jnp.asarray(np.tile(ltri, (1, 2)), BF16),
        "expand": jnp.asarray(expand, BF16),
        "emat2": jnp.asarray(np.tile(emat, (1, 2)), BF16),
        "cmask": jnp.asarray(cmask, F32),
        "pairmask": jnp.asarray(pair, F32),
        "pairmean": jnp.asarray(pair / HG_HEAD_DIM, BF16),
        "headsum": jnp.asarray(bd, BF16), "headmean": jnp.asarray(bd / HG_HEAD_DIM, BF16),
    }


def _block_diag(w):
    hh, d, _ = w.shape
    eye = jnp.eye(hh, dtype=w.dtype)
    return (eye[:, None, :, None] * w[:, :, None, :]).reshape(hh * d, hh * d)


def _pair_gate_weights(wa, wx):
    hh, d, _ = wa.shape
    pa = jnp.stack([_block_diag(wa[2 * g:2 * g + 2]) for g in range(hh // 2)])
    px = jnp.stack([_block_diag(wx[2 * g:2 * g + 2]) for g in range(hh // 2)])
    return jnp.concatenate([pa, px], axis=2)


def _pad_lanes(v):
    return jnp.pad(v, (0, LANES - v.shape[0]))[None, :]


def kernel(x_prompt, x_sample, state_rg_conv, state_rg_h, state_ssd_conv, state_ssd, state_hgrn,
           p_prompt, p_sample, norm_mix, w_in, conv_a_w, conv_a_b, rg_wa, rg_ba, rg_wx, rg_bx,
           rg_lambda, conv_b_w, conv_b_b, ssd_dt_bias, ssd_a_log, ssd_d, ssd_norm,
           hg_lower_bounds, hg_norm, w_out, norm_ffn, w_up, w_down, norm_ple, w_ple_gate,
           w_ple_proj, norm_final):
    bsz, seq, _ = x_prompt.shape
    nb = x_sample.shape[0]
    consts = _constants()
    w_in_t = jnp.swapaxes(w_in, 1, 2)
    nfin = norm_final[None, :]
    ps_all = p_sample.reshape(DEPTH, nb, PLE_DIM)

    hp = x_prompt
    hs = x_sample.reshape(nb, D_MODEL)
    st_p = [[] for _ in range(5)]
    st_s = [[] for _ in range(3)]
    hg_t_all = jnp.transpose(state_hgrn, (0, 2, 3, 4, 1))
    acc_ssd = ()
    acc_hg = ()
    for i in range(DEPTH):
        w_re = _prep_w_in(w_in_t, i)
        gmix = norm_mix[i][None, :]
        mp = {
            "caw": conv_a_w[i], "cab": conv_a_b[i][None, :],
            "wg": jnp.concatenate([_block_diag(rg_wa[i]), _block_diag(rg_wx[i])], axis=1).astype(BF16),
            "bg": jnp.concatenate([rg_ba[i], rg_bx[i]])[None, :],
            "wgp": _pair_gate_weights(rg_wa[i], rg_wx[i]).astype(BF16),
            "bgp": jnp.concatenate([rg_ba[i].reshape(NGRP, 1, LANES),
                                    rg_bx[i].reshape(NGRP, 1, LANES)], axis=2),
            "lam": rg_lambda[i][None, :],
            "cbw": conv_b_w[i], "cbb": conv_b_b[i][None, :],
            "dtb": _pad_lanes(ssd_dt_bias[i]), "alog": _pad_lanes(ssd_a_log[i]),
            "dexp": jnp.repeat(ssd_d[i], SSD_HEAD_DIM)[None, :],
            "snorm": ssd_norm[i][None, :],
            "hlb": hg_lower_bounds, "hnorm": jnp.tile(hg_norm[i], HG_HEADS)[None, :],
        }
        lp = {
            "wo": _cast_bf16(w_out, i, 512), "nffn": norm_ffn[i][None, :],
            "wup": _cast_bf16(w_up, i, 256), "wdown": _cast_bf16(w_down, i, 1024),
            "nple": norm_ple[i][None, :], "wgate": _cast_bf16(w_ple_gate, i, 512),
            "wproj": _cast_bf16(w_ple_proj, i, 256),
        }
        final = i == DEPTH - 1

        hp, rgc, rgh, sc, ssd, hg = _prompt_layer(
            i, final, hp, p_prompt, gmix, w_re, mp, lp, nfin, consts)
        for lst, s in zip(st_p, (rgc, rgh.reshape(bsz, RG_WIDTH), sc,
                                 ssd.reshape(bsz, SSD_HEADS, SSD_HEAD_DIM, SSD_STATE), hg)):
            lst.append(s)

        proj_s = _in_proj(hs, gmix, w_re, nb)
        o_hg_t, hg_acc = _hg_step(i, proj_s, hg_lower_bounds, hg_t_all, acc_hg)
        acc_hg = (hg_acc,)
        ymix_s, rgc, rgh, sc, ssd_acc = _sample_mixer(
            i, proj_s,
            state_rg_conv[i].reshape(nb, 3 * RG_WIDTH), state_rg_h[i],
            state_ssd_conv[i].reshape(nb, 3 * SSD_CONV_DIM),
            state_ssd, o_hg_t.T, acc_ssd, mp, consts)
        acc_ssd = (ssd_acc,)
        hs = _post(final, i, hs, ymix_s, ps_all, lp, nfin, nb)
        for lst, s in zip(st_s, (rgc.reshape(nb, CONV_W - 1, RG_WIDTH), rgh,
                                 sc.reshape(nb, CONV_W - 1, SSD_CONV_DIM))):
            lst.append(s)

    y_prompt = hp
    y_sample = hs.reshape(nb, 1, D_MODEL)
    outs_p = [jnp.stack(lst) for lst in st_p]
    outs_s = [jnp.stack(lst) for lst in st_s] + [
        acc_ssd[0], jnp.transpose(acc_hg[0], (0, 4, 1, 2, 3))]
    return (y_prompt, y_sample, *outs_p, *outs_s)
```

```python
import functools

import numpy as np
import jax
import jax.numpy as jnp
from jax import lax
from jax.experimental import pallas as pl
from jax.experimental.pallas import tpu as pltpu

F32 = jnp.float32
BF16 = jnp.bfloat16

D_MODEL = 1024
DEPTH = 2
PAST_LEN = 16384
PLE_DIM = 256
D_FF = 4 * D_MODEL
CONV_W = 4
EPS = 1e-6
RG_WIDTH = 512
RG_HEADS = 8
RG_HEAD_DIM = 64
RG_C = 8.0
SSD_WIDTH = 512
SSD_HEAD_DIM = 64
SSD_HEADS = 8
SSD_STATE = 128
SSD_CONV_DIM = SSD_WIDTH + 2 * SSD_STATE
HG_WIDTH = 512
HG_HEADS = 8
HG_HEAD_DIM = 64
D_MIX = RG_WIDTH + SSD_WIDTH + HG_WIDTH

LOG2E = 1.4426950408889634
LANES = 128
SUBLANES = 8
NGRP = RG_WIDTH // LANES

OFF_AX = 0
OFF_AG = 512
OFF_BZ = 1024
OFF_XBC = 1536
OFF_CQ = 2304
OFF_CF = 2816
OFF_CI = 3328
OFF_CG = 3840
OFF_DT = 4352
NPROJ = OFF_DT + LANES
DT_COL = 2 * RG_WIDTH + SSD_WIDTH + SSD_CONV_DIM
D_IN_PROJ = DT_COL + SSD_HEADS + 4 * HG_WIDTH

TC = 128
HALF = TC // 2
N_LEVELS = 7
N_MXU_LEVELS = 3
DEC_BB = 16
FF_CHUNK = 1024
VMEM_LIMIT = 58 * 1024 * 1024


def _bf(x):
    return x.astype(BF16)


def _dot(a, b):
    return jnp.dot(a, b, preferred_element_type=F32)


def _dot_nt(a, b):
    return lax.dot_general(a, b, (((1,), (1,)), ((), ())), preferred_element_type=F32)


def _split3(x):
    hi = _bf(x)
    r1 = x - hi.astype(F32)
    mid = _bf(r1)
    lo = _bf(r1 - mid.astype(F32))
    return hi, mid, lo


def _dot_x_c(x, c):
    hi, mid, lo = _split3(x)
    return _dot(hi, c) + _dot(mid, c) + _dot(lo, c)


def _split2(x):
    hi = _bf(x)
    return hi, _bf(x - hi.astype(F32))


def _logistic(x):
    return 1.0 / (1.0 + jnp.exp(-x))


def _sigmoid(x):
    return 0.5 * jnp.tanh(0.5 * x) + 0.5


def _silu(x):
    h = 0.5 * x
    return h * jnp.tanh(h) + h


def _softplus(x):
    return jnp.maximum(x, 0.0) + jnp.log1p(jnp.exp(-jnp.abs(x)))


def _gelu_tanh(x):
    c = np.float32(np.sqrt(2.0 / np.pi))
    return 0.5 * x * (1.0 + jnp.tanh(c * (x + 0.044715 * (x * x * x))))


def _rms(x, gain):
    return x * lax.rsqrt(jnp.mean(x * x, axis=-1, keepdims=True) + EPS) * gain


def _lower_bound_row(hlb, layer):
    m = jnp.max(hlb, axis=0, keepdims=True)
    e = jnp.exp(hlb - m)
    sm = e / jnp.sum(e, axis=0, keepdims=True)
    lb = jnp.zeros((1, HG_WIDTH), F32)
    for j in range(1, layer + 1):
        lb = lb + sm[j:j + 1, :]
    return lb


def _rg_coeffs(gates, lam_row):
    w = gates.shape[1] // 2
    r = _sigmoid(gates[:, :w])
    i = _sigmoid(gates[:, w:])
    log_a = -RG_C * r * _softplus(-lam_row)
    a = jnp.exp(log_a)
    mult = jnp.sqrt(-jnp.tanh(log_a) * (a * a + 1.0))
    return a, mult, i


def _const_spec(arr):
    nd = arr.ndim
    return pl.BlockSpec(arr.shape, lambda *_: (0,) * nd, pipeline_mode=pl.Buffered(1))


CAST_STEPS = 8


def _cast_kernel(*refs):
    n = len(refs) // 2
    for w_ref, o_ref in zip(refs[:n], refs[n:]):
        o_ref[...] = _bf(w_ref[...])


def _cast_layer_weights(layer, *w_alls):
    in_specs, out_specs, out_shape = [], [], []
    for w_all in w_alls:
        _, rows, cols = w_all.shape
        rb = rows // CAST_STEPS
        assert rb * CAST_STEPS == rows and rb % (2 * SUBLANES) == 0
        in_specs.append(pl.BlockSpec((None, rb, cols), lambda i: (layer, i, 0)))
        out_specs.append(pl.BlockSpec((rb, cols), lambda i: (i, 0)))
        out_shape.append(jax.ShapeDtypeStruct((rows, cols), BF16))
    return pl.pallas_call(
        _cast_kernel,
        grid=(CAST_STEPS,),
        in_specs=in_specs,
        out_specs=tuple(out_specs),
        out_shape=tuple(out_shape),
        compiler_params=pltpu.CompilerParams(
            dimension_semantics=("parallel",), vmem_limit_bytes=VMEM_LIMIT),
        name="cast_bf16",
    )(*w_alls)


def _win_kernel(wt_ref, o_ref):
    o_ref[:, 0:OFF_CQ] = _bf(wt_ref[0:DT_COL, :].T)
    o_ref[:, OFF_CQ:OFF_DT] = _bf(wt_ref[DT_COL + SSD_HEADS:D_IN_PROJ, :].T)
    dt = jnp.concatenate([wt_ref[DT_COL:DT_COL + SSD_HEADS, :],
                          jnp.zeros((LANES - SSD_HEADS, LANES), F32)], axis=0)
    o_ref[:, OFF_DT:NPROJ] = _bf(dt.T)


def _prep_w_in(wt_all, layer):
    return pl.pallas_call(
        _win_kernel,
        grid=(D_MODEL // LANES,),
        in_specs=[pl.BlockSpec((None, D_IN_PROJ, LANES), lambda i: (layer, 0, i))],
        out_specs=pl.BlockSpec((LANES, NPROJ), lambda i: (i, 0)),
        out_shape=jax.ShapeDtypeStruct((D_MODEL, NPROJ), BF16),
        compiler_params=pltpu.CompilerParams(dimension_semantics=("parallel",)),
        name="prep_w_in",
    )(wt_all)


def _proj_kernel(x_ref, g_ref, w_ref, o_ref):
    u = _rms(x_ref[...], g_ref[...])
    o_ref[...] = _dot(_bf(u), w_ref[...])


def _in_proj(x2d, gain, w_bf, tm):
    m = x2d.shape[0]
    return pl.pallas_call(
        _proj_kernel,
        grid=(m // tm,),
        in_specs=[pl.BlockSpec((tm, D_MODEL), lambda i: (i, 0)),
                  _const_spec(gain), _const_spec(w_bf)],
        out_specs=pl.BlockSpec((tm, NPROJ), lambda i: (i, 0)),
        out_shape=jax.ShapeDtypeStruct((m, NPROJ), F32),
        compiler_params=pltpu.CompilerParams(
            dimension_semantics=("parallel",), vmem_limit_bytes=VMEM_LIMIT),
        name="in_proj",
    )(x2d, gain, w_bf)


def _mixer_chunk(layer, row0, pv, ymix, side_work,
                 caw, cab, wgp, bgp, lam, cbw, cbb, dtb, alog, dexp, snorm,
                 hlb, hnorm, ltri2, emat2, cmask, pairmask, pairmean,
                 xea, xeb, hcar, s_t, st_hg):
    row_h = lax.broadcasted_iota(jnp.int32, (HALF, 1), 0)
    sub_h = row_h & (SUBLANES - 1)
    first_head = lax.broadcasted_iota(jnp.int32, (HALF, LANES), 1) < HG_HEAD_DIM
    first_head_tc = lax.broadcasted_iota(jnp.int32, (TC, LANES), 1) < HG_HEAD_DIM
    col_tc = lax.broadcasted_iota(jnp.int32, (HALF, TC), 1)
    row_tc = lax.broadcasted_iota(jnp.int32, (HALF, TC), 0)

    tiles = [slice(t * HALF, (t + 1) * HALF) for t in range(TC // HALF)]

    def lanes(off, g):
        return slice(off + g * LANES, off + (g + 1) * LANES)

    def conv_rows(prev8, x, w_ref, b_ref, gs):
        xcat = jnp.concatenate([prev8, x], axis=0)
        y = b_ref[:, gs] + w_ref[CONV_W - 1:CONV_W, gs] * x
        for j in range(1, CONV_W):
            y = y + w_ref[CONV_W - 1 - j:CONV_W - j, gs] * pltpu.roll(xcat, j, 0)[SUBLANES:, :]
        return y, x[x.shape[0] - SUBLANES:, :]

    def conv(tail_ref, w_ref, b_ref, src_off, g):
        gs = lanes(0, g)
        y, tail = conv_rows(tail_ref[:, gs], pv[:, lanes(src_off, g)], w_ref, b_ref, gs)
        tail_ref[:, gs] = tail
        return y

    def mixer_a(g):
        gs = lanes(0, g)
        prev8 = xea[:, gs]
        carry = hcar[0:1, gs]
        xas = []
        for rs in tiles:
            xa, prev8 = conv_rows(prev8, pv[rs, lanes(OFF_AX, g)], caw, cab, gs)
            xas.append(xa)
        gates = _dot(_bf(jnp.concatenate(xas, axis=0)), wgp[g]) + bgp[g]
        for t, rs in enumerate(tiles):
            xa = xas[t]
            a, mult, gi = _rg_coeffs(gates[rs, :], lam[:, gs])
            mult = jnp.where(row_h + (row0 + t * HALF) == 0, 1.0, mult)
            b = mult * (gi * xa)
            k = 1
            while k < SUBLANES:
                keep = sub_h >= k
                a_sh = jnp.where(keep, pltpu.roll(a, k, 0), 1.0)
                b_sh = jnp.where(keep, pltpu.roll(b, k, 0), 0.0)
                b = b + a * b_sh
                a = a * a_sh
                k *= 2
            slabs = []
            for r in range(HALF // SUBLANES):
                sl = slice(r * SUBLANES, (r + 1) * SUBLANES)
                h_r = a[sl, :] * carry + b[sl, :]
                slabs.append(h_r)
                carry = h_r[SUBLANES - 1:SUBLANES, :]
            ymix[rs, gs] = jnp.concatenate(slabs, axis=0) * _gelu_tanh(
                pv[rs, lanes(OFF_AG, g)])
        xea[:, gs] = prev8
        hcar[0:1, gs] = carry

    bm = _silu(conv(xeb, cbw, cbb, OFF_XBC, NGRP))
    cm = _silu(conv(xeb, cbw, cbb, OFF_XBC, NGRP + 1))
    dtp = _softplus(pv[:, OFF_DT:OFF_DT + LANES] + dtb[...])
    v_da = dtp * (-LOG2E * jnp.exp(alog[...]))
    cum = _dot(ltri2[...], jnp.concatenate(_split2(v_da), axis=0))
    cum_t = cum.T

    def per_head(arr, g, rs):
        rows = arr[rs, :]
        return jnp.where(first_head[0:rows.shape[0], :], rows[:, 2 * g:2 * g + 1], rows[:, 2 * g + 1:2 * g + 2])
    cb = _dot_nt(_bf(cm), _bf(bm))
    cmb = _bf(cm)
    bmt = _bf(bm.T)
    ssq_parts = []

    def mixer_b(g):
        gs = lanes(0, g)
        cum_last = per_head(cum, g, slice(TC - 1, TC))
        s_old = _bf(s_t[g])
        prev8 = xeb[:, gs]
        xs_t, x2a, x2b, wx, cum_g = [], [], [], [], []
        for rs in tiles:
            xs, prev8 = conv_rows(prev8, pv[rs, lanes(OFF_XBC, g)], cbw, cbb, gs)
            xs = _silu(xs)
            dt_g = per_head(dtp, g, rs)
            cum_g.append(per_head(cum, g, rs))
            dtx = xs * dt_g
            xs_t.append(xs)
            x2a.append(_bf(jnp.where(first_head, dtx, 0.0)))
            x2b.append(_bf(jnp.where(first_head, 0.0, dtx)))
            wx.append(_bf(jnp.exp2(cum_last - cum_g[-1]) * dtx))
        xeb[:, gs] = prev8
        x2 = jnp.concatenate(x2a + x2b, axis=0)
        g_rows = []
        for t, rs in enumerate(tiles):
            gmats = []
            for hh in range(2):
                hd = 2 * g + hh
                seg = cum[rs, hd:hd + 1] - cum_t[hd:hd + 1, :]
                gmats.append(_bf(cb[rs, :] * jnp.exp2(jnp.where(col_tc <= row_tc + t * HALF, seg, -jnp.inf))))
            g_rows.append(jnp.concatenate(gmats, axis=1))
        y_all = (_dot(jnp.concatenate(g_rows, axis=0), x2)
                 + _dot(cmb, s_old) * jnp.exp2(jnp.concatenate(cum_g, axis=0)))
        for t, rs in enumerate(tiles):
            yz = (y_all[rs, :] + dexp[:, gs] * xs_t[t]) * _silu(pv[rs, lanes(OFF_BZ, g)])
            ssq_parts.append(yz * yz)
            ymix[rs, lanes(RG_WIDTH, g)] = yz
        s_t[g] = jnp.exp2(cum_last) * s_t[g] + _dot(bmt, jnp.concatenate(wx, axis=0))

    lb_all = _lower_bound_row(hlb[...], layer)

    def forget_gate(g):
        lb = lb_all[:, g * LANES:(g + 1) * LANES]
        return lb + (1.0 - lb) * _logistic(pv[:, OFF_CF + g * LANES:OFF_CF + (g + 1) * LANES])

    def mixer_c(g, gt, xall):
        gs = lanes(0, g)
        b_off = N_MXU_LEVELS * TC
        bcum = xall[b_off:b_off + TC, :]
        blast = bcum[TC - 1:TC, :]

        def level_exponent(lev):
            if lev <= N_MXU_LEVELS:
                return xall[(lev - 1) * TC:lev * TC, :]
            blk = 1 << lev
            refs = [jnp.broadcast_to(bcum[j * blk + blk // 2 - 1:j * blk + blk // 2, :], (blk, LANES))
                    for j in range(TC // blk)]
            return -jnp.abs(bcum - jnp.concatenate(refs, axis=0))
        q = _silu(pv[:, lanes(OFF_CQ, g)])
        kk = 1.0 - gt
        vv = pv[:, lanes(OFF_CI, g)]
        zeros = jnp.zeros((HG_HEAD_DIM, TC), BF16)

        def keys_by_head(kt):
            ktt = _bf(kt.T)
            return jnp.concatenate(
                [jnp.concatenate([ktt[:HG_HEAD_DIM, :], zeros], axis=1),
                 jnp.concatenate([zeros, ktt[HG_HEAD_DIM:, :]], axis=1)], axis=0)

        att = _dot(_bf(q), keys_by_head(kk)) * cmask[0]
        for lev in range(1, N_LEVELS + 1):
            e = jnp.exp2(level_exponent(lev))
            att = att + _dot(_bf(q * e), keys_by_head(kk * e)) * cmask[lev]
        v2 = jnp.concatenate([_bf(jnp.where(first_head_tc, vv, 0.0)),
                              _bf(jnp.where(first_head_tc, 0.0, vv))], axis=0)
        o = _dot(_bf(att), v2) + _dot_nt(_bf(q * jnp.exp2(bcum)), _bf(st_hg[g]))
        khat = kk * jnp.exp2(blast - bcum)
        st_hg[g] = jnp.exp2(blast) * st_hg[g] + pairmask[...] * _dot(_bf(vv.T), _bf(khat))
        ms = _dot(_bf(o * o), pairmean[...])
        ymix[:, lanes(RG_WIDTH + SSD_WIDTH, g)] = (
            o * lax.rsqrt(ms + EPS) * hnorm[:, gs] * _silu(pv[:, lanes(OFF_CG, g)]))

    for pair in range(NGRP // 2):
        g0, g1 = 2 * pair, 2 * pair + 1
        gts = [forget_gate(g0), forget_gate(g1)]
        lf = jnp.log(jnp.concatenate(gts, axis=1)) * LOG2E
        hi = _bf(lf)
        lo = _bf(lf - hi.astype(F32))
        xall2 = _dot(emat2[...], jnp.concatenate([hi, lo], axis=0))
        for g, gt in zip((g0, g1), gts):
            side_work[g]()
            mixer_a(g)
            mixer_b(g)
            mixer_c(g, gt, xall2[:, (g - g0) * LANES:(g - g0 + 1) * LANES])

    for t, rs in enumerate(tiles):
        ssq = ssq_parts[t]
        for g in range(1, NGRP):
            ssq = ssq + ssq_parts[g * len(tiles) + t]
        rinv = lax.rsqrt(jnp.sum(ssq, axis=-1, keepdims=True) * (1.0 / SSD_WIDTH) + EPS)
        for g in range(NGRP):
            ys = lanes(RG_WIDTH, g)
            ymix[rs, ys] = ymix[rs, ys] * rinv * snorm[:, lanes(0, g)]


N_MIX_PARAMS = 18
MXU_N = 256
PROJ_BOUNDS = (0, 5 * MXU_N, 9 * MXU_N, 13 * MXU_N, NPROJ)
assert len(PROJ_BOUNDS) == NGRP + 1
CHUNKS_PER_STEP = 2


N_POST_PARAMS = 8


def _prompt_layer_kernel(layer, final, pos0, nsteps, n_valid,
                         x_ref, xn_ref, xp_ref, p_ref, gmix, w_in_ref, *rest):
    params = rest[:N_MIX_PARAMS]
    k = N_MIX_PARAMS + N_POST_PARAMS
    wo, nffn, wup, wdown, nple, wgate, wproj, nfin = rest[N_MIX_PARAMS:k]
    h_out, rgc_o, rgh_o, sc_o, ssd_o, hg_o = rest[k:k + 6]
    proj_a, proj_b, ymix_s, xea, xeb, hcar, s_t, st_hg = rest[k + 6:]
    states = (xea, xeb, hcar, s_t, st_hg)
    s = pl.program_id(0)
    live = s < n_valid
    c = jnp.minimum(s, n_valid - 1) % nsteps

    @pl.when(s == 0)
    def _prologue():
        proj_a[...] = _dot(_bf(_rms(x_ref[0, 0:TC, :], gmix[...])), w_in_ref[...])
        ymix_s[...] = jnp.zeros_like(ymix_s)

    @pl.when(jnp.logical_and(c == 0, live))
    def _init():
        for ref in states:
            ref[...] = jnp.zeros_like(ref)

    ymix_prev = _bf(ymix_s[...])
    n_ff = D_FF // FF_CHUNK
    carry = {}

    def post_piece(j):
        if j == 0:
            carry["h"] = xp_ref[0] + _dot(ymix_prev, wo[...])
            carry["hn"] = _bf(_rms(carry["h"], nffn[...]))
        elif j <= n_ff:
            cs = slice((j - 1) * FF_CHUNK, j * FF_CHUNK)
            z = jnp.square(jnp.maximum(_dot(carry["hn"], wup[:, cs]), 0.0))
            carry["h"] = carry["h"] + _dot(_bf(z), wdown[cs, :])
        elif j == n_ff + 1:
            h = carry["h"]
            gate = _sigmoid(_dot(_bf(_rms(h, nple[...])), wgate[...]))
            h = h + gate * _dot(_bf(p_ref[0]), wproj[...])
            h_out[0] = _rms(h, nfin[...]) if final else h

    def side_work(x, dst, first_piece):
        u = _bf(_rms(x, gmix[...]))

        def slot(j):
            cs = slice(PROJ_BOUNDS[j], PROJ_BOUNDS[j + 1])
            dst[:, cs] = _dot(u, w_in_ref[:, cs])
            post_piece(first_piece + j)

        return [functools.partial(slot, j) for j in range(NGRP)]

    @pl.when(live)
    def _mix_and_side_work():
        row0 = c * (CHUNKS_PER_STEP * TC) + pos0
        _mixer_chunk(layer, row0, proj_a, ymix_s.at[0:TC],
                     side_work(x_ref[0, TC:2 * TC, :], proj_b, 0), *params, *states)
        _mixer_chunk(layer, row0 + TC, proj_b, ymix_s.at[TC:2 * TC],
                     side_work(xn_ref[0], proj_a, NGRP), *params, *states)

    @pl.when(jnp.logical_not(live))
    def _drain():
        for j in range(n_ff + 2):
            post_piece(j)

    @pl.when(jnp.logical_and(c == nsteps - 1, live))
    def _emit():
        rgc_o[0] = xea[SUBLANES - 3:SUBLANES, :]
        sc_o[0] = xeb[SUBLANES - 3:SUBLANES, :]
        rgh_o[0] = hcar[0:1, :]
        for g in range(NGRP):
            ssd_o[0, g * LANES:(g + 1) * LANES, :] = s_t[g].T
            s_pair = st_hg[g].T
            for hh in range(2):
                lo = hh * HG_HEAD_DIM
                hg_o[0, 2 * g + hh] = s_pair[lo:lo + HG_HEAD_DIM, lo:lo + HG_HEAD_DIM]


def _prompt_layer(layer, final, x3, p_all4, gmix, w_in_bf, mp, lp, nfin, consts):
    bsz, seq, _ = x3.shape
    step_rows = CHUNKS_PER_STEP * TC
    nsteps = seq // step_rows
    n_valid = bsz * nsteps
    chunks_per_row = seq // TC
    n_chunks = bsz * chunks_per_row
    params = [mp["caw"], mp["cab"], mp["wgp"], mp["bgp"], mp["lam"], mp["cbw"], mp["cbb"],
              mp["dtb"], mp["alog"], mp["dexp"], mp["snorm"], mp["hlb"], mp["hnorm"],
              consts["ltri2"], consts["emat2"],
              consts["cmask"], consts["pairmask"], consts["pairmean"]]
    post_params = [lp["wo"], lp["nffn"], lp["wup"], lp["wdown"], lp["nple"], lp["wgate"],
                   lp["wproj"], nfin]
    assert len(params) == N_MIX_PARAMS and len(post_params) == N_POST_PARAMS

    def mix_step(s):
        return jnp.minimum(s, n_valid - 1)

    def cur(s):
        return mix_step(s) // nsteps, mix_step(s) % nsteps, 0

    def nxt(s):
        flat = jnp.minimum((mix_step(s) + 1) * CHUNKS_PER_STEP, n_chunks - 1)
        return flat // chunks_per_row, flat % chunks_per_row, 0

    def prv(s):
        sp = jnp.maximum(s - 1, 0)
        return sp // nsteps, sp % nsteps, 0

    def state_idx(nd):
        return lambda s: (mix_step(s) // nsteps,) + (0,) * (nd - 1)

    out_shape = (
        jax.ShapeDtypeStruct((bsz, seq, D_MODEL), F32),
        jax.ShapeDtypeStruct((bsz, CONV_W - 1, RG_WIDTH), F32),
        jax.ShapeDtypeStruct((bsz, 1, RG_WIDTH), F32),
        jax.ShapeDtypeStruct((bsz, CONV_W - 1, SSD_CONV_DIM), F32),
        jax.ShapeDtypeStruct((bsz, SSD_WIDTH, SSD_STATE), F32),
        jax.ShapeDtypeStruct((bsz, HG_HEADS, HG_HEAD_DIM, HG_HEAD_DIM), F32),
    )
    out_specs = (
        pl.BlockSpec((1, step_rows, D_MODEL), prv),
        pl.BlockSpec((1, CONV_W - 1, RG_WIDTH), state_idx(3)),
        pl.BlockSpec((1, 1, RG_WIDTH), state_idx(3)),
        pl.BlockSpec((1, CONV_W - 1, SSD_CONV_DIM), state_idx(3)),
        pl.BlockSpec((1, SSD_WIDTH, SSD_STATE), state_idx(3)),
        pl.BlockSpec((1, HG_HEADS, HG_HEAD_DIM, HG_HEAD_DIM), state_idx(4)),
    )
    scratch = [
        pltpu.VMEM((TC, NPROJ), F32),
        pltpu.VMEM((TC, NPROJ), F32),
        pltpu.VMEM((step_rows, D_MIX), F32),
        pltpu.VMEM((SUBLANES, RG_WIDTH), F32),
        pltpu.VMEM((SUBLANES, SSD_CONV_DIM), F32),
        pltpu.VMEM((SUBLANES, RG_WIDTH), F32),
        pltpu.VMEM((NGRP, SSD_STATE, LANES), F32),
        pltpu.VMEM((NGRP, LANES, LANES), F32),
    ]
    return pl.pallas_call(
        functools.partial(_prompt_layer_kernel, layer, final, 0, nsteps, n_valid),
        grid=(n_valid + 1,),
        in_specs=[pl.BlockSpec((1, step_rows, D_MODEL), cur),
                  pl.BlockSpec((1, TC, D_MODEL), nxt),
                  pl.BlockSpec((1, step_rows, D_MODEL), prv),
                  pl.BlockSpec((None, 1, step_rows, PLE_DIM), lambda s: (layer,) + prv(s)),
                  _const_spec(gmix), _const_spec(w_in_bf)]
        + [_const_spec(p) for p in params + post_params],
        out_specs=out_specs,
        out_shape=out_shape,
        scratch_shapes=scratch,
        compiler_params=pltpu.CompilerParams(
            dimension_semantics=("arbitrary",), vmem_limit_bytes=VMEM_LIMIT),
        name="prompt_layer",
    )(x3, x3, x3, p_all4, gmix, w_in_bf, *params, *post_params)


def _hg_step_kernel(layer, n_acc, proj_ref, hlb, st_ref, *rest):
    ot_ref, st_o, qh_s, f_s, k_s, v_s = rest[n_acc:]
    h = pl.program_id(0)
    if n_acc == 0:
        for other in range(DEPTH):
            if other != layer:
                st_o[other] = jnp.zeros(st_o.shape[1:], F32)
        st_o = st_o.at[layer]

    @pl.when(h == 0)
    def _prep():
        q = _silu(proj_ref[:, OFF_CQ:OFF_CQ + HG_WIDTH])
        lb = _lower_bound_row(hlb[...], layer)
        gt = lb + (1.0 - lb) * _logistic(proj_ref[:, OFF_CF:OFF_CF + HG_WIDTH])
        fdec = jnp.exp(jnp.log(gt))
        qh_s[...] = (q * fdec).T
        f_s[...] = fdec.T
        k_s[...] = (1.0 - gt).T
        v_s[...] = proj_ref[:, OFF_CI:OFF_CI + HG_WIDTH].T

    base = pl.multiple_of(h * HG_HEAD_DIM, HG_HEAD_DIM)
    v_h = v_s[pl.ds(base, HG_HEAD_DIM), :]

    def body(dk, acc):
        s = st_ref[dk]
        st_o[dk] = f_s[pl.ds(base + dk, 1), :] * s + k_s[pl.ds(base + dk, 1), :] * v_h
        return acc + qh_s[pl.ds(base + dk, 1), :] * s

    ot_ref[...] = lax.fori_loop(0, HG_HEAD_DIM, body,
                                jnp.zeros((HG_HEAD_DIM, LANES), F32), unroll=8)


def _hg_step(layer, proj, hlb, st_all, acc):
    nb = proj.shape[0]
    blk = (HG_HEAD_DIM, HG_HEAD_DIM, nb)
    st_spec = pl.BlockSpec((None, None) + blk, lambda h: (layer, h, 0, 0, 0))
    n_acc = len(acc)
    st_out_spec = st_spec if n_acc else pl.BlockSpec(
        (DEPTH, None) + blk, lambda h: (0, h, 0, 0, 0))
    return pl.pallas_call(
        functools.partial(_hg_step_kernel, layer, n_acc),
        grid=(HG_HEADS,),
        in_specs=[_const_spec(proj), _const_spec(hlb), st_spec]
        + [pl.BlockSpec(memory_space=pl.ANY)] * n_acc,
        out_specs=(pl.BlockSpec((HG_HEAD_DIM, nb), lambda h: (h, 0)), st_out_spec),
        out_shape=(jax.ShapeDtypeStruct((HG_WIDTH, nb), F32),
                   jax.ShapeDtypeStruct(st_all.shape, F32)),
        scratch_shapes=[pltpu.VMEM((HG_WIDTH, nb), F32)] * 4,
        input_output_aliases={3 + k: 1 + k for k in range(n_acc)},
        compiler_params=pltpu.CompilerParams(
            dimension_semantics=("arbitrary",), vmem_limit_bytes=VMEM_LIMIT),
        name="hg_step",
    )(proj, hlb, st_all, *acc)


def _pad_rows_t(x):
    pad = jnp.zeros((LANES - x.shape[0], x.shape[1]), F32)
    return jnp.concatenate([x, pad], axis=0).T


def _sample_mixer_kernel(layer, pos0, n_acc, proj_ref, rgc_ref, rgh_ref, sc_ref, ssd_ref, ohg_ref, *rest):
    (caw, cab, wg, bg, lam, cbw, cbb, dtb, alog, dexp, snorm,
     hlb, hnorm, expand, headsum, headmean,
     ymix_ref, rgc_o, rgh_o, sc_o, ssd_o) = rest[n_acc:]
    bb = DEC_BB
    if n_acc == 0:
        for other in range(DEPTH):
            if other != layer:
                ssd_o[other] = jnp.zeros(ssd_o.shape[1:], F32)
        ssd_o = ssd_o.at[layer]
    ax = proj_ref[:, OFF_AX:OFF_AX + RG_WIDTH]
    xa = cab[...] + caw[CONV_W - 1:CONV_W, :] * ax
    for k in range(CONV_W - 1):
        xa = xa + caw[k:k + 1, :] * rgc_ref[:, k * RG_WIDTH:(k + 1) * RG_WIDTH]
    rgc_o[:, 0:2 * RG_WIDTH] = rgc_ref[:, RG_WIDTH:3 * RG_WIDTH]
    rgc_o[:, 2 * RG_WIDTH:] = ax
    a, mult, gi = _rg_coeffs(_dot(_bf(xa), wg[...]) + bg[...], lam[...])
    if pos0 == 0:
        mult = jnp.ones_like(mult)
    h = a * rgh_ref[...] + mult * (gi * xa)
    rgh_o[...] = h
    ymix_ref[:, 0:RG_WIDTH] = h * _gelu_tanh(proj_ref[:, OFF_AG:OFF_AG + RG_WIDTH])

    bx = proj_ref[:, OFF_XBC:OFF_XBC + SSD_CONV_DIM]
    xbc = cbb[...] + cbw[CONV_W - 1:CONV_W, :] * bx
    for k in range(CONV_W - 1):
        xbc = xbc + cbw[k:k + 1, :] * sc_ref[:, k * SSD_CONV_DIM:(k + 1) * SSD_CONV_DIM]
    sc_o[:, 0:2 * SSD_CONV_DIM] = sc_ref[:, SSD_CONV_DIM:3 * SSD_CONV_DIM]
    sc_o[:, 2 * SSD_CONV_DIM:] = bx
    xbc = _silu(xbc)
    xs = xbc[:, :SSD_WIDTH]
    bm = xbc[:, SSD_WIDTH:SSD_WIDTH + SSD_STATE]
    cm = xbc[:, SSD_WIDTH + SSD_STATE:]
    dtp = _softplus(proj_ref[:, OFF_DT:OFF_DT + LANES] + dtb[...])
    v_da = dtp * (-jnp.exp(alog[...]))
    dt_e = _dot_x_c(dtp, expand[...])
    e_e = jnp.exp(_dot_x_c(v_da, expand[...]))
    dtx = dt_e * xs
    cbs = _dot(_bf(cm * bm), jnp.ones((SSD_STATE, LANES), BF16))[:, 0:1]
    dtx_t = _pad_rows_t(dtx)
    e_t = _pad_rows_t(e_e)
    c_pad = _bf(jnp.concatenate([cm, jnp.zeros((LANES - bb, SSD_STATE), F32)], axis=0))
    lane_w = lax.broadcasted_iota(jnp.int32, (SSD_WIDTH, LANES), 1)
    y_t = jnp.zeros((SSD_WIDTH, LANES), F32)
    for j in range(bb):
        s_old = ssd_ref[j].reshape(SSD_WIDTH, SSD_STATE)
        ssd_o[j] = (e_t[:, j:j + 1] * s_old + dtx_t[:, j:j + 1] * bm[j:j + 1, :]).reshape(
            SSD_HEADS, SSD_HEAD_DIM, SSD_STATE)
        y_t = y_t + jnp.where(lane_w == j, _dot_nt(_bf(s_old), c_pad), 0.0)
    y = cbs * dtx + y_t.T[0:bb, :] * e_e
    yb = y + dexp[...] * xs
    ymix_ref[:, RG_WIDTH:RG_WIDTH + SSD_WIDTH] = _rms(
        yb * _silu(proj_ref[:, OFF_BZ:OFF_BZ + SSD_WIDTH]), snorm[...])

    q = _silu(proj_ref[:, OFF_CQ:OFF_CQ + HG_WIDTH])
    lb = _lower_bound_row(hlb[...], layer)
    gt = lb + (1.0 - lb) * _logistic(proj_ref[:, OFF_CF:OFF_CF + HG_WIDTH])
    kk = 1.0 - gt
    vv = proj_ref[:, OFF_CI:OFF_CI + HG_WIDTH]
    att = _dot(_bf(q * kk), headsum[...])
    o = att * vv + ohg_ref[...]
    ms = _dot(_bf(o * o), headmean[...])
    ymix_ref[:, RG_WIDTH + SSD_WIDTH:] = (
        o * lax.rsqrt(ms + EPS) * hnorm[...] * _silu(proj_ref[:, OFF_CG:OFF_CG + HG_WIDTH]))


def _sample_mixer(layer, proj, rgc, rgh, sc, ssd_all, o_hg, acc, mp, consts):
    nb = proj.shape[0]
    bb = DEC_BB
    params = [mp["caw"], mp["cab"], mp["wg"], mp["bg"], mp["lam"], mp["cbw"], mp["cbb"],
              mp["dtb"], mp["alog"], mp["dexp"], mp["snorm"], mp["hlb"], mp["hnorm"],
              consts["expand"], consts["headsum"], consts["headmean"]]
    row2 = lambda w: pl.BlockSpec((bb, w), lambda i: (i, 0))
    small_specs = [row2(3 * RG_WIDTH), row2(RG_WIDTH), row2(3 * SSD_CONV_DIM)]
    ssd_spec = pl.BlockSpec((None, bb, SSD_HEADS, SSD_HEAD_DIM, SSD_STATE),
                            lambda i: (layer, i, 0, 0, 0))
    out_shape = (
        jax.ShapeDtypeStruct((nb, D_MIX), F32),
        jax.ShapeDtypeStruct((nb, 3 * RG_WIDTH), F32),
        jax.ShapeDtypeStruct((nb, RG_WIDTH), F32),
        jax.ShapeDtypeStruct((nb, 3 * SSD_CONV_DIM), F32),
        jax.ShapeDtypeStruct(ssd_all.shape, F32),
    )
    n_acc = len(acc)
    n_lead = 6
    ssd_out_spec = ssd_spec if n_acc else pl.BlockSpec(
        (DEPTH, bb, SSD_HEADS, SSD_HEAD_DIM, SSD_STATE), lambda i: (0, i, 0, 0, 0))
    return pl.pallas_call(
        functools.partial(_sample_mixer_kernel, layer, PAST_LEN, n_acc),
        grid=(nb // bb,),
        in_specs=[row2(NPROJ)] + small_specs + [ssd_spec, row2(HG_WIDTH)]
        + [pl.BlockSpec(memory_space=pl.ANY)] * n_acc + [_const_spec(p) for p in params],
        out_specs=tuple([row2(D_MIX)] + small_specs + [ssd_out_spec]),
        out_shape=out_shape,
        input_output_aliases={n_lead + k: 4 + k for k in range(n_acc)},
        compiler_params=pltpu.CompilerParams(
            dimension_semantics=("parallel",), vmem_limit_bytes=VMEM_LIMIT),
        name="sample_mixer",
    )(proj, rgc, rgh, sc, ssd_all, o_hg, *acc, *params)


def _post_kernel(final, h_ref, y_ref, p_ref, wo, nffn, wup, wdown, nple, wgate, wproj, nfin, o_ref):
    h = h_ref[...] + _dot(_bf(y_ref[...]), wo[...])
    hn = _bf(_rms(h, nffn[...]))
    for j in range(D_FF // FF_CHUNK):
        cs = slice(j * FF_CHUNK, (j + 1) * FF_CHUNK)
        z = jnp.square(jnp.maximum(_dot(hn, wup[:, cs]), 0.0))
        h = h + _dot(_bf(z), wdown[cs, :])
    gate = _sigmoid(_dot(_bf(_rms(h, nple[...])), wgate[...]))
    h = h + gate * _dot(_bf(p_ref[...]), wproj[...])
    if final:
        h = _rms(h, nfin[...])
    o_ref[...] = h


def _post(final, layer, h2d, ymix2d, p_all, lp, nfin, tm):
    m = h2d.shape[0]
    params = [lp["wo"], lp["nffn"], lp["wup"], lp["wdown"], lp["nple"], lp["wgate"], lp["wproj"], nfin]
    return pl.pallas_call(
        functools.partial(_post_kernel, final),
        grid=(m // tm,),
        in_specs=[pl.BlockSpec((tm, D_MODEL), lambda i: (i, 0)),
                  pl.BlockSpec((tm, D_MIX), lambda i: (i, 0)),
                  pl.BlockSpec((None, tm, PLE_DIM), lambda i: (layer, i, 0))]
        + [_const_spec(p) for p in params],
        out_specs=pl.BlockSpec((tm, D_MODEL), lambda i: (i, 0)),
        out_shape=jax.ShapeDtypeStruct((m, D_MODEL), F32),
        compiler_params=pltpu.CompilerParams(
            dimension_semantics=("parallel",), vmem_limit_bytes=VMEM_LIMIT),
        name="post_mixer",
    )(h2d, ymix2d, p_all, *params)


def _constants():
    t = np.arange(TC)
    ltri = (t[None, :] <= t[:, None]).astype(np.float32)
    expand = np.zeros((LANES, SSD_WIDTH), np.float32)
    for hd in range(SSD_HEADS):
        expand[hd, hd * SSD_HEAD_DIM:(hd + 1) * SSD_HEAD_DIM] = 1.0
    emat = np.zeros(((N_MXU_LEVELS + 1) * TC, TC), np.float32)
    for lev in range(1, N_MXU_LEVELS + 1):
        blk = 1 << lev
        mid = (t // blk) * blk + blk // 2
        upper = t >= mid
        r0 = (lev - 1) * TC
        emat[r0:r0 + TC] = (
            (upper[:, None] & (t[None, :] >= mid[:, None]) & (t[None, :] <= t[:, None]))
            | ((~upper)[:, None] & (t[None, :] > t[:, None]) & (t[None, :] < mid[:, None])))
    emat[N_MXU_LEVELS * TC:] = ltri
    cmask = np.zeros((N_LEVELS + 1, TC, 2 * TC), np.float32)
    cmask[0] = np.tile(np.eye(TC, dtype=np.float32), (1, 2))
    for lev in range(1, N_LEVELS + 1):
        blk = 1 << lev
        up = (t % blk) >= blk // 2
        m = up[:, None] & (~up)[None, :] & ((t // blk)[:, None] == (t // blk)[None, :])
        cmask[lev] = np.tile(m.astype(np.float32), (1, 2))
    hidx = np.arange(HG_WIDTH) // HG_HEAD_DIM
    bd = (hidx[:, None] == hidx[None, :]).astype(np.float32)
    pair = bd[:LANES, :LANES]
    return {
        "ltri2": jnp.asarray(np.tile(ltri, (1, 2)), BF16),
        "expand": jnp.asarray(expand, BF16),
        "emat2": jnp.asarray(np.tile(emat, (1, 2)), BF16),
        "cmask": jnp.asarray(cmask, F32),
        "pairmask": jnp.asarray(pair, F32),
        "pairmean": jnp.asarray(pair / HG_HEAD_DIM, BF16),
        "headsum": jnp.asarray(bd, BF16), "headmean": jnp.asarray(bd / HG_HEAD_DIM, BF16),
    }


def _block_diag(w):
    hh, d, _ = w.shape
    eye = jnp.eye(hh, dtype=w.dtype)
    return (eye[:, None, :, None] * w[:, :, None, :]).reshape(hh * d, hh * d)


def _pair_gate_weights(wa, wx):
    hh, d, _ = wa.shape
    pa = jnp.stack([_block_diag(wa[2 * g:2 * g + 2]) for g in range(hh // 2)])
    px = jnp.stack([_block_diag(wx[2 * g:2 * g + 2]) for g in range(hh // 2)])
    return jnp.concatenate([pa, px], axis=2)


def _pad_lanes(v):
    return jnp.pad(v, (0, LANES - v.shape[0]))[None, :]


def kernel(x_prompt, x_sample, state_rg_conv, state_rg_h, state_ssd_conv, state_ssd, state_hgrn,
           p_prompt, p_sample, norm_mix, w_in, conv_a_w, conv_a_b, rg_wa, rg_ba, rg_wx, rg_bx,
           rg_lambda, conv_b_w, conv_b_b, ssd_dt_bias, ssd_a_log, ssd_d, ssd_norm,
           hg_lower_bounds, hg_norm, w_out, norm_ffn, w_up, w_down, norm_ple, w_ple_gate,
           w_ple_proj, norm_final):
    bsz, seq, _ = x_prompt.shape
    nb = x_sample.shape[0]
    consts = _constants()
    w_in_t = jnp.swapaxes(w_in, 1, 2)
    nfin = norm_final[None, :]
    ps_all = p_sample.reshape(DEPTH, nb, PLE_DIM)

    hp = x_prompt
    hs = x_sample.reshape(nb, D_MODEL)
    st_p = [[] for _ in range(5)]
    st_s = [[] for _ in range(3)]
    hg_t_all = jnp.transpose(state_hgrn, (0, 2, 3, 4, 1))
    acc_ssd = ()
    acc_hg = ()
    for i in range(DEPTH):
        w_re = _prep_w_in(w_in_t, i)
        gmix = norm_mix[i][None, :]
        mp = {
            "caw": conv_a_w[i], "cab": conv_a_b[i][None, :],
            "wg": jnp.concatenate([_block_diag(rg_wa[i]), _block_diag(rg_wx[i])], axis=1).astype(BF16),
            "bg": jnp.concatenate([rg_ba[i], rg_bx[i]])[None, :],
            "wgp": _pair_gate_weights(rg_wa[i], rg_wx[i]).astype(BF16),
            "bgp": jnp.concatenate([rg_ba[i].reshape(NGRP, 1, LANES),
                                    rg_bx[i].reshape(NGRP, 1, LANES)], axis=2),
            "lam": rg_lambda[i][None, :],
            "cbw": conv_b_w[i], "cbb": conv_b_b[i][None, :],
            "dtb": _pad_lanes(ssd_dt_bias[i]), "alog": _pad_lanes(ssd_a_log[i]),
            "dexp": jnp.repeat(ssd_d[i], SSD_HEAD_DIM)[None, :],
            "snorm": ssd_norm[i][None, :],
            "hlb": hg_lower_bounds, "hnorm": jnp.tile(hg_norm[i], HG_HEADS)[None, :],
        }
        wo_bf, wup_bf, wdown_bf, wgate_bf, wproj_bf = _cast_layer_weights(
            i, w_out, w_up, w_down, w_ple_gate, w_ple_proj)
        lp = {
            "wo": wo_bf, "nffn": norm_ffn[i][None, :], "wup": wup_bf, "wdown": wdown_bf,
            "nple": norm_ple[i][None, :], "wgate": wgate_bf, "wproj": wproj_bf,
        }
        final = i == DEPTH - 1

        hp, rgc, rgh, sc, ssd, hg = _prompt_layer(
            i, final, hp, p_prompt, gmix, w_re, mp, lp, nfin, consts)
        for lst, s in zip(st_p, (rgc, rgh.reshape(bsz, RG_WIDTH), sc,
                                 ssd.reshape(bsz, SSD_HEADS, SSD_HEAD_DIM, SSD_STATE), hg)):
            lst.append(s)

        proj_s = _in_proj(hs, gmix, w_re, nb)
        o_hg_t, hg_acc = _hg_step(i, proj_s, hg_lower_bounds, hg_t_all, acc_hg)
        acc_hg = (hg_acc,)
        ymix_s, rgc, rgh, sc, ssd_acc = _sample_mixer(
            i, proj_s,
            state_rg_conv[i].reshape(nb, 3 * RG_WIDTH), state_rg_h[i],
            state_ssd_conv[i].reshape(nb, 3 * SSD_CONV_DIM),
            state_ssd, o_hg_t.T, acc_ssd, mp, consts)
        acc_ssd = (ssd_acc,)
        hs = _post(final, i, hs, ymix_s, ps_all, lp, nfin, nb)
        for lst, s in zip(st_s, (rgc.reshape(nb, CONV_W - 1, RG_WIDTH), rgh,
                                 sc.reshape(nb, CONV_W - 1, SSD_CONV_DIM))):
            lst.append(s)

    y_prompt = hp
    y_sample = hs.reshape(nb, 1, D_MODEL)
    outs_p = [jnp.stack(lst) for lst in st_p]
    outs_s = [jnp.stack(lst) for lst in st_s] + [
        acc_ssd[0], jnp.transpose(acc_hg[0], (0, 4, 1, 2, 3))]
    return (y_prompt, y_sample, *outs_p, *outs_s)
```

```python
import functools

import numpy as np
import jax
import jax.numpy as jnp
from jax import lax
from jax.experimental import pallas as pl
from jax.experimental.pallas import tpu as pltpu

F32 = jnp.float32
BF16 = jnp.bfloat16

D_MODEL = 1024
DEPTH = 2
PAST_LEN = 16384
PLE_DIM = 256
D_FF = 4 * D_MODEL
CONV_W = 4
EPS = 1e-6
RG_WIDTH = 512
RG_HEADS = 8
RG_HEAD_DIM = 64
RG_C = 8.0
SSD_WIDTH = 512
SSD_HEAD_DIM = 64
SSD_HEADS = 8
SSD_STATE = 128
SSD_CONV_DIM = SSD_WIDTH + 2 * SSD_STATE
HG_WIDTH = 512
HG_HEADS = 8
HG_HEAD_DIM = 64
D_MIX = RG_WIDTH + SSD_WIDTH + HG_WIDTH

LOG2E = 1.4426950408889634
LANES = 128
SUBLANES = 8
NGRP = RG_WIDTH // LANES

OFF_AX = 0
OFF_AG = 512
OFF_BZ = 1024
OFF_XBC = 1536
OFF_CQ = 2304
OFF_CF = 2816
OFF_CI = 3328
OFF_CG = 3840
OFF_DT = 4352
NPROJ = OFF_DT + LANES
DT_COL = 2 * RG_WIDTH + SSD_WIDTH + SSD_CONV_DIM
D_IN_PROJ = DT_COL + SSD_HEADS + 4 * HG_WIDTH

TC = 128
HALF = TC // 2
N_LEVELS = 7
N_MXU_LEVELS = 3
DEC_BB = 16
FF_CHUNK = 1024
VMEM_LIMIT = 58 * 1024 * 1024


def _bf(x):
    return x.astype(BF16)


def _dot(a, b):
    return jnp.dot(a, b, preferred_element_type=F32)


def _dot_nt(a, b):
    return lax.dot_general(a, b, (((1,), (1,)), ((), ())), preferred_element_type=F32)


def _split3(x):
    hi = _bf(x)
    r1 = x - hi.astype(F32)
    mid = _bf(r1)
    lo = _bf(r1 - mid.astype(F32))
    return hi, mid, lo


def _dot_x_c(x, c):
    hi, mid, lo = _split3(x)
    return _dot(hi, c) + _dot(mid, c) + _dot(lo, c)


def _split2(x):
    hi = _bf(x)
    return hi, _bf(x - hi.astype(F32))


def _logistic(x):
    return 1.0 / (1.0 + jnp.exp(-x))


def _sigmoid(x):
    return 0.5 * jnp.tanh(0.5 * x) + 0.5


def _silu(x):
    h = 0.5 * x
    return h * jnp.tanh(h) + h


def _softplus(x):
    return jnp.maximum(x, 0.0) + jnp.log1p(jnp.exp(-jnp.abs(x)))


def _gelu_tanh(x):
    c = np.float32(np.sqrt(2.0 / np.pi))
    return 0.5 * x * (1.0 + jnp.tanh(c * (x + 0.044715 * (x * x * x))))


def _rms(x, gain):
    return x * lax.rsqrt(jnp.mean(x * x, axis=-1, keepdims=True) + EPS) * gain


def _lower_bound_row(hlb, layer):
    m = jnp.max(hlb, axis=0, keepdims=True)
    e = jnp.exp(hlb - m)
    sm = e / jnp.sum(e, axis=0, keepdims=True)
    lb = jnp.zeros((1, HG_WIDTH), F32)
    for j in range(1, layer + 1):
        lb = lb + sm[j:j + 1, :]
    return lb


def _rg_coeffs(gates, lam_row):
    w = gates.shape[1] // 2
    r = _sigmoid(gates[:, :w])
    i = _sigmoid(gates[:, w:])
    log_a = -RG_C * r * _softplus(-lam_row)
    a = jnp.exp(log_a)
    mult = jnp.sqrt(-jnp.tanh(log_a) * (a * a + 1.0))
    return a, mult, i


def _const_spec(arr):
    nd = arr.ndim
    return pl.BlockSpec(arr.shape, lambda *_: (0,) * nd, pipeline_mode=pl.Buffered(1))


CAST_STEPS = 8


def _cast_kernel(*refs):
    n = len(refs) // 2
    for w_ref, o_ref in zip(refs[:n], refs[n:]):
        o_ref[...] = _bf(w_ref[...])


def _cast_layer_weights(layer, *w_alls):
    in_specs, out_specs, out_shape = [], [], []
    for w_all in w_alls:
        _, rows, cols = w_all.shape
        rb = rows // CAST_STEPS
        assert rb * CAST_STEPS == rows and rb % (2 * SUBLANES) == 0
        in_specs.append(pl.BlockSpec((None, rb, cols), lambda i: (layer, i, 0)))
        out_specs.append(pl.BlockSpec((rb, cols), lambda i: (i, 0)))
        out_shape.append(jax.ShapeDtypeStruct((rows, cols), BF16))
    return pl.pallas_call(
        _cast_kernel,
        grid=(CAST_STEPS,),
        in_specs=in_specs,
        out_specs=tuple(out_specs),
        out_shape=tuple(out_shape),
        compiler_params=pltpu.CompilerParams(
            dimension_semantics=("parallel",), vmem_limit_bytes=VMEM_LIMIT),
        name="cast_bf16",
    )(*w_alls)


def _win_kernel(wt_ref, o_ref):
    o_ref[:, 0:OFF_CQ] = _bf(wt_ref[0:DT_COL, :].T)
    o_ref[:, OFF_CQ:OFF_DT] = _bf(wt_ref[DT_COL + SSD_HEADS:D_IN_PROJ, :].T)
    dt = jnp.concatenate([wt_ref[DT_COL:DT_COL + SSD_HEADS, :],
                          jnp.zeros((LANES - SSD_HEADS, LANES), F32)], axis=0)
    o_ref[:, OFF_DT:NPROJ] = _bf(dt.T)


def _prep_w_in(wt_all, layer):
    return pl.pallas_call(
        _win_kernel,
        grid=(D_MODEL // LANES,),
        in_specs=[pl.BlockSpec((None, D_IN_PROJ, LANES), lambda i: (layer, 0, i))],
        out_specs=pl.BlockSpec((LANES, NPROJ), lambda i: (i, 0)),
        out_shape=jax.ShapeDtypeStruct((D_MODEL, NPROJ), BF16),
        compiler_params=pltpu.CompilerParams(dimension_semantics=("parallel",)),
        name="prep_w_in",
    )(wt_all)


def _proj_kernel(x_ref, g_ref, w_ref, o_ref):
    u = _rms(x_ref[...], g_ref[...])
    o_ref[...] = _dot(_bf(u), w_ref[...])


def _in_proj(x2d, gain, w_bf, tm):
    m = x2d.shape[0]
    return pl.pallas_call(
        _proj_kernel,
        grid=(m // tm,),
        in_specs=[pl.BlockSpec((tm, D_MODEL), lambda i: (i, 0)),
                  _const_spec(gain), _const_spec(w_bf)],
        out_specs=pl.BlockSpec((tm, NPROJ), lambda i: (i, 0)),
        out_shape=jax.ShapeDtypeStruct((m, NPROJ), F32),
        compiler_params=pltpu.CompilerParams(
            dimension_semantics=("parallel",), vmem_limit_bytes=VMEM_LIMIT),
        name="in_proj",
    )(x2d, gain, w_bf)


def _mixer_chunk(layer, row0, pv, ymix, side_work,
                 caw, cab, wgp, bgp, lam, cbw, cbb, dtb, alog, dexp, snorm,
                 hlb, hnorm, ltri2, emat2, cmask, pairmask, pairmean,
                 xea, xeb, hcar, s_t, st_hg):
    row_h = lax.broadcasted_iota(jnp.int32, (HALF, 1), 0)
    sub_h = row_h & (SUBLANES - 1)
    first_head = lax.broadcasted_iota(jnp.int32, (HALF, LANES), 1) < HG_HEAD_DIM
    first_head_tc = lax.broadcasted_iota(jnp.int32, (TC, LANES), 1) < HG_HEAD_DIM
    col_tc = lax.broadcasted_iota(jnp.int32, (HALF, TC), 1)
    row_tc = lax.broadcasted_iota(jnp.int32, (HALF, TC), 0)

    tiles = [slice(t * HALF, (t + 1) * HALF) for t in range(TC // HALF)]

    def lanes(off, g):
        return slice(off + g * LANES, off + (g + 1) * LANES)

    def conv_rows(prev8, x, w_ref, b_ref, gs):
        xcat = jnp.concatenate([prev8, x], axis=0)
        y = b_ref[:, gs] + w_ref[CONV_W - 1:CONV_W, gs] * x
        for j in range(1, CONV_W):
            y = y + w_ref[CONV_W - 1 - j:CONV_W - j, gs] * pltpu.roll(xcat, j, 0)[SUBLANES:, :]
        return y, x[x.shape[0] - SUBLANES:, :]

    def conv(tail_ref, w_ref, b_ref, src_off, g):
        gs = lanes(0, g)
        y, tail = conv_rows(tail_ref[:, gs], pv[:, lanes(src_off, g)], w_ref, b_ref, gs)
        tail_ref[:, gs] = tail
        return y

    def mixer_a(g):
        gs = lanes(0, g)
        prev8 = xea[:, gs]
        carry = hcar[0:1, gs]
        xas = []
        for rs in tiles:
            xa, prev8 = conv_rows(prev8, pv[rs, lanes(OFF_AX, g)], caw, cab, gs)
            xas.append(xa)
        gates = _dot(_bf(jnp.concatenate(xas, axis=0)), wgp[g]) + bgp[g]
        for t, rs in enumerate(tiles):
            xa = xas[t]
            a, mult, gi = _rg_coeffs(gates[rs, :], lam[:, gs])
            mult = jnp.where(row_h + (row0 + t * HALF) == 0, 1.0, mult)
            b = mult * (gi * xa)
            k = 1
            while k < SUBLANES:
                keep = sub_h >= k
                a_sh = jnp.where(keep, pltpu.roll(a, k, 0), 1.0)
                b_sh = jnp.where(keep, pltpu.roll(b, k, 0), 0.0)
                b = b + a * b_sh
                a = a * a_sh
                k *= 2
            slabs = []
            for r in range(HALF // SUBLANES):
                sl = slice(r * SUBLANES, (r + 1) * SUBLANES)
                h_r = a[sl, :] * carry + b[sl, :]
                slabs.append(h_r)
                carry = h_r[SUBLANES - 1:SUBLANES, :]
            ymix[rs, gs] = jnp.concatenate(slabs, axis=0) * _gelu_tanh(
                pv[rs, lanes(OFF_AG, g)])
        xea[:, gs] = prev8
        hcar[0:1, gs] = carry

    bm = _silu(conv(xeb, cbw, cbb, OFF_XBC, NGRP))
    cm = _silu(conv(xeb, cbw, cbb, OFF_XBC, NGRP + 1))
    dtp = _softplus(pv[:, OFF_DT:OFF_DT + LANES] + dtb[...])
    v_da = dtp * (-LOG2E * jnp.exp(alog[...]))
    cum = _dot(ltri2[...], jnp.concatenate(_split2(v_da), axis=0))
    cum_t = cum.T

    def per_head(arr, g, rs):
        rows = arr[rs, :]
        return jnp.where(first_head[0:rows.shape[0], :], rows[:, 2 * g:2 * g + 1], rows[:, 2 * g + 1:2 * g + 2])
    cb = _dot_nt(_bf(cm), _bf(bm))
    cmb = _bf(cm)
    bmt = _bf(bm.T)
    ssq_parts = []

    def mixer_b(g):
        gs = lanes(0, g)
        cum_last = per_head(cum, g, slice(TC - 1, TC))
        s_old = _bf(s_t[g])
        prev8 = xeb[:, gs]
        xs_t, x2a, x2b, wx, cum_g = [], [], [], [], []
        for rs in tiles:
            xs, prev8 = conv_rows(prev8, pv[rs, lanes(OFF_XBC, g)], cbw, cbb, gs)
            xs = _silu(xs)
            dt_g = per_head(dtp, g, rs)
            cum_g.append(per_head(cum, g, rs))
            dtx = xs * dt_g
            xs_t.append(xs)
            x2a.append(_bf(jnp.where(first_head, dtx, 0.0)))
            x2b.append(_bf(jnp.where(first_head, 0.0, dtx)))
            wx.append(_bf(jnp.exp2(cum_last - cum_g[-1]) * dtx))
        xeb[:, gs] = prev8
        x2 = jnp.concatenate(x2a + x2b, axis=0)
        g_rows = []
        for t, rs in enumerate(tiles):
            gmats = []
            for hh in range(2):
                hd = 2 * g + hh
                seg = cum[rs, hd:hd + 1] - cum_t[hd:hd + 1, :]
                gmats.append(_bf(cb[rs, :] * jnp.exp2(jnp.where(col_tc <= row_tc + t * HALF, seg, -jnp.inf))))
            g_rows.append(jnp.concatenate(gmats, axis=1))
        y_all = (_dot(jnp.concatenate(g_rows, axis=0), x2)
                 + _dot(cmb, s_old) * jnp.exp2(jnp.concatenate(cum_g, axis=0)))
        for t, rs in enumerate(tiles):
            yz = (y_all[rs, :] + dexp[:, gs] * xs_t[t]) * _silu(pv[rs, lanes(OFF_BZ, g)])
            ssq_parts.append(yz * yz)
            ymix[rs, lanes(RG_WIDTH, g)] = yz
        s_t[g] = jnp.exp2(cum_last) * s_t[g] + _dot(bmt, jnp.concatenate(wx, axis=0))

    lb_all = _lower_bound_row(hlb[...], layer)

    def forget_gate(g):
        lb = lb_all[:, g * LANES:(g + 1) * LANES]
        return lb + (1.0 - lb) * _logistic(pv[:, OFF_CF + g * LANES:OFF_CF + (g + 1) * LANES])

    def mixer_c(g, gt, xall):
        gs = lanes(0, g)
        b_off = N_MXU_LEVELS * TC
        bcum = xall[b_off:b_off + TC, :]
        blast = bcum[TC - 1:TC, :]

        def level_exponent(lev):
            if lev <= N_MXU_LEVELS:
                return xall[(lev - 1) * TC:lev * TC, :]
            blk = 1 << lev
            refs = [jnp.broadcast_to(bcum[j * blk + blk // 2 - 1:j * blk + blk // 2, :], (blk, LANES))
                    for j in range(TC // blk)]
            return -jnp.abs(bcum - jnp.concatenate(refs, axis=0))
        q = _silu(pv[:, lanes(OFF_CQ, g)])
        kk = 1.0 - gt
        vv = pv[:, lanes(OFF_CI, g)]
        zeros = jnp.zeros((HG_HEAD_DIM, TC), BF16)

        def keys_by_head(kt):
            ktt = _bf(kt.T)
            return jnp.concatenate(
                [jnp.concatenate([ktt[:HG_HEAD_DIM, :], zeros], axis=1),
                 jnp.concatenate([zeros, ktt[HG_HEAD_DIM:, :]], axis=1)], axis=0)

        att = _dot(_bf(q), keys_by_head(kk)) * cmask[0]
        for lev in range(1, N_LEVELS + 1):
            e = jnp.exp2(level_exponent(lev))
            att = att + _dot(_bf(q * e), keys_by_head(kk * e)) * cmask[lev]
        v2 = jnp.concatenate([_bf(jnp.where(first_head_tc, vv, 0.0)),
                              _bf(jnp.where(first_head_tc, 0.0, vv))], axis=0)
        o = _dot(_bf(att), v2) + _dot_nt(_bf(q * jnp.exp2(bcum)), _bf(st_hg[g]))
        khat = kk * jnp.exp2(blast - bcum)
        st_hg[g] = jnp.exp2(blast) * st_hg[g] + pairmask[...] * _dot(_bf(vv.T), _bf(khat))
        ms = _dot(_bf(o * o), pairmean[...])
        ymix[:, lanes(RG_WIDTH + SSD_WIDTH, g)] = (
            o * lax.rsqrt(ms + EPS) * hnorm[:, gs] * _silu(pv[:, lanes(OFF_CG, g)]))

    for pair in range(NGRP // 2):
        g0, g1 = 2 * pair, 2 * pair + 1
        gts = [forget_gate(g0), forget_gate(g1)]
        lf = jnp.log(jnp.concatenate(gts, axis=1)) * LOG2E
        hi = _bf(lf)
        lo = _bf(lf - hi.astype(F32))
        xall2 = _dot(emat2[...], jnp.concatenate([hi, lo], axis=0))
        for g, gt in zip((g0, g1), gts):
            side_work[g]()
            mixer_a(g)
            mixer_b(g)
            mixer_c(g, gt, xall2[:, (g - g0) * LANES:(g - g0 + 1) * LANES])

    for t, rs in enumerate(tiles):
        ssq = ssq_parts[t]
        for g in range(1, NGRP):
            ssq = ssq + ssq_parts[g * len(tiles) + t]
        rinv = lax.rsqrt(jnp.sum(ssq, axis=-1, keepdims=True) * (1.0 / SSD_WIDTH) + EPS)
        for g in range(NGRP):
            ys = lanes(RG_WIDTH, g)
            ymix[rs, ys] = ymix[rs, ys] * rinv * snorm[:, lanes(0, g)]


N_MIX_PARAMS = 18
MXU_N = 256
PROJ_BOUNDS = (0, 5 * MXU_N, 9 * MXU_N, 13 * MXU_N, NPROJ)
assert len(PROJ_BOUNDS) == NGRP + 1
ROWS_PER_STEP = 2


N_POST_PARAMS = 8


def _prompt_layer_kernel(layer, final, pos0, nsteps, n_valid,
                         x_ref, xn_ref, xp_ref, p_ref, gmix, w_in_ref, *rest):
    params = rest[:N_MIX_PARAMS]
    k = N_MIX_PARAMS + N_POST_PARAMS
    wo, nffn, wup, wdown, nple, wgate, wproj, nfin = rest[N_MIX_PARAMS:k]
    h_out, rgc_o, rgh_o, sc_o, ssd_o, hg_o = rest[k:k + 6]
    proj_a, proj_b, ymix_s = rest[k + 6:k + 9]
    n_state = (len(rest) - k - 9) // ROWS_PER_STEP
    states = [rest[k + 9 + r * n_state:k + 9 + (r + 1) * n_state] for r in range(ROWS_PER_STEP)]
    s = pl.program_id(0)
    live = s < n_valid
    c = jnp.minimum(s, n_valid - 1) % nsteps

    @pl.when(s == 0)
    def _prologue():
        proj_a[...] = _dot(_bf(_rms(x_ref[0], gmix[...])), w_in_ref[...])
        ymix_s[...] = jnp.zeros_like(ymix_s)

    @pl.when(jnp.logical_and(c == 0, live))
    def _init():
        for row_states in states:
            for ref in row_states:
                ref[...] = jnp.zeros_like(ref)

    ymix_prev = _bf(ymix_s[...])
    n_ff = D_FF // FF_CHUNK
    carry = {}

    def post_piece(j):
        if j == 0:
            carry["h"] = xp_ref[...].reshape(ROWS_PER_STEP * TC, D_MODEL) + _dot(ymix_prev, wo[...])
            carry["hn"] = _bf(_rms(carry["h"], nffn[...]))
        elif j <= n_ff:
            cs = slice((j - 1) * FF_CHUNK, j * FF_CHUNK)
            z = jnp.square(jnp.maximum(_dot(carry["hn"], wup[:, cs]), 0.0))
            carry["h"] = carry["h"] + _dot(_bf(z), wdown[cs, :])
        elif j == n_ff + 1:
            h = carry["h"]
            gate = _sigmoid(_dot(_bf(_rms(h, nple[...])), wgate[...]))
            h = h + gate * _dot(_bf(p_ref[...].reshape(ROWS_PER_STEP * TC, PLE_DIM)), wproj[...])
            h = _rms(h, nfin[...]) if final else h
            h_out[...] = h.reshape(ROWS_PER_STEP, TC, D_MODEL)

    def side_work(x, dst, first_piece):
        u = _bf(_rms(x, gmix[...]))

        def slot(j):
            cs = slice(PROJ_BOUNDS[j], PROJ_BOUNDS[j + 1])
            dst[:, cs] = _dot(u, w_in_ref[:, cs])
            post_piece(first_piece + j)

        return [functools.partial(slot, j) for j in range(NGRP)]

    @pl.when(live)
    def _mix_and_side_work():
        row0 = c * TC + pos0
        _mixer_chunk(layer, row0, proj_a, ymix_s.at[0:TC],
                     side_work(x_ref[1], proj_b, 0), *params, *states[0])
        _mixer_chunk(layer, row0, proj_b, ymix_s.at[TC:2 * TC],
                     side_work(xn_ref[0], proj_a, NGRP), *params, *states[1])

    @pl.when(jnp.logical_not(live))
    def _drain():
        for j in range(n_ff + 2):
            post_piece(j)

    @pl.when(jnp.logical_and(c == nsteps - 1, live))
    def _emit():
        for r, (xea, xeb, hcar, s_t, st_hg) in enumerate(states):
            rgc_o[r] = xea[SUBLANES - 3:SUBLANES, :]
            sc_o[r] = xeb[SUBLANES - 3:SUBLANES, :]
            rgh_o[r] = hcar[0:1, :]
            for g in range(NGRP):
                ssd_o[r, g * LANES:(g + 1) * LANES, :] = s_t[g].T
                s_pair = st_hg[g].T
                for hh in range(2):
                    lo = hh * HG_HEAD_DIM
                    hg_o[r, 2 * g + hh] = s_pair[lo:lo + HG_HEAD_DIM, lo:lo + HG_HEAD_DIM]


def _prompt_layer(layer, final, x3, p_all4, gmix, w_in_bf, mp, lp, nfin, consts):
    bsz, seq, _ = x3.shape
    rps = ROWS_PER_STEP
    nsteps = seq // TC
    n_valid = (bsz // rps) * nsteps
    assert bsz % rps == 0 and seq % TC == 0
    params = [mp["caw"], mp["cab"], mp["wgp"], mp["bgp"], mp["lam"], mp["cbw"], mp["cbb"],
              mp["dtb"], mp["alog"], mp["dexp"], mp["snorm"], mp["hlb"], mp["hnorm"],
              consts["ltri2"], consts["emat2"],
              consts["cmask"], consts["pairmask"], consts["pairmean"]]
    post_params = [lp["wo"], lp["nffn"], lp["wup"], lp["wdown"], lp["nple"], lp["wgate"],
                   lp["wproj"], nfin]
    assert len(params) == N_MIX_PARAMS and len(post_params) == N_POST_PARAMS

    def mix_step(s):
        return jnp.minimum(s, n_valid - 1)

    def cur(s):
        return mix_step(s) // nsteps, mix_step(s) % nsteps, 0

    def nxt(s):
        ns = jnp.minimum(mix_step(s) + 1, n_valid - 1)
        return rps * (ns // nsteps), ns % nsteps, 0

    def prv(s):
        sp = jnp.maximum(s - 1, 0)
        return sp // nsteps, sp % nsteps, 0

    def state_idx(nd):
        return lambda s: (mix_step(s) // nsteps,) + (0,) * (nd - 1)

    out_shape = (
        jax.ShapeDtypeStruct((bsz, seq, D_MODEL), F32),
        jax.ShapeDtypeStruct((bsz, CONV_W - 1, RG_WIDTH), F32),
        jax.ShapeDtypeStruct((bsz, 1, RG_WIDTH), F32),
        jax.ShapeDtypeStruct((bsz, CONV_W - 1, SSD_CONV_DIM), F32),
        jax.ShapeDtypeStruct((bsz, SSD_WIDTH, SSD_STATE), F32),
        jax.ShapeDtypeStruct((bsz, HG_HEADS, HG_HEAD_DIM, HG_HEAD_DIM), F32),
    )
    out_specs = (
        pl.BlockSpec((rps, TC, D_MODEL), prv),
        pl.BlockSpec((rps, CONV_W - 1, RG_WIDTH), state_idx(3)),
        pl.BlockSpec((rps, 1, RG_WIDTH), state_idx(3)),
        pl.BlockSpec((rps, CONV_W - 1, SSD_CONV_DIM), state_idx(3)),
        pl.BlockSpec((rps, SSD_WIDTH, SSD_STATE), state_idx(3)),
        pl.BlockSpec((rps, HG_HEADS, HG_HEAD_DIM, HG_HEAD_DIM), state_idx(4)),
    )
    row_state = [
        pltpu.VMEM((SUBLANES, RG_WIDTH), F32),
        pltpu.VMEM((SUBLANES, SSD_CONV_DIM), F32),
        pltpu.VMEM((SUBLANES, RG_WIDTH), F32),
        pltpu.VMEM((NGRP, SSD_STATE, LANES), F32),
        pltpu.VMEM((NGRP, LANES, LANES), F32),
    ]
    scratch = [
        pltpu.VMEM((TC, NPROJ), F32),
        pltpu.VMEM((TC, NPROJ), F32),
        pltpu.VMEM((rps * TC, D_MIX), F32),
    ] + row_state * rps
    return pl.pallas_call(
        functools.partial(_prompt_layer_kernel, layer, final, 0, nsteps, n_valid),
        grid=(n_valid + 1,),
        in_specs=[pl.BlockSpec((rps, TC, D_MODEL), cur),
                  pl.BlockSpec((1, TC, D_MODEL), nxt),
                  pl.BlockSpec((rps, TC, D_MODEL), prv),
                  pl.BlockSpec((None, rps, TC, PLE_DIM), lambda s: (layer,) + prv(s)),
                  _const_spec(gmix), _const_spec(w_in_bf)]
        + [_const_spec(p) for p in params + post_params],
        out_specs=out_specs,
        out_shape=out_shape,
        scratch_shapes=scratch,
        compiler_params=pltpu.CompilerParams(
            dimension_semantics=("arbitrary",), vmem_limit_bytes=VMEM_LIMIT),
        name="prompt_layer",
    )(x3, x3, x3, p_all4, gmix, w_in_bf, *params, *post_params)


def _hg_step_kernel(layer, n_acc, proj_ref, hlb, st_ref, *rest):
    ot_ref, st_o, qh_s, f_s, k_s, v_s = rest[n_acc:]
    h = pl.program_id(0)
    if n_acc == 0:
        for other in range(DEPTH):
            if other != layer:
                st_o[other] = jnp.zeros(st_o.shape[1:], F32)
        st_o = st_o.at[layer]

    @pl.when(h == 0)
    def _prep():
        q = _silu(proj_ref[:, OFF_CQ:OFF_CQ + HG_WIDTH])
        lb = _lower_bound_row(hlb[...], layer)
        gt = lb + (1.0 - lb) * _logistic(proj_ref[:, OFF_CF:OFF_CF + HG_WIDTH])
        fdec = jnp.exp(jnp.log(gt))
        qh_s[...] = (q * fdec).T
        f_s[...] = fdec.T
        k_s[...] = (1.0 - gt).T
        v_s[...] = proj_ref[:, OFF_CI:OFF_CI + HG_WIDTH].T

    base = pl.multiple_of(h * HG_HEAD_DIM, HG_HEAD_DIM)
    v_h = v_s[pl.ds(base, HG_HEAD_DIM), :]

    def body(dk, acc):
        s = st_ref[dk]
        st_o[dk] = f_s[pl.ds(base + dk, 1), :] * s + k_s[pl.ds(base + dk, 1), :] * v_h
        return acc + qh_s[pl.ds(base + dk, 1), :] * s

    ot_ref[...] = lax.fori_loop(0, HG_HEAD_DIM, body,
                                jnp.zeros((HG_HEAD_DIM, LANES), F32), unroll=8)


def _hg_step(layer, proj, hlb, st_all, acc):
    nb = proj.shape[0]
    blk = (HG_HEAD_DIM, HG_HEAD_DIM, nb)
    st_spec = pl.BlockSpec((None, None) + blk, lambda h: (layer, h, 0, 0, 0))
    n_acc = len(acc)
    st_out_spec = st_spec if n_acc else pl.BlockSpec(
        (DEPTH, None) + blk, lambda h: (0, h, 0, 0, 0))
    return pl.pallas_call(
        functools.partial(_hg_step_kernel, layer, n_acc),
        grid=(HG_HEADS,),
        in_specs=[_const_spec(proj), _const_spec(hlb), st_spec]
        + [pl.BlockSpec(memory_space=pl.ANY)] * n_acc,
        out_specs=(pl.BlockSpec((HG_HEAD_DIM, nb), lambda h: (h, 0)), st_out_spec),
        out_shape=(jax.ShapeDtypeStruct((HG_WIDTH, nb), F32),
                   jax.ShapeDtypeStruct(st_all.shape, F32)),
        scratch_shapes=[pltpu.VMEM((HG_WIDTH, nb), F32)] * 4,
        input_output_aliases={3 + k: 1 + k for k in range(n_acc)},
        compiler_params=pltpu.CompilerParams(
            dimension_semantics=("arbitrary",), vmem_limit_bytes=VMEM_LIMIT),
        name="hg_step",
    )(proj, hlb, st_all, *acc)


def _pad_rows_t(x):
    pad = jnp.zeros((LANES - x.shape[0], x.shape[1]), F32)
    return jnp.concatenate([x, pad], axis=0).T


def _sample_mixer_kernel(layer, pos0, n_acc, proj_ref, rgc_ref, rgh_ref, sc_ref, ssd_ref, ohg_ref, *rest):
    (caw, cab, wg, bg, lam, cbw, cbb, dtb, alog, dexp, snorm,
     hlb, hnorm, expand, headsum, headmean,
     ymix_ref, rgc_o, rgh_o, sc_o, ssd_o) = rest[n_acc:]
    bb = DEC_BB
    if n_acc == 0:
        for other in range(DEPTH):
            if other != layer:
                ssd_o[other] = jnp.zeros(ssd_o.shape[1:], F32)
        ssd_o = ssd_o.at[layer]
    ax = proj_ref[:, OFF_AX:OFF_AX + RG_WIDTH]
    xa = cab[...] + caw[CONV_W - 1:CONV_W, :] * ax
    for k in range(CONV_W - 1):
        xa = xa + caw[k:k + 1, :] * rgc_ref[:, k * RG_WIDTH:(k + 1) * RG_WIDTH]
    rgc_o[:, 0:2 * RG_WIDTH] = rgc_ref[:, RG_WIDTH:3 * RG_WIDTH]
    rgc_o[:, 2 * RG_WIDTH:] = ax
    a, mult, gi = _rg_coeffs(_dot(_bf(xa), wg[...]) + bg[...], lam[...])
    if pos0 == 0:
        mult = jnp.ones_like(mult)
    h = a * rgh_ref[...] + mult * (gi * xa)
    rgh_o[...] = h
    ymix_ref[:, 0:RG_WIDTH] = h * _gelu_tanh(proj_ref[:, OFF_AG:OFF_AG + RG_WIDTH])

    bx = proj_ref[:, OFF_XBC:OFF_XBC + SSD_CONV_DIM]
    xbc = cbb[...] + cbw[CONV_W - 1:CONV_W, :] * bx
    for k in range(CONV_W - 1):
        xbc = xbc + cbw[k:k + 1, :] * sc_ref[:, k * SSD_CONV_DIM:(k + 1) * SSD_CONV_DIM]
    sc_o[:, 0:2 * SSD_CONV_DIM] = sc_ref[:, SSD_CONV_DIM:3 * SSD_CONV_DIM]
    sc_o[:, 2 * SSD_CONV_DIM:] = bx
    xbc = _silu(xbc)
    xs = xbc[:, :SSD_WIDTH]
    bm = xbc[:, SSD_WIDTH:SSD_WIDTH + SSD_STATE]
    cm = xbc[:, SSD_WIDTH + SSD_STATE:]
    dtp = _softplus(proj_ref[:, OFF_DT:OFF_DT + LANES] + dtb[...])
    v_da = dtp * (-jnp.exp(alog[...]))
    dt_e = _dot_x_c(dtp, expand[...])
    e_e = jnp.exp(_dot_x_c(v_da, expand[...]))
    dtx = dt_e * xs
    cbs = _dot(_bf(cm * bm), jnp.ones((SSD_STATE, LANES), BF16))[:, 0:1]
    dtx_t = _pad_rows_t(dtx)
    e_t = _pad_rows_t(e_e)
    c_pad = _bf(jnp.concatenate([cm, jnp.zeros((LANES - bb, SSD_STATE), F32)], axis=0))
    lane_w = lax.broadcasted_iota(jnp.int32, (SSD_WIDTH, LANES), 1)
    y_t = jnp.zeros((SSD_WIDTH, LANES), F32)
    for j in range(bb):
        s_old = ssd_ref[j].reshape(SSD_WIDTH, SSD_STATE)
        ssd_o[j] = (e_t[:, j:j + 1] * s_old + dtx_t[:, j:j + 1] * bm[j:j + 1, :]).reshape(
            SSD_HEADS, SSD_HEAD_DIM, SSD_STATE)
        y_t = y_t + jnp.where(lane_w == j, _dot_nt(_bf(s_old), c_pad), 0.0)
    y = cbs * dtx + y_t.T[0:bb, :] * e_e
    yb = y + dexp[...] * xs
    ymix_ref[:, RG_WIDTH:RG_WIDTH + SSD_WIDTH] = _rms(
        yb * _silu(proj_ref[:, OFF_BZ:OFF_BZ + SSD_WIDTH]), snorm[...])

    q = _silu(proj_ref[:, OFF_CQ:OFF_CQ + HG_WIDTH])
    lb = _lower_bound_row(hlb[...], layer)
    gt = lb + (1.0 - lb) * _logistic(proj_ref[:, OFF_CF:OFF_CF + HG_WIDTH])
    kk = 1.0 - gt
    vv = proj_ref[:, OFF_CI:OFF_CI + HG_WIDTH]
    att = _dot(_bf(q * kk), headsum[...])
    o = att * vv + ohg_ref[...]
    ms = _dot(_bf(o * o), headmean[...])
    ymix_ref[:, RG_WIDTH + SSD_WIDTH:] = (
        o * lax.rsqrt(ms + EPS) * hnorm[...] * _silu(proj_ref[:, OFF_CG:OFF_CG + HG_WIDTH]))


def _sample_mixer(layer, proj, rgc, rgh, sc, ssd_all, o_hg, acc, mp, consts):
    nb = proj.shape[0]
    bb = DEC_BB
    params = [mp["caw"], mp["cab"], mp["wg"], mp["bg"], mp["lam"], mp["cbw"], mp["cbb"],
              mp["dtb"], mp["alog"], mp["dexp"], mp["snorm"], mp["hlb"], mp["hnorm"],
              consts["expand"], consts["headsum"], consts["headmean"]]
    row2 = lambda w: pl.BlockSpec((bb, w), lambda i: (i, 0))
    small_specs = [row2(3 * RG_WIDTH), row2(RG_WIDTH), row2(3 * SSD_CONV_DIM)]
    ssd_spec = pl.BlockSpec((None, bb, SSD_HEADS, SSD_HEAD_DIM, SSD_STATE),
                            lambda i: (layer, i, 0, 0, 0))
    out_shape = (
        jax.ShapeDtypeStruct((nb, D_MIX), F32),
        jax.ShapeDtypeStruct((nb, 3 * RG_WIDTH), F32),
        jax.ShapeDtypeStruct((nb, RG_WIDTH), F32),
        jax.ShapeDtypeStruct((nb, 3 * SSD_CONV_DIM), F32),
        jax.ShapeDtypeStruct(ssd_all.shape, F32),
    )
    n_acc = len(acc)
    n_lead = 6
    ssd_out_spec = ssd_spec if n_acc else pl.BlockSpec(
        (DEPTH, bb, SSD_HEADS, SSD_HEAD_DIM, SSD_STATE), lambda i: (0, i, 0, 0, 0))
    return pl.pallas_call(
        functools.partial(_sample_mixer_kernel, layer, PAST_LEN, n_acc),
        grid=(nb // bb,),
        in_specs=[row2(NPROJ)] + small_specs + [ssd_spec, row2(HG_WIDTH)]
        + [pl.BlockSpec(memory_space=pl.ANY)] * n_acc + [_const_spec(p) for p in params],
        out_specs=tuple([row2(D_MIX)] + small_specs + [ssd_out_spec]),
        out_shape=out_shape,
        input_output_aliases={n_lead + k: 4 + k for k in range(n_acc)},
        compiler_params=pltpu.CompilerParams(
            dimension_semantics=("parallel",), vmem_limit_bytes=VMEM_LIMIT),
        name="sample_mixer",
    )(proj, rgc, rgh, sc, ssd_all, o_hg, *acc, *params)


def _post_kernel(final, h_ref, y_ref, p_ref, wo, nffn, wup, wdown, nple, wgate, wproj, nfin, o_ref):
    h = h_ref[...] + _dot(_bf(y_ref[...]), wo[...])
    hn = _bf(_rms(h, nffn[...]))
    for j in range(D_FF // FF_CHUNK):
        cs = slice(j * FF_CHUNK, (j + 1) * FF_CHUNK)
        z = jnp.square(jnp.maximum(_dot(hn, wup[:, cs]), 0.0))
        h = h + _dot(_bf(z), wdown[cs, :])
    gate = _sigmoid(_dot(_bf(_rms(h, nple[...])), wgate[...]))
    h = h + gate * _dot(_bf(p_ref[...]), wproj[...])
    if final:
        h = _rms(h, nfin[...])
    o_ref[...] = h


def _post(final, layer, h2d, ymix2d, p_all, lp, nfin, tm):
    m = h2d.shape[0]
    params = [lp["wo"], lp["nffn"], lp["wup"], lp["wdown"], lp["nple"], lp["wgate"], lp["wproj"], nfin]
    return pl.pallas_call(
        functools.partial(_post_kernel, final),
        grid=(m // tm,),
        in_specs=[pl.BlockSpec((tm, D_MODEL), lambda i: (i, 0)),
                  pl.BlockSpec((tm, D_MIX), lambda i: (i, 0)),
                  pl.BlockSpec((None, tm, PLE_DIM), lambda i: (layer, i, 0))]
        + [_const_spec(p) for p in params],
        out_specs=pl.BlockSpec((tm, D_MODEL), lambda i: (i, 0)),
        out_shape=jax.ShapeDtypeStruct((m, D_MODEL), F32),
        compiler_params=pltpu.CompilerParams(
            dimension_semantics=("parallel",), vmem_limit_bytes=VMEM_LIMIT),
        name="post_mixer",
    )(h2d, ymix2d, p_all, *params)


def _constants():
    t = np.arange(TC)
    ltri = (t[None, :] <= t[:, None]).astype(np.float32)
    expand = np.zeros((LANES, SSD_WIDTH), np.float32)
    for hd in range(SSD_HEADS):
        expand[hd, hd * SSD_HEAD_DIM:(hd + 1) * SSD_HEAD_DIM] = 1.0
    emat = np.zeros(((N_MXU_LEVELS + 1) * TC, TC), np.float32)
    for lev in range(1, N_MXU_LEVELS + 1):
        blk = 1 << lev
        mid = (t // blk) * blk + blk // 2
        upper = t >= mid
        r0 = (lev - 1) * TC
        emat[r0:r0 + TC] = (
            (upper[:, None] & (t[None, :] >= mid[:, None]) & (t[None, :] <= t[:, None]))
            | ((~upper)[:, None] & (t[None, :] > t[:, None]) & (t[None, :] < mid[:, None])))
    emat[N_MXU_LEVELS * TC:] = ltri
    cmask = np.zeros((N_LEVELS + 1, TC, 2 * TC), np.float32)
    cmask[0] = np.tile(np.eye(TC, dtype=np.float32), (1, 2))
    for lev in range(1, N_LEVELS + 1):
        blk = 1 << lev
        up = (t % blk) >= blk // 2
        m = up[:, None] & (~up)[None, :] & ((t // blk)[:, None] == (t // blk)[None, :])
        cmask[lev] = np.tile(m.astype(np.float32), (1, 2))
    hidx = np.arange(HG_WIDTH) // HG_HEAD_DIM
    bd = (hidx[:, None] == hidx[None, :]).astype(np.float32)
    pair = bd[:LANES, :LANES]
    return {
        "ltri2": jnp.asarray(np.tile(ltri, (1, 2)), BF16),
        "expand": jnp.asarray(expand, BF16),
        "emat2": jnp.asarray(np.tile(emat, (1, 2)), BF16),
        "cmask": jnp.asarray(cmask, F32),
        "pairmask": jnp.asarray(pair, F32),
        "pairmean": jnp.asarray(pair / HG_HEAD_DIM, BF16),
        "headsum": jnp.asarray(bd, BF16), "headmean": jnp.asarray(bd / HG_HEAD_DIM, BF16),
    }


def _block_diag(w):
    hh, d, _ = w.shape
    eye = jnp.eye(hh, dtype=w.dtype)
    return (eye[:, None, :, None] * w[:, :, None, :]).reshape(hh * d, hh * d)


def _pair_gate_weights(wa, wx):
    hh, d, _ = wa.shape
    pa = jnp.stack([_block_diag(wa[2 * g:2 * g + 2]) for g in range(hh // 2)])
    px = jnp.stack([_block_diag(wx[2 * g:2 * g + 2]) for g in range(hh // 2)])
    return jnp.concatenate([pa, px], axis=2)


def _pad_lanes(v):
    return jnp.pad(v, (0, LANES - v.shape[0]))[None, :]


def kernel(x_prompt, x_sample, state_rg_conv, state_rg_h, state_ssd_conv, state_ssd, state_hgrn,
           p_prompt, p_sample, norm_mix, w_in, conv_a_w, conv_a_b, rg_wa, rg_ba, rg_wx, rg_bx,
           rg_lambda, conv_b_w, conv_b_b, ssd_dt_bias, ssd_a_log, ssd_d, ssd_norm,
           hg_lower_bounds, hg_norm, w_out, norm_ffn, w_up, w_down, norm_ple, w_ple_gate,
           w_ple_proj, norm_final):
    bsz, seq, _ = x_prompt.shape
    nb = x_sample.shape[0]
    consts = _constants()
    w_in_t = jnp.swapaxes(w_in, 1, 2)
    nfin = norm_final[None, :]
    ps_all = p_sample.reshape(DEPTH, nb, PLE_DIM)

    hp = x_prompt
    hs = x_sample.reshape(nb, D_MODEL)
    st_p = [[] for _ in range(5)]
    st_s = [[] for _ in range(3)]
    hg_t_all = jnp.transpose(state_hgrn, (0, 2, 3, 4, 1))
    acc_ssd = ()
    acc_hg = ()
    for i in range(DEPTH):
        w_re = _prep_w_in(w_in_t, i)
        gmix = norm_mix[i][None, :]
        mp = {
            "caw": conv_a_w[i], "cab": conv_a_b[i][None, :],
            "wg": jnp.concatenate([_block_diag(rg_wa[i]), _block_diag(rg_wx[i])], axis=1).astype(BF16),
            "bg": jnp.concatenate([rg_ba[i], rg_bx[i]])[None, :],
            "wgp": _pair_gate_weights(rg_wa[i], rg_wx[i]).astype(BF16),
            "bgp": jnp.concatenate([rg_ba[i].reshape(NGRP, 1, LANES),
                                    rg_bx[i].reshape(NGRP, 1, LANES)], axis=2),
            "lam": rg_lambda[i][None, :],
            "cbw": conv_b_w[i], "cbb": conv_b_b[i][None, :],
            "dtb": _pad_lanes(ssd_dt_bias[i]), "alog": _pad_lanes(ssd_a_log[i]),
            "dexp": jnp.repeat(ssd_d[i], SSD_HEAD_DIM)[None, :],
            "snorm": ssd_norm[i][None, :],
            "hlb": hg_lower_bounds, "hnorm": jnp.tile(hg_norm[i], HG_HEADS)[None, :],
        }
        wo_bf, wup_bf, wdown_bf, wgate_bf, wproj_bf = _cast_layer_weights(
            i, w_out, w_up, w_down, w_ple_gate, w_ple_proj)
        lp = {
            "wo": wo_bf, "nffn": norm_ffn[i][None, :], "wup": wup_bf, "wdown": wdown_bf,
            "nple": norm_ple[i][None, :], "wgate": wgate_bf, "wproj": wproj_bf,
        }
        final = i == DEPTH - 1

        hp, rgc, rgh, sc, ssd, hg = _prompt_layer(
            i, final, hp, p_prompt, gmix, w_re, mp, lp, nfin, consts)
        for lst, s in zip(st_p, (rgc, rgh.reshape(bsz, RG_WIDTH), sc,
                                 ssd.reshape(bsz, SSD_HEADS, SSD_HEAD_DIM, SSD_STATE), hg)):
            lst.append(s)

        proj_s = _in_proj(hs, gmix, w_re, nb)
        o_hg_t, hg_acc = _hg_step(i, proj_s, hg_lower_bounds, hg_t_all, acc_hg)
        acc_hg = (hg_acc,)
        ymix_s, rgc, rgh, sc, ssd_acc = _sample_mixer(
            i, proj_s,
            state_rg_conv[i].reshape(nb, 3 * RG_WIDTH), state_rg_h[i],
            state_ssd_conv[i].reshape(nb, 3 * SSD_CONV_DIM),
            state_ssd, o_hg_t.T, acc_ssd, mp, consts)
        acc_ssd = (ssd_acc,)
        hs = _post(final, i, hs, ymix_s, ps_all, lp, nfin, nb)
        for lst, s in zip(st_s, (rgc.reshape(nb, CONV_W - 1, RG_WIDTH), rgh,
                                 sc.reshape(nb, CONV_W - 1, SSD_CONV_DIM))):
            lst.append(s)

    y_prompt = hp
    y_sample = hs.reshape(nb, 1, D_MODEL)
    outs_p = [jnp.stack(lst) for lst in st_p]
    outs_s = [jnp.stack(lst) for lst in st_s] + [
        acc_ssd[0], jnp.transpose(acc_hg[0], (0, 4, 1, 2, 3))]
    return (y_prompt, y_sample, *outs_p, *outs_s)
```

```python
import functools

import numpy as np
import jax
import jax.numpy as jnp
from jax import lax
from jax.experimental import pallas as pl
from jax.experimental.pallas import tpu as pltpu

F32 = jnp.float32
BF16 = jnp.bfloat16

D_MODEL = 1024
DEPTH = 2
PAST_LEN = 16384
PLE_DIM = 256
D_FF = 4 * D_MODEL
CONV_W = 4
EPS = 1e-6
RG_WIDTH = 512
RG_HEADS = 8
RG_HEAD_DIM = 64
RG_C = 8.0
SSD_WIDTH = 512
SSD_HEAD_DIM = 64
SSD_HEADS = 8
SSD_STATE = 128
SSD_CONV_DIM = SSD_WIDTH + 2 * SSD_STATE
HG_WIDTH = 512
HG_HEADS = 8
HG_HEAD_DIM = 64
D_MIX = RG_WIDTH + SSD_WIDTH + HG_WIDTH

LOG2E = 1.4426950408889634
LANES = 128
SUBLANES = 8
NGRP = RG_WIDTH // LANES

OFF_AX = 0
OFF_AG = 512
OFF_BZ = 1024
OFF_XBC = 1536
OFF_CQ = 2304
OFF_CF = 2816
OFF_CI = 3328
OFF_CG = 3840
OFF_DT = 4352
NPROJ = OFF_DT + LANES
DT_COL = 2 * RG_WIDTH + SSD_WIDTH + SSD_CONV_DIM
D_IN_PROJ = DT_COL + SSD_HEADS + 4 * HG_WIDTH

TC = 128
HALF = TC // 2
N_LEVELS = 7
N_MXU_LEVELS = 3
DEC_BB = 16
FF_CHUNK = 1024
VMEM_LIMIT = 58 * 1024 * 1024


def _bf(x):
    return x.astype(BF16)


def _dot(a, b):
    return jnp.dot(a, b, preferred_element_type=F32)


def _dot_nt(a, b):
    return lax.dot_general(a, b, (((1,), (1,)), ((), ())), preferred_element_type=F32)


def _split3(x):
    hi = _bf(x)
    r1 = x - hi.astype(F32)
    mid = _bf(r1)
    lo = _bf(r1 - mid.astype(F32))
    return hi, mid, lo


def _dot_x_c(x, c):
    hi, mid, lo = _split3(x)
    return _dot(hi, c) + _dot(mid, c) + _dot(lo, c)


def _split2(x):
    hi = _bf(x)
    return hi, _bf(x - hi.astype(F32))


def _logistic(x):
    return 1.0 / (1.0 + jnp.exp(-x))


def _sigmoid(x):
    return 0.5 * jnp.tanh(0.5 * x) + 0.5


def _silu(x):
    h = 0.5 * x
    return h * jnp.tanh(h) + h


def _softplus(x):
    return jnp.maximum(x, 0.0) + jnp.log1p(jnp.exp(-jnp.abs(x)))


def _gelu_tanh(x):
    c = np.float32(np.sqrt(2.0 / np.pi))
    return 0.5 * x * (1.0 + jnp.tanh(c * (x + 0.044715 * (x * x * x))))


def _rms(x, gain):
    return x * lax.rsqrt(jnp.mean(x * x, axis=-1, keepdims=True) + EPS) * gain


def _lower_bound_row(hlb, layer):
    m = jnp.max(hlb, axis=0, keepdims=True)
    e = jnp.exp(hlb - m)
    sm = e / jnp.sum(e, axis=0, keepdims=True)
    lb = jnp.zeros((1, HG_WIDTH), F32)
    for j in range(1, layer + 1):
        lb = lb + sm[j:j + 1, :]
    return lb


def _rg_coeffs(gates, lam_row):
    w = gates.shape[1] // 2
    r = _sigmoid(gates[:, :w])
    i = _sigmoid(gates[:, w:])
    log_a = -RG_C * r * _softplus(-lam_row)
    a = jnp.exp(log_a)
    mult = jnp.sqrt(-jnp.tanh(log_a) * (a * a + 1.0))
    return a, mult, i


def _const_spec(arr):
    nd = arr.ndim
    return pl.BlockSpec(arr.shape, lambda *_: (0,) * nd, pipeline_mode=pl.Buffered(1))


CAST_STEPS = 8


def _cast_kernel(*refs):
    n = len(refs) // 2
    _win_kernel(refs[0], refs[n])
    for w_ref, o_ref in zip(refs[1:n], refs[n + 1:]):
        o_ref[...] = _bf(w_ref[...])


def _prep_layer_weights(layer, wt_in_all, *w_alls):
    assert D_MODEL // LANES == CAST_STEPS
    in_specs = [pl.BlockSpec((None, D_IN_PROJ, LANES), lambda i: (layer, 0, i))]
    out_specs = [pl.BlockSpec((LANES, NPROJ), lambda i: (i, 0))]
    out_shape = [jax.ShapeDtypeStruct((D_MODEL, NPROJ), BF16)]
    for w_all in w_alls:
        _, rows, cols = w_all.shape
        rb = rows // CAST_STEPS
        assert rb * CAST_STEPS == rows and rb % (2 * SUBLANES) == 0
        in_specs.append(pl.BlockSpec((None, rb, cols), lambda i: (layer, i, 0)))
        out_specs.append(pl.BlockSpec((rb, cols), lambda i: (i, 0)))
        out_shape.append(jax.ShapeDtypeStruct((rows, cols), BF16))
    return pl.pallas_call(
        _cast_kernel,
        grid=(CAST_STEPS,),
        in_specs=in_specs,
        out_specs=tuple(out_specs),
        out_shape=tuple(out_shape),
        compiler_params=pltpu.CompilerParams(
            dimension_semantics=("parallel",), vmem_limit_bytes=VMEM_LIMIT),
        name="prep_weights",
    )(wt_in_all, *w_alls)


def _win_kernel(wt_ref, o_ref):
    o_ref[:, 0:OFF_CQ] = _bf(wt_ref[0:DT_COL, :].T)
    o_ref[:, OFF_CQ:OFF_DT] = _bf(wt_ref[DT_COL + SSD_HEADS:D_IN_PROJ, :].T)
    dt = jnp.concatenate([wt_ref[DT_COL:DT_COL + SSD_HEADS, :],
                          jnp.zeros((LANES - SSD_HEADS, LANES), F32)], axis=0)
    o_ref[:, OFF_DT:NPROJ] = _bf(dt.T)


def _proj_kernel(x_ref, g_ref, w_ref, o_ref):
    u = _rms(x_ref[...], g_ref[...])
    o_ref[...] = _dot(_bf(u), w_ref[...])


def _in_proj(x2d, gain, w_bf, tm):
    m = x2d.shape[0]
    return pl.pallas_call(
        _proj_kernel,
        grid=(m // tm,),
        in_specs=[pl.BlockSpec((tm, D_MODEL), lambda i: (i, 0)),
                  _const_spec(gain), _const_spec(w_bf)],
        out_specs=pl.BlockSpec((tm, NPROJ), lambda i: (i, 0)),
        out_shape=jax.ShapeDtypeStruct((m, NPROJ), F32),
        compiler_params=pltpu.CompilerParams(
            dimension_semantics=("parallel",), vmem_limit_bytes=VMEM_LIMIT),
        name="in_proj",
    )(x2d, gain, w_bf)


def _mixer_chunk(layer, row0, pv, ymix, side_work,
                 caw, cab, wgp, bgp, lam, cbw, cbb, dtb, alog, dexp, snorm,
                 hlb, hnorm, ltri2, emat2, cmask, pairmask, pairmean,
                 xea, xeb, hcar, s_t, st_hg):
    row_h = lax.broadcasted_iota(jnp.int32, (HALF, 1), 0)
    sub_h = row_h & (SUBLANES - 1)
    first_head = lax.broadcasted_iota(jnp.int32, (HALF, LANES), 1) < HG_HEAD_DIM
    first_head_tc = lax.broadcasted_iota(jnp.int32, (TC, LANES), 1) < HG_HEAD_DIM
    col_tc = lax.broadcasted_iota(jnp.int32, (HALF, TC), 1)
    row_tc = lax.broadcasted_iota(jnp.int32, (HALF, TC), 0)

    tiles = [slice(t * HALF, (t + 1) * HALF) for t in range(TC // HALF)]

    def lanes(off, g):
        return slice(off + g * LANES, off + (g + 1) * LANES)

    def conv_rows(prev8, x, w_ref, b_ref, gs):
        xcat = jnp.concatenate([prev8, x], axis=0)
        y = b_ref[:, gs] + w_ref[CONV_W - 1:CONV_W, gs] * x
        for j in range(1, CONV_W):
            y = y + w_ref[CONV_W - 1 - j:CONV_W - j, gs] * pltpu.roll(xcat, j, 0)[SUBLANES:, :]
        return y, x[x.shape[0] - SUBLANES:, :]

    def conv(tail_ref, w_ref, b_ref, src_off, g):
        gs = lanes(0, g)
        y, tail = conv_rows(tail_ref[:, gs], pv[:, lanes(src_off, g)], w_ref, b_ref, gs)
        tail_ref[:, gs] = tail
        return y

    def mixer_a(g):
        gs = lanes(0, g)
        prev8 = xea[:, gs]
        carry = hcar[0:1, gs]
        xas = []
        for rs in tiles:
            xa, prev8 = conv_rows(prev8, pv[rs, lanes(OFF_AX, g)], caw, cab, gs)
            xas.append(xa)
        gates = _dot(_bf(jnp.concatenate(xas, axis=0)), wgp[g]) + bgp[g]
        for t, rs in enumerate(tiles):
            xa = xas[t]
            a, mult, gi = _rg_coeffs(gates[rs, :], lam[:, gs])
            mult = jnp.where(row_h + (row0 + t * HALF) == 0, 1.0, mult)
            b = mult * (gi * xa)
            k = 1
            while k < SUBLANES:
                keep = sub_h >= k
                a_sh = jnp.where(keep, pltpu.roll(a, k, 0), 1.0)
                b_sh = jnp.where(keep, pltpu.roll(b, k, 0), 0.0)
                b = b + a * b_sh
                a = a * a_sh
                k *= 2
            slabs = []
            for r in range(HALF // SUBLANES):
                sl = slice(r * SUBLANES, (r + 1) * SUBLANES)
                h_r = a[sl, :] * carry + b[sl, :]
                slabs.append(h_r)
                carry = h_r[SUBLANES - 1:SUBLANES, :]
            ymix[rs, gs] = jnp.concatenate(slabs, axis=0) * _gelu_tanh(
                pv[rs, lanes(OFF_AG, g)])
        xea[:, gs] = prev8
        hcar[0:1, gs] = carry

    bm = _silu(conv(xeb, cbw, cbb, OFF_XBC, NGRP))
    cm = _silu(conv(xeb, cbw, cbb, OFF_XBC, NGRP + 1))
    dtp = _softplus(pv[:, OFF_DT:OFF_DT + LANES] + dtb[...])
    v_da = dtp * (-LOG2E * jnp.exp(alog[...]))
    cum = _dot(ltri2[...], jnp.concatenate(_split2(v_da), axis=0))
    cum_t = cum.T

    def per_head(arr, g, rs):
        rows = arr[rs, :]
        return jnp.where(first_head[0:rows.shape[0], :], rows[:, 2 * g:2 * g + 1], rows[:, 2 * g + 1:2 * g + 2])
    cb = _dot_nt(_bf(cm), _bf(bm))
    cmb = _bf(cm)
    bmt = _bf(bm.T)
    ssq_parts = []

    def mixer_b(g):
        gs = lanes(0, g)
        cum_last = per_head(cum, g, slice(TC - 1, TC))
        s_old = _bf(s_t[g])
        prev8 = xeb[:, gs]
        xs_t, x2a, x2b, wx, cum_g = [], [], [], [], []
        for rs in tiles:
            xs, prev8 = conv_rows(prev8, pv[rs, lanes(OFF_XBC, g)], cbw, cbb, gs)
            xs = _silu(xs)
            dt_g = per_head(dtp, g, rs)
            cum_g.append(per_head(cum, g, rs))
            dtx = xs * dt_g
            xs_t.append(xs)
            x2a.append(_bf(jnp.where(first_head, dtx, 0.0)))
            x2b.append(_bf(jnp.where(first_head, 0.0, dtx)))
            wx.append(_bf(jnp.exp2(cum_last - cum_g[-1]) * dtx))
        xeb[:, gs] = prev8
        x2 = jnp.concatenate(x2a + x2b, axis=0)
        g_rows = []
        for t, rs in enumerate(tiles):
            gmats = []
            for hh in range(2):
                hd = 2 * g + hh
                seg = cum[rs, hd:hd + 1] - cum_t[hd:hd + 1, :]
                gmats.append(_bf(cb[rs, :] * jnp.exp2(jnp.where(col_tc <= row_tc + t * HALF, seg, -jnp.inf))))
            g_rows.append(jnp.concatenate(gmats, axis=1))
        y_all = (_dot(jnp.concatenate(g_rows, axis=0), x2)
                 + _dot(cmb, s_old) * jnp.exp2(jnp.concatenate(cum_g, axis=0)))
        for t, rs in enumerate(tiles):
            yz = (y_all[rs, :] + dexp[:, gs] * xs_t[t]) * _silu(pv[rs, lanes(OFF_BZ, g)])
            ssq_parts.append(yz * yz)
            ymix[rs, lanes(RG_WIDTH, g)] = yz
        s_t[g] = jnp.exp2(cum_last) * s_t[g] + _dot(bmt, jnp.concatenate(wx, axis=0))

    lb_all = _lower_bound_row(hlb[...], layer)

    def forget_gate(g):
        lb = lb_all[:, g * LANES:(g + 1) * LANES]
        return lb + (1.0 - lb) * _logistic(pv[:, OFF_CF + g * LANES:OFF_CF + (g + 1) * LANES])

    def mixer_c(g, gt, xall):
        gs = lanes(0, g)
        b_off = N_MXU_LEVELS * TC
        bcum = xall[b_off:b_off + TC, :]
        blast = bcum[TC - 1:TC, :]

        def level_exponent(lev):
            if lev <= N_MXU_LEVELS:
                return xall[(lev - 1) * TC:lev * TC, :]
            blk = 1 << lev
            refs = [jnp.broadcast_to(bcum[j * blk + blk // 2 - 1:j * blk + blk // 2, :], (blk, LANES))
                    for j in range(TC // blk)]
            return -jnp.abs(bcum - jnp.concatenate(refs, axis=0))
        q = _silu(pv[:, lanes(OFF_CQ, g)])
        kk = 1.0 - gt
        vv = pv[:, lanes(OFF_CI, g)]
        zeros = jnp.zeros((HG_HEAD_DIM, TC), BF16)

        def keys_by_head(kt):
            ktt = _bf(kt.T)
            return jnp.concatenate(
                [jnp.concatenate([ktt[:HG_HEAD_DIM, :], zeros], axis=1),
                 jnp.concatenate([zeros, ktt[HG_HEAD_DIM:, :]], axis=1)], axis=0)

        att = _dot(_bf(q), keys_by_head(kk)) * cmask[0]
        for lev in range(1, N_LEVELS + 1):
            e = jnp.exp2(level_exponent(lev))
            att = att + _dot(_bf(q * e), keys_by_head(kk * e)) * cmask[lev]
        v2 = jnp.concatenate([_bf(jnp.where(first_head_tc, vv, 0.0)),
                              _bf(jnp.where(first_head_tc, 0.0, vv))], axis=0)
        o = _dot(_bf(att), v2) + _dot_nt(_bf(q * jnp.exp2(bcum)), _bf(st_hg[g]))
        khat = kk * jnp.exp2(blast - bcum)
        st_hg[g] = jnp.exp2(blast) * st_hg[g] + pairmask[...] * _dot(_bf(vv.T), _bf(khat))
        ms = _dot(_bf(o * o), pairmean[...])
        ymix[:, lanes(RG_WIDTH + SSD_WIDTH, g)] = (
            o * lax.rsqrt(ms + EPS) * hnorm[:, gs] * _silu(pv[:, lanes(OFF_CG, g)]))

    for pair in range(NGRP // 2):
        g0, g1 = 2 * pair, 2 * pair + 1
        gts = [forget_gate(g0), forget_gate(g1)]
        lf = jnp.log(jnp.concatenate(gts, axis=1)) * LOG2E
        hi = _bf(lf)
        lo = _bf(lf - hi.astype(F32))
        xall2 = _dot(emat2[...], jnp.concatenate([hi, lo], axis=0))
        for g, gt in zip((g0, g1), gts):
            side_work[g]()
            mixer_a(g)
            mixer_b(g)
            mixer_c(g, gt, xall2[:, (g - g0) * LANES:(g - g0 + 1) * LANES])

    for t, rs in enumerate(tiles):
        ssq = ssq_parts[t]
        for g in range(1, NGRP):
            ssq = ssq + ssq_parts[g * len(tiles) + t]
        rinv = lax.rsqrt(jnp.sum(ssq, axis=-1, keepdims=True) * (1.0 / SSD_WIDTH) + EPS)
        for g in range(NGRP):
            ys = lanes(RG_WIDTH, g)
            ymix[rs, ys] = ymix[rs, ys] * rinv * snorm[:, lanes(0, g)]


N_MIX_PARAMS = 18
MXU_N = 256
PROJ_BOUNDS = (0, 5 * MXU_N, 9 * MXU_N, 13 * MXU_N, NPROJ)
assert len(PROJ_BOUNDS) == NGRP + 1
ROWS_PER_STEP = 2


N_POST_PARAMS = 8


def _prompt_layer_kernel(layer, final, pos0, nsteps, n_valid,
                         x_ref, xn_ref, xp_ref, p_ref, gmix, w_in_ref, *rest):
    params = rest[:N_MIX_PARAMS]
    k = N_MIX_PARAMS + N_POST_PARAMS
    wo, nffn, wup, wdown, nple, wgate, wproj, nfin = rest[N_MIX_PARAMS:k]
    h_out, rgc_o, rgh_o, sc_o, ssd_o, hg_o = rest[k:k + 6]
    proj_a, proj_b, ymix_s = rest[k + 6:k + 9]
    n_state = (len(rest) - k - 9) // ROWS_PER_STEP
    states = [rest[k + 9 + r * n_state:k + 9 + (r + 1) * n_state] for r in range(ROWS_PER_STEP)]
    s = pl.program_id(0)
    live = s < n_valid
    c = jnp.minimum(s, n_valid - 1) % nsteps

    @pl.when(s == 0)
    def _prologue():
        proj_a[...] = _dot(_bf(_rms(x_ref[0], gmix[...])), w_in_ref[...])
        ymix_s[...] = jnp.zeros_like(ymix_s)

    @pl.when(jnp.logical_and(c == 0, live))
    def _init():
        for row_states in states:
            for ref in row_states:
                ref[...] = jnp.zeros_like(ref)

    ymix_prev = _bf(ymix_s[...])
    n_ff = D_FF // FF_CHUNK
    carry = {}

    def post_piece(j):
        if j == 0:
            carry["h"] = xp_ref[...].reshape(ROWS_PER_STEP * TC, D_MODEL) + _dot(ymix_prev, wo[...])
            carry["hn"] = _bf(_rms(carry["h"], nffn[...]))
        elif j <= n_ff:
            cs = slice((j - 1) * FF_CHUNK, j * FF_CHUNK)
            z = jnp.square(jnp.maximum(_dot(carry["hn"], wup[:, cs]), 0.0))
            carry["h"] = carry["h"] + _dot(_bf(z), wdown[cs, :])
        elif j == n_ff + 1:
            h = carry["h"]
            gate = _sigmoid(_dot(_bf(_rms(h, nple[...])), wgate[...]))
            h = h + gate * _dot(_bf(p_ref[...].reshape(ROWS_PER_STEP * TC, PLE_DIM)), wproj[...])
            h = _rms(h, nfin[...]) if final else h
            h_out[...] = h.reshape(ROWS_PER_STEP, TC, D_MODEL)

    def side_work(x, dst, first_piece):
        u = _bf(_rms(x, gmix[...]))

        def slot(j):
            cs = slice(PROJ_BOUNDS[j], PROJ_BOUNDS[j + 1])
            dst[:, cs] = _dot(u, w_in_ref[:, cs])
            post_piece(first_piece + j)

        return [functools.partial(slot, j) for j in range(NGRP)]

    @pl.when(live)
    def _mix_and_side_work():
        row0 = c * TC + pos0
        _mixer_chunk(layer, row0, proj_a, ymix_s.at[0:TC],
                     side_work(x_ref[1], proj_b, 0), *params, *states[0])
        _mixer_chunk(layer, row0, proj_b, ymix_s.at[TC:2 * TC],
                     side_work(xn_ref[0], proj_a, NGRP), *params, *states[1])

    @pl.when(jnp.logical_not(live))
    def _drain():
        for j in range(n_ff + 2):
            post_piece(j)

    @pl.when(jnp.logical_and(c == nsteps - 1, live))
    def _emit():
        for r, (xea, xeb, hcar, s_t, st_hg) in enumerate(states):
            rgc_o[r] = xea[SUBLANES - 3:SUBLANES, :]
            sc_o[r] = xeb[SUBLANES - 3:SUBLANES, :]
            rgh_o[r] = hcar[0:1, :]
            for g in range(NGRP):
                ssd_o[r, g * LANES:(g + 1) * LANES, :] = s_t[g].T
                s_pair = st_hg[g].T
                for hh in range(2):
                    lo = hh * HG_HEAD_DIM
                    hg_o[r, 2 * g + hh] = s_pair[lo:lo + HG_HEAD_DIM, lo:lo + HG_HEAD_DIM]


def _prompt_layer(layer, final, x3, p_all4, gmix, w_in_bf, mp, lp, nfin, consts):
    bsz, seq, _ = x3.shape
    rps = ROWS_PER_STEP
    nsteps = seq // TC
    n_valid = (bsz // rps) * nsteps
    assert bsz % rps == 0 and seq % TC == 0
    params = [mp["caw"], mp["cab"], mp["wgp"], mp["bgp"], mp["lam"], mp["cbw"], mp["cbb"],
              mp["dtb"], mp["alog"], mp["dexp"], mp["snorm"], mp["hlb"], mp["hnorm"],
              consts["ltri2"], consts["emat2"],
              consts["cmask"], consts["pairmask"], consts["pairmean"]]
    post_params = [lp["wo"], lp["nffn"], lp["wup"], lp["wdown"], lp["nple"], lp["wgate"],
                   lp["wproj"], nfin]
    assert len(params) == N_MIX_PARAMS and len(post_params) == N_POST_PARAMS

    def mix_step(s):
        return jnp.minimum(s, n_valid - 1)

    def cur(s):
        return mix_step(s) // nsteps, mix_step(s) % nsteps, 0

    def nxt(s):
        ns = jnp.minimum(mix_step(s) + 1, n_valid - 1)
        return rps * (ns // nsteps), ns % nsteps, 0

    def prv(s):
        sp = jnp.maximum(s - 1, 0)
        return sp // nsteps, sp % nsteps, 0

    def state_idx(nd):
        return lambda s: (mix_step(s) // nsteps,) + (0,) * (nd - 1)

    out_shape = (
        jax.ShapeDtypeStruct((bsz, seq, D_MODEL), F32),
        jax.ShapeDtypeStruct((bsz, CONV_W - 1, RG_WIDTH), F32),
        jax.ShapeDtypeStruct((bsz, 1, RG_WIDTH), F32),
        jax.ShapeDtypeStruct((bsz, CONV_W - 1, SSD_CONV_DIM), F32),
        jax.ShapeDtypeStruct((bsz, SSD_WIDTH, SSD_STATE), F32),
        jax.ShapeDtypeStruct((bsz, HG_HEADS, HG_HEAD_DIM, HG_HEAD_DIM), F32),
    )
    out_specs = (
        pl.BlockSpec((rps, TC, D_MODEL), prv),
        pl.BlockSpec((rps, CONV_W - 1, RG_WIDTH), state_idx(3)),
        pl.BlockSpec((rps, 1, RG_WIDTH), state_idx(3)),
        pl.BlockSpec((rps, CONV_W - 1, SSD_CONV_DIM), state_idx(3)),
        pl.BlockSpec((rps, SSD_WIDTH, SSD_STATE), state_idx(3)),
        pl.BlockSpec((rps, HG_HEADS, HG_HEAD_DIM, HG_HEAD_DIM), state_idx(4)),
    )
    row_state = [
        pltpu.VMEM((SUBLANES, RG_WIDTH), F32),
        pltpu.VMEM((SUBLANES, SSD_CONV_DIM), F32),
        pltpu.VMEM((SUBLANES, RG_WIDTH), F32),
        pltpu.VMEM((NGRP, SSD_STATE, LANES), F32),
        pltpu.VMEM((NGRP, LANES, LANES), F32),
    ]
    scratch = [
        pltpu.VMEM((TC, NPROJ), F32),
        pltpu.VMEM((TC, NPROJ), F32),
        pltpu.VMEM((rps * TC, D_MIX), F32),
    ] + row_state * rps
    return pl.pallas_call(
        functools.partial(_prompt_layer_kernel, layer, final, 0, nsteps, n_valid),
        grid=(n_valid + 1,),
        in_specs=[pl.BlockSpec((rps, TC, D_MODEL), cur),
                  pl.BlockSpec((1, TC, D_MODEL), nxt),
                  pl.BlockSpec((rps, TC, D_MODEL), prv),
                  pl.BlockSpec((None, rps, TC, PLE_DIM), lambda s: (layer,) + prv(s)),
                  _const_spec(gmix), _const_spec(w_in_bf)]
        + [_const_spec(p) for p in params + post_params],
        out_specs=out_specs,
        out_shape=out_shape,
        scratch_shapes=scratch,
        compiler_params=pltpu.CompilerParams(
            dimension_semantics=("arbitrary",), vmem_limit_bytes=VMEM_LIMIT),
        name="prompt_layer",
    )(x3, x3, x3, p_all4, gmix, w_in_bf, *params, *post_params)


def _hg_step_kernel(layer, n_acc, proj_ref, hlb, st_ref, *rest):
    ot_ref, st_o, qh_s, f_s, k_s, v_s = rest[n_acc:]
    h = pl.program_id(0)
    if n_acc == 0:
        for other in range(DEPTH):
            if other != layer:
                st_o[other] = jnp.zeros(st_o.shape[1:], F32)
        st_o = st_o.at[layer]

    @pl.when(h == 0)
    def _prep():
        q = _silu(proj_ref[:, OFF_CQ:OFF_CQ + HG_WIDTH])
        lb = _lower_bound_row(hlb[...], layer)
        gt = lb + (1.0 - lb) * _logistic(proj_ref[:, OFF_CF:OFF_CF + HG_WIDTH])
        fdec = jnp.exp(jnp.log(gt))
        qh_s[...] = (q * fdec).T
        f_s[...] = fdec.T
        k_s[...] = (1.0 - gt).T
        v_s[...] = proj_ref[:, OFF_CI:OFF_CI + HG_WIDTH].T

    base = pl.multiple_of(h * HG_HEAD_DIM, HG_HEAD_DIM)
    v_h = v_s[pl.ds(base, HG_HEAD_DIM), :]

    def body(dk, acc):
        s = st_ref[dk]
        st_o[dk] = f_s[pl.ds(base + dk, 1), :] * s + k_s[pl.ds(base + dk, 1), :] * v_h
        return acc + qh_s[pl.ds(base + dk, 1), :] * s

    ot_ref[...] = lax.fori_loop(0, HG_HEAD_DIM, body,
                                jnp.zeros((HG_HEAD_DIM, LANES), F32), unroll=8)


def _hg_step(layer, proj, hlb, st_all, acc):
    nb = proj.shape[0]
    blk = (HG_HEAD_DIM, HG_HEAD_DIM, nb)
    st_spec = pl.BlockSpec((None, None) + blk, lambda h: (layer, h, 0, 0, 0))
    n_acc = len(acc)
    st_out_spec = st_spec if n_acc else pl.BlockSpec(
        (DEPTH, None) + blk, lambda h: (0, h, 0, 0, 0))
    return pl.pallas_call(
        functools.partial(_hg_step_kernel, layer, n_acc),
        grid=(HG_HEADS,),
        in_specs=[_const_spec(proj), _const_spec(hlb), st_spec]
        + [pl.BlockSpec(memory_space=pl.ANY)] * n_acc,
        out_specs=(pl.BlockSpec((HG_HEAD_DIM, nb), lambda h: (h, 0)), st_out_spec),
        out_shape=(jax.ShapeDtypeStruct((HG_WIDTH, nb), F32),
                   jax.ShapeDtypeStruct(st_all.shape, F32)),
        scratch_shapes=[pltpu.VMEM((HG_WIDTH, nb), F32)] * 4,
        input_output_aliases={3 + k: 1 + k for k in range(n_acc)},
        compiler_params=pltpu.CompilerParams(
            dimension_semantics=("arbitrary",), vmem_limit_bytes=VMEM_LIMIT),
        name="hg_step",
    )(proj, hlb, st_all, *acc)


def _pad_rows_t(x):
    pad = jnp.zeros((LANES - x.shape[0], x.shape[1]), F32)
    return jnp.concatenate([x, pad], axis=0).T


def _sample_mixer_kernel(layer, pos0, n_acc, proj_ref, rgc_ref, rgh_ref, sc_ref, ssd_ref, ohg_ref, *rest):
    (caw, cab, wg, bg, lam, cbw, cbb, dtb, alog, dexp, snorm,
     hlb, hnorm, expand, headsum, headmean,
     ymix_ref, rgc_o, rgh_o, sc_o, ssd_o) = rest[n_acc:]
    bb = DEC_BB
    if n_acc == 0:
        for other in range(DEPTH):
            if other != layer:
                ssd_o[other] = jnp.zeros(ssd_o.shape[1:], F32)
        ssd_o = ssd_o.at[layer]
    ax = proj_ref[:, OFF_AX:OFF_AX + RG_WIDTH]
    xa = cab[...] + caw[CONV_W - 1:CONV_W, :] * ax
    for k in range(CONV_W - 1):
        xa = xa + caw[k:k + 1, :] * rgc_ref[:, k * RG_WIDTH:(k + 1) * RG_WIDTH]
    rgc_o[:, 0:2 * RG_WIDTH] = rgc_ref[:, RG_WIDTH:3 * RG_WIDTH]
    rgc_o[:, 2 * RG_WIDTH:] = ax
    a, mult, gi = _rg_coeffs(_dot(_bf(xa), wg[...]) + bg[...], lam[...])
    if pos0 == 0:
        mult = jnp.ones_like(mult)
    h = a * rgh_ref[...] + mult * (gi * xa)
    rgh_o[...] = h
    ymix_ref[:, 0:RG_WIDTH] = h * _gelu_tanh(proj_ref[:, OFF_AG:OFF_AG + RG_WIDTH])

    bx = proj_ref[:, OFF_XBC:OFF_XBC + SSD_CONV_DIM]
    xbc = cbb[...] + cbw[CONV_W - 1:CONV_W, :] * bx
    for k in range(CONV_W - 1):
        xbc = xbc + cbw[k:k + 1, :] * sc_ref[:, k * SSD_CONV_DIM:(k + 1) * SSD_CONV_DIM]
    sc_o[:, 0:2 * SSD_CONV_DIM] = sc_ref[:, SSD_CONV_DIM:3 * SSD_CONV_DIM]
    sc_o[:, 2 * SSD_CONV_DIM:] = bx
    xbc = _silu(xbc)
    xs = xbc[:, :SSD_WIDTH]
    bm = xbc[:, SSD_WIDTH:SSD_WIDTH + SSD_STATE]
    cm = xbc[:, SSD_WIDTH + SSD_STATE:]
    dtp = _softplus(proj_ref[:, OFF_DT:OFF_DT + LANES] + dtb[...])
    v_da = dtp * (-jnp.exp(alog[...]))
    dt_e = _dot_x_c(dtp, expand[...])
    e_e = jnp.exp(_dot_x_c(v_da, expand[...]))
    dtx = dt_e * xs
    cbs = _dot(_bf(cm * bm), jnp.ones((SSD_STATE, LANES), BF16))[:, 0:1]
    dtx_t = _pad_rows_t(dtx)
    e_t = _pad_rows_t(e_e)
    c_pad = _bf(jnp.concatenate([cm, jnp.zeros((LANES - bb, SSD_STATE), F32)], axis=0))
    lane_w = lax.broadcasted_iota(jnp.int32, (SSD_WIDTH, LANES), 1)
    y_t = jnp.zeros((SSD_WIDTH, LANES), F32)
    for j in range(bb):
        s_old = ssd_ref[j].reshape(SSD_WIDTH, SSD_STATE)
        ssd_o[j] = (e_t[:, j:j + 1] * s_old + dtx_t[:, j:j + 1] * bm[j:j + 1, :]).reshape(
            SSD_HEADS, SSD_HEAD_DIM, SSD_STATE)
        y_t = y_t + jnp.where(lane_w == j, _dot_nt(_bf(s_old), c_pad), 0.0)
    y = cbs * dtx + y_t.T[0:bb, :] * e_e
    yb = y + dexp[...] * xs
    ymix_ref[:, RG_WIDTH:RG_WIDTH + SSD_WIDTH] = _rms(
        yb * _silu(proj_ref[:, OFF_BZ:OFF_BZ + SSD_WIDTH]), snorm[...])

    q = _silu(proj_ref[:, OFF_CQ:OFF_CQ + HG_WIDTH])
    lb = _lower_bound_row(hlb[...], layer)
    gt = lb + (1.0 - lb) * _logistic(proj_ref[:, OFF_CF:OFF_CF + HG_WIDTH])
    kk = 1.0 - gt
    vv = proj_ref[:, OFF_CI:OFF_CI + HG_WIDTH]
    att = _dot(_bf(q * kk), headsum[...])
    o = att * vv + ohg_ref[...]
    ms = _dot(_bf(o * o), headmean[...])
    ymix_ref[:, RG_WIDTH + SSD_WIDTH:] = (
        o * lax.rsqrt(ms + EPS) * hnorm[...] * _silu(proj_ref[:, OFF_CG:OFF_CG + HG_WIDTH]))


def _sample_mixer(layer, proj, rgc, rgh, sc, ssd_all, o_hg, acc, mp, consts):
    nb = proj.shape[0]
    bb = DEC_BB
    params = [mp["caw"], mp["cab"], mp["wg"], mp["bg"], mp["lam"], mp["cbw"], mp["cbb"],
              mp["dtb"], mp["alog"], mp["dexp"], mp["snorm"], mp["hlb"], mp["hnorm"],
              consts["expand"], consts["headsum"], consts["headmean"]]
    row2 = lambda w: pl.BlockSpec((bb, w), lambda i: (i, 0))
    small_specs = [row2(3 * RG_WIDTH), row2(RG_WIDTH), row2(3 * SSD_CONV_DIM)]
    ssd_spec = pl.BlockSpec((None, bb, SSD_HEADS, SSD_HEAD_DIM, SSD_STATE),
                            lambda i: (layer, i, 0, 0, 0))
    out_shape = (
        jax.ShapeDtypeStruct((nb, D_MIX), F32),
        jax.ShapeDtypeStruct((nb, 3 * RG_WIDTH), F32),
        jax.ShapeDtypeStruct((nb, RG_WIDTH), F32),
        jax.ShapeDtypeStruct((nb, 3 * SSD_CONV_DIM), F32),
        jax.ShapeDtypeStruct(ssd_all.shape, F32),
    )
    n_acc = len(acc)
    n_lead = 6
    ssd_out_spec = ssd_spec if n_acc else pl.BlockSpec(
        (DEPTH, bb, SSD_HEADS, SSD_HEAD_DIM, SSD_STATE), lambda i: (0, i, 0, 0, 0))
    return pl.pallas_call(
        functools.partial(_sample_mixer_kernel, layer, PAST_LEN, n_acc),
        grid=(nb // bb,),
        in_specs=[row2(NPROJ)] + small_specs + [ssd_spec, row2(HG_WIDTH)]
        + [pl.BlockSpec(memory_space=pl.ANY)] * n_acc + [_const_spec(p) for p in params],
        out_specs=tuple([row2(D_MIX)] + small_specs + [ssd_out_spec]),
        out_shape=out_shape,
        input_output_aliases={n_lead + k: 4 + k for k in range(n_acc)},
        compiler_params=pltpu.CompilerParams(
            dimension_semantics=("parallel",), vmem_limit_bytes=VMEM_LIMIT),
        name="sample_mixer",
    )(proj, rgc, rgh, sc, ssd_all, o_hg, *acc, *params)


def _post_kernel(final, h_ref, y_ref, p_ref, wo, nffn, wup, wdown, nple, wgate, wproj, nfin, o_ref):
    h = h_ref[...] + _dot(_bf(y_ref[...]), wo[...])
    hn = _bf(_rms(h, nffn[...]))
    for j in range(D_FF // FF_CHUNK):
        cs = slice(j * FF_CHUNK, (j + 1) * FF_CHUNK)
        z = jnp.square(jnp.maximum(_dot(hn, wup[:, cs]), 0.0))
        h = h + _dot(_bf(z), wdown[cs, :])
    gate = _sigmoid(_dot(_bf(_rms(h, nple[...])), wgate[...]))
    h = h + gate * _dot(_bf(p_ref[...]), wproj[...])
    if final:
        h = _rms(h, nfin[...])
    o_ref[...] = h


def _post(final, layer, h2d, ymix2d, p_all, lp, nfin, tm):
    m = h2d.shape[0]
    params = [lp["wo"], lp["nffn"], lp["wup"], lp["wdown"], lp["nple"], lp["wgate"], lp["wproj"], nfin]
    return pl.pallas_call(
        functools.partial(_post_kernel, final),
        grid=(m // tm,),
        in_specs=[pl.BlockSpec((tm, D_MODEL), lambda i: (i, 0)),
                  pl.BlockSpec((tm, D_MIX), lambda i: (i, 0)),
                  pl.BlockSpec((None, tm, PLE_DIM), lambda i: (layer, i, 0))]
        + [_const_spec(p) for p in params],
        out_specs=pl.BlockSpec((tm, D_MODEL), lambda i: (i, 0)),
        out_shape=jax.ShapeDtypeStruct((m, D_MODEL), F32),
        compiler_params=pltpu.CompilerParams(
            dimension_semantics=("parallel",), vmem_limit_bytes=VMEM_LIMIT),
        name="post_mixer",
    )(h2d, ymix2d, p_all, *params)


def _constants():
    t = np.arange(TC)
    ltri = (t[None, :] <= t[:, None]).astype(np.float32)
    expand = np.zeros((LANES, SSD_WIDTH), np.float32)
    for hd in range(SSD_HEADS):
        expand[hd, hd * SSD_HEAD_DIM:(hd + 1) * SSD_HEAD_DIM] = 1.0
    emat = np.zeros(((N_MXU_LEVELS + 1) * TC, TC), np.float32)
    for lev in range(1, N_MXU_LEVELS + 1):
        blk = 1 << lev
        mid = (t // blk) * blk + blk // 2
        upper = t >= mid
        r0 = (lev - 1) * TC
        emat[r0:r0 + TC] = (
            (upper[:, None] & (t[None, :] >= mid[:, None]) & (t[None, :] <= t[:, None]))
            | ((~upper)[:, None] & (t[None, :] > t[:, None]) & (t[None, :] < mid[:, None])))
    emat[N_MXU_LEVELS * TC:] = ltri
    cmask = np.zeros((N_LEVELS + 1, TC, 2 * TC), np.float32)
    cmask[0] = np.tile(np.eye(TC, dtype=np.float32), (1, 2))
    for lev in range(1, N_LEVELS + 1):
        blk = 1 << lev
        up = (t % blk) >= blk // 2
        m = up[:, None] & (~up)[None, :] & ((t // blk)[:, None] == (t // blk)[None, :])
        cmask[lev] = np.tile(m.astype(np.float32), (1, 2))
    hidx = np.arange(HG_WIDTH) // HG_HEAD_DIM
    bd = (hidx[:, None] == hidx[None, :]).astype(np.float32)
    pair = bd[:LANES, :LANES]
    return {
        "ltri2": jnp.asarray(np.tile(ltri, (1, 2)), BF16),
        "expand": jnp.asarray(expand, BF16),
        "emat2": jnp.asarray(np.tile(emat, (1, 2)), BF16),
        "cmask": jnp.asarray(cmask, F32),
        "pairmask": jnp.asarray(pair, F32),
        "pairmean": jnp.asarray(pair / HG_HEAD_DIM, BF16),
        "headsum": jnp.asarray(bd, BF16), "headmean": jnp.asarray(bd / HG_HEAD_DIM, BF16),
    }


def _block_diag(w):
    hh, d, _ = w.shape
    eye = jnp.eye(hh, dtype=w.dtype)
    return (eye[:, None, :, None] * w[:, :, None, :]).reshape(hh * d, hh * d)


def _pair_gate_weights(wa, wx):
    hh, d, _ = wa.shape
    pa = jnp.stack([_block_diag(wa[2 * g:2 * g + 2]) for g in range(hh // 2)])
    px = jnp.stack([_block_diag(wx[2 * g:2 * g + 2]) for g in range(hh // 2)])
    return jnp.concatenate([pa, px], axis=2)


def _pad_lanes(v):
    return jnp.pad(v, (0, LANES - v.shape[0]))[None, :]


def kernel(x_prompt, x_sample, state_rg_conv, state_rg_h, state_ssd_conv, state_ssd, state_hgrn,
           p_prompt, p_sample, norm_mix, w_in, conv_a_w, conv_a_b, rg_wa, rg_ba, rg_wx, rg_bx,
           rg_lambda, conv_b_w, conv_b_b, ssd_dt_bias, ssd_a_log, ssd_d, ssd_norm,
           hg_lower_bounds, hg_norm, w_out, norm_ffn, w_up, w_down, norm_ple, w_ple_gate,
           w_ple_proj, norm_final):
    bsz, seq, _ = x_prompt.shape
    nb = x_sample.shape[0]
    consts = _constants()
    w_in_t = jnp.swapaxes(w_in, 1, 2)
    nfin = norm_final[None, :]
    ps_all = p_sample.reshape(DEPTH, nb, PLE_DIM)

    hp = x_prompt
    hs = x_sample.reshape(nb, D_MODEL)
    st_p = [[] for _ in range(5)]
    st_s = [[] for _ in range(3)]
    hg_t_all = jnp.transpose(state_hgrn, (0, 2, 3, 4, 1))
    acc_ssd = ()
    acc_hg = ()
    for i in range(DEPTH):
        w_re, wo_bf, wup_bf, wdown_bf, wgate_bf, wproj_bf = _prep_layer_weights(
            i, w_in_t, w_out, w_up, w_down, w_ple_gate, w_ple_proj)
        gmix = norm_mix[i][None, :]
        mp = {
            "caw": conv_a_w[i], "cab": conv_a_b[i][None, :],
            "wg": jnp.concatenate([_block_diag(rg_wa[i]), _block_diag(rg_wx[i])], axis=1).astype(BF16),
            "bg": jnp.concatenate([rg_ba[i], rg_bx[i]])[None, :],
            "wgp": _pair_gate_weights(rg_wa[i], rg_wx[i]).astype(BF16),
            "bgp": jnp.concatenate([rg_ba[i].reshape(NGRP, 1, LANES),
                                    rg_bx[i].reshape(NGRP, 1, LANES)], axis=2),
            "lam": rg_lambda[i][None, :],
            "cbw": conv_b_w[i], "cbb": conv_b_b[i][None, :],
            "dtb": _pad_lanes(ssd_dt_bias[i]), "alog": _pad_lanes(ssd_a_log[i]),
            "dexp": jnp.repeat(ssd_d[i], SSD_HEAD_DIM)[None, :],
            "snorm": ssd_norm[i][None, :],
            "hlb": hg_lower_bounds, "hnorm": jnp.tile(hg_norm[i], HG_HEADS)[None, :],
        }
        lp = {
            "wo": wo_bf, "nffn": norm_ffn[i][None, :], "wup": wup_bf, "wdown": wdown_bf,
            "nple": norm_ple[i][None, :], "wgate": wgate_bf, "wproj": wproj_bf,
        }
        final = i == DEPTH - 1

        hp, rgc, rgh, sc, ssd, hg = _prompt_layer(
            i, final, hp, p_prompt, gmix, w_re, mp, lp, nfin, consts)
        for lst, s in zip(st_p, (rgc, rgh.reshape(bsz, RG_WIDTH), sc,
                                 ssd.reshape(bsz, SSD_HEADS, SSD_HEAD_DIM, SSD_STATE), hg)):
            lst.append(s)

        proj_s = _in_proj(hs, gmix, w_re, nb)
        o_hg_t, hg_acc = _hg_step(i, proj_s, hg_lower_bounds, hg_t_all, acc_hg)
        acc_hg = (hg_acc,)
        ymix_s, rgc, rgh, sc, ssd_acc = _sample_mixer(
            i, proj_s,
            state_rg_conv[i].reshape(nb, 3 * RG_WIDTH), state_rg_h[i],
            state_ssd_conv[i].reshape(nb, 3 * SSD_CONV_DIM),
            state_ssd, o_hg_t.T, acc_ssd, mp, consts)
        acc_ssd = (ssd_acc,)
        hs = _post(final, i, hs, ymix_s, ps_all, lp, nfin, nb)
        for lst, s in zip(st_s, (rgc.reshape(nb, CONV_W - 1, RG_WIDTH), rgh,
                                 sc.reshape(nb, CONV_W - 1, SSD_CONV_DIM))):
            lst.append(s)

    y_prompt = hp
    y_sample = hs.reshape(nb, 1, D_MODEL)
    outs_p = [jnp.stack(lst) for lst in st_p]
    outs_s = [jnp.stack(lst) for lst in st_s] + [
        acc_ssd[0], jnp.transpose(acc_hg[0], (0, 4, 1, 2, 3))]
    return (y_prompt, y_sample, *outs_p, *outs_s)
```

```python
import functools

import numpy as np
import jax
import jax.numpy as jnp
from jax import lax
from jax.experimental import pallas as pl
from jax.experimental.pallas import tpu as pltpu

F32 = jnp.float32
BF16 = jnp.bfloat16

D_MODEL = 1024
DEPTH = 2
PAST_LEN = 16384
PLE_DIM = 256
D_FF = 4 * D_MODEL
CONV_W = 4
EPS = 1e-6
RG_WIDTH = 512
RG_HEADS = 8
RG_HEAD_DIM = 64
RG_C = 8.0
SSD_WIDTH = 512
SSD_HEAD_DIM = 64
SSD_HEADS = 8
SSD_STATE = 128
SSD_CONV_DIM = SSD_WIDTH + 2 * SSD_STATE
HG_WIDTH = 512
HG_HEADS = 8
HG_HEAD_DIM = 64
D_MIX = RG_WIDTH + SSD_WIDTH + HG_WIDTH

LOG2E = 1.4426950408889634
LANES = 128
SUBLANES = 8
NGRP = RG_WIDTH // LANES

OFF_AX = 0
OFF_AG = 512
OFF_BZ = 1024
OFF_XBC = 1536
OFF_CQ = 2304
OFF_CF = 2816
OFF_CI = 3328
OFF_CG = 3840
OFF_DT = 4352
NPROJ = OFF_DT + LANES
DT_COL = 2 * RG_WIDTH + SSD_WIDTH + SSD_CONV_DIM
D_IN_PROJ = DT_COL + SSD_HEADS + 4 * HG_WIDTH

TC = 128
HALF = TC // 2
N_LEVELS = 7
N_MXU_LEVELS = 3
DEC_BB = 16
FF_CHUNK = 1024
VMEM_LIMIT = 58 * 1024 * 1024


def _bf(x):
    return x.astype(BF16)


def _dot(a, b):
    return jnp.dot(a, b, preferred_element_type=F32)


def _dot_nt(a, b):
    return lax.dot_general(a, b, (((1,), (1,)), ((), ())), preferred_element_type=F32)


def _split3(x):
    hi = _bf(x)
    r1 = x - hi.astype(F32)
    mid = _bf(r1)
    lo = _bf(r1 - mid.astype(F32))
    return hi, mid, lo


def _dot_x_c(x, c):
    hi, mid, lo = _split3(x)
    return _dot(hi, c) + _dot(mid, c) + _dot(lo, c)


def _split2(x):
    hi = _bf(x)
    return hi, _bf(x - hi.astype(F32))


def _logistic(x):
    return 1.0 / (1.0 + jnp.exp(-x))


def _sigmoid(x):
    return 0.5 * jnp.tanh(0.5 * x) + 0.5


def _silu(x):
    h = 0.5 * x
    return h * jnp.tanh(h) + h


def _softplus(x):
    return jnp.maximum(x, 0.0) + jnp.log1p(jnp.exp(-jnp.abs(x)))


def _gelu_tanh(x):
    c = np.float32(np.sqrt(2.0 / np.pi))
    return 0.5 * x * (1.0 + jnp.tanh(c * (x + 0.044715 * (x * x * x))))


def _rms(x, gain):
    return x * lax.rsqrt(jnp.mean(x * x, axis=-1, keepdims=True) + EPS) * gain


def _lower_bound_row(hlb, layer):
    m = jnp.max(hlb, axis=0, keepdims=True)
    e = jnp.exp(hlb - m)
    sm = e / jnp.sum(e, axis=0, keepdims=True)
    lb = jnp.zeros((1, HG_WIDTH), F32)
    for j in range(1, layer + 1):
        lb = lb + sm[j:j + 1, :]
    return lb


def _rg_coeffs(gates, lam_row):
    w = gates.shape[1] // 2
    r = _sigmoid(gates[:, :w])
    i = _sigmoid(gates[:, w:])
    log_a = -RG_C * r * _softplus(-lam_row)
    a = jnp.exp(log_a)
    mult = jnp.sqrt(-jnp.tanh(log_a) * (a * a + 1.0))
    return a, mult, i


def _const_spec(arr):
    nd = arr.ndim
    return pl.BlockSpec(arr.shape, lambda *_: (0,) * nd, pipeline_mode=pl.Buffered(1))


CAST_STEPS = 8


def _cast_kernel(*refs):
    n = len(refs) // 2
    for w_ref, o_ref in zip(refs[:n], refs[n:]):
        o_ref[...] = _bf(w_ref[...])


def _cast_layer_weights(layer, *w_alls):
    in_specs, out_specs, out_shape = [], [], []
    for w_all in w_alls:
        _, rows, cols = w_all.shape
        rb = rows // CAST_STEPS
        assert rb * CAST_STEPS == rows and rb % (2 * SUBLANES) == 0
        in_specs.append(pl.BlockSpec((None, rb, cols), lambda i: (layer, i, 0)))
        out_specs.append(pl.BlockSpec((rb, cols), lambda i: (i, 0)))
        out_shape.append(jax.ShapeDtypeStruct((rows, cols), BF16))
    return pl.pallas_call(
        _cast_kernel,
        grid=(CAST_STEPS,),
        in_specs=in_specs,
        out_specs=tuple(out_specs),
        out_shape=tuple(out_shape),
        compiler_params=pltpu.CompilerParams(
            dimension_semantics=("parallel",), vmem_limit_bytes=VMEM_LIMIT),
        name="cast_bf16",
    )(*w_alls)


def _win_kernel(wt_ref, o_ref):
    o_ref[:, 0:OFF_CQ] = _bf(wt_ref[0:DT_COL, :].T)
    o_ref[:, OFF_CQ:OFF_DT] = _bf(wt_ref[DT_COL + SSD_HEADS:D_IN_PROJ, :].T)
    dt = jnp.concatenate([wt_ref[DT_COL:DT_COL + SSD_HEADS, :],
                          jnp.zeros((LANES - SSD_HEADS, LANES), F32)], axis=0)
    o_ref[:, OFF_DT:NPROJ] = _bf(dt.T)


def _prep_w_in(wt_all, layer):
    return pl.pallas_call(
        _win_kernel,
        grid=(D_MODEL // LANES,),
        in_specs=[pl.BlockSpec((None, D_IN_PROJ, LANES), lambda i: (layer, 0, i))],
        out_specs=pl.BlockSpec((LANES, NPROJ), lambda i: (i, 0)),
        out_shape=jax.ShapeDtypeStruct((D_MODEL, NPROJ), BF16),
        compiler_params=pltpu.CompilerParams(dimension_semantics=("parallel",)),
        name="prep_w_in",
    )(wt_all)


def _proj_kernel(x_ref, g_ref, w_ref, o_ref):
    u = _rms(x_ref[...], g_ref[...])
    o_ref[...] = _dot(_bf(u), w_ref[...])


PROJ_COL_BLOCK = 5 * LANES


def _in_proj(x2d, gain, w_bf, tm):
    m = x2d.shape[0]
    tn = PROJ_COL_BLOCK
    assert NPROJ % tn == 0
    return pl.pallas_call(
        _proj_kernel,
        grid=(m // tm, NPROJ // tn),
        in_specs=[pl.BlockSpec((tm, D_MODEL), lambda i, j: (i, 0)),
                  _const_spec(gain),
                  pl.BlockSpec((D_MODEL, tn), lambda i, j: (0, j))],
        out_specs=pl.BlockSpec((tm, tn), lambda i, j: (i, j)),
        out_shape=jax.ShapeDtypeStruct((m, NPROJ), F32),
        compiler_params=pltpu.CompilerParams(
            dimension_semantics=("parallel", "parallel"), vmem_limit_bytes=VMEM_LIMIT),
        name="in_proj",
    )(x2d, gain, w_bf)


def _mixer_chunk(layer, row0, pv, ymix, side_work,
                 caw, cab, wgp, bgp, lam, cbw, cbb, dtb, alog, dexp, snorm,
                 hlb, hnorm, ltri2, emat2, cmask, pairmask, pairmean,
                 xea, xeb, hcar, s_t, st_hg):
    row_h = lax.broadcasted_iota(jnp.int32, (HALF, 1), 0)
    sub_h = row_h & (SUBLANES - 1)
    first_head = lax.broadcasted_iota(jnp.int32, (HALF, LANES), 1) < HG_HEAD_DIM
    first_head_tc = lax.broadcasted_iota(jnp.int32, (TC, LANES), 1) < HG_HEAD_DIM
    col_tc = lax.broadcasted_iota(jnp.int32, (HALF, TC), 1)
    row_tc = lax.broadcasted_iota(jnp.int32, (HALF, TC), 0)

    tiles = [slice(t * HALF, (t + 1) * HALF) for t in range(TC // HALF)]

    def lanes(off, g):
        return slice(off + g * LANES, off + (g + 1) * LANES)

    def conv_rows(prev8, x, w_ref, b_ref, gs):
        xcat = jnp.concatenate([prev8, x], axis=0)
        y = b_ref[:, gs] + w_ref[CONV_W - 1:CONV_W, gs] * x
        for j in range(1, CONV_W):
            y = y + w_ref[CONV_W - 1 - j:CONV_W - j, gs] * pltpu.roll(xcat, j, 0)[SUBLANES:, :]
        return y, x[x.shape[0] - SUBLANES:, :]

    def conv(tail_ref, w_ref, b_ref, src_off, g):
        gs = lanes(0, g)
        y, tail = conv_rows(tail_ref[:, gs], pv[:, lanes(src_off, g)], w_ref, b_ref, gs)
        tail_ref[:, gs] = tail
        return y

    def mixer_a(g):
        gs = lanes(0, g)
        prev8 = xea[:, gs]
        carry = hcar[0:1, gs]
        xas = []
        for rs in tiles:
            xa, prev8 = conv_rows(prev8, pv[rs, lanes(OFF_AX, g)], caw, cab, gs)
            xas.append(xa)
        gates = _dot(_bf(jnp.concatenate(xas, axis=0)), wgp[g]) + bgp[g]
        for t, rs in enumerate(tiles):
            xa = xas[t]
            a, mult, gi = _rg_coeffs(gates[rs, :], lam[:, gs])
            mult = jnp.where(row_h + (row0 + t * HALF) == 0, 1.0, mult)
            b = mult * (gi * xa)
            k = 1
            while k < SUBLANES:
                keep = sub_h >= k
                a_sh = jnp.where(keep, pltpu.roll(a, k, 0), 1.0)
                b_sh = jnp.where(keep, pltpu.roll(b, k, 0), 0.0)
                b = b + a * b_sh
                a = a * a_sh
                k *= 2
            slabs = []
            for r in range(HALF // SUBLANES):
                sl = slice(r * SUBLANES, (r + 1) * SUBLANES)
                h_r = a[sl, :] * carry + b[sl, :]
                slabs.append(h_r)
                carry = h_r[SUBLANES - 1:SUBLANES, :]
            ymix[rs, gs] = jnp.concatenate(slabs, axis=0) * _gelu_tanh(
                pv[rs, lanes(OFF_AG, g)])
        xea[:, gs] = prev8
        hcar[0:1, gs] = carry

    bm = _silu(conv(xeb, cbw, cbb, OFF_XBC, NGRP))
    cm = _silu(conv(xeb, cbw, cbb, OFF_XBC, NGRP + 1))
    dtp = _softplus(pv[:, OFF_DT:OFF_DT + LANES] + dtb[...])
    v_da = dtp * (-LOG2E * jnp.exp(alog[...]))
    cum = _dot(ltri2[...], jnp.concatenate(_split2(v_da), axis=0))
    cum_t = cum.T

    def per_head(arr, g, rs):
        rows = arr[rs, :]
        return jnp.where(first_head[0:rows.shape[0], :], rows[:, 2 * g:2 * g + 1], rows[:, 2 * g + 1:2 * g + 2])
    cb = _dot_nt(_bf(cm), _bf(bm))
    cmb = _bf(cm)
    bmt = _bf(bm.T)
    ssq_parts = []

    def mixer_b(g):
        gs = lanes(0, g)
        cum_last = per_head(cum, g, slice(TC - 1, TC))
        s_old = _bf(s_t[g])
        prev8 = xeb[:, gs]
        xs_t, x2a, x2b, wx, cum_g = [], [], [], [], []
        for rs in tiles:
            xs, prev8 = conv_rows(prev8, pv[rs, lanes(OFF_XBC, g)], cbw, cbb, gs)
            xs = _silu(xs)
            dt_g = per_head(dtp, g, rs)
            cum_g.append(per_head(cum, g, rs))
            dtx = xs * dt_g
            xs_t.append(xs)
            x2a.append(_bf(jnp.where(first_head, dtx, 0.0)))
            x2b.append(_bf(jnp.where(first_head, 0.0, dtx)))
            wx.append(_bf(jnp.exp2(cum_last - cum_g[-1]) * dtx))
        xeb[:, gs] = prev8
        x2 = jnp.concatenate(x2a + x2b, axis=0)
        g_rows = []
        for t, rs in enumerate(tiles):
            gmats = []
            for hh in range(2):
                hd = 2 * g + hh
                seg = cum[rs, hd:hd + 1] - cum_t[hd:hd + 1, :]
                gmats.append(_bf(cb[rs, :] * jnp.exp2(jnp.where(col_tc <= row_tc + t * HALF, seg, -jnp.inf))))
            g_rows.append(jnp.concatenate(gmats, axis=1))
        y_all = (_dot(jnp.concatenate(g_rows, axis=0), x2)
                 + _dot(cmb, s_old) * jnp.exp2(jnp.concatenate(cum_g, axis=0)))
        for t, rs in enumerate(tiles):
            yz = (y_all[rs, :] + dexp[:, gs] * xs_t[t]) * _silu(pv[rs, lanes(OFF_BZ, g)])
            ssq_parts.append(yz * yz)
            ymix[rs, lanes(RG_WIDTH, g)] = yz
        s_t[g] = jnp.exp2(cum_last) * s_t[g] + _dot(bmt, jnp.concatenate(wx, axis=0))

    lb_all = _lower_bound_row(hlb[...], layer)

    def forget_gate(g):
        lb = lb_all[:, g * LANES:(g + 1) * LANES]
        return lb + (1.0 - lb) * _logistic(pv[:, OFF_CF + g * LANES:OFF_CF + (g + 1) * LANES])

    def mixer_c(g, gt, xall):
        gs = lanes(0, g)
        b_off = N_MXU_LEVELS * TC
        bcum = xall[b_off:b_off + TC, :]
        blast = bcum[TC - 1:TC, :]

        def level_exponent(lev):
            if lev <= N_MXU_LEVELS:
                return xall[(lev - 1) * TC:lev * TC, :]
            blk = 1 << lev
            refs = [jnp.broadcast_to(bcum[j * blk + blk // 2 - 1:j * blk + blk // 2, :], (blk, LANES))
                    for j in range(TC // blk)]
            return -jnp.abs(bcum - jnp.concatenate(refs, axis=0))
        q = _silu(pv[:, lanes(OFF_CQ, g)])
        kk = 1.0 - gt
        vv = pv[:, lanes(OFF_CI, g)]
        zeros = jnp.zeros((HG_HEAD_DIM, TC), BF16)

        def keys_by_head(kt):
            ktt = _bf(kt.T)
            return jnp.concatenate(
                [jnp.concatenate([ktt[:HG_HEAD_DIM, :], zeros], axis=1),
                 jnp.concatenate([zeros, ktt[HG_HEAD_DIM:, :]], axis=1)], axis=0)

        att = _dot(_bf(q), keys_by_head(kk)) * cmask[0]
        for lev in range(1, N_LEVELS + 1):
            e = jnp.exp2(level_exponent(lev))
            att = att + _dot(_bf(q * e), keys_by_head(kk * e)) * cmask[lev]
        v2 = jnp.concatenate([_bf(jnp.where(first_head_tc, vv, 0.0)),
                              _bf(jnp.where(first_head_tc, 0.0, vv))], axis=0)
        o = _dot(_bf(att), v2) + _dot_nt(_bf(q * jnp.exp2(bcum)), _bf(st_hg[g]))
        khat = kk * jnp.exp2(blast - bcum)
        st_hg[g] = jnp.exp2(blast) * st_hg[g] + pairmask[...] * _dot(_bf(vv.T), _bf(khat))
        ms = _dot(_bf(o * o), pairmean[...])
        ymix[:, lanes(RG_WIDTH + SSD_WIDTH, g)] = (
            o * lax.rsqrt(ms + EPS) * hnorm[:, gs] * _silu(pv[:, lanes(OFF_CG, g)]))

    for pair in range(NGRP // 2):
        g0, g1 = 2 * pair, 2 * pair + 1
        gts = [forget_gate(g0), forget_gate(g1)]
        lf = jnp.log(jnp.concatenate(gts, axis=1)) * LOG2E
        hi = _bf(lf)
        lo = _bf(lf - hi.astype(F32))
        xall2 = _dot(emat2[...], jnp.concatenate([hi, lo], axis=0))
        for g, gt in zip((g0, g1), gts):
            side_work[g]()
            mixer_a(g)
            mixer_b(g)
            mixer_c(g, gt, xall2[:, (g - g0) * LANES:(g - g0 + 1) * LANES])

    for t, rs in enumerate(tiles):
        ssq = ssq_parts[t]
        for g in range(1, NGRP):
            ssq = ssq + ssq_parts[g * len(tiles) + t]
        rinv = lax.rsqrt(jnp.sum(ssq, axis=-1, keepdims=True) * (1.0 / SSD_WIDTH) + EPS)
        for g in range(NGRP):
            ys = lanes(RG_WIDTH, g)
            ymix[rs, ys] = ymix[rs, ys] * rinv * snorm[:, lanes(0, g)]


N_MIX_PARAMS = 18
MXU_N = 256
PROJ_BOUNDS = (0, 5 * MXU_N, 9 * MXU_N, 13 * MXU_N, NPROJ)
assert len(PROJ_BOUNDS) == NGRP + 1
ROWS_PER_STEP = 2


N_POST_PARAMS = 8


def _prompt_layer_kernel(layer, final, pos0, nsteps, n_valid,
                         x_ref, xn_ref, xp_ref, p_ref, gmix, w_in_ref, *rest):
    params = rest[:N_MIX_PARAMS]
    k = N_MIX_PARAMS + N_POST_PARAMS
    wo, nffn, wup, wdown, nple, wgate, wproj, nfin = rest[N_MIX_PARAMS:k]
    h_out, rgc_o, rgh_o, sc_o, ssd_o, hg_o = rest[k:k + 6]
    proj_a, proj_b, ymix_s = rest[k + 6:k + 9]
    n_state = (len(rest) - k - 9) // ROWS_PER_STEP
    states = [rest[k + 9 + r * n_state:k + 9 + (r + 1) * n_state] for r in range(ROWS_PER_STEP)]
    s = pl.program_id(0)
    live = s < n_valid
    c = jnp.minimum(s, n_valid - 1) % nsteps

    @pl.when(s == 0)
    def _prologue():
        proj_a[...] = _dot(_bf(_rms(x_ref[0], gmix[...])), w_in_ref[...])
        ymix_s[...] = jnp.zeros_like(ymix_s)

    @pl.when(jnp.logical_and(c == 0, live))
    def _init():
        for row_states in states:
            for ref in row_states:
                ref[...] = jnp.zeros_like(ref)

    ymix_prev = _bf(ymix_s[...])
    n_ff = D_FF // FF_CHUNK
    carry = {}

    def post_piece(j):
        if j == 0:
            carry["h"] = xp_ref[...].reshape(ROWS_PER_STEP * TC, D_MODEL) + _dot(ymix_prev, wo[...])
            carry["hn"] = _bf(_rms(carry["h"], nffn[...]))
        elif j <= n_ff:
            cs = slice((j - 1) * FF_CHUNK, j * FF_CHUNK)
            z = jnp.square(jnp.maximum(_dot(carry["hn"], wup[:, cs]), 0.0))
            carry["h"] = carry["h"] + _dot(_bf(z), wdown[cs, :])
        elif j == n_ff + 1:
            h = carry["h"]
            gate = _sigmoid(_dot(_bf(_rms(h, nple[...])), wgate[...]))
            h = h + gate * _dot(_bf(p_ref[...].reshape(ROWS_PER_STEP * TC, PLE_DIM)), wproj[...])
            h = _rms(h, nfin[...]) if final else h
            h_out[...] = h.reshape(ROWS_PER_STEP, TC, D_MODEL)

    def side_work(x, dst, first_piece):
        u = _bf(_rms(x, gmix[...]))

        def slot(j):
            cs = slice(PROJ_BOUNDS[j], PROJ_BOUNDS[j + 1])
            dst[:, cs] = _dot(u, w_in_ref[:, cs])
            post_piece(first_piece + j)

        return [functools.partial(slot, j) for j in range(NGRP)]

    @pl.when(live)
    def _mix_and_side_work():
        row0 = c * TC + pos0
        _mixer_chunk(layer, row0, proj_a, ymix_s.at[0:TC],
                     side_work(x_ref[1], proj_b, 0), *params, *states[0])
        _mixer_chunk(layer, row0, proj_b, ymix_s.at[TC:2 * TC],
                     side_work(xn_ref[0], proj_a, NGRP), *params, *states[1])

    @pl.when(jnp.logical_not(live))
    def _drain():
        for j in range(n_ff + 2):
            post_piece(j)

    @pl.when(jnp.logical_and(c == nsteps - 1, live))
    def _emit():
        for r, (xea, xeb, hcar, s_t, st_hg) in enumerate(states):
            rgc_o[r] = xea[SUBLANES - 3:SUBLANES, :]
            sc_o[r] = xeb[SUBLANES - 3:SUBLANES, :]
            rgh_o[r] = hcar[0:1, :]
            for g in range(NGRP):
                ssd_o[r, g * LANES:(g + 1) * LANES, :] = s_t[g].T
                s_pair = st_hg[g].T
                for hh in range(2):
                    lo = hh * HG_HEAD_DIM
                    hg_o[r, 2 * g + hh] = s_pair[lo:lo + HG_HEAD_DIM, lo:lo + HG_HEAD_DIM]


def _prompt_layer(layer, final, x3, p_all4, gmix, w_in_bf, mp, lp, nfin, consts):
    bsz, seq, _ = x3.shape
    rps = ROWS_PER_STEP
    nsteps = seq // TC
    n_valid = (bsz // rps) * nsteps
    assert bsz % rps == 0 and seq % TC == 0
    params = [mp["caw"], mp["cab"], mp["wgp"], mp["bgp"], mp["lam"], mp["cbw"], mp["cbb"],
              mp["dtb"], mp["alog"], mp["dexp"], mp["snorm"], mp["hlb"], mp["hnorm"],
              consts["ltri2"], consts["emat2"],
              consts["cmask"], consts["pairmask"], consts["pairmean"]]
    post_params = [lp["wo"], lp["nffn"], lp["wup"], lp["wdown"], lp["nple"], lp["wgate"],
                   lp["wproj"], nfin]
    assert len(params) == N_MIX_PARAMS and len(post_params) == N_POST_PARAMS

    def mix_step(s):
        return jnp.minimum(s, n_valid - 1)

    def cur(s):
        return mix_step(s) // nsteps, mix_step(s) % nsteps, 0

    def nxt(s):
        ns = jnp.minimum(mix_step(s) + 1, n_valid - 1)
        return rps * (ns // nsteps), ns % nsteps, 0

    def prv(s):
        sp = jnp.maximum(s - 1, 0)
        return sp // nsteps, sp % nsteps, 0

    def state_idx(nd):
        return lambda s: (mix_step(s) // nsteps,) + (0,) * (nd - 1)

    out_shape = (
        jax.ShapeDtypeStruct((bsz, seq, D_MODEL), F32),
        jax.ShapeDtypeStruct((bsz, CONV_W - 1, RG_WIDTH), F32),
        jax.ShapeDtypeStruct((bsz, 1, RG_WIDTH), F32),
        jax.ShapeDtypeStruct((bsz, CONV_W - 1, SSD_CONV_DIM), F32),
        jax.ShapeDtypeStruct((bsz, SSD_WIDTH, SSD_STATE), F32),
        jax.ShapeDtypeStruct((bsz, HG_HEADS, HG_HEAD_DIM, HG_HEAD_DIM), F32),
    )
    out_specs = (
        pl.BlockSpec((rps, TC, D_MODEL), prv),
        pl.BlockSpec((rps, CONV_W - 1, RG_WIDTH), state_idx(3)),
        pl.BlockSpec((rps, 1, RG_WIDTH), state_idx(3)),
        pl.BlockSpec((rps, CONV_W - 1, SSD_CONV_DIM), state_idx(3)),
        pl.BlockSpec((rps, SSD_WIDTH, SSD_STATE), state_idx(3)),
        pl.BlockSpec((rps, HG_HEADS, HG_HEAD_DIM, HG_HEAD_DIM), state_idx(4)),
    )
    row_state = [
        pltpu.VMEM((SUBLANES, RG_WIDTH), F32),
        pltpu.VMEM((SUBLANES, SSD_CONV_DIM), F32),
        pltpu.VMEM((SUBLANES, RG_WIDTH), F32),
        pltpu.VMEM((NGRP, SSD_STATE, LANES), F32),
        pltpu.VMEM((NGRP, LANES, LANES), F32),
    ]
    scratch = [
        pltpu.VMEM((TC, NPROJ), F32),
        pltpu.VMEM((TC, NPROJ), F32),
        pltpu.VMEM((rps * TC, D_MIX), F32),
    ] + row_state * rps
    return pl.pallas_call(
        functools.partial(_prompt_layer_kernel, layer, final, 0, nsteps, n_valid),
        grid=(n_valid + 1,),
        in_specs=[pl.BlockSpec((rps, TC, D_MODEL), cur),
                  pl.BlockSpec((1, TC, D_MODEL), nxt),
                  pl.BlockSpec((rps, TC, D_MODEL), prv),
                  pl.BlockSpec((None, rps, TC, PLE_DIM), lambda s: (layer,) + prv(s)),
                  _const_spec(gmix), _const_spec(w_in_bf)]
        + [_const_spec(p) for p in params + post_params],
        out_specs=out_specs,
        out_shape=out_shape,
        scratch_shapes=scratch,
        compiler_params=pltpu.CompilerParams(
            dimension_semantics=("arbitrary",), vmem_limit_bytes=VMEM_LIMIT),
        name="prompt_layer",
    )(x3, x3, x3, p_all4, gmix, w_in_bf, *params, *post_params)


def _hg_step_kernel(layer, n_acc, proj_ref, hlb, st_ref, *rest):
    ot_ref, st_o, qh_s, f_s, k_s, v_s = rest[n_acc:]
    h = pl.program_id(0)
    if n_acc == 0:
        for other in range(DEPTH):
            if other != layer:
                st_o[other] = jnp.zeros(st_o.shape[1:], F32)
        st_o = st_o.at[layer]

    @pl.when(h == 0)
    def _prep():
        q = _silu(proj_ref[:, OFF_CQ:OFF_CQ + HG_WIDTH])
        lb = _lower_bound_row(hlb[...], layer)
        gt = lb + (1.0 - lb) * _logistic(proj_ref[:, OFF_CF:OFF_CF + HG_WIDTH])
        fdec = jnp.exp(jnp.log(gt))
        qh_s[...] = (q * fdec).T
        f_s[...] = fdec.T
        k_s[...] = (1.0 - gt).T
        v_s[...] = proj_ref[:, OFF_CI:OFF_CI + HG_WIDTH].T

    base = pl.multiple_of(h * HG_HEAD_DIM, HG_HEAD_DIM)
    v_h = v_s[pl.ds(base, HG_HEAD_DIM), :]

    def body(dk, acc):
        s = st_ref[dk]
        st_o[dk] = f_s[pl.ds(base + dk, 1), :] * s + k_s[pl.ds(base + dk, 1), :] * v_h
        return acc + qh_s[pl.ds(base + dk, 1), :] * s

    ot_ref[...] = lax.fori_loop(0, HG_HEAD_DIM, body,
                                jnp.zeros((HG_HEAD_DIM, LANES), F32), unroll=8)


def _hg_step(layer, proj, hlb, st_all, acc):
    nb = proj.shape[0]
    blk = (HG_HEAD_DIM, HG_HEAD_DIM, nb)
    st_spec = pl.BlockSpec((None, None) + blk, lambda h: (layer, h, 0, 0, 0))
    n_acc = len(acc)
    st_out_spec = st_spec if n_acc else pl.BlockSpec(
        (DEPTH, None) + blk, lambda h: (0, h, 0, 0, 0))
    return pl.pallas_call(
        functools.partial(_hg_step_kernel, layer, n_acc),
        grid=(HG_HEADS,),
        in_specs=[_const_spec(proj), _const_spec(hlb), st_spec]
        + [pl.BlockSpec(memory_space=pl.ANY)] * n_acc,
        out_specs=(pl.BlockSpec((HG_HEAD_DIM, nb), lambda h: (h, 0)), st_out_spec),
        out_shape=(jax.ShapeDtypeStruct((HG_WIDTH, nb), F32),
                   jax.ShapeDtypeStruct(st_all.shape, F32)),
        scratch_shapes=[pltpu.VMEM((HG_WIDTH, nb), F32)] * 4,
        input_output_aliases={3 + k: 1 + k for k in range(n_acc)},
        compiler_params=pltpu.CompilerParams(
            dimension_semantics=("arbitrary",), vmem_limit_bytes=VMEM_LIMIT),
        name="hg_step",
    )(proj, hlb, st_all, *acc)


def _pad_rows_t(x):
    pad = jnp.zeros((LANES - x.shape[0], x.shape[1]), F32)
    return jnp.concatenate([x, pad], axis=0).T


def _sample_mixer_kernel(layer, pos0, n_acc, proj_ref, rgc_ref, rgh_ref, sc_ref, ssd_ref, ohg_ref, *rest):
    (caw, cab, wg, bg, lam, cbw, cbb, dtb, alog, dexp, snorm,
     hlb, hnorm, expand, headsum, headmean,
     ymix_ref, rgc_o, rgh_o, sc_o, ssd_o) = rest[n_acc:]
    bb = DEC_BB
    if n_acc == 0:
        for other in range(DEPTH):
            if other != layer:
                ssd_o[other] = jnp.zeros(ssd_o.shape[1:], F32)
        ssd_o = ssd_o.at[layer]
    ax = proj_ref[:, OFF_AX:OFF_AX + RG_WIDTH]
    xa = cab[...] + caw[CONV_W - 1:CONV_W, :] * ax
    for k in range(CONV_W - 1):
        xa = xa + caw[k:k + 1, :] * rgc_ref[:, k * RG_WIDTH:(k + 1) * RG_WIDTH]
    rgc_o[:, 0:2 * RG_WIDTH] = rgc_ref[:, RG_WIDTH:3 * RG_WIDTH]
    rgc_o[:, 2 * RG_WIDTH:] = ax
    a, mult, gi = _rg_coeffs(_dot(_bf(xa), wg[...]) + bg[...], lam[...])
    if pos0 == 0:
        mult = jnp.ones_like(mult)
    h = a * rgh_ref[...] + mult * (gi * xa)
    rgh_o[...] = h
    ymix_ref[:, 0:RG_WIDTH] = h * _gelu_tanh(proj_ref[:, OFF_AG:OFF_AG + RG_WIDTH])

    bx = proj_ref[:, OFF_XBC:OFF_XBC + SSD_CONV_DIM]
    xbc = cbb[...] + cbw[CONV_W - 1:CONV_W, :] * bx
    for k in range(CONV_W - 1):
        xbc = xbc + cbw[k:k + 1, :] * sc_ref[:, k * SSD_CONV_DIM:(k + 1) * SSD_CONV_DIM]
    sc_o[:, 0:2 * SSD_CONV_DIM] = sc_ref[:, SSD_CONV_DIM:3 * SSD_CONV_DIM]
    sc_o[:, 2 * SSD_CONV_DIM:] = bx
    xbc = _silu(xbc)
    xs = xbc[:, :SSD_WIDTH]
    bm = xbc[:, SSD_WIDTH:SSD_WIDTH + SSD_STATE]
    cm = xbc[:, SSD_WIDTH + SSD_STATE:]
    dtp = _softplus(proj_ref[:, OFF_DT:OFF_DT + LANES] + dtb[...])
    v_da = dtp * (-jnp.exp(alog[...]))
    dt_e = _dot_x_c(dtp, expand[...])
    e_e = jnp.exp(_dot_x_c(v_da, expand[...]))
    dtx = dt_e * xs
    cbs = _dot(_bf(cm * bm), jnp.ones((SSD_STATE, LANES), BF16))[:, 0:1]
    dtx_t = _pad_rows_t(dtx)
    e_t = _pad_rows_t(e_e)
    c_pad = _bf(jnp.concatenate([cm, jnp.zeros((LANES - bb, SSD_STATE), F32)], axis=0))
    lane_w = lax.broadcasted_iota(jnp.int32, (SSD_WIDTH, LANES), 1)
    y_t = jnp.zeros((SSD_WIDTH, LANES), F32)
    for j in range(bb):
        s_old = ssd_ref[j].reshape(SSD_WIDTH, SSD_STATE)
        ssd_o[j] = (e_t[:, j:j + 1] * s_old + dtx_t[:, j:j + 1] * bm[j:j + 1, :]).reshape(
            SSD_HEADS, SSD_HEAD_DIM, SSD_STATE)
        y_t = y_t + jnp.where(lane_w == j, _dot_nt(_bf(s_old), c_pad), 0.0)
    y = cbs * dtx + y_t.T[0:bb, :] * e_e
    yb = y + dexp[...] * xs
    ymix_ref[:, RG_WIDTH:RG_WIDTH + SSD_WIDTH] = _rms(
        yb * _silu(proj_ref[:, OFF_BZ:OFF_BZ + SSD_WIDTH]), snorm[...])

    q = _silu(proj_ref[:, OFF_CQ:OFF_CQ + HG_WIDTH])
    lb = _lower_bound_row(hlb[...], layer)
    gt = lb + (1.0 - lb) * _logistic(proj_ref[:, OFF_CF:OFF_CF + HG_WIDTH])
    kk = 1.0 - gt
    vv = proj_ref[:, OFF_CI:OFF_CI + HG_WIDTH]
    att = _dot(_bf(q * kk), headsum[...])
    o = att * vv + ohg_ref[...]
    ms = _dot(_bf(o * o), headmean[...])
    ymix_ref[:, RG_WIDTH + SSD_WIDTH:] = (
        o * lax.rsqrt(ms + EPS) * hnorm[...] * _silu(proj_ref[:, OFF_CG:OFF_CG + HG_WIDTH]))


def _sample_mixer(layer, proj, rgc, rgh, sc, ssd_all, o_hg, acc, mp, consts):
    nb = proj.shape[0]
    bb = DEC_BB
    params = [mp["caw"], mp["cab"], mp["wg"], mp["bg"], mp["lam"], mp["cbw"], mp["cbb"],
              mp["dtb"], mp["alog"], mp["dexp"], mp["snorm"], mp["hlb"], mp["hnorm"],
              consts["expand"], consts["headsum"], consts["headmean"]]
    row2 = lambda w: pl.BlockSpec((bb, w), lambda i: (i, 0))
    small_specs = [row2(3 * RG_WIDTH), row2(RG_WIDTH), row2(3 * SSD_CONV_DIM)]
    ssd_spec = pl.BlockSpec((None, bb, SSD_HEADS, SSD_HEAD_DIM, SSD_STATE),
                            lambda i: (layer, i, 0, 0, 0))
    out_shape = (
        jax.ShapeDtypeStruct((nb, D_MIX), F32),
        jax.ShapeDtypeStruct((nb, 3 * RG_WIDTH), F32),
        jax.ShapeDtypeStruct((nb, RG_WIDTH), F32),
        jax.ShapeDtypeStruct((nb, 3 * SSD_CONV_DIM), F32),
        jax.ShapeDtypeStruct(ssd_all.shape, F32),
    )
    n_acc = len(acc)
    n_lead = 6
    ssd_out_spec = ssd_spec if n_acc else pl.BlockSpec(
        (DEPTH, bb, SSD_HEADS, SSD_HEAD_DIM, SSD_STATE), lambda i: (0, i, 0, 0, 0))
    return pl.pallas_call(
        functools.partial(_sample_mixer_kernel, layer, PAST_LEN, n_acc),
        grid=(nb // bb,),
        in_specs=[row2(NPROJ)] + small_specs + [ssd_spec, row2(HG_WIDTH)]
        + [pl.BlockSpec(memory_space=pl.ANY)] * n_acc + [_const_spec(p) for p in params],
        out_specs=tuple([row2(D_MIX)] + small_specs + [ssd_out_spec]),
        out_shape=out_shape,
        input_output_aliases={n_lead + k: 4 + k for k in range(n_acc)},
        compiler_params=pltpu.CompilerParams(
            dimension_semantics=("parallel",), vmem_limit_bytes=VMEM_LIMIT),
        name="sample_mixer",
    )(proj, rgc, rgh, sc, ssd_all, o_hg, *acc, *params)


def _post_kernel(final, h_ref, y_ref, p_ref, wo, nffn, wup, wdown, nple, wgate, wproj, nfin, o_ref):
    h = h_ref[...] + _dot(_bf(y_ref[...]), wo[...])
    hn = _bf(_rms(h, nffn[...]))
    for j in range(D_FF // FF_CHUNK):
        cs = slice(j * FF_CHUNK, (j + 1) * FF_CHUNK)
        z = jnp.square(jnp.maximum(_dot(hn, wup[:, cs]), 0.0))
        h = h + _dot(_bf(z), wdown[cs, :])
    gate = _sigmoid(_dot(_bf(_rms(h, nple[...])), wgate[...]))
    h = h + gate * _dot(_bf(p_ref[...]), wproj[...])
    if final:
        h = _rms(h, nfin[...])
    o_ref[...] = h


def _post(final, layer, h2d, ymix2d, p_all, lp, nfin, tm):
    m = h2d.shape[0]
    params = [lp["wo"], lp["nffn"], lp["wup"], lp["wdown"], lp["nple"], lp["wgate"], lp["wproj"], nfin]
    return pl.pallas_call(
        functools.partial(_post_kernel, final),
        grid=(m // tm,),
        in_specs=[pl.BlockSpec((tm, D_MODEL), lambda i: (i, 0)),
                  pl.BlockSpec((tm, D_MIX), lambda i: (i, 0)),
                  pl.BlockSpec((None, tm, PLE_DIM), lambda i: (layer, i, 0))]
        + [_const_spec(p) for p in params],
        out_specs=pl.BlockSpec((tm, D_MODEL), lambda i: (i, 0)),
        out_shape=jax.ShapeDtypeStruct((m, D_MODEL), F32),
        compiler_params=pltpu.CompilerParams(
            dimension_semantics=("parallel",), vmem_limit_bytes=VMEM_LIMIT),
        name="post_mixer",
    )(h2d, ymix2d, p_all, *params)


def _constants():
    t = np.arange(TC)
    ltri = (t[None, :] <= t[:, None]).astype(np.float32)
    expand = np.zeros((LANES, SSD_WIDTH), np.float32)
    for hd in range(SSD_HEADS):
        expand[hd, hd * SSD_HEAD_DIM:(hd + 1) * SSD_HEAD_DIM] = 1.0
    emat = np.zeros(((N_MXU_LEVELS + 1) * TC, TC), np.float32)
    for lev in range(1, N_MXU_LEVELS + 1):
        blk = 1 << lev
        mid = (t // blk) * blk + blk // 2
        upper = t >= mid
        r0 = (lev - 1) * TC
        emat[r0:r0 + TC] = (
            (upper[:, None] & (t[None, :] >= mid[:, None]) & (t[None, :] <= t[:, None]))
            | ((~upper)[:, None] & (t[None, :] > t[:, None]) & (t[None, :] < mid[:, None])))
    emat[N_MXU_LEVELS * TC:] = ltri
    cmask = np.zeros((N_LEVELS + 1, TC, 2 * TC), np.float32)
    cmask[0] = np.tile(np.eye(TC, dtype=np.float32), (1, 2))
    for lev in range(1, N_LEVELS + 1):
        blk = 1 << lev
        up = (t % blk) >= blk // 2
        m = up[:, None] & (~up)[None, :] & ((t // blk)[:, None] == (t // blk)[None, :])
        cmask[lev] = np.tile(m.astype(np.float32), (1, 2))
    hidx = np.arange(HG_WIDTH) // HG_HEAD_DIM
    bd = (hidx[:, None] == hidx[None, :]).astype(np.float32)
    pair = bd[:LANES, :LANES]
    return {
        "ltri2": jnp.asarray(np.tile(ltri, (1, 2)), BF16),
        "expand": jnp.asarray(expand, BF16),
        "emat2": jnp.asarray(np.tile(emat, (1, 2)), BF16),
        "cmask": jnp.asarray(cmask, F32),
        "pairmask": jnp.asarray(pair, F32),
        "pairmean": jnp.asarray(pair / HG_HEAD_DIM, BF16),
        "headsum": jnp.asarray(bd, BF16), "headmean": jnp.asarray(bd / HG_HEAD_DIM, BF16),
    }


def _block_diag(w):
    hh, d, _ = w.shape
    eye = jnp.eye(hh, dtype=w.dtype)
    return (eye[:, None, :, None] * w[:, :, None, :]).reshape(hh * d, hh * d)


def _pair_gate_weights(wa, wx):
    hh, d, _ = wa.shape
    pa = jnp.stack([_block_diag(wa[2 * g:2 * g + 2]) for g in range(hh // 2)])
    px = jnp.stack([_block_diag(wx[2 * g:2 * g + 2]) for g in range(hh // 2)])
    return jnp.concatenate([pa, px], axis=2)


def _pad_lanes(v):
    return jnp.pad(v, (0, LANES - v.shape[0]))[None, :]


def kernel(x_prompt, x_sample, state_rg_conv, state_rg_h, state_ssd_conv, state_ssd, state_hgrn,
           p_prompt, p_sample, norm_mix, w_in, conv_a_w, conv_a_b, rg_wa, rg_ba, rg_wx, rg_bx,
           rg_lambda, conv_b_w, conv_b_b, ssd_dt_bias, ssd_a_log, ssd_d, ssd_norm,
           hg_lower_bounds, hg_norm, w_out, norm_ffn, w_up, w_down, norm_ple, w_ple_gate,
           w_ple_proj, norm_final):
    bsz, seq, _ = x_prompt.shape
    nb = x_sample.shape[0]
    consts = _constants()
    w_in_t = jnp.swapaxes(w_in, 1, 2)
    nfin = norm_final[None, :]
    ps_all = p_sample.reshape(DEPTH, nb, PLE_DIM)

    hp = x_prompt
    hs = x_sample.reshape(nb, D_MODEL)
    st_p = [[] for _ in range(5)]
    st_s = [[] for _ in range(3)]
    hg_t_all = jnp.transpose(state_hgrn, (0, 2, 3, 4, 1))
    acc_ssd = ()
    acc_hg = ()
    for i in range(DEPTH):
        w_re = _prep_w_in(w_in_t, i)
        gmix = norm_mix[i][None, :]
        mp = {
            "caw": conv_a_w[i], "cab": conv_a_b[i][None, :],
            "wg": jnp.concatenate([_block_diag(rg_wa[i]), _block_diag(rg_wx[i])], axis=1).astype(BF16),
            "bg": jnp.concatenate([rg_ba[i], rg_bx[i]])[None, :],
            "wgp": _pair_gate_weights(rg_wa[i], rg_wx[i]).astype(BF16),
            "bgp": jnp.concatenate([rg_ba[i].reshape(NGRP, 1, LANES),
                                    rg_bx[i].reshape(NGRP, 1, LANES)], axis=2),
            "lam": rg_lambda[i][None, :],
            "cbw": conv_b_w[i], "cbb": conv_b_b[i][None, :],
            "dtb": _pad_lanes(ssd_dt_bias[i]), "alog": _pad_lanes(ssd_a_log[i]),
            "dexp": jnp.repeat(ssd_d[i], SSD_HEAD_DIM)[None, :],
            "snorm": ssd_norm[i][None, :],
            "hlb": hg_lower_bounds, "hnorm": jnp.tile(hg_norm[i], HG_HEADS)[None, :],
        }
        wo_bf, wup_bf, wdown_bf, wgate_bf, wproj_bf = _cast_layer_weights(
            i, w_out, w_up, w_down, w_ple_gate, w_ple_proj)
        lp = {
            "wo": wo_bf, "nffn": norm_ffn[i][None, :], "wup": wup_bf, "wdown": wdown_bf,
            "nple": norm_ple[i][None, :], "wgate": wgate_bf, "wproj": wproj_bf,
        }
        final = i == DEPTH - 1

        hp, rgc, rgh, sc, ssd, hg = _prompt_layer(
            i, final, hp, p_prompt, gmix, w_re, mp, lp, nfin, consts)
        for lst, s in zip(st_p, (rgc, rgh.reshape(bsz, RG_WIDTH), sc,
                                 ssd.reshape(bsz, SSD_HEADS, SSD_HEAD_DIM, SSD_STATE), hg)):
            lst.append(s)

        proj_s = _in_proj(hs, gmix, w_re, nb)
        o_hg_t, hg_acc = _hg_step(i, proj_s, hg_lower_bounds, hg_t_all, acc_hg)
        acc_hg = (hg_acc,)
        ymix_s, rgc, rgh, sc, ssd_acc = _sample_mixer(
            i, proj_s,
            state_rg_conv[i].reshape(nb, 3 * RG_WIDTH), state_rg_h[i],
            state_ssd_conv[i].reshape(nb, 3 * SSD_CONV_DIM),
            state_ssd, o_hg_t.T, acc_ssd, mp, consts)
        acc_ssd = (ssd_acc,)
        hs = _post(final, i, hs, ymix_s, ps_all, lp, nfin, nb)
        for lst, s in zip(st_s, (rgc.reshape(nb, CONV_W - 1, RG_WIDTH), rgh,
                                 sc.reshape(nb, CONV_W - 1, SSD_CONV_DIM))):
            lst.append(s)

    y_prompt = hp
    y_sample = hs.reshape(nb, 1, D_MODEL)
    outs_p = [jnp.stack(lst) for lst in st_p]
    outs_s = [jnp.stack(lst) for lst in st_s] + [
        acc_ssd[0], jnp.transpose(acc_hg[0], (0, 4, 1, 2, 3))]
    return (y_prompt, y_sample, *outs_p, *outs_s)
```
